```python
import numpy as np
import jax
import jax.numpy as jnp
from jax import lax

D_MODEL = 1024
BATCH = 8
SEQ = 2048
DEPTH = 2

D_FF = 4 * D_MODEL
EPS = 1e-6
Q_BLOCK = 128
N_BRANCH = 3
MASK_NEG = -1e30

GDN_HEADS = 4
GDN_DK = 128
GDN_DV = 128
GDN_CONV = 4
GDN_CHUNK = 64
GDN_QK_W = GDN_HEADS * GDN_DK
GDN_V_W = GDN_HEADS * GDN_DV

NSA_HEADS = 8
NSA_KV_HEADS = 2
NSA_DH = 64
NSA_GROUP = NSA_HEADS // NSA_KV_HEADS
NSA_W = NSA_HEADS * NSA_DH
NSA_KV_W = NSA_KV_HEADS * NSA_DH
CMP_LEN = 32
CMP_STRIDE = 16
SEL_LEN = 64
SEL_TOPK = 8
WINDOW = 512
FORCE_BONUS = 1e3

SB_HEADS = 8
SB_DH = 64
SB_W = SB_HEADS * SB_DH

IN_SPLITS = (GDN_QK_W, GDN_QK_W, GDN_V_W, GDN_V_W, GDN_HEADS, GDN_HEADS,
             NSA_W, 6 * NSA_KV_W, 3 * NSA_HEADS,
             SB_W, SB_W, SB_W,
             N_BRANCH * D_MODEL)
N_IN = sum(IN_SPLITS)

kernel_name = 'hybrid_gdn_nsa_stickbreak_block'


def rms_norm(x, g):
    xf = x.astype(jnp.float32)
    y = xf * lax.rsqrt(jnp.mean(xf * xf, axis=-1, keepdims=True) + EPS)
    return (y * g.astype(jnp.float32)).astype(x.dtype)


def l2_norm(x):
    xf = x.astype(jnp.float32)
    return xf * lax.rsqrt(jnp.sum(xf * xf, axis=-1, keepdims=True) + EPS)


def causal_depthwise_conv(x, w):
    c = x.shape[-1]
    k = w.shape[0]
    return lax.conv_general_dilated(x, w[:, None, :].astype(x.dtype), window_strides=(1,),
                                    padding=[(k - 1, 0)], dimension_numbers=('NWC', 'WIO', 'NWC'),
                                    feature_group_count=c)


def chunk_gated_delta_rule(q, k, v, g, beta):
    b, h, s, dk = q.shape
    dv = v.shape[-1]
    c = GDN_CHUNK
    n = s // c
    q = q.reshape(b, h, n, c, dk)
    k = k.reshape(b, h, n, c, dk)
    v = v.reshape(b, h, n, c, dv)
    g = jnp.cumsum(g.reshape(b, h, n, c), axis=-1)
    beta = beta.reshape(b, h, n, c)
    incl = jnp.tril(jnp.ones((c, c), dtype=bool))
    strict = jnp.tril(jnp.ones((c, c), dtype=bool), -1)
    diff = g[..., :, None] - g[..., None, :]
    decay = jnp.where(incl, jnp.exp(jnp.where(incl, diff, 0.0)), 0.0)
    k_beta = k * beta[..., None]
    a_strict = jnp.where(strict, jnp.einsum('bhnid,bhnjd->bhnij', k_beta, k) * decay, 0.0)
    eye = jnp.eye(c, dtype=q.dtype)
    t_inv = lax.linalg.triangular_solve(a_strict + eye, jnp.broadcast_to(eye, a_strict.shape),
                                        left_side=True, lower=True)
    u = jnp.einsum('bhnij,bhnjd->bhnid', t_inv, v * beta[..., None])
    w = jnp.einsum('bhnij,bhnjd->bhnid', t_inv, k_beta * jnp.exp(g)[..., None])
    attn = jnp.einsum('bhnid,bhnjd->bhnij', q, k) * decay
    g_last = g[..., -1]
    k_dec = k * jnp.exp(g_last[..., None] - g)[..., None]

    def step(state, xs):
        q_c, kd_c, u_c, w_c, g_c, attn_c, gl_c = xs
        v_new = u_c - jnp.einsum('bhck,bhkv->bhcv', w_c, state)
        o = (jnp.einsum('bhck,bhkv->bhcv', q_c * jnp.exp(g_c)[..., None], state)
             + jnp.einsum('bhij,bhjv->bhiv', attn_c, v_new))
        state = state * jnp.exp(gl_c)[..., None, None] + jnp.einsum('bhck,bhcv->bhkv', kd_c, v_new)
        return state, o

    xs = tuple(jnp.moveaxis(t, 2, 0) for t in (q, k_dec, u, w, g, attn, g_last))
    state0 = jnp.zeros((b, h, dk, dv), q.dtype)
    _, o = lax.scan(step, state0, xs)
    return jnp.moveaxis(o, 0, 2).reshape(b, h, s, dv)


def gated_deltanet(q, k, v, z, a, bt, conv_w, a_log, dt_bias, norm_g):
    b, s, _ = q.shape
    dt = q.dtype
    f32 = jnp.float32
    qkv = jax.nn.silu(causal_depthwise_conv(jnp.concatenate([q, k, v], axis=-1), conv_w))
    q, k, v = jnp.split(qkv, [GDN_QK_W, 2 * GDN_QK_W], axis=-1)

    def heads(t, d):
        return t.reshape(b, s, GDN_HEADS, d).transpose(0, 2, 1, 3)

    qh = l2_norm(heads(q, GDN_DK)) * (GDN_DK ** -0.5)
    kh = l2_norm(heads(k, GDN_DK))
    vh = heads(v, GDN_DV).astype(f32)
    beta = jax.nn.sigmoid(bt.astype(f32)).transpose(0, 2, 1)
    g = -(jnp.exp(a_log.astype(f32))[:, None]
          * jax.nn.softplus(a.astype(f32) + dt_bias.astype(f32)).transpose(0, 2, 1))
    o = chunk_gated_delta_rule(qh, kh, vh, g, beta).transpose(0, 2, 1, 3)
    o = rms_norm(o, norm_g) * jax.nn.silu(z.astype(f32).reshape(b, s, GDN_HEADS, GDN_DV))
    return o.reshape(b, s, GDN_V_W).astype(dt)


def native_sparse_attention(q, kv, gate_logits, cmp_wk, cmp_pk, cmp_wv, cmp_pv):
    b, s, _ = q.shape
    dt = q.dtype
    f32 = jnp.float32
    g_, hg, dh = NSA_KV_HEADS, NSA_GROUP, NSA_DH
    qh = (q * (dh ** -0.5)).reshape(b, s, g_, hg, dh)
    k_cmp, v_cmp, k_slc, v_slc, k_win, v_win = [t.reshape(b, s, g_, dh) for t in jnp.split(kv, 6, axis=-1)]
    pos = jnp.arange(s)

    n_cmp = (s - CMP_LEN) // CMP_STRIDE + 1
    blk_start = jnp.arange(n_cmp) * CMP_STRIDE
    gidx = blk_start[:, None] + jnp.arange(CMP_LEN)[None, :]
    kc = jnp.einsum('bnlgd,lde->bnge', k_cmp[:, gidx] + cmp_pk[:, None, :], cmp_wk)
    vc = jnp.einsum('bnlgd,lde->bnge', v_cmp[:, gidx] + cmp_pv[:, None, :], cmp_wv)
    s_cmp = jnp.einsum('bsghd,bngd->bghsn', qh, kc).astype(f32)
    cmp_valid = (blk_start + CMP_LEN - 1)[None, :] <= pos[:, None]
    p_cmp = jax.nn.softmax(jnp.where(cmp_valid, s_cmp, MASK_NEG), axis=-1) * cmp_valid
    o_cmp = jnp.einsum('bghsn,bngd->bsghd', p_cmp.astype(dt), vc)

    n_sel = s // SEL_LEN
    top_k = min(SEL_TOPK, n_sel)
    sel_start_np = np.arange(n_sel) * SEL_LEN
    cmp_start_np = np.arange(n_cmp) * CMP_STRIDE
    overlap = ((cmp_start_np[None, :] < sel_start_np[:, None] + SEL_LEN)
               & (cmp_start_np[None, :] + CMP_LEN > sel_start_np[:, None])).astype(np.float32)
    imp = jnp.einsum('bghsn,jn->bgsj', p_cmp, jnp.asarray(overlap))
    blk = jnp.arange(n_sel)[None, :]
    cur = (pos // SEL_LEN)[:, None]
    sel_valid = blk <= cur
    forced = (blk == 0) | (blk == cur) | (blk == cur - 1)
    score = jnp.where(sel_valid, imp + FORCE_BONUS * forced, MASK_NEG)
    _, sel_idx = lax.top_k(score, top_k)

    ks = k_slc.reshape(b, n_sel, SEL_LEN, g_, dh).transpose(0, 3, 1, 2, 4)
    vs = v_slc.reshape(b, n_sel, SEL_LEN, g_, dh).transpose(0, 3, 1, 2, 4)
    kw = jnp.pad(k_win, ((0, 0), (WINDOW, 0), (0, 0), (0, 0)))
    vw = jnp.pad(v_win, ((0, 0), (WINDOW, 0), (0, 0), (0, 0)))
    b_ix = jnp.arange(b)[:, None, None, None]
    g_ix = jnp.arange(g_)[None, :, None, None]

    def query_block(qb):
        q0 = qb * Q_BLOCK
        qt = lax.dynamic_slice_in_dim(qh, q0, Q_BLOCK, axis=1)
        tq = q0 + jnp.arange(Q_BLOCK)
        idx = lax.dynamic_slice_in_dim(sel_idx, q0, Q_BLOCK, axis=2)
        kg = ks[b_ix, g_ix, idx]
        vg = vs[b_ix, g_ix, idx]
        kpos = idx[..., None] * SEL_LEN + jnp.arange(SEL_LEN)
        smask = kpos <= tq[None, None, :, None, None]
        ss = jnp.einsum('bqghd,bgqkld->bghqkl', qt, kg).astype(f32)
        ss = jnp.where(smask[:, :, None], ss, MASK_NEG)
        ps = jax.nn.softmax(ss.reshape(b, g_, hg, Q_BLOCK, -1), axis=-1).reshape(ss.shape).astype(dt)
        o_s = jnp.einsum('bghqkl,bgqkld->bqghd', ps, vg)
        kwt = lax.dynamic_slice_in_dim(kw, q0, WINDOW + Q_BLOCK, axis=1)
        vwt = lax.dynamic_slice_in_dim(vw, q0, WINDOW + Q_BLOCK, axis=1)
        spos = q0 - WINDOW + jnp.arange(WINDOW + Q_BLOCK)
        dist = tq[:, None] - spos[None, :]
        wmask = (dist >= 0) & (dist < WINDOW) & (spos[None, :] >= 0)
        sw = jnp.einsum('bqghd,bsgd->bghqs', qt, kwt).astype(f32)
        pw = jax.nn.softmax(jnp.where(wmask, sw, MASK_NEG), axis=-1).astype(dt)
        o_w = jnp.einsum('bghqs,bsgd->bqghd', pw, vwt)
        return o_s, o_w

    o_slc, o_win = lax.map(query_block, jnp.arange(s // Q_BLOCK))
    o_slc = jnp.moveaxis(o_slc, 0, 1).reshape(b, s, g_, hg, dh)
    o_win = jnp.moveaxis(o_win, 0, 1).reshape(b, s, g_, hg, dh)
    gates = jax.nn.sigmoid(gate_logits.astype(f32)).reshape(b, s, g_, hg, 3).astype(dt)
    o = gates[..., 0:1] * o_cmp + gates[..., 1:2] * o_slc + gates[..., 2:3] * o_win
    return o.reshape(b, s, NSA_W)


def stick_breaking_attention(q, k, v):
    b, s, _ = q.shape
    f32 = jnp.float32
    qh = (q * (SB_DH ** -0.5)).reshape(b, s, SB_HEADS, SB_DH)
    kh = k.reshape(b, s, SB_HEADS, SB_DH)
    vh = v.reshape(b, s, SB_HEADS, SB_DH)
    spos = jnp.arange(s)

    def query_block(qb):
        q0 = qb * Q_BLOCK
        qt = lax.dynamic_slice_in_dim(qh, q0, Q_BLOCK, axis=1)
        tq = q0 + jnp.arange(Q_BLOCK)
        z = jnp.einsum('bqhd,bshd->bhqs', qt, kh).astype(f32)
        causal = spos[None, :] < tq[:, None]
        cum = jnp.cumsum(jnp.where(causal, jax.nn.log_sigmoid(-z), 0.0), axis=-1)
        log_a = jax.nn.log_sigmoid(z) + cum[..., -1:] - cum
        a = jnp.where(causal, jnp.exp(log_a), 0.0).astype(v.dtype)
        return jnp.einsum('bhqs,bshd->bqhd', a, vh)

    o = lax.map(query_block, jnp.arange(s // Q_BLOCK))
    return jnp.moveaxis(o, 0, 1).reshape(b, s, SB_W)


def hybrid_mixer(h, w_in, gdn_conv_w, gdn_a_log, gdn_dt_bias, gdn_norm_g,
                 nsa_cmp_wk, nsa_cmp_pk, nsa_cmp_wv, nsa_cmp_pv,
                 w_br_gdn, w_br_nsa, w_br_sb, w_out):
    b, s, _ = h.shape
    split_points = [int(p) for p in np.cumsum(IN_SPLITS)[:-1]]
    (gq, gk, gv, gz, ga, gb, nq, nkv, ngate, sq, sk, sv, merge_logits) = jnp.split(h @ w_in, split_points, axis=-1)
    y_a = gated_deltanet(gq, gk, gv, gz, ga, gb, gdn_conv_w, gdn_a_log, gdn_dt_bias, gdn_norm_g)
    y_b = native_sparse_attention(nq, nkv, ngate, nsa_cmp_wk, nsa_cmp_pk, nsa_cmp_wv, nsa_cmp_pv)
    y_c = stick_breaking_attention(sq, sk, sv)
    gate = jax.nn.sigmoid(merge_logits.astype(jnp.float32)).astype(h.dtype).reshape(b, s, N_BRANCH, D_MODEL)
    merged = (gate[:, :, 0] * (y_a @ w_br_gdn) + gate[:, :, 1] * (y_b @ w_br_nsa)
              + gate[:, :, 2] * (y_c @ w_br_sb))
    return merged @ w_out


def squared_relu_mlp(h, w1, w2):
    return jnp.square(jax.nn.relu(h @ w1)) @ w2


def setup_inputs(seed: int = 0) -> dict:
    key = jax.random.key(seed)
    k = jax.random.split(key, 24)
    f32 = jnp.float32

    def dense(kk, shape, fan_in):
        return jax.random.normal(kk, shape, f32) * (fan_in ** -0.5)

    def gain(kk, n):
        return 1.0 + 0.02 * jax.random.normal(kk, (DEPTH, n), f32)

    dt = jnp.exp(jax.random.uniform(k[7], (DEPTH, GDN_HEADS), f32, -6.9077553, -2.3025851))
    return {
        'x': jax.random.normal(k[0], (BATCH, SEQ, D_MODEL), f32),
        'g_mix_pre': gain(k[1], D_MODEL),
        'g_mix_post': gain(k[2], D_MODEL),
        'g_ff_pre': gain(k[3], D_MODEL),
        'g_ff_post': gain(k[4], D_MODEL),
        'w_in': dense(k[5], (DEPTH, D_MODEL, N_IN), D_MODEL),
        'gdn_conv_w': dense(k[6], (DEPTH, GDN_CONV, 2 * GDN_QK_W + GDN_V_W), GDN_CONV),
        'gdn_a_log': jnp.log(jax.random.uniform(k[8], (DEPTH, GDN_HEADS), f32, 1.0, 16.0)),
        'gdn_dt_bias': dt + jnp.log(-jnp.expm1(-dt)),
        'gdn_norm_g': gain(k[9], GDN_DV),
        'nsa_cmp_wk': dense(k[10], (DEPTH, CMP_LEN, NSA_DH, NSA_DH), CMP_LEN * NSA_DH),
        'nsa_cmp_pk': 0.1 * jax.random.normal(k[11], (DEPTH, CMP_LEN, NSA_DH), f32),
        'nsa_cmp_wv': dense(k[12], (DEPTH, CMP_LEN, NSA_DH, NSA_DH), CMP_LEN * NSA_DH),
        'nsa_cmp_pv': 0.1 * jax.random.normal(k[13], (DEPTH, CMP_LEN, NSA_DH), f32),
        'w_br_gdn': dense(k[14], (DEPTH, GDN_V_W, D_MODEL), GDN_V_W),
        'w_br_nsa': dense(k[15], (DEPTH, NSA_W, D_MODEL), NSA_W),
        'w_br_sb': dense(k[16], (DEPTH, SB_W, D_MODEL), SB_W),
        'w_out': dense(k[17], (DEPTH, D_MODEL, D_MODEL), D_MODEL),
        'w_ff1': dense(k[18], (DEPTH, D_MODEL, D_FF), D_MODEL),
        'w_ff2': dense(k[19], (DEPTH, D_FF, D_MODEL), D_FF),
    }


def reference(x, g_mix_pre, g_mix_post, g_ff_pre, g_ff_post, w_in, gdn_conv_w, gdn_a_log,
              gdn_dt_bias, gdn_norm_g, nsa_cmp_wk, nsa_cmp_pk, nsa_cmp_wv, nsa_cmp_pv,
              w_br_gdn, w_br_nsa, w_br_sb, w_out, w_ff1, w_ff2):
    for l in range(DEPTH):
        h = rms_norm(x, g_mix_pre[l])
        y = hybrid_mixer(h, w_in[l], gdn_conv_w[l], gdn_a_log[l], gdn_dt_bias[l], gdn_norm_g[l],
                         nsa_cmp_wk[l], nsa_cmp_pk[l], nsa_cmp_wv[l], nsa_cmp_pv[l],
                         w_br_gdn[l], w_br_nsa[l], w_br_sb[l], w_out[l])
        x = x + rms_norm(y, g_mix_post[l])
        h = rms_norm(x, g_ff_pre[l])
        x = x + rms_norm(squared_relu_mlp(h, w_ff1[l], w_ff2[l]), g_ff_post[l])
    return x
```

```python
import functools

import jax
import jax.numpy as jnp
from jax import lax
from jax.experimental import pallas as pl
from jax.experimental.pallas import tpu as pltpu

F32 = jnp.float32
BF16 = jnp.bfloat16
HIGHEST = lax.Precision.HIGHEST

EPS = 1e-6
MASK_NEG = -1e30

GDN_HEADS = 4
GDN_D = 128
GDN_CONV = 4
GDN_CHUNK = 64

NSA_HEADS = 8
NSA_KV_HEADS = 2
NSA_GROUP = NSA_HEADS // NSA_KV_HEADS
NSA_DH = 64
CMP_LEN = 32
CMP_STRIDE = 16
SEL_LEN = 64
SEL_TOPK = 8
WINDOW = 512
FORCE_BONUS = 1e3

SB_HEADS = 8
SB_DH = 64

LANE = 128

COL_GQ, COL_GK, COL_GV, COL_GZ = 0, 4, 8, 12
COL_NQ, COL_NKV, COL_GAB, COL_NGATE = 16, 20, 26, 27
COL_SQ, COL_SK, COL_SV = 28, 32, 36
COL_MERGE = 40
N_PROJ = 64 * LANE

VMEM_LIMIT = 48 * 1024 * 1024


def _cparams(n_axes):
    return pltpu.CompilerParams(dimension_semantics=("arbitrary",) * n_axes,
                                vmem_limit_bytes=VMEM_LIMIT)


def _nt_dot(a, b, precision=None):
    return lax.dot_general(a, b, (((1,), (1,)), ((), ())), precision=precision,
                           preferred_element_type=F32)


def _tn_dot(a, b):
    return lax.dot_general(a, b, (((0,), (0,)), ((), ())), preferred_element_type=F32)


def _dot(a, b, precision=None):
    return jnp.dot(a, b, precision=precision, preferred_element_type=F32)


def _rms_scale(y):
    return y * lax.rsqrt(jnp.mean(y * y, axis=-1, keepdims=True) + EPS)


def _softplus(z):
    return jnp.maximum(z, 0.0) + jnp.log1p(jnp.exp(-jnp.abs(z)))


def _sigmoid(z):
    return jax.nn.sigmoid(z)


def _norm_matmul_kernel(x_ref, g_ref, w_ref, o_ref, h_scr, *, relu2):
    @pl.when(pl.program_id(1) == 0)
    def _():
        h_scr[...] = (_rms_scale(x_ref[...]) * g_ref[...]).astype(BF16)

    y = _dot(h_scr[...], w_ref[...])
    if relu2:
        y = jnp.square(jnp.maximum(y, 0.0))
    o_ref[...] = y.astype(o_ref.dtype)


def _norm_matmul(x, g, w, *, relu2, out_dtype, tm, tn):
    t, d = x.shape
    n = w.shape[1]
    return pl.pallas_call(
        functools.partial(_norm_matmul_kernel, relu2=relu2),
        grid=(t // tm, n // tn),
        in_specs=[pl.BlockSpec((tm, d), lambda i, j: (i, 0)),
                  pl.BlockSpec((1, d), lambda i, j: (0, 0)),
                  pl.BlockSpec((d, tn), lambda i, j: (0, j))],
        out_specs=pl.BlockSpec((tm, tn), lambda i, j: (i, j)),
        out_shape=jax.ShapeDtypeStruct((t, n), out_dtype),
        scratch_shapes=[pltpu.VMEM((tm, d), BF16)],
        compiler_params=_cparams(2),
        name="norm_matmul_relu2" if relu2 else "norm_matmul",
    )(x, g.reshape(1, d), w)


def _matmul_norm_res_kernel(a_ref, w_ref, x_ref, g_ref, o_ref):
    y = _dot(a_ref[...], w_ref[...])
    o_ref[...] = x_ref[...] + _rms_scale(y) * g_ref[...]


def _matmul_norm_res(a, w, x, g, *, tm):
    t, k = a.shape
    d = w.shape[1]
    return pl.pallas_call(
        _matmul_norm_res_kernel,
        grid=(t // tm,),
        in_specs=[pl.BlockSpec((tm, k), lambda i: (i, 0)),
                  pl.BlockSpec((k, d), lambda i: (0, 0)),
                  pl.BlockSpec((tm, d), lambda i: (i, 0)),
                  pl.BlockSpec((1, d), lambda i: (0, 0))],
        out_specs=pl.BlockSpec((tm, d), lambda i: (i, 0)),
        out_shape=jax.ShapeDtypeStruct((t, d), F32),
        compiler_params=_cparams(1),
        name="matmul_norm_res",
    )(a, w, x, g.reshape(1, d))


def _merge_out_kernel(ya_ref, yb_ref, yc_ref, m0_ref, m1_ref, m2_ref, x_ref,
                      wa_ref, wb_ref, wc_ref, wo_ref, g_ref, o_ref):
    merged = (_sigmoid(m0_ref[...]) * _dot(ya_ref[...], wa_ref[...])
              + _sigmoid(m1_ref[...]) * _dot(yb_ref[...], wb_ref[...])
              + _sigmoid(m2_ref[...]) * _dot(yc_ref[...], wc_ref[...]))
    y = _dot(merged.astype(BF16), wo_ref[...])
    o_ref[...] = x_ref[...] + _rms_scale(y) * g_ref[...]


def _merge_out(ya, yb, yc, proj, x, wa, wb, wc, wo, g, *, tm):
    t, d = x.shape
    w_in = ya.shape[1]
    mcol = COL_MERGE * LANE // d

    def row(i):
        return (i, 0)

    def const(i):
        return (0, 0)

    return pl.pallas_call(
        _merge_out_kernel,
        grid=(t // tm,),
        in_specs=[pl.BlockSpec((tm, w_in), row),
                  pl.BlockSpec((tm, w_in), row),
                  pl.BlockSpec((tm, w_in), row),
                  pl.BlockSpec((tm, d), lambda i: (i, mcol)),
                  pl.BlockSpec((tm, d), lambda i: (i, mcol + 1)),
                  pl.BlockSpec((tm, d), lambda i: (i, mcol + 2)),
                  pl.BlockSpec((tm, d), row),
                  pl.BlockSpec((w_in, d), const),
                  pl.BlockSpec((w_in, d), const),
                  pl.BlockSpec((w_in, d), const),
                  pl.BlockSpec((d, d), const),
                  pl.BlockSpec((1, d), const)],
        out_specs=pl.BlockSpec((tm, d), row),
        out_shape=jax.ShapeDtypeStruct((t, d), F32),
        compiler_params=_cparams(1),
        name="merge_out",
    )(ya, yb, yc, proj, proj, proj, x, wa, wb, wc, wo, g.reshape(1, d))


def _sb_kernel(q_ref, k_ref, v_ref, o_ref, *, tq):
    qi = pl.program_id(2)
    tk = tq
    dh = SB_DH
    row = lax.broadcasted_iota(jnp.int32, (tq, tk), 0)
    col = lax.broadcasted_iota(jnp.int32, (tq, tk), 1)
    later = jnp.where(row > col, 1.0, 0.0).astype(BF16)
    outs = []
    for hh in range(LANE // dh):
        lo = hh * dh
        q = (q_ref[:, lo:lo + dh] * (dh ** -0.5)).astype(BF16)

        def body(it, carry, q=q, lo=lo):
            run, acc = carry
            j = qi - it
            k0 = pl.multiple_of(j * tk, tk)
            k = k_ref[pl.ds(k0, tk), lo:lo + dh].astype(BF16)
            v = v_ref[pl.ds(k0, tk), lo:lo + dh].astype(BF16)
            z = _nt_dot(q, k)
            sp = _softplus(z)
            causal = (j * tk + col) < (qi * tq + row)
            log_keep = jnp.where(causal, -sp, 0.0)
            hi = log_keep.astype(BF16)
            lo_part = (log_keep - hi.astype(F32)).astype(BF16)
            suffix = _dot(hi, later) + _dot(lo_part, later)
            a = jnp.where(causal, jnp.exp(z - sp + suffix + run), 0.0)
            acc = acc + _dot(a.astype(BF16), v)
            run = run + jnp.sum(log_keep, axis=-1, keepdims=True)
            return run, acc

        _, acc = lax.fori_loop(0, qi + 1, body,
                               (jnp.zeros((tq, 1), F32), jnp.zeros((tq, dh), F32)))
        outs.append(acc)
    o_ref[...] = jnp.concatenate(outs, axis=-1).astype(o_ref.dtype)


def _sb_attention(proj3, *, tq):
    b, s, _ = proj3.shape
    n_pair = SB_HEADS * SB_DH // LANE
    return pl.pallas_call(
        functools.partial(_sb_kernel, tq=tq),
        grid=(b, n_pair, s // tq),
        in_specs=[pl.BlockSpec((None, tq, LANE), lambda bi, p, i: (bi, i, COL_SQ + p)),
                  pl.BlockSpec((None, s, LANE), lambda bi, p, i: (bi, 0, COL_SK + p)),
                  pl.BlockSpec((None, s, LANE), lambda bi, p, i: (bi, 0, COL_SV + p))],
        out_specs=pl.BlockSpec((None, tq, LANE), lambda bi, p, i: (bi, i, p)),
        out_shape=jax.ShapeDtypeStruct((b, s, SB_HEADS * SB_DH), BF16),
        compiler_params=_cparams(3),
        name="sb_attention",
    )(proj3, proj3, proj3)


def _gdn_kernel(scal_ref, q_ref, k_ref, v_ref, z_ref, ab_ref, cwq_ref, cwk_ref, cwv_ref,
                ng_ref, o_ref, *, seq):
    h = pl.program_id(1)
    c_len = GDN_CHUNK
    d = GDN_D
    a_log = scal_ref[0, h]
    dt_bias = scal_ref[1, h]
    neg_rate = -jnp.exp(jnp.full((1, 1), a_log, F32))

    row = lax.broadcasted_iota(jnp.int32, (c_len, c_len), 0)
    col = lax.broadcasted_iota(jnp.int32, (c_len, c_len), 1)
    incl = row >= col
    strict = row > col
    tri = jnp.where(incl, 1.0, 0.0)
    eye = jnp.where(row == col, 1.0, 0.0)
    lane = lax.broadcasted_iota(jnp.int32, (c_len, LANE), 1)

    def conv_silu(x_ref, cw_ref, c, t0):
        cur = x_ref[pl.ds(t0, c_len), :]
        p0 = pl.multiple_of(jnp.maximum(t0 - 8, 0), 8)
        prev = x_ref[pl.ds(p0, 8), :] * jnp.where(c > 0, 1.0, 0.0)
        ext = jnp.concatenate([prev, cur], axis=0)
        y = jnp.zeros((c_len, d), F32)
        for i in range(GDN_CONV):
            off = 8 - (GDN_CONV - 1) + i
            y = y + cw_ref[i:i + 1, :] * ext[off:off + c_len]
        return y * _sigmoid(y)

    def chunk(c, state):
        t0 = pl.multiple_of(c * c_len, c_len)
        qc = conv_silu(q_ref, cwq_ref, c, t0)
        kc = conv_silu(k_ref, cwk_ref, c, t0)
        vc = conv_silu(v_ref, cwv_ref, c, t0)
        qn = qc * lax.rsqrt(jnp.sum(qc * qc, axis=-1, keepdims=True) + EPS) * (d ** -0.5)
        kn = kc * lax.rsqrt(jnp.sum(kc * kc, axis=-1, keepdims=True) + EPS)

        ab = ab_ref[pl.ds(t0, c_len), :]
        a_col = jnp.sum(jnp.where(lane == h, ab, 0.0), axis=-1, keepdims=True)
        b_col = jnp.sum(jnp.where(lane == GDN_HEADS + h, ab, 0.0), axis=-1, keepdims=True)
        beta = _sigmoid(b_col)
        g_raw = neg_rate * _softplus(a_col + dt_bias)

        g_cum = _dot(tri, jnp.broadcast_to(g_raw, (c_len, c_len)), HIGHEST)
        diff = g_cum - g_cum.T
        decay = jnp.where(incl, jnp.exp(jnp.where(incl, diff, 0.0)), 0.0)
        g_col = g_cum[:, 0:1]
        g_last = g_cum[c_len - 1:c_len, 0:1]
        exp_g = jnp.exp(g_col)

        k_beta = kn * beta
        kn_b = kn.astype(BF16)
        a_mat = jnp.where(strict, _nt_dot(k_beta.astype(BF16), kn_b) * decay, 0.0)
        x_pow = -a_mat
        t_inv = eye + x_pow
        n_sq = max(c_len - 1, 1).bit_length() - 1
        for _ in range(n_sq):
            x_pow = _dot(x_pow, x_pow, HIGHEST)
            t_inv = t_inv + _dot(t_inv, x_pow, HIGHEST)
        t_b = t_inv.astype(BF16)
        u = _dot(t_b, (vc * beta).astype(BF16))
        w = _dot(t_b, (k_beta * exp_g).astype(BF16))
        attn = _nt_dot(qn.astype(BF16), kn_b) * decay
        k_dec = kn * jnp.exp(g_last - g_col)

        state_b = state.astype(BF16)
        v_new = u - _dot(w.astype(BF16), state_b)
        v_new_b = v_new.astype(BF16)
        o = _dot((qn * exp_g).astype(BF16), state_b) + _dot(attn.astype(BF16), v_new_b)
        state = state * jnp.exp(g_last) + _tn_dot(k_dec.astype(BF16), v_new_b)

        zc = z_ref[pl.ds(t0, c_len), :]
        out = _rms_scale(o) * ng_ref[...] * (zc * _sigmoid(zc))
        o_ref[pl.ds(t0, c_len), :] = out.astype(o_ref.dtype)
        return state

    lax.fori_loop(0, seq // c_len, chunk, jnp.zeros((d, d), F32))


def _gated_deltanet(proj3, conv_w, a_log, dt_bias, norm_g):
    b, s, _ = proj3.shape
    scal = jnp.stack([a_log, dt_bias]).astype(F32)

    def seq_block(col0):
        return pl.BlockSpec((None, s, LANE), lambda bi, h: (bi, 0, col0 + h))

    def conv_block(col0):
        return pl.BlockSpec((GDN_CONV, LANE), lambda bi, h: (0, col0 + h))

    return pl.pallas_call(
        functools.partial(_gdn_kernel, seq=s),
        grid=(b, GDN_HEADS),
        in_specs=[pl.BlockSpec(memory_space=pltpu.SMEM),
                  seq_block(COL_GQ), seq_block(COL_GK), seq_block(COL_GV), seq_block(COL_GZ),
                  pl.BlockSpec((None, s, LANE), lambda bi, h: (bi, 0, COL_GAB)),
                  conv_block(COL_GQ), conv_block(COL_GK), conv_block(COL_GV),
                  pl.BlockSpec((1, GDN_D), lambda bi, h: (0, 0))],
        out_specs=pl.BlockSpec((None, s, LANE), lambda bi, h: (bi, 0, h)),
        out_shape=jax.ShapeDtypeStruct((b, s, GDN_HEADS * GDN_D), BF16),
        compiler_params=_cparams(2),
        name="gated_deltanet",
    )(scal, proj3, proj3, proj3, proj3, proj3, conv_w, conv_w, conv_w, norm_g.reshape(1, GDN_D))


def _nsa_kernel(q_ref, kcmp_ref, vcmp_ref, kslc_ref, vslc_ref, kwin_ref, vwin_ref, gate_ref,
                wk_ref, wv_ref, pk_ref, pv_ref, o_ref, kc_scr, vc_scr, *, seq, tq):
    qi = pl.program_id(1)
    tk = tq
    dh = NSA_DH
    hg = NSA_GROUP
    nb = seq // CMP_STRIDE
    n_sel = seq // SEL_LEN
    top_k = min(SEL_TOPK, n_sel)
    half = CMP_LEN // 2

    @pl.when(qi == 0)
    def _():
        for src, pos, w_ref, dst in ((kcmp_ref, pk_ref, wk_ref, kc_scr),
                                     (vcmp_ref, pv_ref, wv_ref, vc_scr)):
            lo_parts, hi_parts = [], []
            for l in range(half):
                rows = src[pl.ds(l, nb, stride=CMP_STRIDE), :]
                lo_parts.append(rows + pos[l:l + 1, :])
                hi_parts.append(rows + pos[half + l:half + l + 1, :])
            a_lo = _dot(jnp.concatenate(lo_parts, axis=1), w_ref[0], HIGHEST)
            a_hi = _dot(jnp.concatenate(hi_parts, axis=1), w_ref[1], HIGHEST)
            a_hi = jnp.concatenate([a_hi[1:], jnp.zeros((1, LANE), F32)], axis=0)
            dst[...] = a_lo + a_hi

    t_col = qi * tq + lax.broadcasted_iota(jnp.int32, (tq, 1), 0)
    t_col4 = jnp.concatenate([t_col] * hg, axis=0)
    kpos_l = lax.broadcasted_iota(jnp.int32, (tq, tk), 1)
    gate = gate_ref[...]

    j_sel = lax.broadcasted_iota(jnp.int32, (n_sel, tq), 0)
    t_row = qi * tq + lax.broadcasted_iota(jnp.int32, (n_sel, tq), 1)
    cur = lax.shift_right_logical(t_row, SEL_LEN.bit_length() - 1)
    sel_valid = j_sel <= cur
    forced = (j_sel == 0) | (j_sel == cur) | (j_sel == cur - 1)
    ov_j = lax.broadcasted_iota(jnp.int32, (n_sel, nb), 0) * SEL_LEN
    ov_n = lax.broadcasted_iota(jnp.int32, (n_sel, nb), 1) * CMP_STRIDE
    overlap = jnp.where((ov_n < ov_j + SEL_LEN) & (ov_n + CMP_LEN > ov_j), 1.0, 0.0)
    e_j = lax.broadcasted_iota(jnp.int32, (n_sel, tk), 0)
    e_c = lax.shift_right_logical(lax.broadcasted_iota(jnp.int32, (n_sel, tk), 1),
                                  SEL_LEN.bit_length() - 1)

    def online_softmax_step(carry, s, mask, v):
        m, l, acc = carry
        s = jnp.where(mask, s, MASK_NEG)
        m_new = jnp.maximum(m, jnp.max(s, axis=-1, keepdims=True))
        alpha = jnp.exp(m - m_new)
        p = jnp.exp(s - m_new)
        l = alpha * l + jnp.sum(p, axis=-1, keepdims=True)
        acc = alpha * acc + _dot(p.astype(BF16), v)
        return m_new, l, acc

    init = (jnp.full((hg * tq, 1), MASK_NEG, F32), jnp.zeros((hg * tq, 1), F32),
            jnp.zeros((hg * tq, dh), F32))

    head_out = []
    for g in range(NSA_KV_HEADS):
        glo = g * dh
        qg = jnp.concatenate(
            [q_ref[:, (g * hg + i) * dh:(g * hg + i + 1) * dh] for i in range(hg)],
            axis=0) * (dh ** -0.5)
        qg_b = qg.astype(BF16)

        kc = kc_scr[:, glo:glo + dh]
        vc = vc_scr[:, glo:glo + dh]
        s_cmp = _nt_dot(qg, kc, HIGHEST)
        n_idx = lax.broadcasted_iota(jnp.int32, (hg * tq, nb), 1)
        cmp_valid = (n_idx * CMP_STRIDE + CMP_LEN - 1 <= t_col4) & (n_idx < nb - 1)
        s_m = jnp.where(cmp_valid, s_cmp, MASK_NEG)
        e = jnp.where(cmp_valid, jnp.exp(s_m - jnp.max(s_m, axis=-1, keepdims=True)), 0.0)
        den = jnp.sum(e, axis=-1, keepdims=True)
        p_cmp = jnp.where(den > 0.0, e / jnp.where(den > 0.0, den, 1.0), 0.0)
        o_cmp = _dot(p_cmp.astype(BF16), vc.astype(BF16))

        p_sum = p_cmp[0:tq]
        for i in range(1, hg):
            p_sum = p_sum + p_cmp[i * tq:(i + 1) * tq]
        imp = _nt_dot(overlap, p_sum, HIGHEST)
        score = jnp.where(sel_valid, imp + jnp.where(forced, FORCE_BONUS, 0.0), MASK_NEG)
        beaten = jnp.zeros((n_sel, tq), F32)
        for jp in range(n_sel):
            other = score[jp:jp + 1, :]
            ge = jnp.where(other >= score, 1.0, 0.0)
            gt = jnp.where(other > score, 1.0, 0.0)
            beaten = beaten + jnp.where(j_sel > jp, ge, gt)
        sel = jnp.where(beaten < top_k, 1.0, 0.0).T.astype(BF16)

        def slc_body(it, carry, g=g, glo=glo, qg_b=qg_b, sel=sel):
            kb = qi - it
            k0 = pl.multiple_of(kb * tk, tk)
            k = kslc_ref[pl.ds(k0, tk), glo:glo + dh].astype(BF16)
            v = vslc_ref[pl.ds(k0, tk), glo:glo + dh].astype(BF16)
            expand = jnp.where(e_j == kb * (tk // SEL_LEN) + e_c, 1.0, 0.0).astype(BF16)
            picked = _dot(sel, expand)
            ok = jnp.where((picked > 0.5) & (k0 + kpos_l <= t_col), 1.0, 0.0)
            mask = jnp.concatenate([ok] * hg, axis=0) > 0.5
            return online_softmax_step(carry, _nt_dot(qg_b, k), mask, v)

        _, l_s, acc_s = lax.fori_loop(0, qi + 1, slc_body, init)
        o_slc = acc_s / l_s

        def win_body(it, carry, g=g, glo=glo, qg_b=qg_b):
            kb = qi - it
            k0 = pl.multiple_of(kb * tk, tk)
            k = kwin_ref[pl.ds(k0, tk), glo:glo + dh].astype(BF16)
            v = vwin_ref[pl.ds(k0, tk), glo:glo + dh].astype(BF16)
            kpos = k0 + kpos_l
            ok = jnp.where((kpos <= t_col) & (kpos > t_col - WINDOW), 1.0, 0.0)
            mask = jnp.concatenate([ok] * hg, axis=0) > 0.5
            return online_softmax_step(carry, _nt_dot(qg_b, k), mask, v)

        n_win = jnp.minimum(qi, WINDOW // tk) + 1
        _, l_w, acc_w = lax.fori_loop(0, n_win, win_body, init)
        o_win = acc_w / l_w

        for i in range(hg):
            hd = g * hg + i
            rows = slice(i * tq, (i + 1) * tq)
            gates = _sigmoid(gate[:, 3 * hd:3 * hd + 3])
            head_out.append(gates[:, 0:1] * o_cmp[rows] + gates[:, 1:2] * o_slc[rows]
                            + gates[:, 2:3] * o_win[rows])
    o_ref[...] = jnp.concatenate(head_out, axis=-1).astype(o_ref.dtype)


def _cmp_weight(w):
    half = CMP_LEN // 2
    w = w.reshape(2, half, NSA_DH, NSA_DH)
    z = jnp.zeros_like(w)
    big = jnp.stack([jnp.concatenate([w, z], axis=-1), jnp.concatenate([z, w], axis=-1)], axis=2)
    return big.reshape(2, half * NSA_KV_HEADS * NSA_DH, NSA_KV_HEADS * NSA_DH)


def _native_sparse_attention(proj3, cmp_wk, cmp_pk, cmp_wv, cmp_pv, *, tq):
    b, s, _ = proj3.shape
    nb = s // CMP_STRIDE
    wk = _cmp_weight(cmp_wk)
    wv = _cmp_weight(cmp_wv)
    pk = jnp.tile(cmp_pk, (1, NSA_KV_HEADS))
    pv = jnp.tile(cmp_pv, (1, NSA_KV_HEADS))
    qw = NSA_HEADS * NSA_DH

    def seq_block(col):
        return pl.BlockSpec((None, s, LANE), lambda bi, i: (bi, 0, col))

    def whole(a):
        return pl.BlockSpec(a.shape, lambda bi, i: (0,) * a.ndim)

    return pl.pallas_call(
        functools.partial(_nsa_kernel, seq=s, tq=tq),
        grid=(b, s // tq),
        in_specs=[pl.BlockSpec((None, tq, qw), lambda bi, i: (bi, i, COL_NQ * LANE // qw))]
        + [seq_block(COL_NKV + c) for c in range(6)]
        + [pl.BlockSpec((None, tq, LANE), lambda bi, i: (bi, i, COL_NGATE)),
           whole(wk), whole(wv), whole(pk), whole(pv)],
        out_specs=pl.BlockSpec((None, tq, qw), lambda bi, i: (bi, i, 0)),
        out_shape=jax.ShapeDtypeStruct((b, s, qw), BF16),
        scratch_shapes=[pltpu.VMEM((nb, LANE), F32), pltpu.VMEM((nb, LANE), F32)],
        compiler_params=_cparams(2),
        name="native_sparse_attention",
    )(proj3, proj3, proj3, proj3, proj3, proj3, proj3, proj3, wk, wv, pk, pv)


def _pad_cols(w, width):
    return jnp.pad(w, ((0, 0), (0, width - w.shape[1])))


def _proj_weight(w_in):
    gqk_v_z = 4 * GDN_HEADS * GDN_D
    n_ab = 2 * GDN_HEADS
    n_nsa = (NSA_HEADS + 6 * NSA_KV_HEADS) * NSA_DH
    n_gate = 3 * NSA_HEADS
    o_ab = gqk_v_z
    o_nsa = o_ab + n_ab
    o_gate = o_nsa + n_nsa
    o_rest = o_gate + n_gate
    w = jnp.concatenate([w_in[:, :o_ab], w_in[:, o_nsa:o_gate],
                         _pad_cols(w_in[:, o_ab:o_nsa], LANE),
                         _pad_cols(w_in[:, o_gate:o_rest], LANE),
                         w_in[:, o_rest:]], axis=1)
    assert w.shape[1] == N_PROJ, w.shape
    return w.astype(BF16)


def _layer(x, p, *, tiles):
    b, s, d = x.shape
    x2 = x.reshape(b * s, d)
    proj = _norm_matmul(x2, p["g_mix_pre"], _proj_weight(p["w_in"]), relu2=False, out_dtype=F32,
                        tm=tiles["tm"], tn=tiles["tn"])
    proj3 = proj.reshape(b, s, N_PROJ)
    ya = _gated_deltanet(proj3, p["gdn_conv_w"], p["gdn_a_log"], p["gdn_dt_bias"], p["gdn_norm_g"])
    yb = _native_sparse_attention(proj3, p["nsa_cmp_wk"], p["nsa_cmp_pk"], p["nsa_cmp_wv"],
                                  p["nsa_cmp_pv"], tq=tiles["tq"])
    yc = _sb_attention(proj3, tq=tiles["tq"])
    x2 = _merge_out(ya.reshape(b * s, -1), yb.reshape(b * s, -1), yc.reshape(b * s, -1), proj, x2,
                    p["w_br_gdn"].astype(BF16), p["w_br_nsa"].astype(BF16),
                    p["w_br_sb"].astype(BF16), p["w_out"].astype(BF16), p["g_mix_post"],
                    tm=tiles["tm_out"])
    act = _norm_matmul(x2, p["g_ff_pre"], p["w_ff1"].astype(BF16), relu2=True, out_dtype=BF16,
                       tm=tiles["tm"], tn=tiles["tn"])
    x2 = _matmul_norm_res(act, p["w_ff2"].astype(BF16), x2, p["g_ff_post"], tm=tiles["tm_out"])
    return x2.reshape(b, s, d)


def _tiles(b, s):
    t = b * s
    return {"tm": min(512, t), "tn": 1024, "tm_out": min(256, t), "tq": min(128, s)}


def kernel(x, g_mix_pre, g_mix_post, g_ff_pre, g_ff_post, w_in, gdn_conv_w, gdn_a_log, gdn_dt_bias,
           gdn_norm_g, nsa_cmp_wk, nsa_cmp_pk, nsa_cmp_wv, nsa_cmp_pv, w_br_gdn, w_br_nsa, w_br_sb,
           w_out, w_ff1, w_ff2):
    params = dict(g_mix_pre=g_mix_pre, g_mix_post=g_mix_post, g_ff_pre=g_ff_pre, g_ff_post=g_ff_post,
                  w_in=w_in, gdn_conv_w=gdn_conv_w, gdn_a_log=gdn_a_log, gdn_dt_bias=gdn_dt_bias,
                  gdn_norm_g=gdn_norm_g, nsa_cmp_wk=nsa_cmp_wk, nsa_cmp_pk=nsa_cmp_pk,
                  nsa_cmp_wv=nsa_cmp_wv, nsa_cmp_pv=nsa_cmp_pv, w_br_gdn=w_br_gdn,
                  w_br_nsa=w_br_nsa, w_br_sb=w_br_sb, w_out=w_out, w_ff1=w_ff1, w_ff2=w_ff2)
    tiles = _tiles(x.shape[0], x.shape[1])
    for layer in range(w_in.shape[0]):
        x = _layer(x, {k: v[layer] for k, v in params.items()}, tiles=tiles)
    return x
```

```python
import functools

import jax
import jax.numpy as jnp
from jax import lax
from jax.experimental import pallas as pl
from jax.experimental.pallas import tpu as pltpu

F32 = jnp.float32
BF16 = jnp.bfloat16
HIGHEST = lax.Precision.HIGHEST

EPS = 1e-6
MASK_NEG = -1e30

GDN_HEADS = 4
GDN_D = 128
GDN_CONV = 4
GDN_CHUNK = 64

NSA_HEADS = 8
NSA_KV_HEADS = 2
NSA_GROUP = NSA_HEADS // NSA_KV_HEADS
NSA_DH = 64
CMP_LEN = 32
CMP_STRIDE = 16
SEL_LEN = 64
SEL_TOPK = 8
WINDOW = 512
FORCE_BONUS = 1e3

SB_HEADS = 8
SB_DH = 64
SB_TK = 128

LANE = 128

COL_GQ, COL_GK, COL_GV, COL_GZ = 0, 4, 8, 12
COL_NQ, COL_NKV, COL_GAB, COL_NGATE = 16, 20, 26, 27
COL_SQ, COL_SK, COL_SV = 28, 32, 36
COL_MERGE = 40
N_PROJ = 64 * LANE

VMEM_LIMIT = 48 * 1024 * 1024


def _cparams(n_axes):
    return pltpu.CompilerParams(dimension_semantics=("arbitrary",) * n_axes,
                                vmem_limit_bytes=VMEM_LIMIT)


def _nt_dot(a, b, precision=None):
    return lax.dot_general(a, b, (((1,), (1,)), ((), ())), precision=precision,
                           preferred_element_type=F32)


def _tn_dot(a, b):
    return lax.dot_general(a, b, (((0,), (0,)), ((), ())), preferred_element_type=F32)


def _dot(a, b, precision=None):
    return jnp.dot(a, b, precision=precision, preferred_element_type=F32)


def _rms_scale(y):
    return y * lax.rsqrt(jnp.mean(y * y, axis=-1, keepdims=True) + EPS)


def _softplus(z):
    return jnp.maximum(z, 0.0) + jnp.log1p(jnp.exp(-jnp.abs(z)))


def _sigmoid(z):
    return jax.nn.sigmoid(z)


def _norm_matmul_kernel(x_ref, g_ref, w_ref, o_ref, h_scr, *, relu2):
    @pl.when(pl.program_id(1) == 0)
    def _():
        h_scr[...] = (_rms_scale(x_ref[...]) * g_ref[...]).astype(BF16)

    y = _dot(h_scr[...], w_ref[...])
    if relu2:
        y = jnp.square(jnp.maximum(y, 0.0))
    o_ref[...] = y.astype(o_ref.dtype)


def _norm_matmul(x, g, w, *, relu2, out_dtype, tm, tn):
    t, d = x.shape
    n = w.shape[1]
    return pl.pallas_call(
        functools.partial(_norm_matmul_kernel, relu2=relu2),
        grid=(t // tm, n // tn),
        in_specs=[pl.BlockSpec((tm, d), lambda i, j: (i, 0)),
                  pl.BlockSpec((1, d), lambda i, j: (0, 0)),
                  pl.BlockSpec((d, tn), lambda i, j: (0, j))],
        out_specs=pl.BlockSpec((tm, tn), lambda i, j: (i, j)),
        out_shape=jax.ShapeDtypeStruct((t, n), out_dtype),
        scratch_shapes=[pltpu.VMEM((tm, d), BF16)],
        compiler_params=_cparams(2),
        name="norm_matmul_relu2" if relu2 else "norm_matmul",
    )(x, g.reshape(1, d), w)


def _matmul_norm_res_kernel(a_ref, w_ref, x_ref, g_ref, o_ref):
    y = _dot(a_ref[...], w_ref[...])
    o_ref[...] = x_ref[...] + _rms_scale(y) * g_ref[...]


def _matmul_norm_res(a, w, x, g, *, tm):
    t, k = a.shape
    d = w.shape[1]
    return pl.pallas_call(
        _matmul_norm_res_kernel,
        grid=(t // tm,),
        in_specs=[pl.BlockSpec((tm, k), lambda i: (i, 0)),
                  pl.BlockSpec((k, d), lambda i: (0, 0)),
                  pl.BlockSpec((tm, d), lambda i: (i, 0)),
                  pl.BlockSpec((1, d), lambda i: (0, 0))],
        out_specs=pl.BlockSpec((tm, d), lambda i: (i, 0)),
        out_shape=jax.ShapeDtypeStruct((t, d), F32),
        compiler_params=_cparams(1),
        name="matmul_norm_res",
    )(a, w, x, g.reshape(1, d))


def _merge_out_kernel(ya_ref, yb_ref, yc_ref, m0_ref, m1_ref, m2_ref, x_ref,
                      wa_ref, wb_ref, wc_ref, wo_ref, g_ref, o_ref):
    merged = (_sigmoid(m0_ref[...]) * _dot(ya_ref[...], wa_ref[...])
              + _sigmoid(m1_ref[...]) * _dot(yb_ref[...], wb_ref[...])
              + _sigmoid(m2_ref[...]) * _dot(yc_ref[...], wc_ref[...]))
    y = _dot(merged.astype(BF16), wo_ref[...])
    o_ref[...] = x_ref[...] + _rms_scale(y) * g_ref[...]


def _merge_out(ya, yb, yc, proj, x, wa, wb, wc, wo, g, *, tm):
    t, d = x.shape
    w_in = ya.shape[1]
    mcol = COL_MERGE * LANE // d

    def row(i):
        return (i, 0)

    def const(i):
        return (0, 0)

    return pl.pallas_call(
        _merge_out_kernel,
        grid=(t // tm,),
        in_specs=[pl.BlockSpec((tm, w_in), row),
                  pl.BlockSpec((tm, w_in), row),
                  pl.BlockSpec((tm, w_in), row),
                  pl.BlockSpec((tm, d), lambda i: (i, mcol)),
                  pl.BlockSpec((tm, d), lambda i: (i, mcol + 1)),
                  pl.BlockSpec((tm, d), lambda i: (i, mcol + 2)),
                  pl.BlockSpec((tm, d), row),
                  pl.BlockSpec((w_in, d), const),
                  pl.BlockSpec((w_in, d), const),
                  pl.BlockSpec((w_in, d), const),
                  pl.BlockSpec((d, d), const),
                  pl.BlockSpec((1, d), const)],
        out_specs=pl.BlockSpec((tm, d), row),
        out_shape=jax.ShapeDtypeStruct((t, d), F32),
        compiler_params=_cparams(1),
        name="merge_out",
    )(ya, yb, yc, proj, proj, proj, x, wa, wb, wc, wo, g.reshape(1, d))


def _sb_kernel(q_ref, k_ref, v_ref, o_ref, run_scr, acc_scr, *, tq):
    qi = pl.program_id(1)
    tk = SB_TK
    n_diag = tq // tk
    dh = SB_DH
    n_pair = SB_HEADS * dh // LANE
    row = lax.broadcasted_iota(jnp.int32, (tq, tk), 0)
    col = lax.broadcasted_iota(jnp.int32, (tq, tk), 1)
    lane = lax.broadcasted_iota(jnp.int32, (tq, LANE), 1)
    first_head = lane < dh
    r2 = lax.broadcasted_iota(jnp.int32, (tk, 2 * tk), 0)
    c2 = lax.broadcasted_iota(jnp.int32, (tk, 2 * tk), 1)
    later_ones = jnp.where((r2 > c2) | (c2 >= tk), 1.0, 0.0).astype(BF16)

    q_heads = []
    for p in range(n_pair):
        qp = q_ref[:, p * LANE:(p + 1) * LANE] * (dh ** -0.5)
        q_heads.append(jnp.where(first_head, qp, 0.0).astype(BF16))
        q_heads.append(jnp.where(first_head, 0.0, qp).astype(BF16))

    run_scr[...] = jnp.zeros_like(run_scr)
    acc_scr[...] = jnp.zeros_like(acc_scr)

    def block(j, diag_offset):
        k0 = pl.multiple_of(j * tk, tk)
        heads = range(SB_HEADS)
        diagonal = diag_offset is not None
        if diagonal:
            causal = col + diag_offset < row
        z = [_nt_dot(q_heads[h],
                     k_ref[pl.ds(k0, tk), (h // 2) * LANE:(h // 2 + 1) * LANE].astype(BF16))
             for h in heads]
        log_sig, hi, lo = [], [], []
        for h in heads:
            sp = jnp.maximum(z[h], 0.0) + jnp.log(1.0 + jnp.exp(-jnp.abs(z[h])))
            log_keep = -sp
            if diagonal:
                log_keep = jnp.where(causal, log_keep, 0.0)
            log_sig.append(z[h] - sp)
            hi.append(log_keep.astype(BF16))
            lo.append((log_keep - hi[h].astype(F32)).astype(BF16))
        sums = [_dot(hi[h], later_ones) + _dot(lo[h], later_ones) for h in heads]
        a = []
        for h in heads:
            run = run_scr[h]
            a_h = jnp.exp(log_sig[h] + sums[h][:, :tk] + run)
            if diagonal:
                a_h = jnp.where(causal, a_h, 0.0)
            run_scr[h] = run + sums[h][:, tk:]
            a.append(a_h.astype(BF16))
        for p in range(n_pair):
            vp = v_ref[pl.ds(k0, tk), p * LANE:(p + 1) * LANE].astype(BF16)
            acc_scr[p] += jnp.where(first_head, _dot(a[2 * p], vp), _dot(a[2 * p + 1], vp))

    for d in reversed(range(n_diag)):
        block(qi * n_diag + d, d * tk)

    def body(it, carry):
        block(qi * n_diag - 1 - it, None)
        return carry

    lax.fori_loop(0, qi * n_diag, body, 0)
    o_ref[...] = jnp.concatenate([acc_scr[p] for p in range(n_pair)], axis=-1).astype(o_ref.dtype)


def _sb_attention(proj3, *, tq):
    b, s, _ = proj3.shape
    width = SB_HEADS * SB_DH
    n_pair = width // LANE
    return pl.pallas_call(
        functools.partial(_sb_kernel, tq=tq),
        grid=(b, s // tq),
        in_specs=[pl.BlockSpec((None, tq, width), lambda bi, i: (bi, i, COL_SQ * LANE // width)),
                  pl.BlockSpec((None, s, width), lambda bi, i: (bi, 0, COL_SK * LANE // width)),
                  pl.BlockSpec((None, s, width), lambda bi, i: (bi, 0, COL_SV * LANE // width))],
        out_specs=pl.BlockSpec((None, tq, width), lambda bi, i: (bi, i, 0)),
        out_shape=jax.ShapeDtypeStruct((b, s, width), BF16),
        scratch_shapes=[pltpu.VMEM((SB_HEADS, tq, LANE), F32), pltpu.VMEM((n_pair, tq, LANE), F32)],
        compiler_params=_cparams(2),
        name="sb_attention",
    )(proj3, proj3, proj3)


def _gdn_kernel(scal_ref, q_ref, k_ref, v_ref, z_ref, ab_ref, cwq_ref, cwk_ref, cwv_ref,
                ng_ref, o_ref, *, seq):
    h = pl.program_id(1)
    c_len = GDN_CHUNK
    d = GDN_D
    a_log = scal_ref[0, h]
    dt_bias = scal_ref[1, h]
    neg_rate = -jnp.exp(jnp.full((1, 1), a_log, F32))

    row = lax.broadcasted_iota(jnp.int32, (c_len, c_len), 0)
    col = lax.broadcasted_iota(jnp.int32, (c_len, c_len), 1)
    incl = row >= col
    strict = row > col
    tri = jnp.where(incl, 1.0, 0.0)
    eye = jnp.where(row == col, 1.0, 0.0)
    lane = lax.broadcasted_iota(jnp.int32, (c_len, LANE), 1)

    def conv_silu(x_ref, cw_ref, c, t0):
        cur = x_ref[pl.ds(t0, c_len), :]
        p0 = pl.multiple_of(jnp.maximum(t0 - 8, 0), 8)
        prev = x_ref[pl.ds(p0, 8), :] * jnp.where(c > 0, 1.0, 0.0)
        ext = jnp.concatenate([prev, cur], axis=0)
        y = jnp.zeros((c_len, d), F32)
        for i in range(GDN_CONV):
            off = 8 - (GDN_CONV - 1) + i
            y = y + cw_ref[i:i + 1, :] * ext[off:off + c_len]
        return y * _sigmoid(y)

    def chunk(c, state):
        t0 = pl.multiple_of(c * c_len, c_len)
        qc = conv_silu(q_ref, cwq_ref, c, t0)
        kc = conv_silu(k_ref, cwk_ref, c, t0)
        vc = conv_silu(v_ref, cwv_ref, c, t0)
        qn = qc * lax.rsqrt(jnp.sum(qc * qc, axis=-1, keepdims=True) + EPS) * (d ** -0.5)
        kn = kc * lax.rsqrt(jnp.sum(kc * kc, axis=-1, keepdims=True) + EPS)

        ab = ab_ref[pl.ds(t0, c_len), :]
        a_col = jnp.sum(jnp.where(lane == h, ab, 0.0), axis=-1, keepdims=True)
        b_col = jnp.sum(jnp.where(lane == GDN_HEADS + h, ab, 0.0), axis=-1, keepdims=True)
        beta = _sigmoid(b_col)
        g_raw = neg_rate * _softplus(a_col + dt_bias)

        g_cum = _dot(tri, jnp.broadcast_to(g_raw, (c_len, c_len)), HIGHEST)
        diff = g_cum - g_cum.T
        decay = jnp.where(incl, jnp.exp(jnp.where(incl, diff, 0.0)), 0.0)
        g_col = g_cum[:, 0:1]
        g_last = g_cum[c_len - 1:c_len, 0:1]
        exp_g = jnp.exp(g_col)

        k_beta = kn * beta
        kn_b = kn.astype(BF16)
        a_mat = jnp.where(strict, _nt_dot(k_beta.astype(BF16), kn_b) * decay, 0.0)
        x_pow = -a_mat
        t_inv = eye + x_pow
        n_sq = max(c_len - 1, 1).bit_length() - 1
        for _ in range(n_sq):
            x_pow = _dot(x_pow, x_pow, HIGHEST)
            t_inv = t_inv + _dot(t_inv, x_pow, HIGHEST)
        t_b = t_inv.astype(BF16)
        u = _dot(t_b, (vc * beta).astype(BF16))
        w = _dot(t_b, (k_beta * exp_g).astype(BF16))
        attn = _nt_dot(qn.astype(BF16), kn_b) * decay
        k_dec = kn * jnp.exp(g_last - g_col)

        state_b = state.astype(BF16)
        v_new = u - _dot(w.astype(BF16), state_b)
        v_new_b = v_new.astype(BF16)
        o = _dot((qn * exp_g).astype(BF16), state_b) + _dot(attn.astype(BF16), v_new_b)
        state = state * jnp.exp(g_last) + _tn_dot(k_dec.astype(BF16), v_new_b)

        zc = z_ref[pl.ds(t0, c_len), :]
        out = _rms_scale(o) * ng_ref[...] * (zc * _sigmoid(zc))
        o_ref[pl.ds(t0, c_len), :] = out.astype(o_ref.dtype)
        return state

    lax.fori_loop(0, seq // c_len, chunk, jnp.zeros((d, d), F32))


def _gated_deltanet(proj3, conv_w, a_log, dt_bias, norm_g):
    b, s, _ = proj3.shape
    scal = jnp.stack([a_log, dt_bias]).astype(F32)

    def seq_block(col0):
        return pl.BlockSpec((None, s, LANE), lambda bi, h: (bi, 0, col0 + h))

    def conv_block(col0):
        return pl.BlockSpec((GDN_CONV, LANE), lambda bi, h: (0, col0 + h))

    return pl.pallas_call(
        functools.partial(_gdn_kernel, seq=s),
        grid=(b, GDN_HEADS),
        in_specs=[pl.BlockSpec(memory_space=pltpu.SMEM),
                  seq_block(COL_GQ), seq_block(COL_GK), seq_block(COL_GV), seq_block(COL_GZ),
                  pl.BlockSpec((None, s, LANE), lambda bi, h: (bi, 0, COL_GAB)),
                  conv_block(COL_GQ), conv_block(COL_GK), conv_block(COL_GV),
                  pl.BlockSpec((1, GDN_D), lambda bi, h: (0, 0))],
        out_specs=pl.BlockSpec((None, s, LANE), lambda bi, h: (bi, 0, h)),
        out_shape=jax.ShapeDtypeStruct((b, s, GDN_HEADS * GDN_D), BF16),
        compiler_params=_cparams(2),
        name="gated_deltanet",
    )(scal, proj3, proj3, proj3, proj3, proj3, conv_w, conv_w, conv_w, norm_g.reshape(1, GDN_D))


def _nsa_kernel(q_ref, kcmp_ref, vcmp_ref, kslc_ref, vslc_ref, kwin_ref, vwin_ref, gate_ref,
                wk_ref, wv_ref, pk_ref, pv_ref, o_ref, kc_scr, vc_scr, *, seq, tq):
    qi = pl.program_id(1)
    tk = tq
    dh = NSA_DH
    hg = NSA_GROUP
    nb = seq // CMP_STRIDE
    n_sel = seq // SEL_LEN
    top_k = min(SEL_TOPK, n_sel)
    half = CMP_LEN // 2

    @pl.when(qi == 0)
    def _():
        for src, pos, w_ref, dst in ((kcmp_ref, pk_ref, wk_ref, kc_scr),
                                     (vcmp_ref, pv_ref, wv_ref, vc_scr)):
            lo_parts, hi_parts = [], []
            for l in range(half):
                rows = src[pl.ds(l, nb, stride=CMP_STRIDE), :]
                lo_parts.append(rows + pos[l:l + 1, :])
                hi_parts.append(rows + pos[half + l:half + l + 1, :])
            a_lo = _dot(jnp.concatenate(lo_parts, axis=1), w_ref[0], HIGHEST)
            a_hi = _dot(jnp.concatenate(hi_parts, axis=1), w_ref[1], HIGHEST)
            a_hi = jnp.concatenate([a_hi[1:], jnp.zeros((1, LANE), F32)], axis=0)
            dst[...] = a_lo + a_hi

    t_col = qi * tq + lax.broadcasted_iota(jnp.int32, (tq, 1), 0)
    t_col4 = jnp.concatenate([t_col] * hg, axis=0)
    kpos_l = lax.broadcasted_iota(jnp.int32, (tq, tk), 1)
    gate = gate_ref[...]

    j_sel = lax.broadcasted_iota(jnp.int32, (n_sel, tq), 0)
    t_row = qi * tq + lax.broadcasted_iota(jnp.int32, (n_sel, tq), 1)
    cur = lax.shift_right_logical(t_row, SEL_LEN.bit_length() - 1)
    sel_valid = j_sel <= cur
    forced = (j_sel == 0) | (j_sel == cur) | (j_sel == cur - 1)
    ov_j = lax.broadcasted_iota(jnp.int32, (n_sel, nb), 0) * SEL_LEN
    ov_n = lax.broadcasted_iota(jnp.int32, (n_sel, nb), 1) * CMP_STRIDE
    overlap = jnp.where((ov_n < ov_j + SEL_LEN) & (ov_n + CMP_LEN > ov_j), 1.0, 0.0)
    e_j = lax.broadcasted_iota(jnp.int32, (n_sel, tk), 0)
    e_c = lax.shift_right_logical(lax.broadcasted_iota(jnp.int32, (n_sel, tk), 1),
                                  SEL_LEN.bit_length() - 1)

    def online_softmax_step(carry, s, mask, v):
        m, l, acc = carry
        s = jnp.where(mask, s, MASK_NEG)
        m_new = jnp.maximum(m, jnp.max(s, axis=-1, keepdims=True))
        alpha = jnp.exp(m - m_new)
        p = jnp.exp(s - m_new)
        l = alpha * l + jnp.sum(p, axis=-1, keepdims=True)
        acc = alpha * acc + _dot(p.astype(BF16), v)
        return m_new, l, acc

    init = (jnp.full((hg * tq, 1), MASK_NEG, F32), jnp.zeros((hg * tq, 1), F32),
            jnp.zeros((hg * tq, dh), F32))

    head_out = []
    for g in range(NSA_KV_HEADS):
        glo = g * dh
        qg = jnp.concatenate(
            [q_ref[:, (g * hg + i) * dh:(g * hg + i + 1) * dh] for i in range(hg)],
            axis=0) * (dh ** -0.5)
        qg_b = qg.astype(BF16)

        kc = kc_scr[:, glo:glo + dh]
        vc = vc_scr[:, glo:glo + dh]
        s_cmp = _nt_dot(qg, kc, HIGHEST)
        n_idx = lax.broadcasted_iota(jnp.int32, (hg * tq, nb), 1)
        cmp_valid = (n_idx * CMP_STRIDE + CMP_LEN - 1 <= t_col4) & (n_idx < nb - 1)
        s_m = jnp.where(cmp_valid, s_cmp, MASK_NEG)
        e = jnp.where(cmp_valid, jnp.exp(s_m - jnp.max(s_m, axis=-1, keepdims=True)), 0.0)
        den = jnp.sum(e, axis=-1, keepdims=True)
        p_cmp = jnp.where(den > 0.0, e / jnp.where(den > 0.0, den, 1.0), 0.0)
        o_cmp = _dot(p_cmp.astype(BF16), vc.astype(BF16))

        p_sum = p_cmp[0:tq]
        for i in range(1, hg):
            p_sum = p_sum + p_cmp[i * tq:(i + 1) * tq]
        imp = _nt_dot(overlap, p_sum, HIGHEST)
        score = jnp.where(sel_valid, imp + jnp.where(forced, FORCE_BONUS, 0.0), MASK_NEG)
        beaten = jnp.zeros((n_sel, tq), F32)
        for jp in range(n_sel):
            other = score[jp:jp + 1, :]
            ge = jnp.where(other >= score, 1.0, 0.0)
            gt = jnp.where(other > score, 1.0, 0.0)
            beaten = beaten + jnp.where(j_sel > jp, ge, gt)
        sel = jnp.where(beaten < top_k, 1.0, 0.0).T.astype(BF16)

        def slc_body(it, carry, g=g, glo=glo, qg_b=qg_b, sel=sel):
            kb = qi - it
            k0 = pl.multiple_of(kb * tk, tk)
            k = kslc_ref[pl.ds(k0, tk), glo:glo + dh].astype(BF16)
            v = vslc_ref[pl.ds(k0, tk), glo:glo + dh].astype(BF16)
            expand = jnp.where(e_j == kb * (tk // SEL_LEN) + e_c, 1.0, 0.0).astype(BF16)
            picked = _dot(sel, expand)
            ok = jnp.where((picked > 0.5) & (k0 + kpos_l <= t_col), 1.0, 0.0)
            mask = jnp.concatenate([ok] * hg, axis=0) > 0.5
            return online_softmax_step(carry, _nt_dot(qg_b, k), mask, v)

        _, l_s, acc_s = lax.fori_loop(0, qi + 1, slc_body, init)
        o_slc = acc_s / l_s

        def win_body(it, carry, g=g, glo=glo, qg_b=qg_b):
            kb = qi - it
            k0 = pl.multiple_of(kb * tk, tk)
            k = kwin_ref[pl.ds(k0, tk), glo:glo + dh].astype(BF16)
            v = vwin_ref[pl.ds(k0, tk), glo:glo + dh].astype(BF16)
            kpos = k0 + kpos_l
            ok = jnp.where((kpos <= t_col) & (kpos > t_col - WINDOW), 1.0, 0.0)
            mask = jnp.concatenate([ok] * hg, axis=0) > 0.5
            return online_softmax_step(carry, _nt_dot(qg_b, k), mask, v)

        n_win = jnp.minimum(qi, WINDOW // tk) + 1
        _, l_w, acc_w = lax.fori_loop(0, n_win, win_body, init)
        o_win = acc_w / l_w

        for i in range(hg):
            hd = g * hg + i
            rows = slice(i * tq, (i + 1) * tq)
            gates = _sigmoid(gate[:, 3 * hd:3 * hd + 3])
            head_out.append(gates[:, 0:1] * o_cmp[rows] + gates[:, 1:2] * o_slc[rows]
                            + gates[:, 2:3] * o_win[rows])
    o_ref[...] = jnp.concatenate(head_out, axis=-1).astype(o_ref.dtype)


def _cmp_weight(w):
    half = CMP_LEN // 2
    w = w.reshape(2, half, NSA_DH, NSA_DH)
    z = jnp.zeros_like(w)
    big = jnp.stack([jnp.concatenate([w, z], axis=-1), jnp.concatenate([z, w], axis=-1)], axis=2)
    return big.reshape(2, half * NSA_KV_HEADS * NSA_DH, NSA_KV_HEADS * NSA_DH)


def _native_sparse_attention(proj3, cmp_wk, cmp_pk, cmp_wv, cmp_pv, *, tq):
    b, s, _ = proj3.shape
    nb = s // CMP_STRIDE
    wk = _cmp_weight(cmp_wk)
    wv = _cmp_weight(cmp_wv)
    pk = jnp.tile(cmp_pk, (1, NSA_KV_HEADS))
    pv = jnp.tile(cmp_pv, (1, NSA_KV_HEADS))
    qw = NSA_HEADS * NSA_DH

    def seq_block(col):
        return pl.BlockSpec((None, s, LANE), lambda bi, i: (bi, 0, col))

    def whole(a):
        return pl.BlockSpec(a.shape, lambda bi, i: (0,) * a.ndim)

    return pl.pallas_call(
        functools.partial(_nsa_kernel, seq=s, tq=tq),
        grid=(b, s // tq),
        in_specs=[pl.BlockSpec((None, tq, qw), lambda bi, i: (bi, i, COL_NQ * LANE // qw))]
        + [seq_block(COL_NKV + c) for c in range(6)]
        + [pl.BlockSpec((None, tq, LANE), lambda bi, i: (bi, i, COL_NGATE)),
           whole(wk), whole(wv), whole(pk), whole(pv)],
        out_specs=pl.BlockSpec((None, tq, qw), lambda bi, i: (bi, i, 0)),
        out_shape=jax.ShapeDtypeStruct((b, s, qw), BF16),
        scratch_shapes=[pltpu.VMEM((nb, LANE), F32), pltpu.VMEM((nb, LANE), F32)],
        compiler_params=_cparams(2),
        name="native_sparse_attention",
    )(proj3, proj3, proj3, proj3, proj3, proj3, proj3, proj3, wk, wv, pk, pv)


def _pad_cols(w, width):
    return jnp.pad(w, ((0, 0), (0, width - w.shape[1])))


def _proj_weight(w_in):
    gqk_v_z = 4 * GDN_HEADS * GDN_D
    n_ab = 2 * GDN_HEADS
    n_nsa = (NSA_HEADS + 6 * NSA_KV_HEADS) * NSA_DH
    n_gate = 3 * NSA_HEADS
    o_ab = gqk_v_z
    o_nsa = o_ab + n_ab
    o_gate = o_nsa + n_nsa
    o_rest = o_gate + n_gate
    w = jnp.concatenate([w_in[:, :o_ab], w_in[:, o_nsa:o_gate],
                         _pad_cols(w_in[:, o_ab:o_nsa], LANE),
                         _pad_cols(w_in[:, o_gate:o_rest], LANE),
                         w_in[:, o_rest:]], axis=1)
    assert w.shape[1] == N_PROJ, w.shape
    return w.astype(BF16)


def _layer(x, p, *, tiles):
    b, s, d = x.shape
    x2 = x.reshape(b * s, d)
    proj = _norm_matmul(x2, p["g_mix_pre"], _proj_weight(p["w_in"]), relu2=False, out_dtype=F32,
                        tm=tiles["tm"], tn=tiles["tn"])
    proj3 = proj.reshape(b, s, N_PROJ)
    ya = _gated_deltanet(proj3, p["gdn_conv_w"], p["gdn_a_log"], p["gdn_dt_bias"], p["gdn_norm_g"])
    yb = _native_sparse_attention(proj3, p["nsa_cmp_wk"], p["nsa_cmp_pk"], p["nsa_cmp_wv"],
                                  p["nsa_cmp_pv"], tq=tiles["tq"])
    yc = _sb_attention(proj3, tq=tiles["tq_sb"])
    x2 = _merge_out(ya.reshape(b * s, -1), yb.reshape(b * s, -1), yc.reshape(b * s, -1), proj, x2,
                    p["w_br_gdn"].astype(BF16), p["w_br_nsa"].astype(BF16),
                    p["w_br_sb"].astype(BF16), p["w_out"].astype(BF16), p["g_mix_post"],
                    tm=tiles["tm_out"])
    act = _norm_matmul(x2, p["g_ff_pre"], p["w_ff1"].astype(BF16), relu2=True, out_dtype=BF16,
                       tm=tiles["tm"], tn=tiles["tn"])
    x2 = _matmul_norm_res(act, p["w_ff2"].astype(BF16), x2, p["g_ff_post"], tm=tiles["tm_out"])
    return x2.reshape(b, s, d)


def _tiles(b, s):
    t = b * s
    return {"tm": min(512, t), "tn": 1024, "tm_out": min(256, t), "tq": min(128, s),
            "tq_sb": min(256, s)}


def kernel(x, g_mix_pre, g_mix_post, g_ff_pre, g_ff_post, w_in, gdn_conv_w, gdn_a_log, gdn_dt_bias,
           gdn_norm_g, nsa_cmp_wk, nsa_cmp_pk, nsa_cmp_wv, nsa_cmp_pv, w_br_gdn, w_br_nsa, w_br_sb,
           w_out, w_ff1, w_ff2):
    params = dict(g_mix_pre=g_mix_pre, g_mix_post=g_mix_post, g_ff_pre=g_ff_pre, g_ff_post=g_ff_post,
                  w_in=w_in, gdn_conv_w=gdn_conv_w, gdn_a_log=gdn_a_log, gdn_dt_bias=gdn_dt_bias,
                  gdn_norm_g=gdn_norm_g, nsa_cmp_wk=nsa_cmp_wk, nsa_cmp_pk=nsa_cmp_pk,
                  nsa_cmp_wv=nsa_cmp_wv, nsa_cmp_pv=nsa_cmp_pv, w_br_gdn=w_br_gdn,
                  w_br_nsa=w_br_nsa, w_br_sb=w_br_sb, w_out=w_out, w_ff1=w_ff1, w_ff2=w_ff2)
    tiles = _tiles(x.shape[0], x.shape[1])
    for layer in range(w_in.shape[0]):
        x = _layer(x, {k: v[layer] for k, v in params.items()}, tiles=tiles)
    return x
```

```python
import functools

import jax
import jax.numpy as jnp
from jax import lax
from jax.experimental import pallas as pl
from jax.experimental.pallas import tpu as pltpu

F32 = jnp.float32
BF16 = jnp.bfloat16
HIGHEST = lax.Precision.HIGHEST

EPS = 1e-6
MASK_NEG = -1e30

GDN_HEADS = 4
GDN_D = 128
GDN_CONV = 4
GDN_CHUNK = 64
GDN_PREP_CHUNKS = 4
GDN_SCAN_TOKENS = 512

NSA_HEADS = 8
NSA_KV_HEADS = 2
NSA_GROUP = NSA_HEADS // NSA_KV_HEADS
NSA_DH = 64
CMP_LEN = 32
CMP_STRIDE = 16
SEL_LEN = 64
SEL_TOPK = 8
WINDOW = 512
FORCE_BONUS = 1e3
NSA_TK = 128
NSA_SLC_BLOCKS = 4

SB_HEADS = 8
SB_DH = 64
SB_TK = 128

LANE = 128

COL_GQ, COL_GK, COL_GV, COL_GZ = 0, 4, 8, 12
COL_NQ, COL_NKV, COL_GAB, COL_NGATE = 16, 20, 26, 27
COL_SQ, COL_SK, COL_SV = 28, 32, 36
COL_MERGE = 40
N_PROJ = 64 * LANE

VMEM_LIMIT = 48 * 1024 * 1024


def _cparams(n_axes):
    return pltpu.CompilerParams(dimension_semantics=("arbitrary",) * n_axes,
                                vmem_limit_bytes=VMEM_LIMIT)


def _nt_dot(a, b, precision=None):
    return lax.dot_general(a, b, (((1,), (1,)), ((), ())), precision=precision,
                           preferred_element_type=F32)


def _tn_dot(a, b):
    return lax.dot_general(a, b, (((0,), (0,)), ((), ())), preferred_element_type=F32)


def _dot(a, b, precision=None):
    return jnp.dot(a, b, precision=precision, preferred_element_type=F32)


def _rms_scale(y):
    return y * lax.rsqrt(jnp.mean(y * y, axis=-1, keepdims=True) + EPS)


def _softplus(z):
    return jnp.maximum(z, 0.0) + jnp.log1p(jnp.exp(-jnp.abs(z)))


def _sigmoid(z):
    return jax.nn.sigmoid(z)


def _norm_matmul_kernel(x_ref, g_ref, w_ref, o_ref, h_scr, *, relu2):
    @pl.when(pl.program_id(1) == 0)
    def _():
        h_scr[...] = (_rms_scale(x_ref[...]) * g_ref[...]).astype(BF16)

    y = _dot(h_scr[...], w_ref[...])
    if relu2:
        y = jnp.square(jnp.maximum(y, 0.0))
    o_ref[...] = y.astype(o_ref.dtype)


def _norm_matmul(x, g, w, *, relu2, out_dtype, tm, tn):
    t, d = x.shape
    n = w.shape[1]
    return pl.pallas_call(
        functools.partial(_norm_matmul_kernel, relu2=relu2),
        grid=(t // tm, n // tn),
        in_specs=[pl.BlockSpec((tm, d), lambda i, j: (i, 0)),
                  pl.BlockSpec((1, d), lambda i, j: (0, 0)),
                  pl.BlockSpec((d, tn), lambda i, j: (0, j))],
        out_specs=pl.BlockSpec((tm, tn), lambda i, j: (i, j)),
        out_shape=jax.ShapeDtypeStruct((t, n), out_dtype),
        scratch_shapes=[pltpu.VMEM((tm, d), BF16)],
        compiler_params=_cparams(2),
        name="norm_matmul_relu2" if relu2 else "norm_matmul",
    )(x, g.reshape(1, d), w)


def _matmul_norm_res_kernel(a_ref, w_ref, x_ref, g_ref, o_ref):
    y = _dot(a_ref[...], w_ref[...])
    o_ref[...] = x_ref[...] + _rms_scale(y) * g_ref[...]


def _matmul_norm_res(a, w, x, g, *, tm):
    t, k = a.shape
    d = w.shape[1]
    return pl.pallas_call(
        _matmul_norm_res_kernel,
        grid=(t // tm,),
        in_specs=[pl.BlockSpec((tm, k), lambda i: (i, 0)),
                  pl.BlockSpec((k, d), lambda i: (0, 0)),
                  pl.BlockSpec((tm, d), lambda i: (i, 0)),
                  pl.BlockSpec((1, d), lambda i: (0, 0))],
        out_specs=pl.BlockSpec((tm, d), lambda i: (i, 0)),
        out_shape=jax.ShapeDtypeStruct((t, d), F32),
        compiler_params=_cparams(1),
        name="matmul_norm_res",
    )(a, w, x, g.reshape(1, d))


def _merge_out_kernel(ya_ref, yb_ref, yc_ref, m0_ref, m1_ref, m2_ref, x_ref,
                      wa_ref, wb_ref, wc_ref, wo_ref, g_ref, o_ref):
    merged = (_sigmoid(m0_ref[...]) * _dot(ya_ref[...], wa_ref[...])
              + _sigmoid(m1_ref[...]) * _dot(yb_ref[...], wb_ref[...])
              + _sigmoid(m2_ref[...]) * _dot(yc_ref[...], wc_ref[...]))
    y = _dot(merged.astype(BF16), wo_ref[...])
    o_ref[...] = x_ref[...] + _rms_scale(y) * g_ref[...]


def _merge_out(ya, yb, yc, proj, x, wa, wb, wc, wo, g, *, tm):
    t, d = x.shape
    w_in = ya.shape[1]
    mcol = COL_MERGE * LANE // d

    def row(i):
        return (i, 0)

    def const(i):
        return (0, 0)

    return pl.pallas_call(
        _merge_out_kernel,
        grid=(t // tm,),
        in_specs=[pl.BlockSpec((tm, w_in), row),
                  pl.BlockSpec((tm, w_in), row),
                  pl.BlockSpec((tm, w_in), row),
                  pl.BlockSpec((tm, d), lambda i: (i, mcol)),
                  pl.BlockSpec((tm, d), lambda i: (i, mcol + 1)),
                  pl.BlockSpec((tm, d), lambda i: (i, mcol + 2)),
                  pl.BlockSpec((tm, d), row),
                  pl.BlockSpec((w_in, d), const),
                  pl.BlockSpec((w_in, d), const),
                  pl.BlockSpec((w_in, d), const),
                  pl.BlockSpec((d, d), const),
                  pl.BlockSpec((1, d), const)],
        out_specs=pl.BlockSpec((tm, d), row),
        out_shape=jax.ShapeDtypeStruct((t, d), F32),
        compiler_params=_cparams(1),
        name="merge_out",
    )(ya, yb, yc, proj, proj, proj, x, wa, wb, wc, wo, g.reshape(1, d))


def _sb_kernel(q_ref, k_ref, v_ref, o_ref, run_scr, acc_scr, *, tq):
    qi = pl.program_id(1)
    tk = SB_TK
    n_diag = tq // tk
    dh = SB_DH
    n_pair = SB_HEADS * dh // LANE
    row = lax.broadcasted_iota(jnp.int32, (tq, tk), 0)
    col = lax.broadcasted_iota(jnp.int32, (tq, tk), 1)
    lane = lax.broadcasted_iota(jnp.int32, (tq, LANE), 1)
    first_head = lane < dh
    r2 = lax.broadcasted_iota(jnp.int32, (tk, 2 * tk), 0)
    c2 = lax.broadcasted_iota(jnp.int32, (tk, 2 * tk), 1)
    later_ones = jnp.where((r2 > c2) | (c2 >= tk), 1.0, 0.0).astype(BF16)

    q_heads = []
    for p in range(n_pair):
        qp = q_ref[:, p * LANE:(p + 1) * LANE] * (dh ** -0.5)
        q_heads.append(jnp.where(first_head, qp, 0.0).astype(BF16))
        q_heads.append(jnp.where(first_head, 0.0, qp).astype(BF16))

    run_scr[...] = jnp.zeros_like(run_scr)
    acc_scr[...] = jnp.zeros_like(acc_scr)

    def block(j, diag_offset):
        k0 = pl.multiple_of(j * tk, tk)
        heads = range(SB_HEADS)
        diagonal = diag_offset is not None
        if diagonal:
            causal = col + diag_offset < row
        z = [_nt_dot(q_heads[h],
                     k_ref[pl.ds(k0, tk), (h // 2) * LANE:(h // 2 + 1) * LANE].astype(BF16))
             for h in heads]
        log_sig, hi, lo = [], [], []
        for h in heads:
            sp = jnp.maximum(z[h], 0.0) + jnp.log(1.0 + jnp.exp(-jnp.abs(z[h])))
            log_keep = -sp
            if diagonal:
                log_keep = jnp.where(causal, log_keep, 0.0)
            log_sig.append(z[h] - sp)
            hi.append(log_keep.astype(BF16))
            lo.append((log_keep - hi[h].astype(F32)).astype(BF16))
        sums = [_dot(hi[h], later_ones) + _dot(lo[h], later_ones) for h in heads]
        a = []
        for h in heads:
            run = run_scr[h]
            a_h = jnp.exp(log_sig[h] + sums[h][:, :tk] + run)
            if diagonal:
                a_h = jnp.where(causal, a_h, 0.0)
            run_scr[h] = run + sums[h][:, tk:]
            a.append(a_h.astype(BF16))
        for p in range(n_pair):
            vp = v_ref[pl.ds(k0, tk), p * LANE:(p + 1) * LANE].astype(BF16)
            acc_scr[p] += jnp.where(first_head, _dot(a[2 * p], vp), _dot(a[2 * p + 1], vp))

    for d in reversed(range(n_diag)):
        block(qi * n_diag + d, d * tk)

    def body(it, carry):
        block(qi * n_diag - 1 - it, None)
        return carry

    lax.fori_loop(0, qi * n_diag, body, 0)
    o_ref[...] = jnp.concatenate([acc_scr[p] for p in range(n_pair)], axis=-1).astype(o_ref.dtype)


def _sb_attention(proj3, *, tq):
    b, s, _ = proj3.shape
    width = SB_HEADS * SB_DH
    n_pair = width // LANE
    return pl.pallas_call(
        functools.partial(_sb_kernel, tq=tq),
        grid=(b, s // tq),
        in_specs=[pl.BlockSpec((None, tq, width), lambda bi, i: (bi, i, COL_SQ * LANE // width)),
                  pl.BlockSpec((None, s, width), lambda bi, i: (bi, 0, COL_SK * LANE // width)),
                  pl.BlockSpec((None, s, width), lambda bi, i: (bi, 0, COL_SV * LANE // width))],
        out_specs=pl.BlockSpec((None, tq, width), lambda bi, i: (bi, i, 0)),
        out_shape=jax.ShapeDtypeStruct((b, s, width), BF16),
        scratch_shapes=[pltpu.VMEM((SB_HEADS, tq, LANE), F32), pltpu.VMEM((n_pair, tq, LANE), F32)],
        compiler_params=_cparams(2),
        name="sb_attention",
    )(proj3, proj3, proj3)


def _split_bf16(x):
    hi = x.astype(BF16)
    return hi, (x - hi.astype(F32)).astype(BF16)


def _gdn_prep_kernel(scal_ref, q_ref, qh_ref, k_ref, kh_ref, v_ref, vh_ref, ab_ref, cw_ref,
                     u_ref, w_ref, qe_ref, kd_ref, attn_ref, egl_ref, *, cb):
    i = pl.program_id(1)
    c_len = GDN_CHUNK
    d = GDN_D
    n_h = GDN_HEADS
    ts = cb * c_len
    halo = qh_ref.shape[0]
    width = n_h * d

    def conv_silu(x_ref, halo_ref, w):
        ext = jnp.concatenate([halo_ref[...] * jnp.where(i > 0, 1.0, 0.0), x_ref[...]], axis=0)
        y = jnp.zeros((ts, width), F32)
        for tap in range(GDN_CONV):
            off = halo - (GDN_CONV - 1) + tap
            y = y + w[tap:tap + 1, :] * ext[off:off + ts]
        return y * _sigmoid(y)

    qc = conv_silu(q_ref, qh_ref, cw_ref[:, 0:width])
    kc = conv_silu(k_ref, kh_ref, cw_ref[:, width:2 * width])
    vc = conv_silu(v_ref, vh_ref, cw_ref[:, 2 * width:3 * width])

    ab = ab_ref[...]
    lane = lax.broadcasted_iota(jnp.int32, (1, LANE), 1)
    a_log = jnp.zeros((1, LANE), F32)
    dt_bias = jnp.zeros((1, LANE), F32)
    for h in range(n_h):
        a_log = jnp.where(lane == h, scal_ref[0, h], a_log)
        dt_bias = jnp.where(lane == h, scal_ref[1, h], dt_bias)
    g_cum = -jnp.exp(a_log) * _softplus(ab + dt_bias)
    row_in_chunk = lax.broadcasted_iota(jnp.int32, (ts, LANE), 0) & (c_len - 1)
    shift = 1
    while shift < c_len:
        g_cum = g_cum + jnp.where(row_in_chunk >= shift, pltpu.roll(g_cum, shift, axis=0), 0.0)
        shift *= 2
    beta_all = _sigmoid(ab)

    row = lax.broadcasted_iota(jnp.int32, (c_len, c_len), 0)
    col = lax.broadcasted_iota(jnp.int32, (c_len, c_len), 1)
    incl = row >= col
    strict = row > col
    eye = jnp.where(row == col, 1.0, 0.0)

    problems = [(c, h) for c in range(cb) for h in range(n_h)]
    qn, kn, k_beta, v_beta, g_col, decay = {}, {}, {}, {}, {}, {}
    for c, h in problems:
        rows = slice(c * c_len, (c + 1) * c_len)
        cols = slice(h * d, (h + 1) * d)
        qh, kh = qc[rows, cols], kc[rows, cols]
        qn[c, h] = qh * lax.rsqrt(jnp.sum(qh * qh, axis=-1, keepdims=True) + EPS) * (d ** -0.5)
        kn[c, h] = kh * lax.rsqrt(jnp.sum(kh * kh, axis=-1, keepdims=True) + EPS)
        beta = beta_all[rows, n_h + h:n_h + h + 1]
        k_beta[c, h] = kn[c, h] * beta
        v_beta[c, h] = (vc[rows, cols] * beta).astype(BF16)
        g_col[c, h] = g_cum[rows, h:h + 1]
        g_sq = jnp.broadcast_to(g_col[c, h], (c_len, c_len))
        decay[c, h] = jnp.where(incl, jnp.exp(jnp.where(incl, g_sq - g_sq.T, 0.0)), 0.0)

    kn_b = {p: kn[p].astype(BF16) for p in problems}
    kkt = {p: _nt_dot(k_beta[p].astype(BF16), kn_b[p]) for p in problems}
    qkt = {p: _nt_dot(qn[p].astype(BF16), kn_b[p]) for p in problems}

    a_mat = {p: jnp.where(strict, kkt[p] * decay[p], 0.0) for p in problems}
    left = lax.broadcasted_iota(jnp.int32, (c_len, 2 * c_len), 1) < c_len
    pair = {p: jnp.concatenate([-a_mat[p], eye], axis=1) for p in problems}
    for _ in range(max(c_len - 1, 1).bit_length()):
        y = {p: _dot(pair[p][:, :c_len].astype(BF16), pair[p].astype(BF16)) for p in problems}
        pair = {p: jnp.where(left, y[p], pair[p] + y[p]) for p in problems}
    t_inv = {p: pair[p][:, c_len:] for p in problems}
    resid = {}
    for p in problems:
        m_hi, m_lo = _split_bf16(eye + a_mat[p])
        t_hi, t_lo = _split_bf16(t_inv[p])
        resid[p] = (eye - (_dot(m_hi, t_hi) + _dot(m_lo, t_hi) + _dot(m_hi, t_lo)), t_hi)
    t_b = {p: (t_inv[p] + _dot(resid[p][1], resid[p][0].astype(BF16))).astype(BF16)
           for p in problems}

    for c, h in problems:
        p = (c, h)
        rows = slice(c * c_len, (c + 1) * c_len)
        cols = slice(h * d, (h + 1) * d)
        exp_g = jnp.exp(g_col[p])
        g_last = g_col[p][c_len - 1:c_len, :]
        u_ref[rows, cols] = _dot(t_b[p], v_beta[p])
        w_ref[rows, cols] = _dot(t_b[p], (k_beta[p] * exp_g).astype(BF16)).astype(BF16)
        qe_ref[rows, cols] = (qn[p] * exp_g).astype(BF16)
        kd_ref[rows, cols] = (kn[p] * jnp.exp(g_last - g_col[p])).astype(BF16)
    egl_rows = [jnp.concatenate([jnp.broadcast_to(jnp.exp(g_col[c, h][c_len - 1:c_len, :]), (1, d))
                                 for h in range(n_h)], axis=1) for c in range(cb)]
    if egl_ref.shape[0] > cb:
        egl_rows.append(jnp.zeros((egl_ref.shape[0] - cb, width), F32))
    egl_ref[...] = jnp.concatenate(egl_rows, axis=0)
    for c in range(cb):
        rows = slice(c * c_len, (c + 1) * c_len)
        attn_ref[rows, :] = jnp.concatenate(
            [(qkt[c, h] * decay[c, h]).astype(BF16) for h in range(n_h)], axis=1)


def _gdn_scan_kernel(u_ref, w_ref, qe_ref, kd_ref, attn_ref, egl_ref, z_ref, ng_ref, o_ref,
                     state_scr, *, bb, ts, cb):
    j = pl.program_id(1)
    c_len = GDN_CHUNK
    d = GDN_D
    n_c = ts // c_len
    chains = [(b_, h) for b_ in range(bb) for h in range(GDN_HEADS)]

    @pl.when(j == 0)
    def _():
        state_scr[...] = jnp.zeros_like(state_scr)

    def chunk(c, carry):
        r0 = pl.multiple_of(c * c_len, c_len)
        rows = pl.ds(r0, c_len)
        cg = j * n_c + c
        state = {p: state_scr[p[0], p[1]] for p in chains}
        state_b = {p: state[p].astype(BF16) for p in chains}
        w_s = {(b_, h): _dot(w_ref[b_, rows, h * d:(h + 1) * d], state_b[b_, h]) for b_, h in chains}
        q_s = {(b_, h): _dot(qe_ref[b_, rows, h * d:(h + 1) * d], state_b[b_, h]) for b_, h in chains}
        v_new = {(b_, h): (u_ref[b_, rows, h * d:(h + 1) * d] - w_s[b_, h]).astype(BF16)
                 for b_, h in chains}
        sub = lax.broadcasted_iota(jnp.int32, egl_ref.shape[2:], 0)
        chunk_decay = [jnp.sum(jnp.where(sub == cg % cb, egl_ref[b_, cg // cb], 0.0),
                               axis=0, keepdims=True) for b_ in range(bb)]
        for b_, h in chains:
            cols = slice(h * d, (h + 1) * d)
            o = q_s[b_, h] + _dot(attn_ref[b_, rows, h * c_len:(h + 1) * c_len], v_new[b_, h])
            state_scr[b_, h] = (state[b_, h] * chunk_decay[b_][:, cols]
                                + _tn_dot(kd_ref[b_, rows, cols], v_new[b_, h]))
            zc = z_ref[b_, rows, cols]
            o_ref[b_, rows, cols] = (_rms_scale(o) * ng_ref[...] * (zc * _sigmoid(zc))
                                     ).astype(o_ref.dtype)
        return carry

    lax.fori_loop(0, n_c, chunk, 0)


def _gated_deltanet(proj3, conv_w, a_log, dt_bias, norm_g):
    b, s, _ = proj3.shape
    scal = jnp.stack([a_log, dt_bias]).astype(F32)
    width = GDN_HEADS * GDN_D
    n_chunks = s // GDN_CHUNK
    cb = min(GDN_PREP_CHUNKS, n_chunks)
    ts = cb * GDN_CHUNK
    halo = 8
    egl_rows = -(-cb // 8) * 8

    def main(col0):
        return pl.BlockSpec((None, ts, width), lambda bi, i: (bi, i, col0 * LANE // width))

    def before(col0):
        return pl.BlockSpec((None, halo, width),
                            lambda bi, i: (bi, jnp.maximum(i * (ts // halo) - 1, 0),
                                           col0 * LANE // width))

    def out_block(w_):
        return pl.BlockSpec((None, ts, w_), lambda bi, i: (bi, i, 0))

    u, w, qe, kd, attn, egl = pl.pallas_call(
        functools.partial(_gdn_prep_kernel, cb=cb),
        grid=(b, n_chunks // cb),
        in_specs=[pl.BlockSpec(memory_space=pltpu.SMEM),
                  main(COL_GQ), before(COL_GQ), main(COL_GK), before(COL_GK),
                  main(COL_GV), before(COL_GV),
                  pl.BlockSpec((None, ts, LANE), lambda bi, i: (bi, i, COL_GAB)),
                  pl.BlockSpec(conv_w.shape, lambda bi, i: (0, 0))],
        out_specs=[out_block(width), out_block(width), out_block(width), out_block(width),
                   out_block(GDN_HEADS * GDN_CHUNK),
                   pl.BlockSpec((None, None, egl_rows, width), lambda bi, i: (bi, i, 0, 0))],
        out_shape=[jax.ShapeDtypeStruct((b, s, width), F32),
                   jax.ShapeDtypeStruct((b, s, width), BF16),
                   jax.ShapeDtypeStruct((b, s, width), BF16),
                   jax.ShapeDtypeStruct((b, s, width), BF16),
                   jax.ShapeDtypeStruct((b, s, GDN_HEADS * GDN_CHUNK), BF16),
                   jax.ShapeDtypeStruct((b, n_chunks // cb, egl_rows, width), F32)],
        compiler_params=_cparams(2),
        name="gdn_prep",
    )(scal, proj3, proj3, proj3, proj3, proj3, proj3, proj3, conv_w)

    bb = 2 if b % 2 == 0 else 1
    t_scan = min(GDN_SCAN_TOKENS, s)

    def scan_block(w_):
        return pl.BlockSpec((bb, t_scan, w_), lambda bi, j: (bi, j, 0))

    return pl.pallas_call(
        functools.partial(_gdn_scan_kernel, bb=bb, ts=t_scan, cb=cb),
        grid=(b // bb, s // t_scan),
        in_specs=[scan_block(width), scan_block(width), scan_block(width), scan_block(width),
                  scan_block(GDN_HEADS * GDN_CHUNK),
                  pl.BlockSpec((bb, n_chunks // cb, egl_rows, width), lambda bi, j: (bi, 0, 0, 0)),
                  pl.BlockSpec((bb, t_scan, width), lambda bi, j: (bi, j, COL_GZ * LANE // width)),
                  pl.BlockSpec((1, GDN_D), lambda bi, j: (0, 0))],
        out_specs=scan_block(width),
        out_shape=jax.ShapeDtypeStruct((b, s, width), BF16),
        scratch_shapes=[pltpu.VMEM((bb, GDN_HEADS, GDN_D, GDN_D), F32)],
        compiler_params=_cparams(2),
        name="gdn_scan",
    )(u, w, qe, kd, attn, egl, proj3, norm_g.reshape(1, GDN_D))


def _nsa_kernel(q_ref, kcmp_ref, vcmp_ref, kslc_ref, vslc_ref, kwin_ref, vwin_ref, gate_ref,
                wk_ref, wv_ref, pk_ref, pv_ref, o_ref,
                kc_scr, vct_scr, vst_scr, vwt_scr, selk_scr, *, seq, tq):
    qi = pl.program_id(1)
    tk = NSA_TK
    dh = NSA_DH
    hg = NSA_GROUP
    n_grp = NSA_KV_HEADS
    nq = hg * tq
    nb = seq // CMP_STRIDE
    n_sel = seq // SEL_LEN
    n_kb = seq // tk
    top_k = min(SEL_TOPK, n_sel)
    half = CMP_LEN // 2
    slc_step = min(NSA_SLC_BLOCKS, n_kb)
    n_win = WINDOW // tk + 1

    @pl.when(qi == 0)
    def _():
        cmp_out = []
        for src, pos, w_ref in ((kcmp_ref, pk_ref, wk_ref), (vcmp_ref, pv_ref, wv_ref)):
            lo_parts, hi_parts = [], []
            for l in range(half):
                rows = src[pl.ds(l, nb, stride=CMP_STRIDE), :]
                lo_parts.append(rows + pos[l:l + 1, :])
                hi_parts.append(rows + pos[half + l:half + l + 1, :])
            a_lo = _dot(jnp.concatenate(lo_parts, axis=1), w_ref[0], HIGHEST)
            a_hi = _dot(jnp.concatenate(hi_parts, axis=1), w_ref[1], HIGHEST)
            a_hi = jnp.concatenate([a_hi[1:], jnp.zeros((1, LANE), F32)], axis=0)
            cmp_out.append(a_lo + a_hi)
        kc_scr[...] = cmp_out[0]
        vct_scr[...] = cmp_out[1].T.astype(BF16)

        def transpose_values(kb, carry):
            k0 = pl.multiple_of(kb * tk, tk)
            vst_scr[kb] = vslc_ref[pl.ds(k0, tk), :].T.astype(BF16)
            vwt_scr[kb] = vwin_ref[pl.ds(k0, tk), :].T.astype(BF16)
            return carry

        lax.fori_loop(0, n_kb, transpose_values, 0)

    lane = lax.broadcasted_iota(jnp.int32, (tq, LANE), 1)
    qg = []
    for g in range(n_grp):
        parts = []
        for i in range(hg):
            hd = g * hg + i
            pair = q_ref[:, (hd // 2) * LANE:(hd // 2 + 1) * LANE] * (dh ** -0.5)
            if hd % 2 != g:
                pair = pltpu.roll(pair, dh, axis=1)
            parts.append(jnp.where((lane >= g * dh) & (lane < (g + 1) * dh), pair, 0.0))
        qg.append(jnp.concatenate(parts, axis=0))

    t_row = qi * tq + lax.broadcasted_iota(jnp.int32, (1, tq), 1)
    t_row4 = jnp.concatenate([t_row] * hg, axis=1)
    key_iota = lax.broadcasted_iota(jnp.int32, (tk, 1), 0)

    j_sel = lax.broadcasted_iota(jnp.int32, (n_sel, tq), 0)
    cur = lax.shift_right_logical(t_row, SEL_LEN.bit_length() - 1)
    sel_valid = j_sel <= cur
    forced = (j_sel == 0) | (j_sel == cur) | (j_sel == cur - 1)
    ov_j = lax.broadcasted_iota(jnp.int32, (n_sel, nb), 0) * SEL_LEN
    ov_n = lax.broadcasted_iota(jnp.int32, (n_sel, nb), 1) * CMP_STRIDE
    overlap = jnp.where((ov_n < ov_j + SEL_LEN) & (ov_n + CMP_LEN > ov_j), 1.0, 0.0).astype(BF16)
    n_idx = lax.broadcasted_iota(jnp.int32, (nb, 1), 0)
    cmp_valid = (n_idx * CMP_STRIDE + CMP_LEN - 1 <= t_row4) & (n_idx < nb - 1)

    def mask_heads(ok, s):
        return jnp.where(jnp.concatenate([ok] * hg, axis=1), s, MASK_NEG)

    kc_hi, kc_lo = _split_bf16(kc_scr[...])
    q_b, o_cmp = [], []
    for g in range(n_grp):
        q_hi, q_lo = _split_bf16(qg[g])
        q_b.append(q_hi)

        s_cmp = _nt_dot(kc_hi, q_hi) + _nt_dot(kc_lo, q_hi) + _nt_dot(kc_hi, q_lo)
        s_m = jnp.where(cmp_valid, s_cmp, MASK_NEG)
        e = jnp.where(cmp_valid, jnp.exp(s_m - jnp.max(s_m, axis=0, keepdims=True)), 0.0)
        den = jnp.sum(e, axis=0, keepdims=True)
        p_cmp = jnp.where(den > 0.0, e / jnp.where(den > 0.0, den, 1.0), 0.0)
        o_cmp.append(_dot(vct_scr[g * dh:(g + 1) * dh, :], p_cmp.astype(BF16)))

        p_sum = p_cmp[:, 0:tq]
        for i in range(1, hg):
            p_sum = p_sum + p_cmp[:, i * tq:(i + 1) * tq]
        ps_hi, ps_lo = _split_bf16(p_sum)
        imp = _dot(overlap, ps_hi) + _dot(overlap, ps_lo)
        score = jnp.where(sel_valid, imp + jnp.where(forced, FORCE_BONUS, 0.0), MASK_NEG)
        beaten = jnp.zeros((n_sel, tq), F32)
        for jp in range(n_sel):
            other = score[jp:jp + 1, :]
            ge = jnp.where(other >= score, 1.0, 0.0)
            gt = jnp.where(other > score, 1.0, 0.0)
            beaten = beaten + jnp.where(j_sel > jp, ge, gt)
        picked = jnp.where(beaten < top_k, 1.0, 0.0)
        for j in range(n_sel):
            selk_scr[g, j * SEL_LEN:(j + 1) * SEL_LEN, :] = jnp.broadcast_to(
                picked[j:j + 1, :], (SEL_LEN, tq))

    def slc_body(it, carry):
        kb0 = it * slc_step
        k_blocks = [kslc_ref[pl.ds(pl.multiple_of((kb0 + u) * tk, tk), tk), :].astype(BF16)
                    for u in range(slc_step)]
        new = []
        for g in range(n_grp):
            m, l, acc = carry[g]
            tiles = []
            for u in range(slc_step):
                k0 = pl.multiple_of((kb0 + u) * tk, tk)
                ok = (selk_scr[g, pl.ds(k0, tk), :] > 0.5) & (k0 + key_iota <= t_row)
                tiles.append(mask_heads(ok, _nt_dot(k_blocks[u], q_b[g])))
            s_all = jnp.concatenate(tiles, axis=0)
            m_new = jnp.maximum(m, jnp.max(s_all, axis=0, keepdims=True))
            alpha = jnp.exp(m - m_new)
            p = jnp.exp(s_all - m_new)
            l = alpha * l + jnp.sum(p, axis=0, keepdims=True)
            v_t = jnp.concatenate([vst_scr[kb0 + u, g * dh:(g + 1) * dh, :]
                                   for u in range(slc_step)], axis=1)
            acc = alpha * acc + _dot(v_t, p.astype(BF16))
            new.append((m_new, l, acc))
        return tuple(new)

    init = tuple((jnp.full((1, nq), MASK_NEG, F32), jnp.zeros((1, nq), F32),
                  jnp.zeros((dh, nq), F32)) for _ in range(n_grp))
    span = slc_step * tk
    n_iter = ((qi + 1) * tq + span - 1) // span
    slc = lax.fori_loop(0, n_iter, slc_body, init)
    o_slc = [acc / l for (_, l, acc) in slc]

    win_blocks = []
    for w in range(n_win):
        kb_int = qi - w
        kb = jnp.maximum(kb_int, 0)
        kpos = kb_int * tk + key_iota
        ok = (kpos <= t_row) & (kpos > t_row - WINDOW) & (kpos >= 0)
        k_blk = kwin_ref[pl.ds(pl.multiple_of(kb * tk, tk), tk), :].astype(BF16)
        win_blocks.append((kb, k_blk, ok))
    o_win = []
    for g in range(n_grp):
        s_all = jnp.concatenate([mask_heads(ok, _nt_dot(k_blk, q_b[g]))
                                 for (_, k_blk, ok) in win_blocks], axis=0)
        p = jnp.exp(s_all - jnp.max(s_all, axis=0, keepdims=True))
        l = jnp.sum(p, axis=0, keepdims=True)
        v_t = jnp.concatenate([vwt_scr[kb, g * dh:(g + 1) * dh, :] for (kb, _, _) in win_blocks],
                              axis=1)
        o_win.append(_dot(v_t, p.astype(BF16)) / l)

    gates = _sigmoid(gate_ref[...].T)
    outs = []
    for g in range(n_grp):
        for i in range(hg):
            hd = g * hg + i
            cols = slice(i * tq, (i + 1) * tq)
            outs.append(gates[3 * hd:3 * hd + 1, :] * o_cmp[g][:, cols]
                        + gates[3 * hd + 1:3 * hd + 2, :] * o_slc[g][:, cols]
                        + gates[3 * hd + 2:3 * hd + 3, :] * o_win[g][:, cols])
    o_ref[...] = jnp.concatenate(outs, axis=0).T.astype(o_ref.dtype)


def _cmp_weight(w):
    half = CMP_LEN // 2
    w = w.reshape(2, half, NSA_DH, NSA_DH)
    z = jnp.zeros_like(w)
    big = jnp.stack([jnp.concatenate([w, z], axis=-1), jnp.concatenate([z, w], axis=-1)], axis=2)
    return big.reshape(2, half * NSA_KV_HEADS * NSA_DH, NSA_KV_HEADS * NSA_DH)


def _native_sparse_attention(proj3, cmp_wk, cmp_pk, cmp_wv, cmp_pv, *, tq):
    b, s, _ = proj3.shape
    nb = s // CMP_STRIDE
    n_kb = s // NSA_TK
    assert tq == NSA_TK and n_kb % min(NSA_SLC_BLOCKS, n_kb) == 0
    wk = _cmp_weight(cmp_wk)
    wv = _cmp_weight(cmp_wv)
    pk = jnp.tile(cmp_pk, (1, NSA_KV_HEADS))
    pv = jnp.tile(cmp_pv, (1, NSA_KV_HEADS))
    qw = NSA_HEADS * NSA_DH

    def seq_block(col):
        return pl.BlockSpec((None, s, LANE), lambda bi, i: (bi, 0, col))

    def whole(a):
        return pl.BlockSpec(a.shape, lambda bi, i: (0,) * a.ndim)

    return pl.pallas_call(
        functools.partial(_nsa_kernel, seq=s, tq=tq),
        grid=(b, s // tq),
        in_specs=[pl.BlockSpec((None, tq, qw), lambda bi, i: (bi, i, COL_NQ * LANE // qw))]
        + [seq_block(COL_NKV + c) for c in range(6)]
        + [pl.BlockSpec((None, tq, LANE), lambda bi, i: (bi, i, COL_NGATE)),
           whole(wk), whole(wv), whole(pk), whole(pv)],
        out_specs=pl.BlockSpec((None, tq, qw), lambda bi, i: (bi, i, 0)),
        out_shape=jax.ShapeDtypeStruct((b, s, qw), BF16),
        scratch_shapes=[pltpu.VMEM((nb, LANE), F32),
                        pltpu.VMEM((LANE, nb), BF16),
                        pltpu.VMEM((n_kb, LANE, NSA_TK), BF16),
                        pltpu.VMEM((n_kb, LANE, NSA_TK), BF16),
                        pltpu.VMEM((NSA_KV_HEADS, s, tq), F32)],
        compiler_params=_cparams(2),
        name="native_sparse_attention",
    )(proj3, proj3, proj3, proj3, proj3, proj3, proj3, proj3, wk, wv, pk, pv)


def _pad_cols(w, width):
    return jnp.pad(w, ((0, 0), (0, width - w.shape[1])))


def _proj_weight(w_in):
    gqk_v_z = 4 * GDN_HEADS * GDN_D
    n_ab = 2 * GDN_HEADS
    n_nsa = (NSA_HEADS + 6 * NSA_KV_HEADS) * NSA_DH
    n_gate = 3 * NSA_HEADS
    o_ab = gqk_v_z
    o_nsa = o_ab + n_ab
    o_gate = o_nsa + n_nsa
    o_rest = o_gate + n_gate
    w = jnp.concatenate([w_in[:, :o_ab], w_in[:, o_nsa:o_gate],
                         _pad_cols(w_in[:, o_ab:o_nsa], LANE),
                         _pad_cols(w_in[:, o_gate:o_rest], LANE),
                         w_in[:, o_rest:]], axis=1)
    assert w.shape[1] == N_PROJ, w.shape
    return w.astype(BF16)


def _layer(x, p, *, tiles):
    b, s, d = x.shape
    x2 = x.reshape(b * s, d)
    proj = _norm_matmul(x2, p["g_mix_pre"], _proj_weight(p["w_in"]), relu2=False, out_dtype=F32,
                        tm=tiles["tm"], tn=tiles["tn"])
    proj3 = proj.reshape(b, s, N_PROJ)
    ya = _gated_deltanet(proj3, p["gdn_conv_w"], p["gdn_a_log"], p["gdn_dt_bias"], p["gdn_norm_g"])
    yb = _native_sparse_attention(proj3, p["nsa_cmp_wk"], p["nsa_cmp_pk"], p["nsa_cmp_wv"],
                                  p["nsa_cmp_pv"], tq=tiles["tq"])
    yc = _sb_attention(proj3, tq=tiles["tq_sb"])
    x2 = _merge_out(ya.reshape(b * s, -1), yb.reshape(b * s, -1), yc.reshape(b * s, -1), proj, x2,
                    p["w_br_gdn"].astype(BF16), p["w_br_nsa"].astype(BF16),
                    p["w_br_sb"].astype(BF16), p["w_out"].astype(BF16), p["g_mix_post"],
                    tm=tiles["tm_out"])
    act = _norm_matmul(x2, p["g_ff_pre"], p["w_ff1"].astype(BF16), relu2=True, out_dtype=BF16,
                       tm=tiles["tm"], tn=tiles["tn"])
    x2 = _matmul_norm_res(act, p["w_ff2"].astype(BF16), x2, p["g_ff_post"], tm=tiles["tm_out"])
    return x2.reshape(b, s, d)


def _tiles(b, s):
    t = b * s
    return {"tm": min(512, t), "tn": 1024, "tm_out": min(256, t), "tq": min(128, s),
            "tq_sb": min(256, s)}


def kernel(x, g_mix_pre, g_mix_post, g_ff_pre, g_ff_post, w_in, gdn_conv_w, gdn_a_log, gdn_dt_bias,
           gdn_norm_g, nsa_cmp_wk, nsa_cmp_pk, nsa_cmp_wv, nsa_cmp_pv, w_br_gdn, w_br_nsa, w_br_sb,
           w_out, w_ff1, w_ff2):
    params = dict(g_mix_pre=g_mix_pre, g_mix_post=g_mix_post, g_ff_pre=g_ff_pre, g_ff_post=g_ff_post,
                  w_in=w_in, gdn_conv_w=gdn_conv_w, gdn_a_log=gdn_a_log, gdn_dt_bias=gdn_dt_bias,
                  gdn_norm_g=gdn_norm_g, nsa_cmp_wk=nsa_cmp_wk, nsa_cmp_pk=nsa_cmp_pk,
                  nsa_cmp_wv=nsa_cmp_wv, nsa_cmp_pv=nsa_cmp_pv, w_br_gdn=w_br_gdn,
                  w_br_nsa=w_br_nsa, w_br_sb=w_br_sb, w_out=w_out, w_ff1=w_ff1, w_ff2=w_ff2)
    tiles = _tiles(x.shape[0], x.shape[1])
    for layer in range(w_in.shape[0]):
        x = _layer(x, {k: v[layer] for k, v in params.items()}, tiles=tiles)
    return x
```

```python
import functools

import jax
import jax.numpy as jnp
from jax import lax
from jax.experimental import pallas as pl
from jax.experimental.pallas import tpu as pltpu

F32 = jnp.float32
BF16 = jnp.bfloat16
HIGHEST = lax.Precision.HIGHEST

EPS = 1e-6
MASK_NEG = -1e30
LOG2E = 1.4426950408889634

GDN_HEADS = 4
GDN_D = 128
GDN_CONV = 4
GDN_CHUNK = 64
GDN_PREP_CHUNKS = 4
GDN_SCAN_TOKENS = 512

NSA_HEADS = 8
NSA_KV_HEADS = 2
NSA_GROUP = NSA_HEADS // NSA_KV_HEADS
NSA_DH = 64
CMP_LEN = 32
CMP_STRIDE = 16
SEL_LEN = 64
SEL_TOPK = 8
WINDOW = 512
FORCE_BONUS = 1e3
NSA_TK = 128
NSA_SLC_BLOCKS = 4

SB_HEADS = 8
SB_DH = 64
SB_TK = 128

LANE = 128

COL_GQ, COL_GK, COL_GV, COL_GZ = 0, 4, 8, 12
COL_NQ, COL_NKV, COL_GAB, COL_NGATE = 16, 20, 26, 27
COL_SQ, COL_SK, COL_SV = 28, 32, 36
COL_MERGE = 40
N_PROJ = 64 * LANE

VMEM_LIMIT = 48 * 1024 * 1024


def _cparams(n_axes):
    return pltpu.CompilerParams(dimension_semantics=("arbitrary",) * n_axes,
                                vmem_limit_bytes=VMEM_LIMIT)


def _nt_dot(a, b, precision=None):
    return lax.dot_general(a, b, (((1,), (1,)), ((), ())), precision=precision,
                           preferred_element_type=F32)


def _tn_dot(a, b):
    return lax.dot_general(a, b, (((0,), (0,)), ((), ())), preferred_element_type=F32)


def _dot(a, b, precision=None):
    return jnp.dot(a, b, precision=precision, preferred_element_type=F32)


def _rms_scale(y):
    return y * lax.rsqrt(jnp.mean(y * y, axis=-1, keepdims=True) + EPS)


def _softplus(z):
    return jnp.maximum(z, 0.0) + jnp.log1p(jnp.exp(-jnp.abs(z)))


def _sigmoid(z):
    return jax.nn.sigmoid(z)


def _norm_matmul_kernel(x_ref, g_ref, w_ref, o_ref, h_scr, *, relu2):
    @pl.when(pl.program_id(1) == 0)
    def _():
        h_scr[...] = (_rms_scale(x_ref[...]) * g_ref[...]).astype(BF16)

    y = _dot(h_scr[...], w_ref[...])
    if relu2:
        y = jnp.square(jnp.maximum(y, 0.0))
    o_ref[...] = y.astype(o_ref.dtype)


def _norm_matmul(x, g, w, *, relu2, out_dtype, tm, tn):
    t, d = x.shape
    n = w.shape[1]
    return pl.pallas_call(
        functools.partial(_norm_matmul_kernel, relu2=relu2),
        grid=(t // tm, n // tn),
        in_specs=[pl.BlockSpec((tm, d), lambda i, j: (i, 0)),
                  pl.BlockSpec((1, d), lambda i, j: (0, 0)),
                  pl.BlockSpec((d, tn), lambda i, j: (0, j))],
        out_specs=pl.BlockSpec((tm, tn), lambda i, j: (i, j)),
        out_shape=jax.ShapeDtypeStruct((t, n), out_dtype),
        scratch_shapes=[pltpu.VMEM((tm, d), BF16)],
        compiler_params=_cparams(2),
        name="norm_matmul_relu2" if relu2 else "norm_matmul",
    )(x, g.reshape(1, d), w)


def _matmul_norm_res_kernel(a_ref, w_ref, x_ref, g_ref, o_ref):
    y = _dot(a_ref[...], w_ref[...])
    o_ref[...] = x_ref[...] + _rms_scale(y) * g_ref[...]


def _matmul_norm_res(a, w, x, g, *, tm):
    t, k = a.shape
    d = w.shape[1]
    return pl.pallas_call(
        _matmul_norm_res_kernel,
        grid=(t // tm,),
        in_specs=[pl.BlockSpec((tm, k), lambda i: (i, 0)),
                  pl.BlockSpec((k, d), lambda i: (0, 0)),
                  pl.BlockSpec((tm, d), lambda i: (i, 0)),
                  pl.BlockSpec((1, d), lambda i: (0, 0))],
        out_specs=pl.BlockSpec((tm, d), lambda i: (i, 0)),
        out_shape=jax.ShapeDtypeStruct((t, d), F32),
        compiler_params=_cparams(1),
        name="matmul_norm_res",
    )(a, w, x, g.reshape(1, d))


def _merge_out_kernel(ya_ref, yb_ref, yc_ref, m0_ref, m1_ref, m2_ref, x_ref,
                      wa_ref, wb_ref, wc_ref, wo_ref, g_ref, o_ref):
    merged = (_sigmoid(m0_ref[...]) * _dot(ya_ref[...], wa_ref[...])
              + _sigmoid(m1_ref[...]) * _dot(yb_ref[...], wb_ref[...])
              + _sigmoid(m2_ref[...]) * _dot(yc_ref[...], wc_ref[...]))
    y = _dot(merged.astype(BF16), wo_ref[...])
    o_ref[...] = x_ref[...] + _rms_scale(y) * g_ref[...]


def _merge_out(ya, yb, yc, proj, x, wa, wb, wc, wo, g, *, tm):
    t, d = x.shape
    w_in = ya.shape[1]
    mcol = COL_MERGE * LANE // d

    def row(i):
        return (i, 0)

    def const(i):
        return (0, 0)

    return pl.pallas_call(
        _merge_out_kernel,
        grid=(t // tm,),
        in_specs=[pl.BlockSpec((tm, w_in), row),
                  pl.BlockSpec((tm, w_in), row),
                  pl.BlockSpec((tm, w_in), row),
                  pl.BlockSpec((tm, d), lambda i: (i, mcol)),
                  pl.BlockSpec((tm, d), lambda i: (i, mcol + 1)),
                  pl.BlockSpec((tm, d), lambda i: (i, mcol + 2)),
                  pl.BlockSpec((tm, d), row),
                  pl.BlockSpec((w_in, d), const),
                  pl.BlockSpec((w_in, d), const),
                  pl.BlockSpec((w_in, d), const),
                  pl.BlockSpec((d, d), const),
                  pl.BlockSpec((1, d), const)],
        out_specs=pl.BlockSpec((tm, d), row),
        out_shape=jax.ShapeDtypeStruct((t, d), F32),
        compiler_params=_cparams(1),
        name="merge_out",
    )(ya, yb, yc, proj, proj, proj, x, wa, wb, wc, wo, g.reshape(1, d))


def _sb_kernel(q_ref, k_ref, v_ref, o_ref, run_scr, acc_scr, *, tq):
    qi = pl.program_id(1)
    tk = SB_TK
    n_diag = tq // tk
    dh = SB_DH
    n_pair = SB_HEADS * dh // LANE
    row = lax.broadcasted_iota(jnp.int32, (tq, tk), 0)
    col = lax.broadcasted_iota(jnp.int32, (tq, tk), 1)
    first_head = lax.broadcasted_iota(jnp.int32, (tk, LANE), 1) < dh
    r2 = lax.broadcasted_iota(jnp.int32, (2 * tk, 2 * tk), 0) & (tk - 1)
    c2 = lax.broadcasted_iota(jnp.int32, (2 * tk, 2 * tk), 1)
    later_ones = jnp.where((r2 > c2) | (c2 >= tk), 1.0, 0.0).astype(BF16)

    q_pairs = [(q_ref[:, p * LANE:(p + 1) * LANE] * (dh ** -0.5 * LOG2E)).astype(BF16)
               for p in range(n_pair)]

    run_scr[...] = jnp.zeros_like(run_scr)
    acc_scr[...] = jnp.zeros_like(acc_scr)

    def split_heads(x):
        zero = jnp.zeros_like(x)
        return jnp.concatenate([jnp.where(first_head, x, zero), jnp.where(first_head, zero, x)],
                               axis=0)

    def block(j, diag_offset):
        k0 = pl.multiple_of(j * tk, tk)
        pairs = range(n_pair)
        diagonal = diag_offset is not None
        if diagonal:
            causal = col + diag_offset < row
            causal2 = jnp.concatenate([causal, causal], axis=1)
        z = [_nt_dot(q_pairs[p],
                     split_heads(k_ref[pl.ds(k0, tk), p * LANE:(p + 1) * LANE].astype(BF16)))
             for p in pairs]
        log_sig, hi, lo = [], [], []
        for p in pairs:
            log_sig.append(jnp.minimum(z[p], 0.0) - jnp.log2(1.0 + jnp.exp2(-jnp.abs(z[p]))))
            log_keep = log_sig[p] - z[p]
            if diagonal:
                log_keep = jnp.where(causal2, log_keep, 0.0)
            hi.append(log_keep.astype(BF16))
            lo.append((log_keep - hi[p].astype(F32)).astype(BF16))
        sums = [[_dot(jnp.concatenate([hi[p][:, hh * tk:(hh + 1) * tk],
                                       lo[p][:, hh * tk:(hh + 1) * tk]], axis=1), later_ones)
                 for hh in range(2)] for p in pairs]
        for p in pairs:
            a = []
            for hh in range(2):
                h = 2 * p + hh
                run = run_scr[h]
                a_h = jnp.exp2(log_sig[p][:, hh * tk:(hh + 1) * tk] + sums[p][hh][:, :tk] + run)
                if diagonal:
                    a_h = jnp.where(causal, a_h, 0.0)
                run_scr[h] = run + sums[p][hh][:, tk:]
                a.append(a_h.astype(BF16))
            vp = v_ref[pl.ds(k0, tk), p * LANE:(p + 1) * LANE].astype(BF16)
            acc_scr[p] += _dot(jnp.concatenate(a, axis=1), split_heads(vp))

    for d in reversed(range(n_diag)):
        block(qi * n_diag + d, d * tk)

    def body(it, carry):
        block(qi * n_diag - 1 - it, None)
        return carry

    lax.fori_loop(0, qi * n_diag, body, 0)
    o_ref[...] = jnp.concatenate([acc_scr[p] for p in range(n_pair)], axis=-1).astype(o_ref.dtype)


def _sb_attention(proj3, *, tq):
    b, s, _ = proj3.shape
    width = SB_HEADS * SB_DH
    n_pair = width // LANE
    return pl.pallas_call(
        functools.partial(_sb_kernel, tq=tq),
        grid=(b, s // tq),
        in_specs=[pl.BlockSpec((None, tq, width), lambda bi, i: (bi, i, COL_SQ * LANE // width)),
                  pl.BlockSpec((None, s, width), lambda bi, i: (bi, 0, COL_SK * LANE // width)),
                  pl.BlockSpec((None, s, width), lambda bi, i: (bi, 0, COL_SV * LANE // width))],
        out_specs=pl.BlockSpec((None, tq, width), lambda bi, i: (bi, i, 0)),
        out_shape=jax.ShapeDtypeStruct((b, s, width), BF16),
        scratch_shapes=[pltpu.VMEM((SB_HEADS, tq, LANE), F32), pltpu.VMEM((n_pair, tq, LANE), F32)],
        compiler_params=_cparams(2),
        name="sb_attention",
    )(proj3, proj3, proj3)


def _split_bf16(x):
    hi = x.astype(BF16)
    return hi, (x - hi.astype(F32)).astype(BF16)


def _gdn_prep_kernel(scal_ref, q_ref, qh_ref, k_ref, kh_ref, v_ref, vh_ref, ab_ref, cw_ref,
                     u_ref, w_ref, qe_ref, kd_ref, attn_ref, egl_ref, *, cb):
    i = pl.program_id(1)
    c_len = GDN_CHUNK
    d = GDN_D
    n_h = GDN_HEADS
    ts = cb * c_len
    halo = qh_ref.shape[0]
    width = n_h * d

    def conv_silu(x_ref, halo_ref, w):
        ext = jnp.concatenate([halo_ref[...] * jnp.where(i > 0, 1.0, 0.0), x_ref[...]], axis=0)
        y = jnp.zeros((ts, width), F32)
        for tap in range(GDN_CONV):
            off = halo - (GDN_CONV - 1) + tap
            y = y + w[tap:tap + 1, :] * ext[off:off + ts]
        return y * _sigmoid(y)

    qc = conv_silu(q_ref, qh_ref, cw_ref[:, 0:width])
    kc = conv_silu(k_ref, kh_ref, cw_ref[:, width:2 * width])
    vc = conv_silu(v_ref, vh_ref, cw_ref[:, 2 * width:3 * width])

    ab = ab_ref[...]
    lane = lax.broadcasted_iota(jnp.int32, (1, LANE), 1)
    a_log = jnp.zeros((1, LANE), F32)
    dt_bias = jnp.zeros((1, LANE), F32)
    for h in range(n_h):
        a_log = jnp.where(lane == h, scal_ref[0, h], a_log)
        dt_bias = jnp.where(lane == h, scal_ref[1, h], dt_bias)
    g_cum = -jnp.exp(a_log) * _softplus(ab + dt_bias)
    row_in_chunk = lax.broadcasted_iota(jnp.int32, (ts, LANE), 0) & (c_len - 1)
    shift = 1
    while shift < c_len:
        g_cum = g_cum + jnp.where(row_in_chunk >= shift, pltpu.roll(g_cum, shift, axis=0), 0.0)
        shift *= 2
    beta_all = _sigmoid(ab)

    row = lax.broadcasted_iota(jnp.int32, (c_len, c_len), 0)
    col = lax.broadcasted_iota(jnp.int32, (c_len, c_len), 1)
    incl = row >= col
    strict = row > col
    eye = jnp.where(row == col, 1.0, 0.0)

    problems = [(c, h) for c in range(cb) for h in range(n_h)]
    qn, kn, k_beta, v_beta, g_col, decay = {}, {}, {}, {}, {}, {}
    for c, h in problems:
        rows = slice(c * c_len, (c + 1) * c_len)
        cols = slice(h * d, (h + 1) * d)
        qh, kh = qc[rows, cols], kc[rows, cols]
        qn[c, h] = qh * lax.rsqrt(jnp.sum(qh * qh, axis=-1, keepdims=True) + EPS) * (d ** -0.5)
        kn[c, h] = kh * lax.rsqrt(jnp.sum(kh * kh, axis=-1, keepdims=True) + EPS)
        beta = beta_all[rows, n_h + h:n_h + h + 1]
        k_beta[c, h] = kn[c, h] * beta
        v_beta[c, h] = (vc[rows, cols] * beta).astype(BF16)
        g_col[c, h] = g_cum[rows, h:h + 1]
        g_sq = jnp.broadcast_to(g_col[c, h], (c_len, c_len))
        decay[c, h] = jnp.where(incl, jnp.exp(jnp.where(incl, g_sq - g_sq.T, 0.0)), 0.0)

    kn_b = {p: kn[p].astype(BF16) for p in problems}
    kkt = {p: _nt_dot(k_beta[p].astype(BF16), kn_b[p]) for p in problems}
    qkt = {p: _nt_dot(qn[p].astype(BF16), kn_b[p]) for p in problems}

    a_mat = {p: jnp.where(strict, kkt[p] * decay[p], 0.0) for p in problems}
    left = lax.broadcasted_iota(jnp.int32, (c_len, 2 * c_len), 1) < c_len
    pair = {p: jnp.concatenate([-a_mat[p], eye], axis=1) for p in problems}
    for _ in range(max(c_len - 1, 1).bit_length()):
        y = {p: _dot(pair[p][:, :c_len].astype(BF16), pair[p].astype(BF16)) for p in problems}
        pair = {p: jnp.where(left, y[p], pair[p] + y[p]) for p in problems}
    t_inv = {p: pair[p][:, c_len:] for p in problems}
    resid = {}
    for p in problems:
        m_hi, m_lo = _split_bf16(eye + a_mat[p])
        t_hi, t_lo = _split_bf16(t_inv[p])
        resid[p] = (eye - (_dot(m_hi, t_hi) + _dot(m_lo, t_hi) + _dot(m_hi, t_lo)), t_hi)
    t_b = {p: (t_inv[p] + _dot(resid[p][1], resid[p][0].astype(BF16))).astype(BF16)
           for p in problems}

    for c, h in problems:
        p = (c, h)
        rows = slice(c * c_len, (c + 1) * c_len)
        cols = slice(h * d, (h + 1) * d)
        exp_g = jnp.exp(g_col[p])
        g_last = g_col[p][c_len - 1:c_len, :]
        u_ref[rows, cols] = _dot(t_b[p], v_beta[p])
        w_ref[rows, cols] = _dot(t_b[p], (k_beta[p] * exp_g).astype(BF16)).astype(BF16)
        qe_ref[rows, cols] = (qn[p] * exp_g).astype(BF16)
        kd_ref[rows, cols] = (kn[p] * jnp.exp(g_last - g_col[p])).astype(BF16)
    egl_rows = [jnp.concatenate([jnp.broadcast_to(jnp.exp(g_col[c, h][c_len - 1:c_len, :]), (1, d))
                                 for h in range(n_h)], axis=1) for c in range(cb)]
    if egl_ref.shape[0] > cb:
        egl_rows.append(jnp.zeros((egl_ref.shape[0] - cb, width), F32))
    egl_ref[...] = jnp.concatenate(egl_rows, axis=0)
    for c in range(cb):
        rows = slice(c * c_len, (c + 1) * c_len)
        attn_ref[rows, :] = jnp.concatenate(
            [(qkt[c, h] * decay[c, h]).astype(BF16) for h in range(n_h)], axis=1)


def _gdn_scan_kernel(u_ref, w_ref, qe_ref, kd_ref, attn_ref, egl_ref, z_ref, ng_ref, o_ref,
                     state_scr, *, bb, ts, cb):
    j = pl.program_id(1)
    c_len = GDN_CHUNK
    d = GDN_D
    n_c = ts // c_len
    chains = [(b_, h) for b_ in range(bb) for h in range(GDN_HEADS)]

    @pl.when(j == 0)
    def _():
        state_scr[...] = jnp.zeros_like(state_scr)

    def chunk(c, carry):
        r0 = pl.multiple_of(c * c_len, c_len)
        rows = pl.ds(r0, c_len)
        cg = j * n_c + c
        state = {p: state_scr[p[0], p[1]] for p in chains}
        state_b = {p: state[p].astype(BF16) for p in chains}
        w_s = {(b_, h): _dot(w_ref[b_, rows, h * d:(h + 1) * d], state_b[b_, h]) for b_, h in chains}
        q_s = {(b_, h): _dot(qe_ref[b_, rows, h * d:(h + 1) * d], state_b[b_, h]) for b_, h in chains}
        v_new = {(b_, h): (u_ref[b_, rows, h * d:(h + 1) * d] - w_s[b_, h]).astype(BF16)
                 for b_, h in chains}
        sub = lax.broadcasted_iota(jnp.int32, egl_ref.shape[2:], 0)
        chunk_decay = [jnp.sum(jnp.where(sub == cg % cb, egl_ref[b_, cg // cb], 0.0),
                               axis=0, keepdims=True) for b_ in range(bb)]
        for b_, h in chains:
            cols = slice(h * d, (h + 1) * d)
            o = q_s[b_, h] + _dot(attn_ref[b_, rows, h * c_len:(h + 1) * c_len], v_new[b_, h])
            state_scr[b_, h] = (state[b_, h] * chunk_decay[b_][:, cols]
                                + _tn_dot(kd_ref[b_, rows, cols], v_new[b_, h]))
            zc = z_ref[b_, rows, cols]
            o_ref[b_, rows, cols] = (_rms_scale(o) * ng_ref[...] * (zc * _sigmoid(zc))
                                     ).astype(o_ref.dtype)
        return carry

    lax.fori_loop(0, n_c, chunk, 0)


def _gated_deltanet(proj3, conv_w, a_log, dt_bias, norm_g):
    b, s, _ = proj3.shape
    scal = jnp.stack([a_log, dt_bias]).astype(F32)
    width = GDN_HEADS * GDN_D
    n_chunks = s // GDN_CHUNK
    cb = min(GDN_PREP_CHUNKS, n_chunks)
    ts = cb * GDN_CHUNK
    halo = 8
    egl_rows = -(-cb // 8) * 8

    def main(col0):
        return pl.BlockSpec((None, ts, width), lambda bi, i: (bi, i, col0 * LANE // width))

    def before(col0):
        return pl.BlockSpec((None, halo, width),
                            lambda bi, i: (bi, jnp.maximum(i * (ts // halo) - 1, 0),
                                           col0 * LANE // width))

    def out_block(w_):
        return pl.BlockSpec((None, ts, w_), lambda bi, i: (bi, i, 0))

    u, w, qe, kd, attn, egl = pl.pallas_call(
        functools.partial(_gdn_prep_kernel, cb=cb),
        grid=(b, n_chunks // cb),
        in_specs=[pl.BlockSpec(memory_space=pltpu.SMEM),
                  main(COL_GQ), before(COL_GQ), main(COL_GK), before(COL_GK),
                  main(COL_GV), before(COL_GV),
                  pl.BlockSpec((None, ts, LANE), lambda bi, i: (bi, i, COL_GAB)),
                  pl.BlockSpec(conv_w.shape, lambda bi, i: (0, 0))],
        out_specs=[out_block(width), out_block(width), out_block(width), out_block(width),
                   out_block(GDN_HEADS * GDN_CHUNK),
                   pl.BlockSpec((None, None, egl_rows, width), lambda bi, i: (bi, i, 0, 0))],
        out_shape=[jax.ShapeDtypeStruct((b, s, width), F32),
                   jax.ShapeDtypeStruct((b, s, width), BF16),
                   jax.ShapeDtypeStruct((b, s, width), BF16),
                   jax.ShapeDtypeStruct((b, s, width), BF16),
                   jax.ShapeDtypeStruct((b, s, GDN_HEADS * GDN_CHUNK), BF16),
                   jax.ShapeDtypeStruct((b, n_chunks // cb, egl_rows, width), F32)],
        compiler_params=_cparams(2),
        name="gdn_prep",
    )(scal, proj3, proj3, proj3, proj3, proj3, proj3, proj3, conv_w)

    bb = 2 if b % 2 == 0 else 1
    t_scan = min(GDN_SCAN_TOKENS, s)

    def scan_block(w_):
        return pl.BlockSpec((bb, t_scan, w_), lambda bi, j: (bi, j, 0))

    return pl.pallas_call(
        functools.partial(_gdn_scan_kernel, bb=bb, ts=t_scan, cb=cb),
        grid=(b // bb, s // t_scan),
        in_specs=[scan_block(width), scan_block(width), scan_block(width), scan_block(width),
                  scan_block(GDN_HEADS * GDN_CHUNK),
                  pl.BlockSpec((bb, n_chunks // cb, egl_rows, width), lambda bi, j: (bi, 0, 0, 0)),
                  pl.BlockSpec((bb, t_scan, width), lambda bi, j: (bi, j, COL_GZ * LANE // width)),
                  pl.BlockSpec((1, GDN_D), lambda bi, j: (0, 0))],
        out_specs=scan_block(width),
        out_shape=jax.ShapeDtypeStruct((b, s, width), BF16),
        scratch_shapes=[pltpu.VMEM((bb, GDN_HEADS, GDN_D, GDN_D), F32)],
        compiler_params=_cparams(2),
        name="gdn_scan",
    )(u, w, qe, kd, attn, egl, proj3, norm_g.reshape(1, GDN_D))


def _nsa_kernel(q_ref, kcmp_ref, vcmp_ref, kslc_ref, vslc_ref, kwin_ref, vwin_ref, gate_ref,
                wk_ref, wv_ref, pk_ref, pv_ref, o_ref,
                kc_scr, vct_scr, vst_scr, vwt_scr, selk_scr, *, seq, tq):
    qi = pl.program_id(1)
    tk = NSA_TK
    dh = NSA_DH
    hg = NSA_GROUP
    n_grp = NSA_KV_HEADS
    nq = hg * tq
    nb = seq // CMP_STRIDE
    n_sel = seq // SEL_LEN
    n_kb = seq // tk
    top_k = min(SEL_TOPK, n_sel)
    half = CMP_LEN // 2
    slc_step = min(NSA_SLC_BLOCKS, n_kb)
    n_win = WINDOW // tk + 1

    @pl.when(qi == 0)
    def _():
        cmp_out = []
        for src, pos, w_ref in ((kcmp_ref, pk_ref, wk_ref), (vcmp_ref, pv_ref, wv_ref)):
            lo_parts, hi_parts = [], []
            for l in range(half):
                rows = src[pl.ds(l, nb, stride=CMP_STRIDE), :]
                lo_parts.append(rows + pos[l:l + 1, :])
                hi_parts.append(rows + pos[half + l:half + l + 1, :])
            a_lo = _dot(jnp.concatenate(lo_parts, axis=1), w_ref[0], HIGHEST)
            a_hi = _dot(jnp.concatenate(hi_parts, axis=1), w_ref[1], HIGHEST)
            a_hi = jnp.concatenate([a_hi[1:], jnp.zeros((1, LANE), F32)], axis=0)
            cmp_out.append(a_lo + a_hi)
        kc_scr[...] = cmp_out[0]
        vct_scr[...] = cmp_out[1].T.astype(BF16)

        def transpose_values(kb, carry):
            k0 = pl.multiple_of(kb * tk, tk)
            vst_scr[kb] = vslc_ref[pl.ds(k0, tk), :].T.astype(BF16)
            vwt_scr[kb] = vwin_ref[pl.ds(k0, tk), :].T.astype(BF16)
            return carry

        lax.fori_loop(0, n_kb, transpose_values, 0)

    lane = lax.broadcasted_iota(jnp.int32, (tq, LANE), 1)
    qg = []
    for g in range(n_grp):
        parts = []
        for i in range(hg):
            hd = g * hg + i
            pair = q_ref[:, (hd // 2) * LANE:(hd // 2 + 1) * LANE] * (dh ** -0.5)
            if hd % 2 != g:
                pair = pltpu.roll(pair, dh, axis=1)
            parts.append(jnp.where((lane >= g * dh) & (lane < (g + 1) * dh), pair, 0.0))
        qg.append(jnp.concatenate(parts, axis=0))

    t_row = qi * tq + lax.broadcasted_iota(jnp.int32, (1, tq), 1)
    t_row4 = jnp.concatenate([t_row] * hg, axis=1)
    key_iota = lax.broadcasted_iota(jnp.int32, (tk, 1), 0)

    j_sel = lax.broadcasted_iota(jnp.int32, (n_sel, tq), 0)
    cur = lax.shift_right_logical(t_row, SEL_LEN.bit_length() - 1)
    sel_valid = j_sel <= cur
    forced = (j_sel == 0) | (j_sel == cur) | (j_sel == cur - 1)
    ov_j = lax.broadcasted_iota(jnp.int32, (n_sel, nb), 0) * SEL_LEN
    ov_n = lax.broadcasted_iota(jnp.int32, (n_sel, nb), 1) * CMP_STRIDE
    overlap = jnp.where((ov_n < ov_j + SEL_LEN) & (ov_n + CMP_LEN > ov_j), 1.0, 0.0).astype(BF16)
    n_idx = lax.broadcasted_iota(jnp.int32, (nb, 1), 0)
    cmp_valid = (n_idx * CMP_STRIDE + CMP_LEN - 1 <= t_row4) & (n_idx < nb - 1)

    def mask_heads(ok, s):
        return jnp.where(jnp.concatenate([ok] * hg, axis=1), s, MASK_NEG)

    kc_hi, kc_lo = _split_bf16(kc_scr[...])
    q_b, o_cmp = [], []
    for g in range(n_grp):
        q_hi, q_lo = _split_bf16(qg[g])
        q_b.append(q_hi)

        s_cmp = _nt_dot(kc_hi, q_hi) + _nt_dot(kc_lo, q_hi) + _nt_dot(kc_hi, q_lo)
        s_m = jnp.where(cmp_valid, s_cmp, MASK_NEG)
        e = jnp.where(cmp_valid, jnp.exp(s_m - jnp.max(s_m, axis=0, keepdims=True)), 0.0)
        den = jnp.sum(e, axis=0, keepdims=True)
        p_cmp = jnp.where(den > 0.0, e / jnp.where(den > 0.0, den, 1.0), 0.0)
        o_cmp.append(_dot(vct_scr[g * dh:(g + 1) * dh, :], p_cmp.astype(BF16)))

        p_sum = p_cmp[:, 0:tq]
        for i in range(1, hg):
            p_sum = p_sum + p_cmp[:, i * tq:(i + 1) * tq]
        ps_hi, ps_lo = _split_bf16(p_sum)
        imp = _dot(overlap, ps_hi) + _dot(overlap, ps_lo)
        score = jnp.where(sel_valid, imp + jnp.where(forced, FORCE_BONUS, 0.0), MASK_NEG)
        beaten = jnp.zeros((n_sel, tq), F32)
        for jp in range(n_sel):
            other = score[jp:jp + 1, :]
            ge = jnp.where(other >= score, 1.0, 0.0)
            gt = jnp.where(other > score, 1.0, 0.0)
            beaten = beaten + jnp.where(j_sel > jp, ge, gt)
        picked = jnp.where(beaten < top_k, 1.0, 0.0)
        for j in range(n_sel):
            selk_scr[g, j * SEL_LEN:(j + 1) * SEL_LEN, :] = jnp.broadcast_to(
                picked[j:j + 1, :], (SEL_LEN, tq))

    def slc_body(it, carry):
        kb0 = it * slc_step
        k_blocks = [kslc_ref[pl.ds(pl.multiple_of((kb0 + u) * tk, tk), tk), :].astype(BF16)
                    for u in range(slc_step)]
        new = []
        for g in range(n_grp):
            m, l, acc = carry[g]
            tiles = []
            for u in range(slc_step):
                k0 = pl.multiple_of((kb0 + u) * tk, tk)
                ok = (selk_scr[g, pl.ds(k0, tk), :] > 0.5) & (k0 + key_iota <= t_row)
                tiles.append(mask_heads(ok, _nt_dot(k_blocks[u], q_b[g])))
            s_all = jnp.concatenate(tiles, axis=0)
            m_new = jnp.maximum(m, jnp.max(s_all, axis=0, keepdims=True))
            alpha = jnp.exp(m - m_new)
            p = jnp.exp(s_all - m_new)
            l = alpha * l + jnp.sum(p, axis=0, keepdims=True)
            v_t = jnp.concatenate([vst_scr[kb0 + u, g * dh:(g + 1) * dh, :]
                                   for u in range(slc_step)], axis=1)
            acc = alpha * acc + _dot(v_t, p.astype(BF16))
            new.append((m_new, l, acc))
        return tuple(new)

    init = tuple((jnp.full((1, nq), MASK_NEG, F32), jnp.zeros((1, nq), F32),
                  jnp.zeros((dh, nq), F32)) for _ in range(n_grp))
    span = slc_step * tk
    n_iter = ((qi + 1) * tq + span - 1) // span
    slc = lax.fori_loop(0, n_iter, slc_body, init)
    o_slc = [acc / l for (_, l, acc) in slc]

    win_blocks = []
    for w in range(n_win):
        kb_int = qi - w
        kb = jnp.maximum(kb_int, 0)
        kpos = kb_int * tk + key_iota
        ok = (kpos <= t_row) & (kpos > t_row - WINDOW) & (kpos >= 0)
        k_blk = kwin_ref[pl.ds(pl.multiple_of(kb * tk, tk), tk), :].astype(BF16)
        win_blocks.append((kb, k_blk, ok))
    o_win = []
    for g in range(n_grp):
        s_all = jnp.concatenate([mask_heads(ok, _nt_dot(k_blk, q_b[g]))
                                 for (_, k_blk, ok) in win_blocks], axis=0)
        p = jnp.exp(s_all - jnp.max(s_all, axis=0, keepdims=True))
        l = jnp.sum(p, axis=0, keepdims=True)
        v_t = jnp.concatenate([vwt_scr[kb, g * dh:(g + 1) * dh, :] for (kb, _, _) in win_blocks],
                              axis=1)
        o_win.append(_dot(v_t, p.astype(BF16)) / l)

    gates = _sigmoid(gate_ref[...].T)
    outs = []
    for g in range(n_grp):
        for i in range(hg):
            hd = g * hg + i
            cols = slice(i * tq, (i + 1) * tq)
            outs.append(gates[3 * hd:3 * hd + 1, :] * o_cmp[g][:, cols]
                        + gates[3 * hd + 1:3 * hd + 2, :] * o_slc[g][:, cols]
                        + gates[3 * hd + 2:3 * hd + 3, :] * o_win[g][:, cols])
    o_ref[...] = jnp.concatenate(outs, axis=0).T.astype(o_ref.dtype)


def _cmp_weight(w):
    half = CMP_LEN // 2
    w = w.reshape(2, half, NSA_DH, NSA_DH)
    z = jnp.zeros_like(w)
    big = jnp.stack([jnp.concatenate([w, z], axis=-1), jnp.concatenate([z, w], axis=-1)], axis=2)
    return big.reshape(2, half * NSA_KV_HEADS * NSA_DH, NSA_KV_HEADS * NSA_DH)


def _native_sparse_attention(proj3, cmp_wk, cmp_pk, cmp_wv, cmp_pv, *, tq):
    b, s, _ = proj3.shape
    nb = s // CMP_STRIDE
    n_kb = s // NSA_TK
    assert tq == NSA_TK and n_kb % min(NSA_SLC_BLOCKS, n_kb) == 0
    wk = _cmp_weight(cmp_wk)
    wv = _cmp_weight(cmp_wv)
    pk = jnp.tile(cmp_pk, (1, NSA_KV_HEADS))
    pv = jnp.tile(cmp_pv, (1, NSA_KV_HEADS))
    qw = NSA_HEADS * NSA_DH

    def seq_block(col):
        return pl.BlockSpec((None, s, LANE), lambda bi, i: (bi, 0, col))

    def whole(a):
        return pl.BlockSpec(a.shape, lambda bi, i: (0,) * a.ndim)

    return pl.pallas_call(
        functools.partial(_nsa_kernel, seq=s, tq=tq),
        grid=(b, s // tq),
        in_specs=[pl.BlockSpec((None, tq, qw), lambda bi, i: (bi, i, COL_NQ * LANE // qw))]
        + [seq_block(COL_NKV + c) for c in range(6)]
        + [pl.BlockSpec((None, tq, LANE), lambda bi, i: (bi, i, COL_NGATE)),
           whole(wk), whole(wv), whole(pk), whole(pv)],
        out_specs=pl.BlockSpec((None, tq, qw), lambda bi, i: (bi, i, 0)),
        out_shape=jax.ShapeDtypeStruct((b, s, qw), BF16),
        scratch_shapes=[pltpu.VMEM((nb, LANE), F32),
                        pltpu.VMEM((LANE, nb), BF16),
                        pltpu.VMEM((n_kb, LANE, NSA_TK), BF16),
                        pltpu.VMEM((n_kb, LANE, NSA_TK), BF16),
                        pltpu.VMEM((NSA_KV_HEADS, s, tq), F32)],
        compiler_params=_cparams(2),
        name="native_sparse_attention",
    )(proj3, proj3, proj3, proj3, proj3, proj3, proj3, proj3, wk, wv, pk, pv)


def _pad_cols(w, width):
    return jnp.pad(w, ((0, 0), (0, width - w.shape[1])))


def _proj_weight(w_in):
    gqk_v_z = 4 * GDN_HEADS * GDN_D
    n_ab = 2 * GDN_HEADS
    n_nsa = (NSA_HEADS + 6 * NSA_KV_HEADS) * NSA_DH
    n_gate = 3 * NSA_HEADS
    o_ab = gqk_v_z
    o_nsa = o_ab + n_ab
    o_gate = o_nsa + n_nsa
    o_rest = o_gate + n_gate
    w = jnp.concatenate([w_in[:, :o_ab], w_in[:, o_nsa:o_gate],
                         _pad_cols(w_in[:, o_ab:o_nsa], LANE),
                         _pad_cols(w_in[:, o_gate:o_rest], LANE),
                         w_in[:, o_rest:]], axis=1)
    assert w.shape[1] == N_PROJ, w.shape
    return w.astype(BF16)


def _layer(x, p, *, tiles):
    b, s, d = x.shape
    x2 = x.reshape(b * s, d)
    proj = _norm_matmul(x2, p["g_mix_pre"], _proj_weight(p["w_in"]), relu2=False, out_dtype=F32,
                        tm=tiles["tm"], tn=tiles["tn"])
    proj3 = proj.reshape(b, s, N_PROJ)
    ya = _gated_deltanet(proj3, p["gdn_conv_w"], p["gdn_a_log"], p["gdn_dt_bias"], p["gdn_norm_g"])
    yb = _native_sparse_attention(proj3, p["nsa_cmp_wk"], p["nsa_cmp_pk"], p["nsa_cmp_wv"],
                                  p["nsa_cmp_pv"], tq=tiles["tq"])
    yc = _sb_attention(proj3, tq=tiles["tq_sb"])
    x2 = _merge_out(ya.reshape(b * s, -1), yb.reshape(b * s, -1), yc.reshape(b * s, -1), proj, x2,
                    p["w_br_gdn"].astype(BF16), p["w_br_nsa"].astype(BF16),
                    p["w_br_sb"].astype(BF16), p["w_out"].astype(BF16), p["g_mix_post"],
                    tm=tiles["tm_out"])
    act = _norm_matmul(x2, p["g_ff_pre"], p["w_ff1"].astype(BF16), relu2=True, out_dtype=BF16,
                       tm=tiles["tm"], tn=tiles["tn"])
    x2 = _matmul_norm_res(act, p["w_ff2"].astype(BF16), x2, p["g_ff_post"], tm=tiles["tm_out"])
    return x2.reshape(b, s, d)


def _tiles(b, s):
    t = b * s
    return {"tm": min(1024, t), "tn": 1024, "tm_out": min(512, t), "tq": min(128, s),
            "tq_sb": min(256, s)}


def kernel(x, g_mix_pre, g_mix_post, g_ff_pre, g_ff_post, w_in, gdn_conv_w, gdn_a_log, gdn_dt_bias,
           gdn_norm_g, nsa_cmp_wk, nsa_cmp_pk, nsa_cmp_wv, nsa_cmp_pv, w_br_gdn, w_br_nsa, w_br_sb,
           w_out, w_ff1, w_ff2):
    params = dict(g_mix_pre=g_mix_pre, g_mix_post=g_mix_post, g_ff_pre=g_ff_pre, g_ff_post=g_ff_post,
                  w_in=w_in, gdn_conv_w=gdn_conv_w, gdn_a_log=gdn_a_log, gdn_dt_bias=gdn_dt_bias,
                  gdn_norm_g=gdn_norm_g, nsa_cmp_wk=nsa_cmp_wk, nsa_cmp_pk=nsa_cmp_pk,
                  nsa_cmp_wv=nsa_cmp_wv, nsa_cmp_pv=nsa_cmp_pv, w_br_gdn=w_br_gdn,
                  w_br_nsa=w_br_nsa, w_br_sb=w_br_sb, w_out=w_out, w_ff1=w_ff1, w_ff2=w_ff2)
    tiles = _tiles(x.shape[0], x.shape[1])
    for layer in range(w_in.shape[0]):
        x = _layer(x, {k: v[layer] for k, v in params.items()}, tiles=tiles)
    return x
```

```python
import functools

import jax
import jax.numpy as jnp
from jax import lax
from jax.experimental import pallas as pl
from jax.experimental.pallas import tpu as pltpu

F32 = jnp.float32
BF16 = jnp.bfloat16
HIGHEST = lax.Precision.HIGHEST

EPS = 1e-6
MASK_NEG = -1e30
LOG2E = 1.4426950408889634

GDN_HEADS = 4
GDN_D = 128
GDN_CONV = 4
GDN_CHUNK = 64
GDN_PREP_CHUNKS = 4
GDN_SCAN_TOKENS = 512

NSA_HEADS = 8
NSA_KV_HEADS = 2
NSA_GROUP = NSA_HEADS // NSA_KV_HEADS
NSA_DH = 64
CMP_LEN = 32
CMP_STRIDE = 16
SEL_LEN = 64
SEL_TOPK = 8
WINDOW = 512
FORCE_BONUS = 1e3
NSA_TK = 128
NSA_SLC_BLOCKS = 4

SB_HEADS = 8
SB_DH = 64
SB_TK = 128

LANE = 128

COL_GQ, COL_GK, COL_GV, COL_GZ = 0, 4, 8, 12
COL_NQ, COL_NKV, COL_GAB, COL_NGATE = 16, 20, 26, 27
COL_SQ, COL_SK, COL_SV = 28, 32, 36
COL_MERGE = 40
N_PROJ = 64 * LANE

VMEM_LIMIT = 48 * 1024 * 1024


def _cparams(n_axes):
    return pltpu.CompilerParams(dimension_semantics=("arbitrary",) * n_axes,
                                vmem_limit_bytes=VMEM_LIMIT)


def _nt_dot(a, b, precision=None):
    return lax.dot_general(a, b, (((1,), (1,)), ((), ())), precision=precision,
                           preferred_element_type=F32)


def _tn_dot(a, b):
    return lax.dot_general(a, b, (((0,), (0,)), ((), ())), preferred_element_type=F32)


def _dot(a, b, precision=None):
    return jnp.dot(a, b, precision=precision, preferred_element_type=F32)


def _rms_scale(y):
    return y * lax.rsqrt(jnp.mean(y * y, axis=-1, keepdims=True) + EPS)


def _softplus(z):
    return jnp.maximum(z, 0.0) + jnp.log1p(jnp.exp(-jnp.abs(z)))


def _sigmoid(z):
    return jax.nn.sigmoid(z)


def _norm_matmul_kernel(x_ref, g_ref, w_ref, o_ref, h_scr, *, relu2):
    @pl.when(pl.program_id(1) == 0)
    def _():
        h_scr[...] = (_rms_scale(x_ref[...]) * g_ref[...]).astype(BF16)

    y = _dot(h_scr[...], w_ref[...])
    if relu2:
        y = jnp.square(jnp.maximum(y, 0.0))
    o_ref[...] = y.astype(o_ref.dtype)


def _norm_matmul(x, g, w, *, relu2, out_dtype, tm, tn):
    t, d = x.shape
    n = w.shape[1]
    return pl.pallas_call(
        functools.partial(_norm_matmul_kernel, relu2=relu2),
        grid=(t // tm, n // tn),
        in_specs=[pl.BlockSpec((tm, d), lambda i, j: (i, 0)),
                  pl.BlockSpec((1, d), lambda i, j: (0, 0)),
                  pl.BlockSpec((d, tn), lambda i, j: (0, j))],
        out_specs=pl.BlockSpec((tm, tn), lambda i, j: (i, j)),
        out_shape=jax.ShapeDtypeStruct((t, n), out_dtype),
        scratch_shapes=[pltpu.VMEM((tm, d), BF16)],
        compiler_params=_cparams(2),
        name="norm_matmul_relu2" if relu2 else "norm_matmul",
    )(x, g.reshape(1, d), w)


def _matmul_norm_res_kernel(a_ref, w_ref, x_ref, g_ref, o_ref):
    y = _dot(a_ref[...], w_ref[...])
    o_ref[...] = x_ref[...] + _rms_scale(y) * g_ref[...]


def _matmul_norm_res(a, w, x, g, *, tm):
    t, k = a.shape
    d = w.shape[1]
    return pl.pallas_call(
        _matmul_norm_res_kernel,
        grid=(t // tm,),
        in_specs=[pl.BlockSpec((tm, k), lambda i: (i, 0)),
                  pl.BlockSpec((k, d), lambda i: (0, 0)),
                  pl.BlockSpec((tm, d), lambda i: (i, 0)),
                  pl.BlockSpec((1, d), lambda i: (0, 0))],
        out_specs=pl.BlockSpec((tm, d), lambda i: (i, 0)),
        out_shape=jax.ShapeDtypeStruct((t, d), F32),
        compiler_params=_cparams(1),
        name="matmul_norm_res",
    )(a, w, x, g.reshape(1, d))


def _merge_out_kernel(ya_ref, yb_ref, yc_ref, m0_ref, m1_ref, m2_ref, x_ref,
                      wa_ref, wb_ref, wc_ref, wo_ref, g_ref, o_ref):
    merged = (_sigmoid(m0_ref[...]) * _dot(ya_ref[...], wa_ref[...])
              + _sigmoid(m1_ref[...]) * _dot(yb_ref[...], wb_ref[...])
              + _sigmoid(m2_ref[...]) * _dot(yc_ref[...], wc_ref[...]))
    y = _dot(merged.astype(BF16), wo_ref[...])
    o_ref[...] = x_ref[...] + _rms_scale(y) * g_ref[...]


def _merge_out(ya, yb, yc, proj, x, wa, wb, wc, wo, g, *, tm):
    t, d = x.shape
    w_in = ya.shape[1]
    mcol = COL_MERGE * LANE // d

    def row(i):
        return (i, 0)

    def const(i):
        return (0, 0)

    return pl.pallas_call(
        _merge_out_kernel,
        grid=(t // tm,),
        in_specs=[pl.BlockSpec((tm, w_in), row),
                  pl.BlockSpec((tm, w_in), row),
                  pl.BlockSpec((tm, w_in), row),
                  pl.BlockSpec((tm, d), lambda i: (i, mcol)),
                  pl.BlockSpec((tm, d), lambda i: (i, mcol + 1)),
                  pl.BlockSpec((tm, d), lambda i: (i, mcol + 2)),
                  pl.BlockSpec((tm, d), row),
                  pl.BlockSpec((w_in, d), const),
                  pl.BlockSpec((w_in, d), const),
                  pl.BlockSpec((w_in, d), const),
                  pl.BlockSpec((d, d), const),
                  pl.BlockSpec((1, d), const)],
        out_specs=pl.BlockSpec((tm, d), row),
        out_shape=jax.ShapeDtypeStruct((t, d), F32),
        compiler_params=_cparams(1),
        name="merge_out",
    )(ya, yb, yc, proj, proj, proj, x, wa, wb, wc, wo, g.reshape(1, d))


def _sb_kernel(q_ref, k_ref, v_ref, o_ref, run_scr, acc_scr, *, tq):
    qi = pl.program_id(1)
    tk = SB_TK
    n_diag = tq // tk
    dh = SB_DH
    n_pair = SB_HEADS * dh // LANE
    row = lax.broadcasted_iota(jnp.int32, (tq, tk), 0)
    col = lax.broadcasted_iota(jnp.int32, (tq, tk), 1)
    first_head = lax.broadcasted_iota(jnp.int32, (tk, LANE), 1) < dh
    r2 = lax.broadcasted_iota(jnp.int32, (2 * tk, 2 * tk), 0) & (tk - 1)
    c2 = lax.broadcasted_iota(jnp.int32, (2 * tk, 2 * tk), 1)
    later_ones = jnp.where((r2 > c2) | (c2 >= tk), 1.0, 0.0).astype(BF16)

    q_pairs = [(q_ref[:, p * LANE:(p + 1) * LANE] * (dh ** -0.5 * LOG2E)).astype(BF16)
               for p in range(n_pair)]

    run_scr[...] = jnp.zeros_like(run_scr)
    acc_scr[...] = jnp.zeros_like(acc_scr)

    def split_heads(x):
        zero = jnp.zeros_like(x)
        return jnp.concatenate([jnp.where(first_head, x, zero), jnp.where(first_head, zero, x)],
                               axis=0)

    def block(j, diag_offset):
        k0 = pl.multiple_of(j * tk, tk)
        pairs = range(n_pair)
        diagonal = diag_offset is not None
        if diagonal:
            causal = col + diag_offset < row
            causal2 = jnp.concatenate([causal, causal], axis=1)
        z = [_nt_dot(q_pairs[p],
                     split_heads(k_ref[pl.ds(k0, tk), p * LANE:(p + 1) * LANE].astype(BF16)))
             for p in pairs]
        log_sig, hi, lo = [], [], []
        for p in pairs:
            log_sig.append(jnp.minimum(z[p], 0.0) - jnp.log2(1.0 + jnp.exp2(-jnp.abs(z[p]))))
            log_keep = log_sig[p] - z[p]
            if diagonal:
                log_keep = jnp.where(causal2, log_keep, 0.0)
            hi.append(log_keep.astype(BF16))
            lo.append((log_keep - hi[p].astype(F32)).astype(BF16))
        sums = [[_dot(jnp.concatenate([hi[p][:, hh * tk:(hh + 1) * tk],
                                       lo[p][:, hh * tk:(hh + 1) * tk]], axis=1), later_ones)
                 for hh in range(2)] for p in pairs]
        for p in pairs:
            a = []
            for hh in range(2):
                h = 2 * p + hh
                run = run_scr[h]
                a_h = jnp.exp2(log_sig[p][:, hh * tk:(hh + 1) * tk] + sums[p][hh][:, :tk] + run)
                if diagonal:
                    a_h = jnp.where(causal, a_h, 0.0)
                run_scr[h] = run + sums[p][hh][:, tk:]
                a.append(a_h.astype(BF16))
            vp = v_ref[pl.ds(k0, tk), p * LANE:(p + 1) * LANE].astype(BF16)
            acc_scr[p] += _dot(jnp.concatenate(a, axis=1), split_heads(vp))

    for d in reversed(range(n_diag)):
        block(qi * n_diag + d, d * tk)

    def body(it, carry):
        block(qi * n_diag - 1 - it, None)
        return carry

    lax.fori_loop(0, qi * n_diag, body, 0)
    o_ref[...] = jnp.concatenate([acc_scr[p] for p in range(n_pair)], axis=-1).astype(o_ref.dtype)


def _sb_attention(proj3, *, tq):
    b, s, _ = proj3.shape
    width = SB_HEADS * SB_DH
    n_pair = width // LANE
    return pl.pallas_call(
        functools.partial(_sb_kernel, tq=tq),
        grid=(b, s // tq),
        in_specs=[pl.BlockSpec((None, tq, width), lambda bi, i: (bi, i, COL_SQ * LANE // width)),
                  pl.BlockSpec((None, s, width), lambda bi, i: (bi, 0, COL_SK * LANE // width)),
                  pl.BlockSpec((None, s, width), lambda bi, i: (bi, 0, COL_SV * LANE // width))],
        out_specs=pl.BlockSpec((None, tq, width), lambda bi, i: (bi, i, 0)),
        out_shape=jax.ShapeDtypeStruct((b, s, width), BF16),
        scratch_shapes=[pltpu.VMEM((SB_HEADS, tq, LANE), F32), pltpu.VMEM((n_pair, tq, LANE), F32)],
        compiler_params=_cparams(2),
        name="sb_attention",
    )(proj3, proj3, proj3)


def _split_bf16(x):
    hi = x.astype(BF16)
    return hi, (x - hi.astype(F32)).astype(BF16)


def _gdn_prep_kernel(scal_ref, q_ref, qh_ref, k_ref, kh_ref, v_ref, vh_ref, ab_ref, cw_ref,
                     u_ref, w_ref, qe_ref, kd_ref, attn_ref, egl_ref, *, cb):
    i = pl.program_id(1)
    c_len = GDN_CHUNK
    d = GDN_D
    n_h = GDN_HEADS
    ts = cb * c_len
    halo = qh_ref.shape[0]
    width = n_h * d

    def conv_silu(x_ref, halo_ref, w):
        ext = jnp.concatenate([halo_ref[...] * jnp.where(i > 0, 1.0, 0.0), x_ref[...]], axis=0)
        y = jnp.zeros((ts, width), F32)
        for tap in range(GDN_CONV):
            off = halo - (GDN_CONV - 1) + tap
            y = y + w[tap:tap + 1, :] * ext[off:off + ts]
        return y * _sigmoid(y)

    qc = conv_silu(q_ref, qh_ref, cw_ref[:, 0:width])
    kc = conv_silu(k_ref, kh_ref, cw_ref[:, width:2 * width])
    vc = conv_silu(v_ref, vh_ref, cw_ref[:, 2 * width:3 * width])

    ab = ab_ref[...]
    lane = lax.broadcasted_iota(jnp.int32, (1, LANE), 1)
    a_log = jnp.zeros((1, LANE), F32)
    dt_bias = jnp.zeros((1, LANE), F32)
    for h in range(n_h):
        a_log = jnp.where(lane == h, scal_ref[0, h], a_log)
        dt_bias = jnp.where(lane == h, scal_ref[1, h], dt_bias)
    g_cum = -jnp.exp(a_log) * _softplus(ab + dt_bias)
    row_in_chunk = lax.broadcasted_iota(jnp.int32, (ts, LANE), 0) & (c_len - 1)
    shift = 1
    while shift < c_len:
        g_cum = g_cum + jnp.where(row_in_chunk >= shift, pltpu.roll(g_cum, shift, axis=0), 0.0)
        shift *= 2
    beta_all = _sigmoid(ab)

    row = lax.broadcasted_iota(jnp.int32, (c_len, c_len), 0)
    col = lax.broadcasted_iota(jnp.int32, (c_len, c_len), 1)
    incl = row >= col
    strict = row > col
    eye = jnp.where(row == col, 1.0, 0.0)

    problems = [(c, h) for c in range(cb) for h in range(n_h)]
    qn, kn, k_beta, v_beta, g_col, decay = {}, {}, {}, {}, {}, {}
    for c, h in problems:
        rows = slice(c * c_len, (c + 1) * c_len)
        cols = slice(h * d, (h + 1) * d)
        qh, kh = qc[rows, cols], kc[rows, cols]
        qn[c, h] = qh * lax.rsqrt(jnp.sum(qh * qh, axis=-1, keepdims=True) + EPS) * (d ** -0.5)
        kn[c, h] = kh * lax.rsqrt(jnp.sum(kh * kh, axis=-1, keepdims=True) + EPS)
        beta = beta_all[rows, n_h + h:n_h + h + 1]
        k_beta[c, h] = kn[c, h] * beta
        v_beta[c, h] = (vc[rows, cols] * beta).astype(BF16)
        g_col[c, h] = g_cum[rows, h:h + 1]
        g_sq = jnp.broadcast_to(g_col[c, h], (c_len, c_len))
        decay[c, h] = jnp.where(incl, jnp.exp(jnp.where(incl, g_sq - g_sq.T, 0.0)), 0.0)

    kn_b = {p: kn[p].astype(BF16) for p in problems}
    kkt = {p: _nt_dot(k_beta[p].astype(BF16), kn_b[p]) for p in problems}
    qkt = {p: _nt_dot(qn[p].astype(BF16), kn_b[p]) for p in problems}

    a_mat = {p: jnp.where(strict, kkt[p] * decay[p], 0.0) for p in problems}
    left = lax.broadcasted_iota(jnp.int32, (c_len, 2 * c_len), 1) < c_len
    pair = {p: jnp.concatenate([-a_mat[p], eye], axis=1) for p in problems}
    for _ in range(max(c_len - 1, 1).bit_length()):
        y = {p: _dot(pair[p][:, :c_len].astype(BF16), pair[p].astype(BF16)) for p in problems}
        pair = {p: jnp.where(left, y[p], pair[p] + y[p]) for p in problems}
    t_inv = {p: pair[p][:, c_len:] for p in problems}
    resid = {}
    for p in problems:
        m_hi, m_lo = _split_bf16(eye + a_mat[p])
        t_hi, t_lo = _split_bf16(t_inv[p])
        resid[p] = (eye - (_dot(m_hi, t_hi) + _dot(m_lo, t_hi) + _dot(m_hi, t_lo)), t_hi)
    t_b = {p: (t_inv[p] + _dot(resid[p][1], resid[p][0].astype(BF16))).astype(BF16)
           for p in problems}

    for c, h in problems:
        p = (c, h)
        rows = slice(c * c_len, (c + 1) * c_len)
        cols = slice(h * d, (h + 1) * d)
        exp_g = jnp.exp(g_col[p])
        g_last = g_col[p][c_len - 1:c_len, :]
        u_ref[rows, cols] = _dot(t_b[p], v_beta[p])
        w_ref[rows, cols] = _dot(t_b[p], (k_beta[p] * exp_g).astype(BF16)).astype(BF16)
        qe_ref[rows, cols] = (qn[p] * exp_g).astype(BF16)
        kd_ref[rows, cols] = (kn[p] * jnp.exp(g_last - g_col[p])).astype(BF16)
    egl_rows = [jnp.concatenate([jnp.broadcast_to(jnp.exp(g_col[c, h][c_len - 1:c_len, :]), (1, d))
                                 for h in range(n_h)], axis=1) for c in range(cb)]
    if egl_ref.shape[0] > cb:
        egl_rows.append(jnp.zeros((egl_ref.shape[0] - cb, width), F32))
    egl_ref[...] = jnp.concatenate(egl_rows, axis=0)
    for c in range(cb):
        rows = slice(c * c_len, (c + 1) * c_len)
        attn_ref[rows, :] = jnp.concatenate(
            [(qkt[c, h] * decay[c, h]).astype(BF16) for h in range(n_h)], axis=1)


def _gdn_scan_kernel(u_ref, w_ref, qe_ref, kd_ref, attn_ref, egl_ref, z_ref, ng_ref, o_ref,
                     state_scr, *, bb, ts, cb):
    j = pl.program_id(1)
    c_len = GDN_CHUNK
    d = GDN_D
    n_c = ts // c_len
    chains = [(b_, h) for b_ in range(bb) for h in range(GDN_HEADS)]

    @pl.when(j == 0)
    def _():
        state_scr[...] = jnp.zeros_like(state_scr)

    def chunk(c, carry):
        r0 = pl.multiple_of(c * c_len, c_len)
        rows = pl.ds(r0, c_len)
        cg = j * n_c + c
        state = {p: state_scr[p[0], p[1]] for p in chains}
        state_b = {p: state[p].astype(BF16) for p in chains}
        w_s = {(b_, h): _dot(w_ref[b_, rows, h * d:(h + 1) * d], state_b[b_, h]) for b_, h in chains}
        q_s = {(b_, h): _dot(qe_ref[b_, rows, h * d:(h + 1) * d], state_b[b_, h]) for b_, h in chains}
        v_new = {(b_, h): (u_ref[b_, rows, h * d:(h + 1) * d] - w_s[b_, h]).astype(BF16)
                 for b_, h in chains}
        sub = lax.broadcasted_iota(jnp.int32, egl_ref.shape[2:], 0)
        chunk_decay = [jnp.sum(jnp.where(sub == cg % cb, egl_ref[b_, cg // cb], 0.0),
                               axis=0, keepdims=True) for b_ in range(bb)]
        for b_, h in chains:
            cols = slice(h * d, (h + 1) * d)
            o = q_s[b_, h] + _dot(attn_ref[b_, rows, h * c_len:(h + 1) * c_len], v_new[b_, h])
            state_scr[b_, h] = (state[b_, h] * chunk_decay[b_][:, cols]
                                + _tn_dot(kd_ref[b_, rows, cols], v_new[b_, h]))
            zc = z_ref[b_, rows, cols]
            o_ref[b_, rows, cols] = (_rms_scale(o) * ng_ref[...] * (zc * _sigmoid(zc))
                                     ).astype(o_ref.dtype)
        return carry

    lax.fori_loop(0, n_c, chunk, 0)


def _gated_deltanet(proj3, conv_w, a_log, dt_bias, norm_g):
    b, s, _ = proj3.shape
    scal = jnp.stack([a_log, dt_bias]).astype(F32)
    width = GDN_HEADS * GDN_D
    n_chunks = s // GDN_CHUNK
    cb = min(GDN_PREP_CHUNKS, n_chunks)
    ts = cb * GDN_CHUNK
    halo = 8
    egl_rows = -(-cb // 8) * 8

    def main(col0):
        return pl.BlockSpec((None, ts, width), lambda bi, i: (bi, i, col0 * LANE // width))

    def before(col0):
        return pl.BlockSpec((None, halo, width),
                            lambda bi, i: (bi, jnp.maximum(i * (ts // halo) - 1, 0),
                                           col0 * LANE // width))

    def out_block(w_):
        return pl.BlockSpec((None, ts, w_), lambda bi, i: (bi, i, 0))

    u, w, qe, kd, attn, egl = pl.pallas_call(
        functools.partial(_gdn_prep_kernel, cb=cb),
        grid=(b, n_chunks // cb),
        in_specs=[pl.BlockSpec(memory_space=pltpu.SMEM),
                  main(COL_GQ), before(COL_GQ), main(COL_GK), before(COL_GK),
                  main(COL_GV), before(COL_GV),
                  pl.BlockSpec((None, ts, LANE), lambda bi, i: (bi, i, COL_GAB)),
                  pl.BlockSpec(conv_w.shape, lambda bi, i: (0, 0))],
        out_specs=[out_block(width), out_block(width), out_block(width), out_block(width),
                   out_block(GDN_HEADS * GDN_CHUNK),
                   pl.BlockSpec((None, None, egl_rows, width), lambda bi, i: (bi, i, 0, 0))],
        out_shape=[jax.ShapeDtypeStruct((b, s, width), F32),
                   jax.ShapeDtypeStruct((b, s, width), BF16),
                   jax.ShapeDtypeStruct((b, s, width), BF16),
                   jax.ShapeDtypeStruct((b, s, width), BF16),
                   jax.ShapeDtypeStruct((b, s, GDN_HEADS * GDN_CHUNK), BF16),
                   jax.ShapeDtypeStruct((b, n_chunks // cb, egl_rows, width), F32)],
        compiler_params=_cparams(2),
        name="gdn_prep",
    )(scal, proj3, proj3, proj3, proj3, proj3, proj3, proj3, conv_w)

    bb = 2 if b % 2 == 0 else 1
    t_scan = min(GDN_SCAN_TOKENS, s)

    def scan_block(w_):
        return pl.BlockSpec((bb, t_scan, w_), lambda bi, j: (bi, j, 0))

    return pl.pallas_call(
        functools.partial(_gdn_scan_kernel, bb=bb, ts=t_scan, cb=cb),
        grid=(b // bb, s // t_scan),
        in_specs=[scan_block(width), scan_block(width), scan_block(width), scan_block(width),
                  scan_block(GDN_HEADS * GDN_CHUNK),
                  pl.BlockSpec((bb, n_chunks // cb, egl_rows, width), lambda bi, j: (bi, 0, 0, 0)),
                  pl.BlockSpec((bb, t_scan, width), lambda bi, j: (bi, j, COL_GZ * LANE // width)),
                  pl.BlockSpec((1, GDN_D), lambda bi, j: (0, 0))],
        out_specs=scan_block(width),
        out_shape=jax.ShapeDtypeStruct((b, s, width), BF16),
        scratch_shapes=[pltpu.VMEM((bb, GDN_HEADS, GDN_D, GDN_D), F32)],
        compiler_params=_cparams(2),
        name="gdn_scan",
    )(u, w, qe, kd, attn, egl, proj3, norm_g.reshape(1, GDN_D))


def _nsa_kernel(q_ref, kcmp_ref, vcmp_ref, kslc_ref, vslc_ref, kwin_ref, vwin_ref, gate_ref,
                wk_ref, wv_ref, pk_ref, pv_ref, o_ref,
                kc_scr, vct_scr, vst_scr, vwt_scr, selk_scr, *, seq, tq):
    qi = pl.program_id(1)
    tk = NSA_TK
    dh = NSA_DH
    hg = NSA_GROUP
    n_grp = NSA_KV_HEADS
    nq = hg * tq
    nb = seq // CMP_STRIDE
    n_sel = seq // SEL_LEN
    n_kb = seq // tk
    top_k = min(SEL_TOPK, n_sel)
    half = CMP_LEN // 2
    slc_step = min(NSA_SLC_BLOCKS, n_kb)
    n_win = WINDOW // tk + 1

    @pl.when(qi == 0)
    def _():
        cmp_out = []
        for src, pos, w_ref in ((kcmp_ref, pk_ref, wk_ref), (vcmp_ref, pv_ref, wv_ref)):
            lo_parts, hi_parts = [], []
            for l in range(half):
                rows = src[pl.ds(l, nb, stride=CMP_STRIDE), :]
                lo_parts.append(rows + pos[l:l + 1, :])
                hi_parts.append(rows + pos[half + l:half + l + 1, :])
            a_lo = _dot(jnp.concatenate(lo_parts, axis=1), w_ref[0], HIGHEST)
            a_hi = _dot(jnp.concatenate(hi_parts, axis=1), w_ref[1], HIGHEST)
            a_hi = jnp.concatenate([a_hi[1:], jnp.zeros((1, LANE), F32)], axis=0)
            cmp_out.append(a_lo + a_hi)
        kc_scr[...] = cmp_out[0]
        vct_scr[...] = cmp_out[1].T.astype(BF16)

        def transpose_values(kb, carry):
            k0 = pl.multiple_of(kb * tk, tk)
            vst_scr[kb] = vslc_ref[pl.ds(k0, tk), :].T.astype(BF16)
            vwt_scr[kb] = vwin_ref[pl.ds(k0, tk), :].T.astype(BF16)
            return carry

        lax.fori_loop(0, n_kb, transpose_values, 0)

    lane = lax.broadcasted_iota(jnp.int32, (tq, LANE), 1)
    qg = []
    for g in range(n_grp):
        parts = []
        for i in range(hg):
            hd = g * hg + i
            pair = q_ref[:, (hd // 2) * LANE:(hd // 2 + 1) * LANE] * (dh ** -0.5)
            if hd % 2 != g:
                pair = pltpu.roll(pair, dh, axis=1)
            parts.append(jnp.where((lane >= g * dh) & (lane < (g + 1) * dh), pair, 0.0))
        qg.append(jnp.concatenate(parts, axis=0))

    t_row = qi * tq + lax.broadcasted_iota(jnp.int32, (1, tq), 1)
    t_row4 = jnp.concatenate([t_row] * hg, axis=1)
    key_iota = lax.broadcasted_iota(jnp.int32, (tk, 1), 0)

    j_sel = lax.broadcasted_iota(jnp.int32, (n_sel, tq), 0)
    cur = lax.shift_right_logical(t_row, SEL_LEN.bit_length() - 1)
    sel_valid = j_sel <= cur
    forced = (j_sel == 0) | (j_sel == cur) | (j_sel == cur - 1)
    ov_j = lax.broadcasted_iota(jnp.int32, (n_sel, nb), 0) * SEL_LEN
    ov_n = lax.broadcasted_iota(jnp.int32, (n_sel, nb), 1) * CMP_STRIDE
    overlap = jnp.where((ov_n < ov_j + SEL_LEN) & (ov_n + CMP_LEN > ov_j), 1.0, 0.0).astype(BF16)
    n_idx = lax.broadcasted_iota(jnp.int32, (nb, 1), 0)
    cmp_valid = (n_idx * CMP_STRIDE + CMP_LEN - 1 <= t_row4) & (n_idx < nb - 1)

    def mask_heads(ok, s):
        return jnp.where(jnp.concatenate([ok] * hg, axis=1), s, MASK_NEG)

    groups = range(n_grp)
    kc_hi, kc_lo = _split_bf16(kc_scr[...])
    q_split = [_split_bf16(qg[g]) for g in groups]
    q_b = [q_split[g][0] for g in groups]

    win_blocks = []
    for w in range(n_win):
        kb_int = qi - w
        kb = jnp.maximum(kb_int, 0)
        kpos = kb_int * tk + key_iota
        ok = (kpos <= t_row) & (kpos > t_row - WINDOW) & (kpos >= 0)
        k_blk = kwin_ref[pl.ds(pl.multiple_of(kb * tk, tk), tk), :].astype(BF16)
        win_blocks.append((kb, k_blk, ok))

    s_cmp = [_nt_dot(kc_hi, q_split[g][0]) + _nt_dot(kc_lo, q_split[g][0])
             + _nt_dot(kc_hi, q_split[g][1]) for g in groups]
    s_win = [jnp.concatenate([mask_heads(ok, _nt_dot(k_blk, q_b[g]))
                              for (_, k_blk, ok) in win_blocks], axis=0) for g in groups]

    p_cmp, p_win, l_win = [], [], []
    for g in groups:
        s_m = jnp.where(cmp_valid, s_cmp[g], MASK_NEG)
        e = jnp.where(cmp_valid, jnp.exp(s_m - jnp.max(s_m, axis=0, keepdims=True)), 0.0)
        den = jnp.sum(e, axis=0, keepdims=True)
        p_cmp.append(jnp.where(den > 0.0, e / jnp.where(den > 0.0, den, 1.0), 0.0))
    for g in groups:
        p = jnp.exp(s_win[g] - jnp.max(s_win[g], axis=0, keepdims=True))
        l_win.append(jnp.sum(p, axis=0, keepdims=True))
        p_win.append(p.astype(BF16))

    o_cmp = [_dot(vct_scr[g * dh:(g + 1) * dh, :], p_cmp[g].astype(BF16)) for g in groups]
    imp = []
    for g in groups:
        p_sum = p_cmp[g][:, 0:tq]
        for i in range(1, hg):
            p_sum = p_sum + p_cmp[g][:, i * tq:(i + 1) * tq]
        ps_hi, ps_lo = _split_bf16(p_sum)
        imp.append(_dot(overlap, ps_hi) + _dot(overlap, ps_lo))
    o_win = []
    for g in groups:
        v_t = jnp.concatenate([vwt_scr[kb, g * dh:(g + 1) * dh, :] for (kb, _, _) in win_blocks],
                              axis=1)
        o_win.append(_dot(v_t, p_win[g]) / l_win[g])

    for g in groups:
        score = jnp.where(sel_valid, imp[g] + jnp.where(forced, FORCE_BONUS, 0.0), MASK_NEG)
        beaten = jnp.zeros((n_sel, tq), F32)
        for jp in range(n_sel):
            other = score[jp:jp + 1, :]
            ge = jnp.where(other >= score, 1.0, 0.0)
            gt = jnp.where(other > score, 1.0, 0.0)
            beaten = beaten + jnp.where(j_sel > jp, ge, gt)
        picked = jnp.where(beaten < top_k, 1.0, 0.0)
        for j in range(n_sel):
            selk_scr[g, j * SEL_LEN:(j + 1) * SEL_LEN, :] = jnp.broadcast_to(
                picked[j:j + 1, :], (SEL_LEN, tq))

    def slc_body(it, carry):
        kb0 = it * slc_step
        k_blocks = [kslc_ref[pl.ds(pl.multiple_of((kb0 + u) * tk, tk), tk), :].astype(BF16)
                    for u in range(slc_step)]
        tiles = {}
        for g in range(n_grp):
            for u in range(slc_step):
                k0 = pl.multiple_of((kb0 + u) * tk, tk)
                ok = (selk_scr[g, pl.ds(k0, tk), :] > 0.5) & (k0 + key_iota <= t_row)
                tiles[g, u] = mask_heads(ok, _nt_dot(k_blocks[u], q_b[g]))
        new = []
        for g in range(n_grp):
            m, l, acc = carry[g]
            for u in range(slc_step):
                s = tiles[g, u]
                m_new = jnp.maximum(m, jnp.max(s, axis=0, keepdims=True))
                alpha = jnp.exp(m - m_new)
                p = jnp.exp(s - m_new)
                l = alpha * l + jnp.sum(p, axis=0, keepdims=True)
                acc = alpha * acc + _dot(vst_scr[kb0 + u, g * dh:(g + 1) * dh, :], p.astype(BF16))
                m = m_new
            new.append((m, l, acc))
        return tuple(new)

    init = tuple((jnp.full((1, nq), MASK_NEG, F32), jnp.zeros((1, nq), F32),
                  jnp.zeros((dh, nq), F32)) for _ in range(n_grp))
    span = slc_step * tk
    n_iter = ((qi + 1) * tq + span - 1) // span
    slc = lax.fori_loop(0, n_iter, slc_body, init)
    o_slc = [acc / l for (_, l, acc) in slc]

    gates = _sigmoid(gate_ref[...].T)
    outs = []
    for g in range(n_grp):
        for i in range(hg):
            hd = g * hg + i
            cols = slice(i * tq, (i + 1) * tq)
            outs.append(gates[3 * hd:3 * hd + 1, :] * o_cmp[g][:, cols]
                        + gates[3 * hd + 1:3 * hd + 2, :] * o_slc[g][:, cols]
                        + gates[3 * hd + 2:3 * hd + 3, :] * o_win[g][:, cols])
    o_ref[...] = jnp.concatenate(outs, axis=0).T.astype(o_ref.dtype)


def _cmp_weight(w):
    half = CMP_LEN // 2
    w = w.reshape(2, half, NSA_DH, NSA_DH)
    z = jnp.zeros_like(w)
    big = jnp.stack([jnp.concatenate([w, z], axis=-1), jnp.concatenate([z, w], axis=-1)], axis=2)
    return big.reshape(2, half * NSA_KV_HEADS * NSA_DH, NSA_KV_HEADS * NSA_DH)


def _native_sparse_attention(proj3, cmp_wk, cmp_pk, cmp_wv, cmp_pv, *, tq):
    b, s, _ = proj3.shape
    nb = s // CMP_STRIDE
    n_kb = s // NSA_TK
    assert tq == NSA_TK and n_kb % min(NSA_SLC_BLOCKS, n_kb) == 0
    wk = _cmp_weight(cmp_wk)
    wv = _cmp_weight(cmp_wv)
    pk = jnp.tile(cmp_pk, (1, NSA_KV_HEADS))
    pv = jnp.tile(cmp_pv, (1, NSA_KV_HEADS))
    qw = NSA_HEADS * NSA_DH

    def seq_block(col):
        return pl.BlockSpec((None, s, LANE), lambda bi, i: (bi, 0, col))

    def whole(a):
        return pl.BlockSpec(a.shape, lambda bi, i: (0,) * a.ndim)

    return pl.pallas_call(
        functools.partial(_nsa_kernel, seq=s, tq=tq),
        grid=(b, s // tq),
        in_specs=[pl.BlockSpec((None, tq, qw), lambda bi, i: (bi, i, COL_NQ * LANE // qw))]
        + [seq_block(COL_NKV + c) for c in range(6)]
        + [pl.BlockSpec((None, tq, LANE), lambda bi, i: (bi, i, COL_NGATE)),
           whole(wk), whole(wv), whole(pk), whole(pv)],
        out_specs=pl.BlockSpec((None, tq, qw), lambda bi, i: (bi, i, 0)),
        out_shape=jax.ShapeDtypeStruct((b, s, qw), BF16),
        scratch_shapes=[pltpu.VMEM((nb, LANE), F32),
                        pltpu.VMEM((LANE, nb), BF16),
                        pltpu.VMEM((n_kb, LANE, NSA_TK), BF16),
                        pltpu.VMEM((n_kb, LANE, NSA_TK), BF16),
                        pltpu.VMEM((NSA_KV_HEADS, s, tq), F32)],
        compiler_params=_cparams(2),
        name="native_sparse_attention",
    )(proj3, proj3, proj3, proj3, proj3, proj3, proj3, proj3, wk, wv, pk, pv)


def _pad_cols(w, width):
    return jnp.pad(w, ((0, 0), (0, width - w.shape[1])))


def _proj_weight(w_in):
    gqk_v_z = 4 * GDN_HEADS * GDN_D
    n_ab = 2 * GDN_HEADS
    n_nsa = (NSA_HEADS + 6 * NSA_KV_HEADS) * NSA_DH
    n_gate = 3 * NSA_HEADS
    o_ab = gqk_v_z
    o_nsa = o_ab + n_ab
    o_gate = o_nsa + n_nsa
    o_rest = o_gate + n_gate
    w = jnp.concatenate([w_in[:, :o_ab], w_in[:, o_nsa:o_gate],
                         _pad_cols(w_in[:, o_ab:o_nsa], LANE),
                         _pad_cols(w_in[:, o_gate:o_rest], LANE),
                         w_in[:, o_rest:]], axis=1)
    assert w.shape[1] == N_PROJ, w.shape
    return w.astype(BF16)


def _layer(x, p, *, tiles):
    b, s, d = x.shape
    x2 = x.reshape(b * s, d)
    proj = _norm_matmul(x2, p["g_mix_pre"], _proj_weight(p["w_in"]), relu2=False, out_dtype=F32,
                        tm=tiles["tm"], tn=tiles["tn"])
    proj3 = proj.reshape(b, s, N_PROJ)
    ya = _gated_deltanet(proj3, p["gdn_conv_w"], p["gdn_a_log"], p["gdn_dt_bias"], p["gdn_norm_g"])
    yb = _native_sparse_attention(proj3, p["nsa_cmp_wk"], p["nsa_cmp_pk"], p["nsa_cmp_wv"],
                                  p["nsa_cmp_pv"], tq=tiles["tq"])
    yc = _sb_attention(proj3, tq=tiles["tq_sb"])
    x2 = _merge_out(ya.reshape(b * s, -1), yb.reshape(b * s, -1), yc.reshape(b * s, -1), proj, x2,
                    p["w_br_gdn"].astype(BF16), p["w_br_nsa"].astype(BF16),
                    p["w_br_sb"].astype(BF16), p["w_out"].astype(BF16), p["g_mix_post"],
                    tm=tiles["tm_out"])
    act = _norm_matmul(x2, p["g_ff_pre"], p["w_ff1"].astype(BF16), relu2=True, out_dtype=BF16,
                       tm=tiles["tm"], tn=tiles["tn"])
    x2 = _matmul_norm_res(act, p["w_ff2"].astype(BF16), x2, p["g_ff_post"], tm=tiles["tm_out"])
    return x2.reshape(b, s, d)


def _tiles(b, s):
    t = b * s
    return {"tm": min(1024, t), "tn": 1024, "tm_out": min(512, t), "tq": min(128, s),
            "tq_sb": min(256, s)}


def kernel(x, g_mix_pre, g_mix_post, g_ff_pre, g_ff_post, w_in, gdn_conv_w, gdn_a_log, gdn_dt_bias,
           gdn_norm_g, nsa_cmp_wk, nsa_cmp_pk, nsa_cmp_wv, nsa_cmp_pv, w_br_gdn, w_br_nsa, w_br_sb,
           w_out, w_ff1, w_ff2):
    params = dict(g_mix_pre=g_mix_pre, g_mix_post=g_mix_post, g_ff_pre=g_ff_pre, g_ff_post=g_ff_post,
                  w_in=w_in, gdn_conv_w=gdn_conv_w, gdn_a_log=gdn_a_log, gdn_dt_bias=gdn_dt_bias,
                  gdn_norm_g=gdn_norm_g, nsa_cmp_wk=nsa_cmp_wk, nsa_cmp_pk=nsa_cmp_pk,
                  nsa_cmp_wv=nsa_cmp_wv, nsa_cmp_pv=nsa_cmp_pv, w_br_gdn=w_br_gdn,
                  w_br_nsa=w_br_nsa, w_br_sb=w_br_sb, w_out=w_out, w_ff1=w_ff1, w_ff2=w_ff2)
    tiles = _tiles(x.shape[0], x.shape[1])
    for layer in range(w_in.shape[0]):
        x = _layer(x, {k: v[layer] for k, v in params.items()}, tiles=tiles)
    return x
```

```python
import functools

import jax
import jax.numpy as jnp
from jax import lax
from jax.experimental import pallas as pl
from jax.experimental.pallas import tpu as pltpu

F32 = jnp.float32
BF16 = jnp.bfloat16
HIGHEST = lax.Precision.HIGHEST

EPS = 1e-6
MASK_NEG = -1e30
LOG2E = 1.4426950408889634

GDN_HEADS = 4
GDN_D = 128
GDN_CONV = 4
GDN_CHUNK = 64
GDN_PREP_CHUNKS = 4
GDN_SCAN_TOKENS = 512

NSA_HEADS = 8
NSA_KV_HEADS = 2
NSA_GROUP = NSA_HEADS // NSA_KV_HEADS
NSA_DH = 64
CMP_LEN = 32
CMP_STRIDE = 16
SEL_LEN = 64
SEL_TOPK = 8
WINDOW = 512
FORCE_BONUS = 1e3
NSA_TK = 128
NSA_SLC_BLOCKS = 4

SB_HEADS = 8
SB_DH = 64
SB_TK = 128
SB_DEAD_LOG2 = -160.0

LANE = 128

COL_GQ, COL_GK, COL_GV, COL_GZ = 0, 4, 8, 12
COL_NQ, COL_NKV, COL_GAB, COL_NGATE = 16, 20, 26, 27
COL_SQ, COL_SK, COL_SV = 28, 32, 36
COL_MERGE = 40
N_PROJ = 64 * LANE

VMEM_LIMIT = 48 * 1024 * 1024


def _cparams(n_axes):
    return pltpu.CompilerParams(dimension_semantics=("arbitrary",) * n_axes,
                                vmem_limit_bytes=VMEM_LIMIT)


def _nt_dot(a, b, precision=None):
    return lax.dot_general(a, b, (((1,), (1,)), ((), ())), precision=precision,
                           preferred_element_type=F32)


def _tn_dot(a, b):
    return lax.dot_general(a, b, (((0,), (0,)), ((), ())), preferred_element_type=F32)


def _dot(a, b, precision=None):
    return jnp.dot(a, b, precision=precision, preferred_element_type=F32)


def _rms_scale(y):
    return y * lax.rsqrt(jnp.mean(y * y, axis=-1, keepdims=True) + EPS)


def _softplus(z):
    return jnp.maximum(z, 0.0) + jnp.log1p(jnp.exp(-jnp.abs(z)))


def _sigmoid(z):
    return jax.nn.sigmoid(z)


def _norm_matmul_kernel(x_ref, g_ref, w_ref, o_ref, h_scr, *, relu2):
    @pl.when(pl.program_id(1) == 0)
    def _():
        h_scr[...] = (_rms_scale(x_ref[...]) * g_ref[...]).astype(BF16)

    y = _dot(h_scr[...], w_ref[...])
    if relu2:
        y = jnp.square(jnp.maximum(y, 0.0))
    o_ref[...] = y.astype(o_ref.dtype)


def _norm_matmul(x, g, w, *, relu2, out_dtype, tm, tn):
    t, d = x.shape
    n = w.shape[1]
    return pl.pallas_call(
        functools.partial(_norm_matmul_kernel, relu2=relu2),
        grid=(t // tm, n // tn),
        in_specs=[pl.BlockSpec((tm, d), lambda i, j: (i, 0)),
                  pl.BlockSpec((1, d), lambda i, j: (0, 0)),
                  pl.BlockSpec((d, tn), lambda i, j: (0, j))],
        out_specs=pl.BlockSpec((tm, tn), lambda i, j: (i, j)),
        out_shape=jax.ShapeDtypeStruct((t, n), out_dtype),
        scratch_shapes=[pltpu.VMEM((tm, d), BF16)],
        compiler_params=_cparams(2),
        name="norm_matmul_relu2" if relu2 else "norm_matmul",
    )(x, g.reshape(1, d), w)


def _matmul_norm_res_kernel(a_ref, w_ref, x_ref, g_ref, o_ref):
    y = _dot(a_ref[...], w_ref[...])
    o_ref[...] = x_ref[...] + _rms_scale(y) * g_ref[...]


def _matmul_norm_res(a, w, x, g, *, tm):
    t, k = a.shape
    d = w.shape[1]
    return pl.pallas_call(
        _matmul_norm_res_kernel,
        grid=(t // tm,),
        in_specs=[pl.BlockSpec((tm, k), lambda i: (i, 0)),
                  pl.BlockSpec((k, d), lambda i: (0, 0)),
                  pl.BlockSpec((tm, d), lambda i: (i, 0)),
                  pl.BlockSpec((1, d), lambda i: (0, 0))],
        out_specs=pl.BlockSpec((tm, d), lambda i: (i, 0)),
        out_shape=jax.ShapeDtypeStruct((t, d), F32),
        compiler_params=_cparams(1),
        name="matmul_norm_res",
    )(a, w, x, g.reshape(1, d))


def _merge_out_kernel(ya_ref, yb_ref, yc_ref, m0_ref, m1_ref, m2_ref, x_ref,
                      wa_ref, wb_ref, wc_ref, wo_ref, g_ref, o_ref):
    merged = (_sigmoid(m0_ref[...]) * _dot(ya_ref[...], wa_ref[...])
              + _sigmoid(m1_ref[...]) * _dot(yb_ref[...], wb_ref[...])
              + _sigmoid(m2_ref[...]) * _dot(yc_ref[...], wc_ref[...]))
    y = _dot(merged.astype(BF16), wo_ref[...])
    o_ref[...] = x_ref[...] + _rms_scale(y) * g_ref[...]


def _merge_out(ya, yb, yc, proj, x, wa, wb, wc, wo, g, *, tm):
    t, d = x.shape
    w_in = ya.shape[1]
    mcol = COL_MERGE * LANE // d

    def row(i):
        return (i, 0)

    def const(i):
        return (0, 0)

    return pl.pallas_call(
        _merge_out_kernel,
        grid=(t // tm,),
        in_specs=[pl.BlockSpec((tm, w_in), row),
                  pl.BlockSpec((tm, w_in), row),
                  pl.BlockSpec((tm, w_in), row),
                  pl.BlockSpec((tm, d), lambda i: (i, mcol)),
                  pl.BlockSpec((tm, d), lambda i: (i, mcol + 1)),
                  pl.BlockSpec((tm, d), lambda i: (i, mcol + 2)),
                  pl.BlockSpec((tm, d), row),
                  pl.BlockSpec((w_in, d), const),
                  pl.BlockSpec((w_in, d), const),
                  pl.BlockSpec((w_in, d), const),
                  pl.BlockSpec((d, d), const),
                  pl.BlockSpec((1, d), const)],
        out_specs=pl.BlockSpec((tm, d), row),
        out_shape=jax.ShapeDtypeStruct((t, d), F32),
        compiler_params=_cparams(1),
        name="merge_out",
    )(ya, yb, yc, proj, proj, proj, x, wa, wb, wc, wo, g.reshape(1, d))


def _sb_kernel(q_ref, k_ref, v_ref, o_ref, run_scr, acc_scr, *, tq):
    qi = pl.program_id(1)
    tk = SB_TK
    n_diag = tq // tk
    dh = SB_DH
    n_pair = SB_HEADS * dh // LANE
    row = lax.broadcasted_iota(jnp.int32, (tq, tk), 0)
    col = lax.broadcasted_iota(jnp.int32, (tq, tk), 1)
    first_head = lax.broadcasted_iota(jnp.int32, (tk, LANE), 1) < dh
    r2 = lax.broadcasted_iota(jnp.int32, (2 * tk, 2 * tk), 0) & (tk - 1)
    c2 = lax.broadcasted_iota(jnp.int32, (2 * tk, 2 * tk), 1)
    later_ones = jnp.where((r2 > c2) | (c2 >= tk), 1.0, 0.0).astype(BF16)

    q_pairs = [(q_ref[:, p * LANE:(p + 1) * LANE] * (dh ** -0.5 * LOG2E)).astype(BF16)
               for p in range(n_pair)]

    run_scr[...] = jnp.zeros_like(run_scr)
    acc_scr[...] = jnp.zeros_like(acc_scr)

    def split_heads(x):
        zero = jnp.zeros_like(x)
        return jnp.concatenate([jnp.where(first_head, x, zero), jnp.where(first_head, zero, x)],
                               axis=0)

    def block(j, diag_offset):
        k0 = pl.multiple_of(j * tk, tk)
        pairs = range(n_pair)
        diagonal = diag_offset is not None
        if diagonal:
            causal = col + diag_offset < row
            causal2 = jnp.concatenate([causal, causal], axis=1)
        z = [_nt_dot(q_pairs[p],
                     split_heads(k_ref[pl.ds(k0, tk), p * LANE:(p + 1) * LANE].astype(BF16)))
             for p in pairs]
        log_sig, hi, lo = [], [], []
        for p in pairs:
            log_sig.append(jnp.minimum(z[p], 0.0) - jnp.log2(1.0 + jnp.exp2(-jnp.abs(z[p]))))
            log_keep = log_sig[p] - z[p]
            if diagonal:
                log_keep = jnp.where(causal2, log_keep, 0.0)
            hi.append(log_keep.astype(BF16))
            lo.append((log_keep - hi[p].astype(F32)).astype(BF16))
        sums = [[_dot(jnp.concatenate([hi[p][:, hh * tk:(hh + 1) * tk],
                                       lo[p][:, hh * tk:(hh + 1) * tk]], axis=1), later_ones)
                 for hh in range(2)] for p in pairs]
        for p in pairs:
            a = []
            for hh in range(2):
                h = 2 * p + hh
                run = run_scr[h]
                a_h = jnp.exp2(log_sig[p][:, hh * tk:(hh + 1) * tk] + sums[p][hh][:, :tk] + run)
                if diagonal:
                    a_h = jnp.where(causal, a_h, 0.0)
                run_scr[h] = run + sums[p][hh][:, tk:]
                a.append(a_h.astype(BF16))
            vp = v_ref[pl.ds(k0, tk), p * LANE:(p + 1) * LANE].astype(BF16)
            acc_scr[p] += _dot(jnp.concatenate(a, axis=1), split_heads(vp))

    for d in reversed(range(n_diag)):
        block(qi * n_diag + d, d * tk)

    def still_live():
        top = run_scr[0]
        for h in range(1, SB_HEADS):
            top = jnp.maximum(top, run_scr[h])
        return jnp.max(top) > SB_DEAD_LOG2

    def body(carry):
        it, _ = carry
        block(qi * n_diag - 1 - it, None)
        return it + 1, still_live()

    lax.while_loop(lambda c: (c[0] < qi * n_diag) & c[1], body, (jnp.int32(0), still_live()))
    o_ref[...] = jnp.concatenate([acc_scr[p] for p in range(n_pair)], axis=-1).astype(o_ref.dtype)


def _sb_attention(proj3, *, tq):
    b, s, _ = proj3.shape
    width = SB_HEADS * SB_DH
    n_pair = width // LANE
    return pl.pallas_call(
        functools.partial(_sb_kernel, tq=tq),
        grid=(b, s // tq),
        in_specs=[pl.BlockSpec((None, tq, width), lambda bi, i: (bi, i, COL_SQ * LANE // width)),
                  pl.BlockSpec((None, s, width), lambda bi, i: (bi, 0, COL_SK * LANE // width)),
                  pl.BlockSpec((None, s, width), lambda bi, i: (bi, 0, COL_SV * LANE // width))],
        out_specs=pl.BlockSpec((None, tq, width), lambda bi, i: (bi, i, 0)),
        out_shape=jax.ShapeDtypeStruct((b, s, width), BF16),
        scratch_shapes=[pltpu.VMEM((SB_HEADS, tq, LANE), F32), pltpu.VMEM((n_pair, tq, LANE), F32)],
        compiler_params=_cparams(2),
        name="sb_attention",
    )(proj3, proj3, proj3)


def _split_bf16(x):
    hi = x.astype(BF16)
    return hi, (x - hi.astype(F32)).astype(BF16)


def _gdn_prep_kernel(scal_ref, q_ref, qh_ref, k_ref, kh_ref, v_ref, vh_ref, ab_ref, cw_ref,
                     u_ref, w_ref, qe_ref, kd_ref, attn_ref, egl_ref, *, cb):
    i = pl.program_id(1)
    c_len = GDN_CHUNK
    d = GDN_D
    n_h = GDN_HEADS
    ts = cb * c_len
    halo = qh_ref.shape[0]
    width = n_h * d

    def conv_silu(x_ref, halo_ref, w):
        ext = jnp.concatenate([halo_ref[...] * jnp.where(i > 0, 1.0, 0.0), x_ref[...]], axis=0)
        y = jnp.zeros((ts, width), F32)
        for tap in range(GDN_CONV):
            off = halo - (GDN_CONV - 1) + tap
            y = y + w[tap:tap + 1, :] * ext[off:off + ts]
        return y * _sigmoid(y)

    qc = conv_silu(q_ref, qh_ref, cw_ref[:, 0:width])
    kc = conv_silu(k_ref, kh_ref, cw_ref[:, width:2 * width])
    vc = conv_silu(v_ref, vh_ref, cw_ref[:, 2 * width:3 * width])

    ab = ab_ref[...]
    lane = lax.broadcasted_iota(jnp.int32, (1, LANE), 1)
    a_log = jnp.zeros((1, LANE), F32)
    dt_bias = jnp.zeros((1, LANE), F32)
    for h in range(n_h):
        a_log = jnp.where(lane == h, scal_ref[0, h], a_log)
        dt_bias = jnp.where(lane == h, scal_ref[1, h], dt_bias)
    g_cum = -jnp.exp(a_log) * _softplus(ab + dt_bias)
    row_in_chunk = lax.broadcasted_iota(jnp.int32, (ts, LANE), 0) & (c_len - 1)
    shift = 1
    while shift < c_len:
        g_cum = g_cum + jnp.where(row_in_chunk >= shift, pltpu.roll(g_cum, shift, axis=0), 0.0)
        shift *= 2
    beta_all = _sigmoid(ab)

    row = lax.broadcasted_iota(jnp.int32, (c_len, c_len), 0)
    col = lax.broadcasted_iota(jnp.int32, (c_len, c_len), 1)
    incl = row >= col
    strict = row > col
    eye = jnp.where(row == col, 1.0, 0.0)

    problems = [(c, h) for c in range(cb) for h in range(n_h)]
    qn, kn, k_beta, v_beta, g_col, decay = {}, {}, {}, {}, {}, {}
    for c, h in problems:
        rows = slice(c * c_len, (c + 1) * c_len)
        cols = slice(h * d, (h + 1) * d)
        qh, kh = qc[rows, cols], kc[rows, cols]
        qn[c, h] = qh * lax.rsqrt(jnp.sum(qh * qh, axis=-1, keepdims=True) + EPS) * (d ** -0.5)
        kn[c, h] = kh * lax.rsqrt(jnp.sum(kh * kh, axis=-1, keepdims=True) + EPS)
        beta = beta_all[rows, n_h + h:n_h + h + 1]
        k_beta[c, h] = kn[c, h] * beta
        v_beta[c, h] = (vc[rows, cols] * beta).astype(BF16)
        g_col[c, h] = g_cum[rows, h:h + 1]
        g_sq = jnp.broadcast_to(g_col[c, h], (c_len, c_len))
        decay[c, h] = jnp.where(incl, jnp.exp(jnp.where(incl, g_sq - g_sq.T, 0.0)), 0.0)

    kn_b = {p: kn[p].astype(BF16) for p in problems}
    kkt = {p: _nt_dot(k_beta[p].astype(BF16), kn_b[p]) for p in problems}
    qkt = {p: _nt_dot(qn[p].astype(BF16), kn_b[p]) for p in problems}

    a_mat = {p: jnp.where(strict, kkt[p] * decay[p], 0.0) for p in problems}
    left = lax.broadcasted_iota(jnp.int32, (c_len, 2 * c_len), 1) < c_len
    pair = {p: jnp.concatenate([-a_mat[p], eye], axis=1) for p in problems}
    for _ in range(max(c_len - 1, 1).bit_length()):
        y = {p: _dot(pair[p][:, :c_len].astype(BF16), pair[p].astype(BF16)) for p in problems}
        pair = {p: jnp.where(left, y[p], pair[p] + y[p]) for p in problems}
    t_inv = {p: pair[p][:, c_len:] for p in problems}
    resid = {}
    for p in problems:
        m_hi, m_lo = _split_bf16(eye + a_mat[p])
        t_hi, t_lo = _split_bf16(t_inv[p])
        resid[p] = (eye - (_dot(m_hi, t_hi) + _dot(m_lo, t_hi) + _dot(m_hi, t_lo)), t_hi)
    t_b = {p: (t_inv[p] + _dot(resid[p][1], resid[p][0].astype(BF16))).astype(BF16)
           for p in problems}

    for c, h in problems:
        p = (c, h)
        rows = slice(c * c_len, (c + 1) * c_len)
        cols = slice(h * d, (h + 1) * d)
        exp_g = jnp.exp(g_col[p])
        g_last = g_col[p][c_len - 1:c_len, :]
        u_ref[rows, cols] = _dot(t_b[p], v_beta[p])
        w_ref[rows, cols] = _dot(t_b[p], (k_beta[p] * exp_g).astype(BF16)).astype(BF16)
        qe_ref[rows, cols] = (qn[p] * exp_g).astype(BF16)
        kd_ref[rows, cols] = (kn[p] * jnp.exp(g_last - g_col[p])).astype(BF16)
    egl_rows = [jnp.concatenate([jnp.broadcast_to(jnp.exp(g_col[c, h][c_len - 1:c_len, :]), (1, d))
                                 for h in range(n_h)], axis=1) for c in range(cb)]
    if egl_ref.shape[0] > cb:
        egl_rows.append(jnp.zeros((egl_ref.shape[0] - cb, width), F32))
    egl_ref[...] = jnp.concatenate(egl_rows, axis=0)
    for c in range(cb):
        rows = slice(c * c_len, (c + 1) * c_len)
        attn_ref[rows, :] = jnp.concatenate(
            [(qkt[c, h] * decay[c, h]).astype(BF16) for h in range(n_h)], axis=1)


def _gdn_scan_kernel(u_ref, w_ref, qe_ref, kd_ref, attn_ref, egl_ref, z_ref, ng_ref, o_ref,
                     state_scr, *, bb, ts, cb):
    j = pl.program_id(1)
    c_len = GDN_CHUNK
    d = GDN_D
    n_c = ts // c_len
    chains = [(b_, h) for b_ in range(bb) for h in range(GDN_HEADS)]

    @pl.when(j == 0)
    def _():
        state_scr[...] = jnp.zeros_like(state_scr)

    def chunk(c, carry):
        r0 = pl.multiple_of(c * c_len, c_len)
        rows = pl.ds(r0, c_len)
        cg = j * n_c + c
        state = {p: state_scr[p[0], p[1]] for p in chains}
        state_b = {p: state[p].astype(BF16) for p in chains}
        w_s = {(b_, h): _dot(w_ref[b_, rows, h * d:(h + 1) * d], state_b[b_, h]) for b_, h in chains}
        q_s = {(b_, h): _dot(qe_ref[b_, rows, h * d:(h + 1) * d], state_b[b_, h]) for b_, h in chains}
        v_new = {(b_, h): (u_ref[b_, rows, h * d:(h + 1) * d] - w_s[b_, h]).astype(BF16)
                 for b_, h in chains}
        sub = lax.broadcasted_iota(jnp.int32, egl_ref.shape[2:], 0)
        chunk_decay = [jnp.sum(jnp.where(sub == cg % cb, egl_ref[b_, cg // cb], 0.0),
                               axis=0, keepdims=True) for b_ in range(bb)]
        for b_, h in chains:
            cols = slice(h * d, (h + 1) * d)
            o = q_s[b_, h] + _dot(attn_ref[b_, rows, h * c_len:(h + 1) * c_len], v_new[b_, h])
            state_scr[b_, h] = (state[b_, h] * chunk_decay[b_][:, cols]
                                + _tn_dot(kd_ref[b_, rows, cols], v_new[b_, h]))
            zc = z_ref[b_, rows, cols]
            o_ref[b_, rows, cols] = (_rms_scale(o) * ng_ref[...] * (zc * _sigmoid(zc))
                                     ).astype(o_ref.dtype)
        return carry

    lax.fori_loop(0, n_c, chunk, 0)


def _gated_deltanet(proj3, conv_w, a_log, dt_bias, norm_g):
    b, s, _ = proj3.shape
    scal = jnp.stack([a_log, dt_bias]).astype(F32)
    width = GDN_HEADS * GDN_D
    n_chunks = s // GDN_CHUNK
    cb = min(GDN_PREP_CHUNKS, n_chunks)
    ts = cb * GDN_CHUNK
    halo = 8
    egl_rows = -(-cb // 8) * 8

    def main(col0):
        return pl.BlockSpec((None, ts, width), lambda bi, i: (bi, i, col0 * LANE // width))

    def before(col0):
        return pl.BlockSpec((None, halo, width),
                            lambda bi, i: (bi, jnp.maximum(i * (ts // halo) - 1, 0),
                                           col0 * LANE // width))

    def out_block(w_):
        return pl.BlockSpec((None, ts, w_), lambda bi, i: (bi, i, 0))

    u, w, qe, kd, attn, egl = pl.pallas_call(
        functools.partial(_gdn_prep_kernel, cb=cb),
        grid=(b, n_chunks // cb),
        in_specs=[pl.BlockSpec(memory_space=pltpu.SMEM),
                  main(COL_GQ), before(COL_GQ), main(COL_GK), before(COL_GK),
                  main(COL_GV), before(COL_GV),
                  pl.BlockSpec((None, ts, LANE), lambda bi, i: (bi, i, COL_GAB)),
                  pl.BlockSpec(conv_w.shape, lambda bi, i: (0, 0))],
        out_specs=[out_block(width), out_block(width), out_block(width), out_block(width),
                   out_block(GDN_HEADS * GDN_CHUNK),
                   pl.BlockSpec((None, None, egl_rows, width), lambda bi, i: (bi, i, 0, 0))],
        out_shape=[jax.ShapeDtypeStruct((b, s, width), F32),
                   jax.ShapeDtypeStruct((b, s, width), BF16),
                   jax.ShapeDtypeStruct((b, s, width), BF16),
                   jax.ShapeDtypeStruct((b, s, width), BF16),
                   jax.ShapeDtypeStruct((b, s, GDN_HEADS * GDN_CHUNK), BF16),
                   jax.ShapeDtypeStruct((b, n_chunks // cb, egl_rows, width), F32)],
        compiler_params=_cparams(2),
        name="gdn_prep",
    )(scal, proj3, proj3, proj3, proj3, proj3, proj3, proj3, conv_w)

    bb = 2 if b % 2 == 0 else 1
    t_scan = min(GDN_SCAN_TOKENS, s)

    def scan_block(w_):
        return pl.BlockSpec((bb, t_scan, w_), lambda bi, j: (bi, j, 0))

    return pl.pallas_call(
        functools.partial(_gdn_scan_kernel, bb=bb, ts=t_scan, cb=cb),
        grid=(b // bb, s // t_scan),
        in_specs=[scan_block(width), scan_block(width), scan_block(width), scan_block(width),
                  scan_block(GDN_HEADS * GDN_CHUNK),
                  pl.BlockSpec((bb, n_chunks // cb, egl_rows, width), lambda bi, j: (bi, 0, 0, 0)),
                  pl.BlockSpec((bb, t_scan, width), lambda bi, j: (bi, j, COL_GZ * LANE // width)),
                  pl.BlockSpec((1, GDN_D), lambda bi, j: (0, 0))],
        out_specs=scan_block(width),
        out_shape=jax.ShapeDtypeStruct((b, s, width), BF16),
        scratch_shapes=[pltpu.VMEM((bb, GDN_HEADS, GDN_D, GDN_D), F32)],
        compiler_params=_cparams(2),
        name="gdn_scan",
    )(u, w, qe, kd, attn, egl, proj3, norm_g.reshape(1, GDN_D))


def _nsa_kernel(q_ref, kcmp_ref, vcmp_ref, kslc_ref, vslc_ref, kwin_ref, vwin_ref, gate_ref,
                wk_ref, wv_ref, pk_ref, pv_ref, o_ref,
                kc_scr, vct_scr, vst_scr, vwt_scr, selk_scr, *, seq, tq):
    qi = pl.program_id(1)
    tk = NSA_TK
    dh = NSA_DH
    hg = NSA_GROUP
    n_grp = NSA_KV_HEADS
    nq = hg * tq
    nb = seq // CMP_STRIDE
    n_sel = seq // SEL_LEN
    n_kb = seq // tk
    top_k = min(SEL_TOPK, n_sel)
    half = CMP_LEN // 2
    slc_step = min(NSA_SLC_BLOCKS, n_kb)
    n_win = WINDOW // tk + 1

    @pl.when(qi == 0)
    def _():
        cmp_out = []
        for src, pos, w_ref in ((kcmp_ref, pk_ref, wk_ref), (vcmp_ref, pv_ref, wv_ref)):
            lo_parts, hi_parts = [], []
            for l in range(half):
                rows = src[pl.ds(l, nb, stride=CMP_STRIDE), :]
                lo_parts.append(rows + pos[l:l + 1, :])
                hi_parts.append(rows + pos[half + l:half + l + 1, :])
            a_lo = _dot(jnp.concatenate(lo_parts, axis=1), w_ref[0], HIGHEST)
            a_hi = _dot(jnp.concatenate(hi_parts, axis=1), w_ref[1], HIGHEST)
            a_hi = jnp.concatenate([a_hi[1:], jnp.zeros((1, LANE), F32)], axis=0)
            cmp_out.append(a_lo + a_hi)
        kc_scr[...] = cmp_out[0]
        vct_scr[...] = cmp_out[1].T.astype(BF16)

        def transpose_values(kb, carry):
            k0 = pl.multiple_of(kb * tk, tk)
            vst_scr[kb] = vslc_ref[pl.ds(k0, tk), :].T.astype(BF16)
            vwt_scr[kb] = vwin_ref[pl.ds(k0, tk), :].T.astype(BF16)
            return carry

        lax.fori_loop(0, n_kb, transpose_values, 0)

    lane = lax.broadcasted_iota(jnp.int32, (tq, LANE), 1)
    qg = []
    for g in range(n_grp):
        parts = []
        for i in range(hg):
            hd = g * hg + i
            pair = q_ref[:, (hd // 2) * LANE:(hd // 2 + 1) * LANE] * (dh ** -0.5)
            if hd % 2 != g:
                pair = pltpu.roll(pair, dh, axis=1)
            parts.append(jnp.where((lane >= g * dh) & (lane < (g + 1) * dh), pair, 0.0))
        qg.append(jnp.concatenate(parts, axis=0))

    t_row = qi * tq + lax.broadcasted_iota(jnp.int32, (1, tq), 1)
    t_row4 = jnp.concatenate([t_row] * hg, axis=1)
    key_iota = lax.broadcasted_iota(jnp.int32, (tk, 1), 0)

    j_sel = lax.broadcasted_iota(jnp.int32, (n_sel, tq), 0)
    cur = lax.shift_right_logical(t_row, SEL_LEN.bit_length() - 1)
    sel_valid = j_sel <= cur
    forced = (j_sel == 0) | (j_sel == cur) | (j_sel == cur - 1)
    ov_j = lax.broadcasted_iota(jnp.int32, (n_sel, nb), 0) * SEL_LEN
    ov_n = lax.broadcasted_iota(jnp.int32, (n_sel, nb), 1) * CMP_STRIDE
    overlap = jnp.where((ov_n < ov_j + SEL_LEN) & (ov_n + CMP_LEN > ov_j), 1.0, 0.0).astype(BF16)
    n_idx = lax.broadcasted_iota(jnp.int32, (nb, 1), 0)
    cmp_valid = (n_idx * CMP_STRIDE + CMP_LEN - 1 <= t_row4) & (n_idx < nb - 1)

    def mask_heads(ok, s):
        return jnp.where(jnp.concatenate([ok] * hg, axis=1), s, MASK_NEG)

    groups = range(n_grp)
    kc_hi, kc_lo = _split_bf16(kc_scr[...])
    q_split = [_split_bf16(qg[g]) for g in groups]
    q_b = [q_split[g][0] for g in groups]

    win_blocks = []
    for w in range(n_win):
        kb_int = qi - w
        kb = jnp.maximum(kb_int, 0)
        kpos = kb_int * tk + key_iota
        ok = (kpos <= t_row) & (kpos > t_row - WINDOW) & (kpos >= 0)
        k_blk = kwin_ref[pl.ds(pl.multiple_of(kb * tk, tk), tk), :].astype(BF16)
        win_blocks.append((kb, k_blk, ok))

    s_cmp = [_nt_dot(kc_hi, q_split[g][0]) + _nt_dot(kc_lo, q_split[g][0])
             + _nt_dot(kc_hi, q_split[g][1]) for g in groups]
    s_win = [jnp.concatenate([mask_heads(ok, _nt_dot(k_blk, q_b[g]))
                              for (_, k_blk, ok) in win_blocks], axis=0) for g in groups]

    p_cmp, p_win, l_win = [], [], []
    for g in groups:
        s_m = jnp.where(cmp_valid, s_cmp[g], MASK_NEG)
        e = jnp.where(cmp_valid, jnp.exp(s_m - jnp.max(s_m, axis=0, keepdims=True)), 0.0)
        den = jnp.sum(e, axis=0, keepdims=True)
        p_cmp.append(jnp.where(den > 0.0, e / jnp.where(den > 0.0, den, 1.0), 0.0))
    for g in groups:
        p = jnp.exp(s_win[g] - jnp.max(s_win[g], axis=0, keepdims=True))
        l_win.append(jnp.sum(p, axis=0, keepdims=True))
        p_win.append(p.astype(BF16))

    o_cmp = [_dot(vct_scr[g * dh:(g + 1) * dh, :], p_cmp[g].astype(BF16)) for g in groups]
    imp = []
    for g in groups:
        p_sum = p_cmp[g][:, 0:tq]
        for i in range(1, hg):
            p_sum = p_sum + p_cmp[g][:, i * tq:(i + 1) * tq]
        ps_hi, ps_lo = _split_bf16(p_sum)
        imp.append(_dot(overlap, ps_hi) + _dot(overlap, ps_lo))
    o_win = []
    for g in groups:
        v_t = jnp.concatenate([vwt_scr[kb, g * dh:(g + 1) * dh, :] for (kb, _, _) in win_blocks],
                              axis=1)
        o_win.append(_dot(v_t, p_win[g]) / l_win[g])

    for g in groups:
        score = jnp.where(sel_valid, imp[g] + jnp.where(forced, FORCE_BONUS, 0.0), MASK_NEG)
        beaten = jnp.zeros((n_sel, tq), F32)
        for jp in range(n_sel):
            other = score[jp:jp + 1, :]
            ge = jnp.where(other >= score, 1.0, 0.0)
            gt = jnp.where(other > score, 1.0, 0.0)
            beaten = beaten + jnp.where(j_sel > jp, ge, gt)
        picked = jnp.where(beaten < top_k, 1.0, 0.0)
        for j in range(n_sel):
            selk_scr[g, j * SEL_LEN:(j + 1) * SEL_LEN, :] = jnp.broadcast_to(
                picked[j:j + 1, :], (SEL_LEN, tq))

    def slc_body(it, carry):
        kb0 = it * slc_step
        k_blocks = [kslc_ref[pl.ds(pl.multiple_of((kb0 + u) * tk, tk), tk), :].astype(BF16)
                    for u in range(slc_step)]
        tiles = {}
        for g in range(n_grp):
            for u in range(slc_step):
                k0 = pl.multiple_of((kb0 + u) * tk, tk)
                ok = (selk_scr[g, pl.ds(k0, tk), :] > 0.5) & (k0 + key_iota <= t_row)
                tiles[g, u] = mask_heads(ok, _nt_dot(k_blocks[u], q_b[g]))
        new = []
        for g in range(n_grp):
            m, l, acc = carry[g]
            for u in range(slc_step):
                s = tiles[g, u]
                m_new = jnp.maximum(m, jnp.max(s, axis=0, keepdims=True))
                alpha = jnp.exp(m - m_new)
                p = jnp.exp(s - m_new)
                l = alpha * l + jnp.sum(p, axis=0, keepdims=True)
                acc = alpha * acc + _dot(vst_scr[kb0 + u, g * dh:(g + 1) * dh, :], p.astype(BF16))
                m = m_new
            new.append((m, l, acc))
        return tuple(new)

    init = tuple((jnp.full((1, nq), MASK_NEG, F32), jnp.zeros((1, nq), F32),
                  jnp.zeros((dh, nq), F32)) for _ in range(n_grp))
    span = slc_step * tk
    n_iter = ((qi + 1) * tq + span - 1) // span
    slc = lax.fori_loop(0, n_iter, slc_body, init)
    o_slc = [acc / l for (_, l, acc) in slc]

    gates = _sigmoid(gate_ref[...].T)
    outs = []
    for g in range(n_grp):
        for i in range(hg):
            hd = g * hg + i
            cols = slice(i * tq, (i + 1) * tq)
            outs.append(gates[3 * hd:3 * hd + 1, :] * o_cmp[g][:, cols]
                        + gates[3 * hd + 1:3 * hd + 2, :] * o_slc[g][:, cols]
                        + gates[3 * hd + 2:3 * hd + 3, :] * o_win[g][:, cols])
    o_ref[...] = jnp.concatenate(outs, axis=0).T.astype(o_ref.dtype)


def _cmp_weight(w):
    half = CMP_LEN // 2
    w = w.reshape(2, half, NSA_DH, NSA_DH)
    z = jnp.zeros_like(w)
    big = jnp.stack([jnp.concatenate([w, z], axis=-1), jnp.concatenate([z, w], axis=-1)], axis=2)
    return big.reshape(2, half * NSA_KV_HEADS * NSA_DH, NSA_KV_HEADS * NSA_DH)


def _native_sparse_attention(proj3, cmp_wk, cmp_pk, cmp_wv, cmp_pv, *, tq):
    b, s, _ = proj3.shape
    nb = s // CMP_STRIDE
    n_kb = s // NSA_TK
    assert tq == NSA_TK and n_kb % min(NSA_SLC_BLOCKS, n_kb) == 0
    wk = _cmp_weight(cmp_wk)
    wv = _cmp_weight(cmp_wv)
    pk = jnp.tile(cmp_pk, (1, NSA_KV_HEADS))
    pv = jnp.tile(cmp_pv, (1, NSA_KV_HEADS))
    qw = NSA_HEADS * NSA_DH

    def seq_block(col):
        return pl.BlockSpec((None, s, LANE), lambda bi, i: (bi, 0, col))

    def whole(a):
        return pl.BlockSpec(a.shape, lambda bi, i: (0,) * a.ndim)

    return pl.pallas_call(
        functools.partial(_nsa_kernel, seq=s, tq=tq),
        grid=(b, s // tq),
        in_specs=[pl.BlockSpec((None, tq, qw), lambda bi, i: (bi, i, COL_NQ * LANE // qw))]
        + [seq_block(COL_NKV + c) for c in range(6)]
        + [pl.BlockSpec((None, tq, LANE), lambda bi, i: (bi, i, COL_NGATE)),
           whole(wk), whole(wv), whole(pk), whole(pv)],
        out_specs=pl.BlockSpec((None, tq, qw), lambda bi, i: (bi, i, 0)),
        out_shape=jax.ShapeDtypeStruct((b, s, qw), BF16),
        scratch_shapes=[pltpu.VMEM((nb, LANE), F32),
                        pltpu.VMEM((LANE, nb), BF16),
                        pltpu.VMEM((n_kb, LANE, NSA_TK), BF16),
                        pltpu.VMEM((n_kb, LANE, NSA_TK), BF16),
                        pltpu.VMEM((NSA_KV_HEADS, s, tq), F32)],
        compiler_params=_cparams(2),
        name="native_sparse_attention",
    )(proj3, proj3, proj3, proj3, proj3, proj3, proj3, proj3, wk, wv, pk, pv)


def _pad_cols(w, width):
    return jnp.pad(w, ((0, 0), (0, width - w.shape[1])))


def _proj_weight(w_in):
    gqk_v_z = 4 * GDN_HEADS * GDN_D
    n_ab = 2 * GDN_HEADS
    n_nsa = (NSA_HEADS + 6 * NSA_KV_HEADS) * NSA_DH
    n_gate = 3 * NSA_HEADS
    o_ab = gqk_v_z
    o_nsa = o_ab + n_ab
    o_gate = o_nsa + n_nsa
    o_rest = o_gate + n_gate
    w = jnp.concatenate([w_in[:, :o_ab], w_in[:, o_nsa:o_gate],
                         _pad_cols(w_in[:, o_ab:o_nsa], LANE),
                         _pad_cols(w_in[:, o_gate:o_rest], LANE),
                         w_in[:, o_rest:]], axis=1)
    assert w.shape[1] == N_PROJ, w.shape
    return w.astype(BF16)


def _layer(x, p, *, tiles):
    b, s, d = x.shape
    x2 = x.reshape(b * s, d)
    proj = _norm_matmul(x2, p["g_mix_pre"], _proj_weight(p["w_in"]), relu2=False, out_dtype=F32,
                        tm=tiles["tm"], tn=tiles["tn"])
    proj3 = proj.reshape(b, s, N_PROJ)
    ya = _gated_deltanet(proj3, p["gdn_conv_w"], p["gdn_a_log"], p["gdn_dt_bias"], p["gdn_norm_g"])
    yb = _native_sparse_attention(proj3, p["nsa_cmp_wk"], p["nsa_cmp_pk"], p["nsa_cmp_wv"],
                                  p["nsa_cmp_pv"], tq=tiles["tq"])
    yc = _sb_attention(proj3, tq=tiles["tq_sb"])
    x2 = _merge_out(ya.reshape(b * s, -1), yb.reshape(b * s, -1), yc.reshape(b * s, -1), proj, x2,
                    p["w_br_gdn"].astype(BF16), p["w_br_nsa"].astype(BF16),
                    p["w_br_sb"].astype(BF16), p["w_out"].astype(BF16), p["g_mix_post"],
                    tm=tiles["tm_out"])
    act = _norm_matmul(x2, p["g_ff_pre"], p["w_ff1"].astype(BF16), relu2=True, out_dtype=BF16,
                       tm=tiles["tm"], tn=tiles["tn"])
    x2 = _matmul_norm_res(act, p["w_ff2"].astype(BF16), x2, p["g_ff_post"], tm=tiles["tm_out"])
    return x2.reshape(b, s, d)


def _tiles(b, s):
    t = b * s
    return {"tm": min(1024, t), "tn": 1024, "tm_out": min(512, t), "tq": min(128, s),
            "tq_sb": min(256, s)}


def kernel(x, g_mix_pre, g_mix_post, g_ff_pre, g_ff_post, w_in, gdn_conv_w, gdn_a_log, gdn_dt_bias,
           gdn_norm_g, nsa_cmp_wk, nsa_cmp_pk, nsa_cmp_wv, nsa_cmp_pv, w_br_gdn, w_br_nsa, w_br_sb,
           w_out, w_ff1, w_ff2):
    params = dict(g_mix_pre=g_mix_pre, g_mix_post=g_mix_post, g_ff_pre=g_ff_pre, g_ff_post=g_ff_post,
                  w_in=w_in, gdn_conv_w=gdn_conv_w, gdn_a_log=gdn_a_log, gdn_dt_bias=gdn_dt_bias,
                  gdn_norm_g=gdn_norm_g, nsa_cmp_wk=nsa_cmp_wk, nsa_cmp_pk=nsa_cmp_pk,
                  nsa_cmp_wv=nsa_cmp_wv, nsa_cmp_pv=nsa_cmp_pv, w_br_gdn=w_br_gdn,
                  w_br_nsa=w_br_nsa, w_br_sb=w_br_sb, w_out=w_out, w_ff1=w_ff1, w_ff2=w_ff2)
    tiles = _tiles(x.shape[0], x.shape[1])
    for layer in range(w_in.shape[0]):
        x = _layer(x, {k: v[layer] for k, v in params.items()}, tiles=tiles)
    return x
```

```python
import functools

import jax
import jax.numpy as jnp
from jax import lax
from jax.experimental import pallas as pl
from jax.experimental.pallas import tpu as pltpu

F32 = jnp.float32
BF16 = jnp.bfloat16
HIGHEST = lax.Precision.HIGHEST

EPS = 1e-6
MASK_NEG = -1e30
LOG2E = 1.4426950408889634

GDN_HEADS = 4
GDN_D = 128
GDN_CONV = 4
GDN_CHUNK = 64
GDN_PREP_CHUNKS = 4
GDN_SCAN_TOKENS = 512

NSA_HEADS = 8
NSA_KV_HEADS = 2
NSA_GROUP = NSA_HEADS // NSA_KV_HEADS
NSA_DH = 64
CMP_LEN = 32
CMP_STRIDE = 16
SEL_LEN = 64
SEL_TOPK = 8
WINDOW = 512
FORCE_BONUS = 1e3
NSA_TK = 128
NSA_SLC_BLOCKS = 4

SB_HEADS = 8
SB_DH = 64
SB_TK = 128
SB_DEAD_LOG2 = -160.0

LANE = 128

COL_GQ, COL_GK, COL_GV, COL_GZ = 0, 4, 8, 12
COL_NQ, COL_NCMP, COL_GAB, COL_NGATE = 16, 20, 22, 23
COL_MERGE = 24
N_PROJ32 = 48 * LANE
COL16_NKV, COL16_SQ, COL16_SK, COL16_SV = 0, 4, 8, 12
N_PROJ16 = 16 * LANE

VMEM_LIMIT = 48 * 1024 * 1024


def _cparams(n_axes):
    return pltpu.CompilerParams(dimension_semantics=("arbitrary",) * n_axes,
                                vmem_limit_bytes=VMEM_LIMIT)


def _nt_dot(a, b, precision=None):
    return lax.dot_general(a, b, (((1,), (1,)), ((), ())), precision=precision,
                           preferred_element_type=F32)


def _tn_dot(a, b):
    return lax.dot_general(a, b, (((0,), (0,)), ((), ())), preferred_element_type=F32)


def _dot(a, b, precision=None):
    return jnp.dot(a, b, precision=precision, preferred_element_type=F32)


def _rms_scale(y):
    return y * lax.rsqrt(jnp.mean(y * y, axis=-1, keepdims=True) + EPS)


def _softplus(z):
    return jnp.maximum(z, 0.0) + jnp.log1p(jnp.exp(-jnp.abs(z)))


def _sigmoid(z):
    return jax.nn.sigmoid(z)


def _norm_matmul_kernel(x_ref, g_ref, w_ref, o32_ref, o16_ref, h_scr, *, n32):
    j = pl.program_id(1)

    @pl.when(j == 0)
    def _():
        h_scr[...] = (_rms_scale(x_ref[...]) * g_ref[...]).astype(BF16)

    y = _dot(h_scr[...], w_ref[...])

    @pl.when(j < n32)
    def _():
        o32_ref[...] = y

    @pl.when(j >= n32)
    def _():
        o16_ref[...] = y.astype(BF16)


def _norm_matmul(x, g, w, *, tm, tn):
    t, d = x.shape
    n32, n16 = N_PROJ32 // tn, N_PROJ16 // tn
    return pl.pallas_call(
        functools.partial(_norm_matmul_kernel, n32=n32),
        grid=(t // tm, n32 + n16),
        in_specs=[pl.BlockSpec((tm, d), lambda i, j: (i, 0)),
                  pl.BlockSpec((1, d), lambda i, j: (0, 0)),
                  pl.BlockSpec((d, tn), lambda i, j: (0, j))],
        out_specs=[pl.BlockSpec((tm, tn), lambda i, j: (i, jnp.minimum(j, n32 - 1))),
                   pl.BlockSpec((tm, tn), lambda i, j: (i, jnp.maximum(j - n32, 0)))],
        out_shape=[jax.ShapeDtypeStruct((t, N_PROJ32), F32),
                   jax.ShapeDtypeStruct((t, N_PROJ16), BF16)],
        scratch_shapes=[pltpu.VMEM((tm, d), BF16)],
        compiler_params=_cparams(2),
        name="norm_matmul",
    )(x, g.reshape(1, d), w)


def _mlp_kernel(x_ref, g_pre_ref, w1_ref, w2_ref, g_post_ref, o_ref, h_scr, acc_scr):
    j = pl.program_id(1)

    @pl.when(j == 0)
    def _():
        h_scr[...] = (_rms_scale(x_ref[...]) * g_pre_ref[...]).astype(BF16)

    act = jnp.square(jnp.maximum(_dot(h_scr[...], w1_ref[...]), 0.0)).astype(BF16)
    part = _dot(act, w2_ref[...])

    @pl.when(j == 0)
    def _():
        acc_scr[...] = part

    @pl.when(j > 0)
    def _():
        acc_scr[...] += part

    @pl.when(j == pl.num_programs(1) - 1)
    def _():
        o_ref[...] = x_ref[...] + _rms_scale(acc_scr[...]) * g_post_ref[...]


def _mlp(x, g_pre, w1, w2, g_post, *, tm, tn):
    t, d = x.shape
    d_ff = w1.shape[1]
    return pl.pallas_call(
        _mlp_kernel,
        grid=(t // tm, d_ff // tn),
        in_specs=[pl.BlockSpec((tm, d), lambda i, j: (i, 0)),
                  pl.BlockSpec((1, d), lambda i, j: (0, 0)),
                  pl.BlockSpec((d, tn), lambda i, j: (0, j)),
                  pl.BlockSpec((tn, d), lambda i, j: (j, 0)),
                  pl.BlockSpec((1, d), lambda i, j: (0, 0))],
        out_specs=pl.BlockSpec((tm, d), lambda i, j: (i, 0)),
        out_shape=jax.ShapeDtypeStruct((t, d), F32),
        scratch_shapes=[pltpu.VMEM((tm, d), BF16), pltpu.VMEM((tm, d), F32)],
        compiler_params=_cparams(2),
        name="mlp",
    )(x, g_pre.reshape(1, d), w1, w2, g_post.reshape(1, d))


def _merge_out_kernel(ya_ref, yb_ref, yc_ref, m0_ref, m1_ref, m2_ref, x_ref,
                      wa_ref, wb_ref, wc_ref, wo_ref, g_ref, o_ref):
    merged = (_sigmoid(m0_ref[...]) * _dot(ya_ref[...], wa_ref[...])
              + _sigmoid(m1_ref[...]) * _dot(yb_ref[...], wb_ref[...])
              + _sigmoid(m2_ref[...]) * _dot(yc_ref[...], wc_ref[...]))
    y = _dot(merged.astype(BF16), wo_ref[...])
    o_ref[...] = x_ref[...] + _rms_scale(y) * g_ref[...]


def _merge_out(ya, yb, yc, proj, x, wa, wb, wc, wo, g, *, tm):
    t, d = x.shape
    w_in = ya.shape[1]
    mcol = COL_MERGE * LANE // d

    def row(i):
        return (i, 0)

    def const(i):
        return (0, 0)

    return pl.pallas_call(
        _merge_out_kernel,
        grid=(t // tm,),
        in_specs=[pl.BlockSpec((tm, w_in), row),
                  pl.BlockSpec((tm, w_in), row),
                  pl.BlockSpec((tm, w_in), row),
                  pl.BlockSpec((tm, d), lambda i: (i, mcol)),
                  pl.BlockSpec((tm, d), lambda i: (i, mcol + 1)),
                  pl.BlockSpec((tm, d), lambda i: (i, mcol + 2)),
                  pl.BlockSpec((tm, d), row),
                  pl.BlockSpec((w_in, d), const),
                  pl.BlockSpec((w_in, d), const),
                  pl.BlockSpec((w_in, d), const),
                  pl.BlockSpec((d, d), const),
                  pl.BlockSpec((1, d), const)],
        out_specs=pl.BlockSpec((tm, d), row),
        out_shape=jax.ShapeDtypeStruct((t, d), F32),
        compiler_params=_cparams(1),
        name="merge_out",
    )(ya, yb, yc, proj, proj, proj, x, wa, wb, wc, wo, g.reshape(1, d))


def _sb_kernel(q_ref, k_ref, v_ref, o_ref, run_scr, acc_scr, *, tq):
    qi = pl.program_id(1)
    tk = SB_TK
    n_diag = tq // tk
    dh = SB_DH
    n_pair = SB_HEADS * dh // LANE
    row = lax.broadcasted_iota(jnp.int32, (tq, tk), 0)
    col = lax.broadcasted_iota(jnp.int32, (tq, tk), 1)
    first_head = lax.broadcasted_iota(jnp.int32, (tk, LANE), 1) < dh
    r2 = lax.broadcasted_iota(jnp.int32, (2 * tk, 2 * tk), 0) & (tk - 1)
    c2 = lax.broadcasted_iota(jnp.int32, (2 * tk, 2 * tk), 1)
    later_ones = jnp.where((r2 > c2) | (c2 >= tk), 1.0, 0.0).astype(BF16)

    q_pairs = [(q_ref[:, p * LANE:(p + 1) * LANE].astype(F32) * (dh ** -0.5 * LOG2E)).astype(BF16)
               for p in range(n_pair)]

    run_scr[...] = jnp.zeros_like(run_scr)
    acc_scr[...] = jnp.zeros_like(acc_scr)

    def split_heads(x):
        zero = jnp.zeros_like(x)
        return jnp.concatenate([jnp.where(first_head, x, zero), jnp.where(first_head, zero, x)],
                               axis=0)

    def block(j, diag_offset):
        k0 = pl.multiple_of(j * tk, tk)
        pairs = range(n_pair)
        diagonal = diag_offset is not None
        if diagonal:
            causal = col + diag_offset < row
            causal2 = jnp.concatenate([causal, causal], axis=1)
        z = [_nt_dot(q_pairs[p],
                     split_heads(k_ref[pl.ds(k0, tk), p * LANE:(p + 1) * LANE].astype(BF16)))
             for p in pairs]
        log_sig, hi, lo = [], [], []
        for p in pairs:
            log_sig.append(jnp.minimum(z[p], 0.0) - jnp.log2(1.0 + jnp.exp2(-jnp.abs(z[p]))))
            log_keep = log_sig[p] - z[p]
            if diagonal:
                log_keep = jnp.where(causal2, log_keep, 0.0)
            hi.append(log_keep.astype(BF16))
            lo.append((log_keep - hi[p].astype(F32)).astype(BF16))
        sums = [[_dot(jnp.concatenate([hi[p][:, hh * tk:(hh + 1) * tk],
                                       lo[p][:, hh * tk:(hh + 1) * tk]], axis=1), later_ones)
                 for hh in range(2)] for p in pairs]
        for p in pairs:
            a = []
            for hh in range(2):
                h = 2 * p + hh
                run = run_scr[h]
                a_h = jnp.exp2(log_sig[p][:, hh * tk:(hh + 1) * tk] + sums[p][hh][:, :tk] + run)
                if diagonal:
                    a_h = jnp.where(causal, a_h, 0.0)
                run_scr[h] = run + sums[p][hh][:, tk:]
                a.append(a_h.astype(BF16))
            vp = v_ref[pl.ds(k0, tk), p * LANE:(p + 1) * LANE].astype(BF16)
            acc_scr[p] += _dot(jnp.concatenate(a, axis=1), split_heads(vp))

    for d in reversed(range(n_diag)):
        block(qi * n_diag + d, d * tk)

    def still_live():
        top = run_scr[0]
        for h in range(1, SB_HEADS):
            top = jnp.maximum(top, run_scr[h])
        return jnp.max(top) > SB_DEAD_LOG2

    def body(carry):
        it, _ = carry
        block(qi * n_diag - 1 - it, None)
        return it + 1, still_live()

    lax.while_loop(lambda c: (c[0] < qi * n_diag) & c[1], body, (jnp.int32(0), still_live()))
    o_ref[...] = jnp.concatenate([acc_scr[p] for p in range(n_pair)], axis=-1).astype(o_ref.dtype)


def _sb_attention(proj16, *, tq):
    b, s, _ = proj16.shape
    width = SB_HEADS * SB_DH
    n_pair = width // LANE
    return pl.pallas_call(
        functools.partial(_sb_kernel, tq=tq),
        grid=(b, s // tq),
        in_specs=[pl.BlockSpec((None, tq, width), lambda bi, i: (bi, i, COL16_SQ * LANE // width)),
                  pl.BlockSpec((None, s, width), lambda bi, i: (bi, 0, COL16_SK * LANE // width)),
                  pl.BlockSpec((None, s, width), lambda bi, i: (bi, 0, COL16_SV * LANE // width))],
        out_specs=pl.BlockSpec((None, tq, width), lambda bi, i: (bi, i, 0)),
        out_shape=jax.ShapeDtypeStruct((b, s, width), BF16),
        scratch_shapes=[pltpu.VMEM((SB_HEADS, tq, LANE), F32), pltpu.VMEM((n_pair, tq, LANE), F32)],
        compiler_params=_cparams(2),
        name="sb_attention",
    )(proj16, proj16, proj16)


def _split_bf16(x):
    hi = x.astype(BF16)
    return hi, (x - hi.astype(F32)).astype(BF16)


def _gdn_prep_kernel(scal_ref, q_ref, qh_ref, k_ref, kh_ref, v_ref, vh_ref, ab_ref, cw_ref,
                     u_ref, w_ref, qe_ref, kd_ref, attn_ref, egl_ref, *, cb):
    i = pl.program_id(1)
    c_len = GDN_CHUNK
    d = GDN_D
    n_h = GDN_HEADS
    ts = cb * c_len
    halo = qh_ref.shape[0]
    width = n_h * d

    def conv_silu(x_ref, halo_ref, w):
        ext = jnp.concatenate([halo_ref[...] * jnp.where(i > 0, 1.0, 0.0), x_ref[...]], axis=0)
        y = jnp.zeros((ts, width), F32)
        for tap in range(GDN_CONV):
            off = halo - (GDN_CONV - 1) + tap
            y = y + w[tap:tap + 1, :] * ext[off:off + ts]
        return y * _sigmoid(y)

    qc = conv_silu(q_ref, qh_ref, cw_ref[:, 0:width])
    kc = conv_silu(k_ref, kh_ref, cw_ref[:, width:2 * width])
    vc = conv_silu(v_ref, vh_ref, cw_ref[:, 2 * width:3 * width])

    ab = ab_ref[...]
    lane = lax.broadcasted_iota(jnp.int32, (1, LANE), 1)
    a_log = jnp.zeros((1, LANE), F32)
    dt_bias = jnp.zeros((1, LANE), F32)
    for h in range(n_h):
        a_log = jnp.where(lane == h, scal_ref[0, h], a_log)
        dt_bias = jnp.where(lane == h, scal_ref[1, h], dt_bias)
    g_cum = -jnp.exp(a_log) * _softplus(ab + dt_bias)
    row_in_chunk = lax.broadcasted_iota(jnp.int32, (ts, LANE), 0) & (c_len - 1)
    shift = 1
    while shift < c_len:
        g_cum = g_cum + jnp.where(row_in_chunk >= shift, pltpu.roll(g_cum, shift, axis=0), 0.0)
        shift *= 2
    beta_all = _sigmoid(ab)

    row = lax.broadcasted_iota(jnp.int32, (c_len, c_len), 0)
    col = lax.broadcasted_iota(jnp.int32, (c_len, c_len), 1)
    incl = row >= col
    strict = row > col
    eye = jnp.where(row == col, 1.0, 0.0)

    problems = [(c, h) for c in range(cb) for h in range(n_h)]
    qn, kn, k_beta, v_beta, g_col, decay = {}, {}, {}, {}, {}, {}
    for c, h in problems:
        rows = slice(c * c_len, (c + 1) * c_len)
        cols = slice(h * d, (h + 1) * d)
        qh, kh = qc[rows, cols], kc[rows, cols]
        qn[c, h] = qh * lax.rsqrt(jnp.sum(qh * qh, axis=-1, keepdims=True) + EPS) * (d ** -0.5)
        kn[c, h] = kh * lax.rsqrt(jnp.sum(kh * kh, axis=-1, keepdims=True) + EPS)
        beta = beta_all[rows, n_h + h:n_h + h + 1]
        k_beta[c, h] = kn[c, h] * beta
        v_beta[c, h] = (vc[rows, cols] * beta).astype(BF16)
        g_col[c, h] = g_cum[rows, h:h + 1]
        g_sq = jnp.broadcast_to(g_col[c, h], (c_len, c_len))
        decay[c, h] = jnp.where(incl, jnp.exp(jnp.where(incl, g_sq - g_sq.T, 0.0)), 0.0)

    kn_b = {p: kn[p].astype(BF16) for p in problems}
    kkt = {p: _nt_dot(k_beta[p].astype(BF16), kn_b[p]) for p in problems}
    qkt = {p: _nt_dot(qn[p].astype(BF16), kn_b[p]) for p in problems}

    a_mat = {p: jnp.where(strict, kkt[p] * decay[p], 0.0) for p in problems}
    left = lax.broadcasted_iota(jnp.int32, (c_len, 2 * c_len), 1) < c_len
    pair = {p: jnp.concatenate([-a_mat[p], eye], axis=1) for p in problems}
    for _ in range(max(c_len - 1, 1).bit_length()):
        y = {p: _dot(pair[p][:, :c_len].astype(BF16), pair[p].astype(BF16)) for p in problems}
        pair = {p: jnp.where(left, y[p], pair[p] + y[p]) for p in problems}
    t_inv = {p: pair[p][:, c_len:] for p in problems}
    resid = {}
    for p in problems:
        m_hi, m_lo = _split_bf16(eye + a_mat[p])
        t_hi, t_lo = _split_bf16(t_inv[p])
        resid[p] = (eye - (_dot(m_hi, t_hi) + _dot(m_lo, t_hi) + _dot(m_hi, t_lo)), t_hi)
    t_b = {p: (t_inv[p] + _dot(resid[p][1], resid[p][0].astype(BF16))).astype(BF16)
           for p in problems}

    for c, h in problems:
        p = (c, h)
        rows = slice(c * c_len, (c + 1) * c_len)
        cols = slice(h * d, (h + 1) * d)
        exp_g = jnp.exp(g_col[p])
        g_last = g_col[p][c_len - 1:c_len, :]
        u_ref[rows, cols] = _dot(t_b[p], v_beta[p])
        w_ref[rows, cols] = _dot(t_b[p], (k_beta[p] * exp_g).astype(BF16)).astype(BF16)
        qe_ref[rows, cols] = (qn[p] * exp_g).astype(BF16)
        kd_ref[rows, cols] = (kn[p] * jnp.exp(g_last - g_col[p])).astype(BF16)
    egl_rows = [jnp.concatenate([jnp.broadcast_to(jnp.exp(g_col[c, h][c_len - 1:c_len, :]), (1, d))
                                 for h in range(n_h)], axis=1) for c in range(cb)]
    if egl_ref.shape[0] > cb:
        egl_rows.append(jnp.zeros((egl_ref.shape[0] - cb, width), F32))
    egl_ref[...] = jnp.concatenate(egl_rows, axis=0)
    for c in range(cb):
        rows = slice(c * c_len, (c + 1) * c_len)
        attn_ref[rows, :] = jnp.concatenate(
            [(qkt[c, h] * decay[c, h]).astype(BF16) for h in range(n_h)], axis=1)


def _gdn_scan_kernel(u_ref, w_ref, qe_ref, kd_ref, attn_ref, egl_ref, z_ref, ng_ref, o_ref,
                     state_scr, *, bb, ts, cb):
    j = pl.program_id(1)
    c_len = GDN_CHUNK
    d = GDN_D
    n_c = ts // c_len
    chains = [(b_, h) for b_ in range(bb) for h in range(GDN_HEADS)]

    @pl.when(j == 0)
    def _():
        state_scr[...] = jnp.zeros_like(state_scr)

    def chunk(c, carry):
        r0 = pl.multiple_of(c * c_len, c_len)
        rows = pl.ds(r0, c_len)
        cg = j * n_c + c
        state = {p: state_scr[p[0], p[1]] for p in chains}
        state_b = {p: state[p].astype(BF16) for p in chains}
        w_s = {(b_, h): _dot(w_ref[b_, rows, h * d:(h + 1) * d], state_b[b_, h]) for b_, h in chains}
        q_s = {(b_, h): _dot(qe_ref[b_, rows, h * d:(h + 1) * d], state_b[b_, h]) for b_, h in chains}
        v_new = {(b_, h): (u_ref[b_, rows, h * d:(h + 1) * d] - w_s[b_, h]).astype(BF16)
                 for b_, h in chains}
        sub = lax.broadcasted_iota(jnp.int32, egl_ref.shape[2:], 0)
        chunk_decay = [jnp.sum(jnp.where(sub == cg % cb, egl_ref[b_, cg // cb], 0.0),
                               axis=0, keepdims=True) for b_ in range(bb)]
        for b_, h in chains:
            cols = slice(h * d, (h + 1) * d)
            o = q_s[b_, h] + _dot(attn_ref[b_, rows, h * c_len:(h + 1) * c_len], v_new[b_, h])
            state_scr[b_, h] = (state[b_, h] * chunk_decay[b_][:, cols]
                                + _tn_dot(kd_ref[b_, rows, cols], v_new[b_, h]))
            zc = z_ref[b_, rows, cols]
            o_ref[b_, rows, cols] = (_rms_scale(o) * ng_ref[...] * (zc * _sigmoid(zc))
                                     ).astype(o_ref.dtype)
        return carry

    lax.fori_loop(0, n_c, chunk, 0)


def _gated_deltanet(proj3, conv_w, a_log, dt_bias, norm_g):
    b, s, _ = proj3.shape
    scal = jnp.stack([a_log, dt_bias]).astype(F32)
    width = GDN_HEADS * GDN_D
    n_chunks = s // GDN_CHUNK
    cb = min(GDN_PREP_CHUNKS, n_chunks)
    ts = cb * GDN_CHUNK
    halo = 8
    egl_rows = -(-cb // 8) * 8

    def main(col0):
        return pl.BlockSpec((None, ts, width), lambda bi, i: (bi, i, col0 * LANE // width))

    def before(col0):
        return pl.BlockSpec((None, halo, width),
                            lambda bi, i: (bi, jnp.maximum(i * (ts // halo) - 1, 0),
                                           col0 * LANE // width))

    def out_block(w_):
        return pl.BlockSpec((None, ts, w_), lambda bi, i: (bi, i, 0))

    u, w, qe, kd, attn, egl = pl.pallas_call(
        functools.partial(_gdn_prep_kernel, cb=cb),
        grid=(b, n_chunks // cb),
        in_specs=[pl.BlockSpec(memory_space=pltpu.SMEM),
                  main(COL_GQ), before(COL_GQ), main(COL_GK), before(COL_GK),
                  main(COL_GV), before(COL_GV),
                  pl.BlockSpec((None, ts, LANE), lambda bi, i: (bi, i, COL_GAB)),
                  pl.BlockSpec(conv_w.shape, lambda bi, i: (0, 0))],
        out_specs=[out_block(width), out_block(width), out_block(width), out_block(width),
                   out_block(GDN_HEADS * GDN_CHUNK),
                   pl.BlockSpec((None, None, egl_rows, width), lambda bi, i: (bi, i, 0, 0))],
        out_shape=[jax.ShapeDtypeStruct((b, s, width), F32),
                   jax.ShapeDtypeStruct((b, s, width), BF16),
                   jax.ShapeDtypeStruct((b, s, width), BF16),
                   jax.ShapeDtypeStruct((b, s, width), BF16),
                   jax.ShapeDtypeStruct((b, s, GDN_HEADS * GDN_CHUNK), BF16),
                   jax.ShapeDtypeStruct((b, n_chunks // cb, egl_rows, width), F32)],
        compiler_params=_cparams(2),
        name="gdn_prep",
    )(scal, proj3, proj3, proj3, proj3, proj3, proj3, proj3, conv_w)

    bb = 2 if b % 2 == 0 else 1
    t_scan = min(GDN_SCAN_TOKENS, s)

    def scan_block(w_):
        return pl.BlockSpec((bb, t_scan, w_), lambda bi, j: (bi, j, 0))

    return pl.pallas_call(
        functools.partial(_gdn_scan_kernel, bb=bb, ts=t_scan, cb=cb),
        grid=(b // bb, s // t_scan),
        in_specs=[scan_block(width), scan_block(width), scan_block(width), scan_block(width),
                  scan_block(GDN_HEADS * GDN_CHUNK),
                  pl.BlockSpec((bb, n_chunks // cb, egl_rows, width), lambda bi, j: (bi, 0, 0, 0)),
                  pl.BlockSpec((bb, t_scan, width), lambda bi, j: (bi, j, COL_GZ * LANE // width)),
                  pl.BlockSpec((1, GDN_D), lambda bi, j: (0, 0))],
        out_specs=scan_block(width),
        out_shape=jax.ShapeDtypeStruct((b, s, width), BF16),
        scratch_shapes=[pltpu.VMEM((bb, GDN_HEADS, GDN_D, GDN_D), F32)],
        compiler_params=_cparams(2),
        name="gdn_scan",
    )(u, w, qe, kd, attn, egl, proj3, norm_g.reshape(1, GDN_D))


def _nsa_kernel(q_ref, kcmp_ref, vcmp_ref, kslc_ref, vslc_ref, kwin_ref, vwin_ref, gate_ref,
                wk_ref, wv_ref, pk_ref, pv_ref, o_ref,
                kc_scr, vct_scr, vst_scr, vwt_scr, selk_scr, *, seq, tq):
    qi = pl.program_id(1)
    tk = NSA_TK
    dh = NSA_DH
    hg = NSA_GROUP
    n_grp = NSA_KV_HEADS
    nq = hg * tq
    nb = seq // CMP_STRIDE
    n_sel = seq // SEL_LEN
    n_kb = seq // tk
    top_k = min(SEL_TOPK, n_sel)
    half = CMP_LEN // 2
    slc_step = min(NSA_SLC_BLOCKS, n_kb)
    n_win = WINDOW // tk + 1

    @pl.when(qi == 0)
    def _():
        cmp_out = []
        for src, pos, w_ref in ((kcmp_ref, pk_ref, wk_ref), (vcmp_ref, pv_ref, wv_ref)):
            lo_parts, hi_parts = [], []
            for l in range(half):
                rows = src[pl.ds(l, nb, stride=CMP_STRIDE), :]
                lo_parts.append(rows + pos[l:l + 1, :])
                hi_parts.append(rows + pos[half + l:half + l + 1, :])
            a_lo = _dot(jnp.concatenate(lo_parts, axis=1), w_ref[0], HIGHEST)
            a_hi = _dot(jnp.concatenate(hi_parts, axis=1), w_ref[1], HIGHEST)
            a_hi = jnp.concatenate([a_hi[1:], jnp.zeros((1, LANE), F32)], axis=0)
            cmp_out.append(a_lo + a_hi)
        kc_scr[...] = cmp_out[0]
        vct_scr[...] = cmp_out[1].T.astype(BF16)

        def transpose_values(kb, carry):
            k0 = pl.multiple_of(kb * tk, tk)
            vst_scr[kb] = vslc_ref[pl.ds(k0, tk), :].astype(F32).T.astype(BF16)
            vwt_scr[kb] = vwin_ref[pl.ds(k0, tk), :].astype(F32).T.astype(BF16)
            return carry

        lax.fori_loop(0, n_kb, transpose_values, 0)

    lane = lax.broadcasted_iota(jnp.int32, (tq, LANE), 1)
    qg = []
    for g in range(n_grp):
        parts = []
        for i in range(hg):
            hd = g * hg + i
            pair = q_ref[:, (hd // 2) * LANE:(hd // 2 + 1) * LANE] * (dh ** -0.5)
            if hd % 2 != g:
                pair = pltpu.roll(pair, dh, axis=1)
            parts.append(jnp.where((lane >= g * dh) & (lane < (g + 1) * dh), pair, 0.0))
        qg.append(jnp.concatenate(parts, axis=0))

    t_row = qi * tq + lax.broadcasted_iota(jnp.int32, (1, tq), 1)
    t_row4 = jnp.concatenate([t_row] * hg, axis=1)
    key_iota = lax.broadcasted_iota(jnp.int32, (tk, 1), 0)

    j_sel = lax.broadcasted_iota(jnp.int32, (n_sel, tq), 0)
    cur = lax.shift_right_logical(t_row, SEL_LEN.bit_length() - 1)
    sel_valid = j_sel <= cur
    forced = (j_sel == 0) | (j_sel == cur) | (j_sel == cur - 1)
    ov_j = lax.broadcasted_iota(jnp.int32, (n_sel, nb), 0) * SEL_LEN
    ov_n = lax.broadcasted_iota(jnp.int32, (n_sel, nb), 1) * CMP_STRIDE
    overlap = jnp.where((ov_n < ov_j + SEL_LEN) & (ov_n + CMP_LEN > ov_j), 1.0, 0.0).astype(BF16)
    n_idx = lax.broadcasted_iota(jnp.int32, (nb, 1), 0)
    cmp_valid = (n_idx * CMP_STRIDE + CMP_LEN - 1 <= t_row4) & (n_idx < nb - 1)

    def mask_heads(ok, s):
        return jnp.where(jnp.concatenate([ok] * hg, axis=1), s, MASK_NEG)

    groups = range(n_grp)
    kc_hi, kc_lo = _split_bf16(kc_scr[...])
    q_split = [_split_bf16(qg[g]) for g in groups]
    q_b = [q_split[g][0] for g in groups]

    win_blocks = []
    for w in range(n_win):
        kb_int = qi - w
        kb = jnp.maximum(kb_int, 0)
        kpos = kb_int * tk + key_iota
        ok = (kpos <= t_row) & (kpos > t_row - WINDOW) & (kpos >= 0)
        k_blk = kwin_ref[pl.ds(pl.multiple_of(kb * tk, tk), tk), :].astype(BF16)
        win_blocks.append((kb, k_blk, ok))

    s_cmp = [_nt_dot(kc_hi, q_split[g][0]) + _nt_dot(kc_lo, q_split[g][0])
             + _nt_dot(kc_hi, q_split[g][1]) for g in groups]
    s_win = [jnp.concatenate([mask_heads(ok, _nt_dot(k_blk, q_b[g]))
                              for (_, k_blk, ok) in win_blocks], axis=0) for g in groups]

    p_cmp, p_win, l_win = [], [], []
    for g in groups:
        s_m = jnp.where(cmp_valid, s_cmp[g], MASK_NEG)
        e = jnp.where(cmp_valid, jnp.exp(s_m - jnp.max(s_m, axis=0, keepdims=True)), 0.0)
        den = jnp.sum(e, axis=0, keepdims=True)
        p_cmp.append(jnp.where(den > 0.0, e / jnp.where(den > 0.0, den, 1.0), 0.0))
    for g in groups:
        p = jnp.exp(s_win[g] - jnp.max(s_win[g], axis=0, keepdims=True))
        l_win.append(jnp.sum(p, axis=0, keepdims=True))
        p_win.append(p.astype(BF16))

    o_cmp = [_dot(vct_scr[g * dh:(g + 1) * dh, :], p_cmp[g].astype(BF16)) for g in groups]
    imp = []
    for g in groups:
        p_sum = p_cmp[g][:, 0:tq]
        for i in range(1, hg):
            p_sum = p_sum + p_cmp[g][:, i * tq:(i + 1) * tq]
        ps_hi, ps_lo = _split_bf16(p_sum)
        imp.append(_dot(overlap, ps_hi) + _dot(overlap, ps_lo))
    o_win = []
    for g in groups:
        v_t = jnp.concatenate([vwt_scr[kb, g * dh:(g + 1) * dh, :] for (kb, _, _) in win_blocks],
                              axis=1)
        o_win.append(_dot(v_t, p_win[g]) / l_win[g])

    for g in groups:
        score = jnp.where(sel_valid, imp[g] + jnp.where(forced, FORCE_BONUS, 0.0), MASK_NEG)
        beaten = jnp.zeros((n_sel, tq), F32)
        for jp in range(n_sel):
            other = score[jp:jp + 1, :]
            ge = jnp.where(other >= score, 1.0, 0.0)
            gt = jnp.where(other > score, 1.0, 0.0)
            beaten = beaten + jnp.where(j_sel > jp, ge, gt)
        picked = jnp.where(beaten < top_k, 1.0, 0.0)
        for j in range(n_sel):
            selk_scr[g, j * SEL_LEN:(j + 1) * SEL_LEN, :] = jnp.broadcast_to(
                picked[j:j + 1, :], (SEL_LEN, tq))

    def slc_body(it, carry):
        kb0 = it * slc_step
        k_blocks = [kslc_ref[pl.ds(pl.multiple_of((kb0 + u) * tk, tk), tk), :].astype(BF16)
                    for u in range(slc_step)]
        tiles = {}
        for g in range(n_grp):
            for u in range(slc_step):
                k0 = pl.multiple_of((kb0 + u) * tk, tk)
                ok = (selk_scr[g, pl.ds(k0, tk), :] > 0.5) & (k0 + key_iota <= t_row)
                tiles[g, u] = mask_heads(ok, _nt_dot(k_blocks[u], q_b[g]))
        new = []
        for g in range(n_grp):
            m, l, acc = carry[g]
            for u in range(slc_step):
                s = tiles[g, u]
                m_new = jnp.maximum(m, jnp.max(s, axis=0, keepdims=True))
                alpha = jnp.exp(m - m_new)
                p = jnp.exp(s - m_new)
                l = alpha * l + jnp.sum(p, axis=0, keepdims=True)
                acc = alpha * acc + _dot(vst_scr[kb0 + u, g * dh:(g + 1) * dh, :], p.astype(BF16))
                m = m_new
            new.append((m, l, acc))
        return tuple(new)

    init = tuple((jnp.full((1, nq), MASK_NEG, F32), jnp.zeros((1, nq), F32),
                  jnp.zeros((dh, nq), F32)) for _ in range(n_grp))
    span = slc_step * tk
    n_iter = ((qi + 1) * tq + span - 1) // span
    slc = lax.fori_loop(0, n_iter, slc_body, init)
    o_slc = [acc / l for (_, l, acc) in slc]

    gates = _sigmoid(gate_ref[...].T)
    outs = []
    for g in range(n_grp):
        for i in range(hg):
            hd = g * hg + i
            cols = slice(i * tq, (i + 1) * tq)
            outs.append(gates[3 * hd:3 * hd + 1, :] * o_cmp[g][:, cols]
                        + gates[3 * hd + 1:3 * hd + 2, :] * o_slc[g][:, cols]
                        + gates[3 * hd + 2:3 * hd + 3, :] * o_win[g][:, cols])
    o_ref[...] = jnp.concatenate(outs, axis=0).T.astype(o_ref.dtype)


def _cmp_weight(w):
    half = CMP_LEN // 2
    w = w.reshape(2, half, NSA_DH, NSA_DH)
    z = jnp.zeros_like(w)
    big = jnp.stack([jnp.concatenate([w, z], axis=-1), jnp.concatenate([z, w], axis=-1)], axis=2)
    return big.reshape(2, half * NSA_KV_HEADS * NSA_DH, NSA_KV_HEADS * NSA_DH)


def _native_sparse_attention(proj3, proj16, cmp_wk, cmp_pk, cmp_wv, cmp_pv, *, tq):
    b, s, _ = proj3.shape
    nb = s // CMP_STRIDE
    n_kb = s // NSA_TK
    assert tq == NSA_TK and n_kb % min(NSA_SLC_BLOCKS, n_kb) == 0
    wk = _cmp_weight(cmp_wk)
    wv = _cmp_weight(cmp_wv)
    pk = jnp.tile(cmp_pk, (1, NSA_KV_HEADS))
    pv = jnp.tile(cmp_pv, (1, NSA_KV_HEADS))
    qw = NSA_HEADS * NSA_DH

    def seq_block(col):
        return pl.BlockSpec((None, s, LANE), lambda bi, i: (bi, 0, col))

    def whole(a):
        return pl.BlockSpec(a.shape, lambda bi, i: (0,) * a.ndim)

    return pl.pallas_call(
        functools.partial(_nsa_kernel, seq=s, tq=tq),
        grid=(b, s // tq),
        in_specs=[pl.BlockSpec((None, tq, qw), lambda bi, i: (bi, i, COL_NQ * LANE // qw))]
        + [seq_block(COL_NCMP + c) for c in range(2)]
        + [seq_block(COL16_NKV + c) for c in range(4)]
        + [pl.BlockSpec((None, tq, LANE), lambda bi, i: (bi, i, COL_NGATE)),
           whole(wk), whole(wv), whole(pk), whole(pv)],
        out_specs=pl.BlockSpec((None, tq, qw), lambda bi, i: (bi, i, 0)),
        out_shape=jax.ShapeDtypeStruct((b, s, qw), BF16),
        scratch_shapes=[pltpu.VMEM((nb, LANE), F32),
                        pltpu.VMEM((LANE, nb), BF16),
                        pltpu.VMEM((n_kb, LANE, NSA_TK), BF16),
                        pltpu.VMEM((n_kb, LANE, NSA_TK), BF16),
                        pltpu.VMEM((NSA_KV_HEADS, s, tq), F32)],
        compiler_params=_cparams(2),
        name="native_sparse_attention",
    )(proj3, proj3, proj3, proj16, proj16, proj16, proj16, proj3, wk, wv, pk, pv)


def _pad_cols(w, width):
    return jnp.pad(w, ((0, 0), (0, width - w.shape[1])))


def _proj_weight(w_in):
    n_ab = 2 * GDN_HEADS
    n_gate = 3 * NSA_HEADS
    n_cmp = 2 * NSA_KV_HEADS * NSA_DH
    n_slc_win = 4 * NSA_KV_HEADS * NSA_DH
    n_sb = 3 * SB_HEADS * SB_DH
    o_ab = 4 * GDN_HEADS * GDN_D
    o_nq = o_ab + n_ab
    o_cmp = o_nq + NSA_HEADS * NSA_DH
    o_slc = o_cmp + n_cmp
    o_gate = o_slc + n_slc_win
    o_sb = o_gate + n_gate
    o_merge = o_sb + n_sb
    w = jnp.concatenate([w_in[:, :o_ab], w_in[:, o_nq:o_slc],
                         _pad_cols(w_in[:, o_ab:o_nq], LANE),
                         _pad_cols(w_in[:, o_gate:o_sb], LANE),
                         w_in[:, o_merge:],
                         w_in[:, o_slc:o_gate], w_in[:, o_sb:o_merge]], axis=1)
    assert w.shape[1] == N_PROJ32 + N_PROJ16, w.shape
    return w.astype(BF16)


def _layer(x, p, *, tiles):
    b, s, d = x.shape
    x2 = x.reshape(b * s, d)
    proj, proj16 = _norm_matmul(x2, p["g_mix_pre"], _proj_weight(p["w_in"]),
                                tm=tiles["tm"], tn=tiles["tn"])
    proj3 = proj.reshape(b, s, N_PROJ32)
    proj16 = proj16.reshape(b, s, N_PROJ16)
    ya = _gated_deltanet(proj3, p["gdn_conv_w"], p["gdn_a_log"], p["gdn_dt_bias"], p["gdn_norm_g"])
    yb = _native_sparse_attention(proj3, proj16, p["nsa_cmp_wk"], p["nsa_cmp_pk"], p["nsa_cmp_wv"],
                                  p["nsa_cmp_pv"], tq=tiles["tq"])
    yc = _sb_attention(proj16, tq=tiles["tq_sb"])
    x2 = _merge_out(ya.reshape(b * s, -1), yb.reshape(b * s, -1), yc.reshape(b * s, -1), proj, x2,
                    p["w_br_gdn"].astype(BF16), p["w_br_nsa"].astype(BF16),
                    p["w_br_sb"].astype(BF16), p["w_out"].astype(BF16), p["g_mix_post"],
                    tm=tiles["tm_out"])
    x2 = _mlp(x2, p["g_ff_pre"], p["w_ff1"].astype(BF16), p["w_ff2"].astype(BF16), p["g_ff_post"],
              tm=tiles["tm"], tn=tiles["tn"])
    return x2.reshape(b, s, d)


def _tiles(b, s):
    t = b * s
    return {"tm": min(1024, t), "tn": 1024, "tm_out": min(512, t), "tq": min(128, s),
            "tq_sb": min(256, s)}


def kernel(x, g_mix_pre, g_mix_post, g_ff_pre, g_ff_post, w_in, gdn_conv_w, gdn_a_log, gdn_dt_bias,
           gdn_norm_g, nsa_cmp_wk, nsa_cmp_pk, nsa_cmp_wv, nsa_cmp_pv, w_br_gdn, w_br_nsa, w_br_sb,
           w_out, w_ff1, w_ff2):
    params = dict(g_mix_pre=g_mix_pre, g_mix_post=g_mix_post, g_ff_pre=g_ff_pre, g_ff_post=g_ff_post,
                  w_in=w_in, gdn_conv_w=gdn_conv_w, gdn_a_log=gdn_a_log, gdn_dt_bias=gdn_dt_bias,
                  gdn_norm_g=gdn_norm_g, nsa_cmp_wk=nsa_cmp_wk, nsa_cmp_pk=nsa_cmp_pk,
                  nsa_cmp_wv=nsa_cmp_wv, nsa_cmp_pv=nsa_cmp_pv, w_br_gdn=w_br_gdn,
                  w_br_nsa=w_br_nsa, w_br_sb=w_br_sb, w_out=w_out, w_ff1=w_ff1, w_ff2=w_ff2)
    tiles = _tiles(x.shape[0], x.shape[1])
    for layer in range(w_in.shape[0]):
        x = _layer(x, {k: v[layer] for k, v in params.items()}, tiles=tiles)
    return x
```

```python
import functools

import jax
import jax.numpy as jnp
from jax import lax
from jax.experimental import pallas as pl
from jax.experimental.pallas import tpu as pltpu

F32 = jnp.float32
BF16 = jnp.bfloat16
HIGHEST = lax.Precision.HIGHEST

EPS = 1e-6
MASK_NEG = -1e30
LOG2E = 1.4426950408889634

GDN_HEADS = 4
GDN_D = 128
GDN_CONV = 4
GDN_CHUNK = 64
GDN_PREP_CHUNKS = 4
GDN_SCAN_TOKENS = 512

NSA_HEADS = 8
NSA_KV_HEADS = 2
NSA_GROUP = NSA_HEADS // NSA_KV_HEADS
NSA_DH = 64
CMP_LEN = 32
CMP_STRIDE = 16
SEL_LEN = 64
SEL_TOPK = 8
WINDOW = 512
FORCE_BONUS = 1e3
NSA_TK = 128
NSA_SLC_BLOCKS = 4

SB_HEADS = 8
SB_DH = 64
SB_TK = 128
SB_DEAD_LOG2 = -160.0

LANE = 128

COL_GQ, COL_GK, COL_GV, COL_GZ = 0, 4, 8, 12
COL_NQ, COL_NCMP, COL_GAB, COL_NGATE = 16, 20, 22, 23
N_PROJ32 = 24 * LANE
COL16_NKV, COL16_SQ, COL16_SK, COL16_SV, COL16_GATE = 0, 4, 8, 12, 16
N_PROJ16 = 40 * LANE

VMEM_LIMIT = 48 * 1024 * 1024


def _cparams(n_axes):
    return pltpu.CompilerParams(dimension_semantics=("arbitrary",) * n_axes,
                                vmem_limit_bytes=VMEM_LIMIT)


def _nt_dot(a, b, precision=None):
    return lax.dot_general(a, b, (((1,), (1,)), ((), ())), precision=precision,
                           preferred_element_type=F32)


def _tn_dot(a, b):
    return lax.dot_general(a, b, (((0,), (0,)), ((), ())), preferred_element_type=F32)


def _dot(a, b, precision=None):
    return jnp.dot(a, b, precision=precision, preferred_element_type=F32)


def _rms_scale(y):
    return y * lax.rsqrt(jnp.mean(y * y, axis=-1, keepdims=True) + EPS)


def _softplus(z):
    return jnp.maximum(z, 0.0) + jnp.log1p(jnp.exp(-jnp.abs(z)))


def _sigmoid(z):
    return jax.nn.sigmoid(z)


def _norm_matmul_kernel(x_ref, g_ref, w_ref, o32_ref, o16_ref, h_scr, *, n32, gate_from):
    j = pl.program_id(1)

    @pl.when(j == 0)
    def _():
        h_scr[...] = (_rms_scale(x_ref[...]) * g_ref[...]).astype(BF16)

    y = _dot(h_scr[...], w_ref[...])

    @pl.when(j < n32)
    def _():
        o32_ref[...] = y

    @pl.when((j >= n32) & (j < gate_from))
    def _():
        o16_ref[...] = y.astype(BF16)

    @pl.when(j >= gate_from)
    def _():
        o16_ref[...] = _sigmoid(y).astype(BF16)


def _norm_matmul(x, g, w, *, tm, tn):
    t, d = x.shape
    n32, n16 = N_PROJ32 // tn, N_PROJ16 // tn
    gate_from = (N_PROJ32 + COL16_GATE * LANE) // tn
    return pl.pallas_call(
        functools.partial(_norm_matmul_kernel, n32=n32, gate_from=gate_from),
        grid=(t // tm, n32 + n16),
        in_specs=[pl.BlockSpec((tm, d), lambda i, j: (i, 0)),
                  pl.BlockSpec((1, d), lambda i, j: (0, 0)),
                  pl.BlockSpec((d, tn), lambda i, j: (0, j))],
        out_specs=[pl.BlockSpec((tm, tn), lambda i, j: (i, jnp.minimum(j, n32 - 1))),
                   pl.BlockSpec((tm, tn), lambda i, j: (i, jnp.maximum(j - n32, 0)))],
        out_shape=[jax.ShapeDtypeStruct((t, N_PROJ32), F32),
                   jax.ShapeDtypeStruct((t, N_PROJ16), BF16)],
        scratch_shapes=[pltpu.VMEM((tm, d), BF16)],
        compiler_params=_cparams(2),
        name="norm_matmul",
    )(x, g.reshape(1, d), w)


def _mlp_kernel(x_ref, g_pre_ref, w1_ref, w2_ref, g_post_ref, o_ref, h_scr, acc_scr):
    j = pl.program_id(1)

    @pl.when(j == 0)
    def _():
        h_scr[...] = (_rms_scale(x_ref[...]) * g_pre_ref[...]).astype(BF16)

    act = jnp.square(jnp.maximum(_dot(h_scr[...], w1_ref[...]), 0.0)).astype(BF16)
    part = _dot(act, w2_ref[...])

    @pl.when(j == 0)
    def _():
        acc_scr[...] = part

    @pl.when(j > 0)
    def _():
        acc_scr[...] += part

    @pl.when(j == pl.num_programs(1) - 1)
    def _():
        o_ref[...] = x_ref[...] + _rms_scale(acc_scr[...]) * g_post_ref[...]


def _mlp(x, g_pre, w1, w2, g_post, *, tm, tn):
    t, d = x.shape
    d_ff = w1.shape[1]
    return pl.pallas_call(
        _mlp_kernel,
        grid=(t // tm, d_ff // tn),
        in_specs=[pl.BlockSpec((tm, d), lambda i, j: (i, 0)),
                  pl.BlockSpec((1, d), lambda i, j: (0, 0)),
                  pl.BlockSpec((d, tn), lambda i, j: (0, j)),
                  pl.BlockSpec((tn, d), lambda i, j: (j, 0)),
                  pl.BlockSpec((1, d), lambda i, j: (0, 0))],
        out_specs=pl.BlockSpec((tm, d), lambda i, j: (i, 0)),
        out_shape=jax.ShapeDtypeStruct((t, d), F32),
        scratch_shapes=[pltpu.VMEM((tm, d), BF16), pltpu.VMEM((tm, d), F32)],
        compiler_params=_cparams(2),
        name="mlp",
    )(x, g_pre.reshape(1, d), w1, w2, g_post.reshape(1, d))


def _merge_out_kernel(ya_ref, yb_ref, yc_ref, g0_ref, g1_ref, g2_ref, x_ref,
                      wa_ref, wb_ref, wc_ref, wo_ref, g_ref, o_ref):
    merged = (g0_ref[...].astype(F32) * _dot(ya_ref[...], wa_ref[...])
              + g1_ref[...].astype(F32) * _dot(yb_ref[...], wb_ref[...])
              + g2_ref[...].astype(F32) * _dot(yc_ref[...], wc_ref[...]))
    y = _dot(merged.astype(BF16), wo_ref[...])
    o_ref[...] = x_ref[...] + _rms_scale(y) * g_ref[...]


def _merge_out(ya, yb, yc, proj16, x, wa, wb, wc, wo, g, *, tm):
    t, d = x.shape
    w_in = ya.shape[1]
    mcol = COL16_GATE * LANE // d

    def row(i):
        return (i, 0)

    def const(i):
        return (0, 0)

    return pl.pallas_call(
        _merge_out_kernel,
        grid=(t // tm,),
        in_specs=[pl.BlockSpec((tm, w_in), row),
                  pl.BlockSpec((tm, w_in), row),
                  pl.BlockSpec((tm, w_in), row),
                  pl.BlockSpec((tm, d), lambda i: (i, mcol)),
                  pl.BlockSpec((tm, d), lambda i: (i, mcol + 1)),
                  pl.BlockSpec((tm, d), lambda i: (i, mcol + 2)),
                  pl.BlockSpec((tm, d), row),
                  pl.BlockSpec((w_in, d), const),
                  pl.BlockSpec((w_in, d), const),
                  pl.BlockSpec((w_in, d), const),
                  pl.BlockSpec((d, d), const),
                  pl.BlockSpec((1, d), const)],
        out_specs=pl.BlockSpec((tm, d), row),
        out_shape=jax.ShapeDtypeStruct((t, d), F32),
        compiler_params=_cparams(1),
        name="merge_out",
    )(ya, yb, yc, proj16, proj16, proj16, x, wa, wb, wc, wo, g.reshape(1, d))


def _sb_kernel(q_ref, k_ref, v_ref, o_ref, run_scr, acc_scr, *, tq):
    qi = pl.program_id(1)
    tk = SB_TK
    n_diag = tq // tk
    dh = SB_DH
    n_pair = SB_HEADS * dh // LANE
    row = lax.broadcasted_iota(jnp.int32, (tq, tk), 0)
    col = lax.broadcasted_iota(jnp.int32, (tq, tk), 1)
    first_head = lax.broadcasted_iota(jnp.int32, (tk, LANE), 1) < dh
    r2 = lax.broadcasted_iota(jnp.int32, (2 * tk, 2 * tk), 0) & (tk - 1)
    c2 = lax.broadcasted_iota(jnp.int32, (2 * tk, 2 * tk), 1)
    later_ones = jnp.where((r2 > c2) | (c2 >= tk), 1.0, 0.0).astype(BF16)

    q_pairs = [(q_ref[:, p * LANE:(p + 1) * LANE].astype(F32) * (dh ** -0.5 * LOG2E)).astype(BF16)
               for p in range(n_pair)]

    run_scr[...] = jnp.zeros_like(run_scr)
    acc_scr[...] = jnp.zeros_like(acc_scr)

    def split_heads(x):
        zero = jnp.zeros_like(x)
        return jnp.concatenate([jnp.where(first_head, x, zero), jnp.where(first_head, zero, x)],
                               axis=0)

    def block(j, diag_offset):
        k0 = pl.multiple_of(j * tk, tk)
        pairs = range(n_pair)
        diagonal = diag_offset is not None
        if diagonal:
            causal = col + diag_offset < row
            causal2 = jnp.concatenate([causal, causal], axis=1)
        z = [_nt_dot(q_pairs[p],
                     split_heads(k_ref[pl.ds(k0, tk), p * LANE:(p + 1) * LANE].astype(BF16)))
             for p in pairs]
        log_sig, hi, lo = [], [], []
        for p in pairs:
            log_sig.append(jnp.minimum(z[p], 0.0) - jnp.log2(1.0 + jnp.exp2(-jnp.abs(z[p]))))
            log_keep = log_sig[p] - z[p]
            if diagonal:
                log_keep = jnp.where(causal2, log_keep, 0.0)
            hi.append(log_keep.astype(BF16))
            lo.append((log_keep - hi[p].astype(F32)).astype(BF16))
        sums = [[_dot(jnp.concatenate([hi[p][:, hh * tk:(hh + 1) * tk],
                                       lo[p][:, hh * tk:(hh + 1) * tk]], axis=1), later_ones)
                 for hh in range(2)] for p in pairs]
        for p in pairs:
            a = []
            for hh in range(2):
                h = 2 * p + hh
                run = run_scr[h]
                a_h = jnp.exp2(log_sig[p][:, hh * tk:(hh + 1) * tk] + sums[p][hh][:, :tk] + run)
                if diagonal:
                    a_h = jnp.where(causal, a_h, 0.0)
                run_scr[h] = run + sums[p][hh][:, tk:]
                a.append(a_h.astype(BF16))
            vp = v_ref[pl.ds(k0, tk), p * LANE:(p + 1) * LANE].astype(BF16)
            acc_scr[p] += _dot(jnp.concatenate(a, axis=1), split_heads(vp))

    for d in reversed(range(n_diag)):
        block(qi * n_diag + d, d * tk)

    def still_live():
        top = run_scr[0]
        for h in range(1, SB_HEADS):
            top = jnp.maximum(top, run_scr[h])
        return jnp.max(top) > SB_DEAD_LOG2

    def body(carry):
        it, _ = carry
        block(qi * n_diag - 1 - it, None)
        return it + 1, still_live()

    lax.while_loop(lambda c: (c[0] < qi * n_diag) & c[1], body, (jnp.int32(0), still_live()))
    o_ref[...] = jnp.concatenate([acc_scr[p] for p in range(n_pair)], axis=-1).astype(o_ref.dtype)


def _sb_attention(proj16, *, tq):
    b, s, _ = proj16.shape
    width = SB_HEADS * SB_DH
    n_pair = width // LANE
    return pl.pallas_call(
        functools.partial(_sb_kernel, tq=tq),
        grid=(b, s // tq),
        in_specs=[pl.BlockSpec((None, tq, width), lambda bi, i: (bi, i, COL16_SQ * LANE // width)),
                  pl.BlockSpec((None, s, width), lambda bi, i: (bi, 0, COL16_SK * LANE // width)),
                  pl.BlockSpec((None, s, width), lambda bi, i: (bi, 0, COL16_SV * LANE // width))],
        out_specs=pl.BlockSpec((None, tq, width), lambda bi, i: (bi, i, 0)),
        out_shape=jax.ShapeDtypeStruct((b, s, width), BF16),
        scratch_shapes=[pltpu.VMEM((SB_HEADS, tq, LANE), F32), pltpu.VMEM((n_pair, tq, LANE), F32)],
        compiler_params=_cparams(2),
        name="sb_attention",
    )(proj16, proj16, proj16)


def _split_bf16(x):
    hi = x.astype(BF16)
    return hi, (x - hi.astype(F32)).astype(BF16)


def _gdn_prep_kernel(scal_ref, q_ref, qh_ref, k_ref, kh_ref, v_ref, vh_ref, ab_ref, cw_ref,
                     u_ref, w_ref, qe_ref, kd_ref, attn_ref, egl_ref, *, cb):
    i = pl.program_id(1)
    c_len = GDN_CHUNK
    d = GDN_D
    n_h = GDN_HEADS
    ts = cb * c_len
    halo = qh_ref.shape[0]
    width = n_h * d

    def conv_silu(x_ref, halo_ref, w):
        ext = jnp.concatenate([halo_ref[...] * jnp.where(i > 0, 1.0, 0.0), x_ref[...]], axis=0)
        y = jnp.zeros((ts, width), F32)
        for tap in range(GDN_CONV):
            off = halo - (GDN_CONV - 1) + tap
            y = y + w[tap:tap + 1, :] * ext[off:off + ts]
        return y * _sigmoid(y)

    qc = conv_silu(q_ref, qh_ref, cw_ref[:, 0:width])
    kc = conv_silu(k_ref, kh_ref, cw_ref[:, width:2 * width])
    vc = conv_silu(v_ref, vh_ref, cw_ref[:, 2 * width:3 * width])

    ab = ab_ref[...]
    lane = lax.broadcasted_iota(jnp.int32, (1, LANE), 1)
    a_log = jnp.zeros((1, LANE), F32)
    dt_bias = jnp.zeros((1, LANE), F32)
    for h in range(n_h):
        a_log = jnp.where(lane == h, scal_ref[0, h], a_log)
        dt_bias = jnp.where(lane == h, scal_ref[1, h], dt_bias)
    g_cum = -jnp.exp(a_log) * _softplus(ab + dt_bias)
    row_in_chunk = lax.broadcasted_iota(jnp.int32, (ts, LANE), 0) & (c_len - 1)
    shift = 1
    while shift < c_len:
        g_cum = g_cum + jnp.where(row_in_chunk >= shift, pltpu.roll(g_cum, shift, axis=0), 0.0)
        shift *= 2
    beta_all = _sigmoid(ab)

    row = lax.broadcasted_iota(jnp.int32, (c_len, c_len), 0)
    col = lax.broadcasted_iota(jnp.int32, (c_len, c_len), 1)
    incl = row >= col
    strict = row > col
    eye = jnp.where(row == col, 1.0, 0.0)

    problems = [(c, h) for c in range(cb) for h in range(n_h)]
    qn, kn, k_beta, v_beta, g_col, decay = {}, {}, {}, {}, {}, {}
    for c, h in problems:
        rows = slice(c * c_len, (c + 1) * c_len)
        cols = slice(h * d, (h + 1) * d)
        qh, kh = qc[rows, cols], kc[rows, cols]
        qn[c, h] = qh * lax.rsqrt(jnp.sum(qh * qh, axis=-1, keepdims=True) + EPS) * (d ** -0.5)
        kn[c, h] = kh * lax.rsqrt(jnp.sum(kh * kh, axis=-1, keepdims=True) + EPS)
        beta = beta_all[rows, n_h + h:n_h + h + 1]
        k_beta[c, h] = kn[c, h] * beta
        v_beta[c, h] = (vc[rows, cols] * beta).astype(BF16)
        g_col[c, h] = g_cum[rows, h:h + 1]
        g_sq = jnp.broadcast_to(g_col[c, h], (c_len, c_len))
        decay[c, h] = jnp.where(incl, jnp.exp(jnp.where(incl, g_sq - g_sq.T, 0.0)), 0.0)

    kn_b = {p: kn[p].astype(BF16) for p in problems}
    kkt = {p: _nt_dot(k_beta[p].astype(BF16), kn_b[p]) for p in problems}
    qkt = {p: _nt_dot(qn[p].astype(BF16), kn_b[p]) for p in problems}

    a_mat = {p: jnp.where(strict, kkt[p] * decay[p], 0.0) for p in problems}
    left = lax.broadcasted_iota(jnp.int32, (c_len, 2 * c_len), 1) < c_len
    pair = {p: jnp.concatenate([-a_mat[p], eye], axis=1) for p in problems}
    for _ in range(max(c_len - 1, 1).bit_length()):
        y = {p: _dot(pair[p][:, :c_len].astype(BF16), pair[p].astype(BF16)) for p in problems}
        pair = {p: jnp.where(left, y[p], pair[p] + y[p]) for p in problems}
    t_inv = {p: pair[p][:, c_len:] for p in problems}
    resid = {}
    for p in problems:
        m_hi, m_lo = _split_bf16(eye + a_mat[p])
        t_hi, t_lo = _split_bf16(t_inv[p])
        resid[p] = (eye - (_dot(m_hi, t_hi) + _dot(m_lo, t_hi) + _dot(m_hi, t_lo)), t_hi)
    t_b = {p: (t_inv[p] + _dot(resid[p][1], resid[p][0].astype(BF16))).astype(BF16)
           for p in problems}

    for c, h in problems:
        p = (c, h)
        rows = slice(c * c_len, (c + 1) * c_len)
        cols = slice(h * d, (h + 1) * d)
        exp_g = jnp.exp(g_col[p])
        g_last = g_col[p][c_len - 1:c_len, :]
        u_ref[rows, cols] = _dot(t_b[p], v_beta[p])
        w_ref[rows, cols] = _dot(t_b[p], (k_beta[p] * exp_g).astype(BF16)).astype(BF16)
        qe_ref[rows, cols] = (qn[p] * exp_g).astype(BF16)
        kd_ref[rows, cols] = (kn[p] * jnp.exp(g_last - g_col[p])).astype(BF16)
    egl_rows = [jnp.concatenate([jnp.broadcast_to(jnp.exp(g_col[c, h][c_len - 1:c_len, :]), (1, d))
                                 for h in range(n_h)], axis=1) for c in range(cb)]
    if egl_ref.shape[0] > cb:
        egl_rows.append(jnp.zeros((egl_ref.shape[0] - cb, width), F32))
    egl_ref[...] = jnp.concatenate(egl_rows, axis=0)
    for c in range(cb):
        rows = slice(c * c_len, (c + 1) * c_len)
        attn_ref[rows, :] = jnp.concatenate(
            [(qkt[c, h] * decay[c, h]).astype(BF16) for h in range(n_h)], axis=1)


def _gdn_scan_kernel(u_ref, w_ref, qe_ref, kd_ref, attn_ref, egl_ref, z_ref, ng_ref, o_ref,
                     state_scr, *, bb, ts, cb):
    j = pl.program_id(1)
    c_len = GDN_CHUNK
    d = GDN_D
    n_c = ts // c_len
    chains = [(b_, h) for b_ in range(bb) for h in range(GDN_HEADS)]

    @pl.when(j == 0)
    def _():
        state_scr[...] = jnp.zeros_like(state_scr)

    def chunk(c, carry):
        r0 = pl.multiple_of(c * c_len, c_len)
        rows = pl.ds(r0, c_len)
        cg = j * n_c + c
        state = {p: state_scr[p[0], p[1]] for p in chains}
        state_b = {p: state[p].astype(BF16) for p in chains}
        w_s = {(b_, h): _dot(w_ref[b_, rows, h * d:(h + 1) * d], state_b[b_, h]) for b_, h in chains}
        q_s = {(b_, h): _dot(qe_ref[b_, rows, h * d:(h + 1) * d], state_b[b_, h]) for b_, h in chains}
        v_new = {(b_, h): (u_ref[b_, rows, h * d:(h + 1) * d] - w_s[b_, h]).astype(BF16)
                 for b_, h in chains}
        sub = lax.broadcasted_iota(jnp.int32, egl_ref.shape[2:], 0)
        chunk_decay = [jnp.sum(jnp.where(sub == cg % cb, egl_ref[b_, cg // cb], 0.0),
                               axis=0, keepdims=True) for b_ in range(bb)]
        for b_, h in chains:
            cols = slice(h * d, (h + 1) * d)
            o = q_s[b_, h] + _dot(attn_ref[b_, rows, h * c_len:(h + 1) * c_len], v_new[b_, h])
            state_scr[b_, h] = (state[b_, h] * chunk_decay[b_][:, cols]
                                + _tn_dot(kd_ref[b_, rows, cols], v_new[b_, h]))
            zc = z_ref[b_, rows, cols]
            o_ref[b_, rows, cols] = (_rms_scale(o) * ng_ref[...] * (zc * _sigmoid(zc))
                                     ).astype(o_ref.dtype)
        return carry

    lax.fori_loop(0, n_c, chunk, 0)


def _gated_deltanet(proj3, conv_w, a_log, dt_bias, norm_g):
    b, s, _ = proj3.shape
    scal = jnp.stack([a_log, dt_bias]).astype(F32)
    width = GDN_HEADS * GDN_D
    n_chunks = s // GDN_CHUNK
    cb = min(GDN_PREP_CHUNKS, n_chunks)
    ts = cb * GDN_CHUNK
    halo = 8
    egl_rows = -(-cb // 8) * 8

    def main(col0):
        return pl.BlockSpec((None, ts, width), lambda bi, i: (bi, i, col0 * LANE // width))

    def before(col0):
        return pl.BlockSpec((None, halo, width),
                            lambda bi, i: (bi, jnp.maximum(i * (ts // halo) - 1, 0),
                                           col0 * LANE // width))

    def out_block(w_):
        return pl.BlockSpec((None, ts, w_), lambda bi, i: (bi, i, 0))

    u, w, qe, kd, attn, egl = pl.pallas_call(
        functools.partial(_gdn_prep_kernel, cb=cb),
        grid=(b, n_chunks // cb),
        in_specs=[pl.BlockSpec(memory_space=pltpu.SMEM),
                  main(COL_GQ), before(COL_GQ), main(COL_GK), before(COL_GK),
                  main(COL_GV), before(COL_GV),
                  pl.BlockSpec((None, ts, LANE), lambda bi, i: (bi, i, COL_GAB)),
                  pl.BlockSpec(conv_w.shape, lambda bi, i: (0, 0))],
        out_specs=[out_block(width), out_block(width), out_block(width), out_block(width),
                   out_block(GDN_HEADS * GDN_CHUNK),
                   pl.BlockSpec((None, None, egl_rows, width), lambda bi, i: (bi, i, 0, 0))],
        out_shape=[jax.ShapeDtypeStruct((b, s, width), F32),
                   jax.ShapeDtypeStruct((b, s, width), BF16),
                   jax.ShapeDtypeStruct((b, s, width), BF16),
                   jax.ShapeDtypeStruct((b, s, width), BF16),
                   jax.ShapeDtypeStruct((b, s, GDN_HEADS * GDN_CHUNK), BF16),
                   jax.ShapeDtypeStruct((b, n_chunks // cb, egl_rows, width), F32)],
        compiler_params=_cparams(2),
        name="gdn_prep",
    )(scal, proj3, proj3, proj3, proj3, proj3, proj3, proj3, conv_w)

    bb = 2 if b % 2 == 0 else 1
    t_scan = min(GDN_SCAN_TOKENS, s)

    def scan_block(w_):
        return pl.BlockSpec((bb, t_scan, w_), lambda bi, j: (bi, j, 0))

    return pl.pallas_call(
        functools.partial(_gdn_scan_kernel, bb=bb, ts=t_scan, cb=cb),
        grid=(b // bb, s // t_scan),
        in_specs=[scan_block(width), scan_block(width), scan_block(width), scan_block(width),
                  scan_block(GDN_HEADS * GDN_CHUNK),
                  pl.BlockSpec((bb, n_chunks // cb, egl_rows, width), lambda bi, j: (bi, 0, 0, 0)),
                  pl.BlockSpec((bb, t_scan, width), lambda bi, j: (bi, j, COL_GZ * LANE // width)),
                  pl.BlockSpec((1, GDN_D), lambda bi, j: (0, 0))],
        out_specs=scan_block(width),
        out_shape=jax.ShapeDtypeStruct((b, s, width), BF16),
        scratch_shapes=[pltpu.VMEM((bb, GDN_HEADS, GDN_D, GDN_D), F32)],
        compiler_params=_cparams(2),
        name="gdn_scan",
    )(u, w, qe, kd, attn, egl, proj3, norm_g.reshape(1, GDN_D))


def _nsa_kernel(q_ref, kcmp_ref, vcmp_ref, kslc_ref, vslc_ref, kwin_ref, vwin_ref, gate_ref,
                wk_ref, wv_ref, pk_ref, pv_ref, o_ref,
                kc_scr, vct_scr, vst_scr, vwt_scr, selk_scr, *, seq, tq):
    qi = pl.program_id(1)
    tk = NSA_TK
    dh = NSA_DH
    hg = NSA_GROUP
    n_grp = NSA_KV_HEADS
    nq = hg * tq
    nb = seq // CMP_STRIDE
    n_sel = seq // SEL_LEN
    n_kb = seq // tk
    top_k = min(SEL_TOPK, n_sel)
    half = CMP_LEN // 2
    slc_step = min(NSA_SLC_BLOCKS, n_kb)
    n_win = WINDOW // tk + 1

    @pl.when(qi == 0)
    def _():
        cmp_out = []
        for src, pos, w_ref in ((kcmp_ref, pk_ref, wk_ref), (vcmp_ref, pv_ref, wv_ref)):
            lo_parts, hi_parts = [], []
            for l in range(half):
                rows = src[pl.ds(l, nb, stride=CMP_STRIDE), :]
                lo_parts.append(rows + pos[l:l + 1, :])
                hi_parts.append(rows + pos[half + l:half + l + 1, :])
            a_lo = _dot(jnp.concatenate(lo_parts, axis=1), w_ref[0], HIGHEST)
            a_hi = _dot(jnp.concatenate(hi_parts, axis=1), w_ref[1], HIGHEST)
            a_hi = jnp.concatenate([a_hi[1:], jnp.zeros((1, LANE), F32)], axis=0)
            cmp_out.append(a_lo + a_hi)
        kc_scr[...] = cmp_out[0]
        vct_scr[...] = cmp_out[1].T.astype(BF16)

        def transpose_values(kb, carry):
            k0 = pl.multiple_of(kb * tk, tk)
            vst_scr[kb] = vslc_ref[pl.ds(k0, tk), :].astype(F32).T.astype(BF16)
            vwt_scr[kb] = vwin_ref[pl.ds(k0, tk), :].astype(F32).T.astype(BF16)
            return carry

        lax.fori_loop(0, n_kb, transpose_values, 0)

    lane = lax.broadcasted_iota(jnp.int32, (tq, LANE), 1)
    qg = []
    for g in range(n_grp):
        parts = []
        for i in range(hg):
            hd = g * hg + i
            pair = q_ref[:, (hd // 2) * LANE:(hd // 2 + 1) * LANE] * (dh ** -0.5)
            if hd % 2 != g:
                pair = pltpu.roll(pair, dh, axis=1)
            parts.append(jnp.where((lane >= g * dh) & (lane < (g + 1) * dh), pair, 0.0))
        qg.append(jnp.concatenate(parts, axis=0))

    t_row = qi * tq + lax.broadcasted_iota(jnp.int32, (1, tq), 1)
    t_row4 = jnp.concatenate([t_row] * hg, axis=1)
    key_iota = lax.broadcasted_iota(jnp.int32, (tk, 1), 0)

    j_sel = lax.broadcasted_iota(jnp.int32, (n_sel, tq), 0)
    cur = lax.shift_right_logical(t_row, SEL_LEN.bit_length() - 1)
    sel_valid = j_sel <= cur
    forced = (j_sel == 0) | (j_sel == cur) | (j_sel == cur - 1)
    ov_j = lax.broadcasted_iota(jnp.int32, (n_sel, nb), 0) * SEL_LEN
    ov_n = lax.broadcasted_iota(jnp.int32, (n_sel, nb), 1) * CMP_STRIDE
    overlap = jnp.where((ov_n < ov_j + SEL_LEN) & (ov_n + CMP_LEN > ov_j), 1.0, 0.0).astype(BF16)
    n_idx = lax.broadcasted_iota(jnp.int32, (nb, 1), 0)
    cmp_valid = (n_idx * CMP_STRIDE + CMP_LEN - 1 <= t_row4) & (n_idx < nb - 1)

    def mask_heads(ok, s):
        return jnp.where(jnp.concatenate([ok] * hg, axis=1), s, MASK_NEG)

    groups = range(n_grp)
    kc_hi, kc_lo = _split_bf16(kc_scr[...])
    q_split = [_split_bf16(qg[g]) for g in groups]
    q_b = [q_split[g][0] for g in groups]

    win_blocks = []
    for w in range(n_win):
        kb_int = qi - w
        kb = jnp.maximum(kb_int, 0)
        kpos = kb_int * tk + key_iota
        ok = (kpos <= t_row) & (kpos > t_row - WINDOW) & (kpos >= 0)
        k_blk = kwin_ref[pl.ds(pl.multiple_of(kb * tk, tk), tk), :].astype(BF16)
        win_blocks.append((kb, k_blk, ok))

    s_cmp = [_nt_dot(kc_hi, q_split[g][0]) + _nt_dot(kc_lo, q_split[g][0])
             + _nt_dot(kc_hi, q_split[g][1]) for g in groups]
    s_win = [jnp.concatenate([mask_heads(ok, _nt_dot(k_blk, q_b[g]))
                              for (_, k_blk, ok) in win_blocks], axis=0) for g in groups]

    p_cmp, p_win, l_win = [], [], []
    for g in groups:
        s_m = jnp.where(cmp_valid, s_cmp[g], MASK_NEG)
        e = jnp.where(cmp_valid, jnp.exp(s_m - jnp.max(s_m, axis=0, keepdims=True)), 0.0)
        den = jnp.sum(e, axis=0, keepdims=True)
        p_cmp.append(jnp.where(den > 0.0, e / jnp.where(den > 0.0, den, 1.0), 0.0))
    for g in groups:
        p = jnp.exp(s_win[g] - jnp.max(s_win[g], axis=0, keepdims=True))
        l_win.append(jnp.sum(p, axis=0, keepdims=True))
        p_win.append(p.astype(BF16))

    o_cmp = [_dot(vct_scr[g * dh:(g + 1) * dh, :], p_cmp[g].astype(BF16)) for g in groups]
    imp = []
    for g in groups:
        p_sum = p_cmp[g][:, 0:tq]
        for i in range(1, hg):
            p_sum = p_sum + p_cmp[g][:, i * tq:(i + 1) * tq]
        ps_hi, ps_lo = _split_bf16(p_sum)
        imp.append(_dot(overlap, ps_hi) + _dot(overlap, ps_lo))
    o_win = []
    for g in groups:
        v_t = jnp.concatenate([vwt_scr[kb, g * dh:(g + 1) * dh, :] for (kb, _, _) in win_blocks],
                              axis=1)
        o_win.append(_dot(v_t, p_win[g]) / l_win[g])

    for g in groups:
        score = jnp.where(sel_valid, imp[g] + jnp.where(forced, FORCE_BONUS, 0.0), MASK_NEG)
        beaten = jnp.zeros((n_sel, tq), F32)
        for jp in range(n_sel):
            other = score[jp:jp + 1, :]
            ge = jnp.where(other >= score, 1.0, 0.0)
            gt = jnp.where(other > score, 1.0, 0.0)
            beaten = beaten + jnp.where(j_sel > jp, ge, gt)
        picked = jnp.where(beaten < top_k, 1.0, 0.0)
        for j in range(n_sel):
            selk_scr[g, j * SEL_LEN:(j + 1) * SEL_LEN, :] = jnp.broadcast_to(
                picked[j:j + 1, :], (SEL_LEN, tq))

    def slc_body(it, carry):
        kb0 = it * slc_step
        k_blocks = [kslc_ref[pl.ds(pl.multiple_of((kb0 + u) * tk, tk), tk), :].astype(BF16)
                    for u in range(slc_step)]
        tiles = {}
        for g in range(n_grp):
            for u in range(slc_step):
                k0 = pl.multiple_of((kb0 + u) * tk, tk)
                ok = (selk_scr[g, pl.ds(k0, tk), :] > 0.5) & (k0 + key_iota <= t_row)
                tiles[g, u] = mask_heads(ok, _nt_dot(k_blocks[u], q_b[g]))
        new = []
        for g in range(n_grp):
            m, l, acc = carry[g]
            for u in range(slc_step):
                s = tiles[g, u]
                m_new = jnp.maximum(m, jnp.max(s, axis=0, keepdims=True))
                alpha = jnp.exp(m - m_new)
                p = jnp.exp(s - m_new)
                l = alpha * l + jnp.sum(p, axis=0, keepdims=True)
                acc = alpha * acc + _dot(vst_scr[kb0 + u, g * dh:(g + 1) * dh, :], p.astype(BF16))
                m = m_new
            new.append((m, l, acc))
        return tuple(new)

    init = tuple((jnp.full((1, nq), MASK_NEG, F32), jnp.zeros((1, nq), F32),
                  jnp.zeros((dh, nq), F32)) for _ in range(n_grp))
    span = slc_step * tk
    n_iter = ((qi + 1) * tq + span - 1) // span
    slc = lax.fori_loop(0, n_iter, slc_body, init)
    o_slc = [acc / l for (_, l, acc) in slc]

    gates = _sigmoid(gate_ref[...].T)
    outs = []
    for g in range(n_grp):
        for i in range(hg):
            hd = g * hg + i
            cols = slice(i * tq, (i + 1) * tq)
            outs.append(gates[3 * hd:3 * hd + 1, :] * o_cmp[g][:, cols]
                        + gates[3 * hd + 1:3 * hd + 2, :] * o_slc[g][:, cols]
                        + gates[3 * hd + 2:3 * hd + 3, :] * o_win[g][:, cols])
    o_ref[...] = jnp.concatenate(outs, axis=0).T.astype(o_ref.dtype)


def _cmp_weight(w):
    half = CMP_LEN // 2
    w = w.reshape(2, half, NSA_DH, NSA_DH)
    z = jnp.zeros_like(w)
    big = jnp.stack([jnp.concatenate([w, z], axis=-1), jnp.concatenate([z, w], axis=-1)], axis=2)
    return big.reshape(2, half * NSA_KV_HEADS * NSA_DH, NSA_KV_HEADS * NSA_DH)


def _native_sparse_attention(proj3, proj16, cmp_wk, cmp_pk, cmp_wv, cmp_pv, *, tq):
    b, s, _ = proj3.shape
    nb = s // CMP_STRIDE
    n_kb = s // NSA_TK
    assert tq == NSA_TK and n_kb % min(NSA_SLC_BLOCKS, n_kb) == 0
    wk = _cmp_weight(cmp_wk)
    wv = _cmp_weight(cmp_wv)
    pk = jnp.tile(cmp_pk, (1, NSA_KV_HEADS))
    pv = jnp.tile(cmp_pv, (1, NSA_KV_HEADS))
    qw = NSA_HEADS * NSA_DH

    def seq_block(col):
        return pl.BlockSpec((None, s, LANE), lambda bi, i: (bi, 0, col))

    def whole(a):
        return pl.BlockSpec(a.shape, lambda bi, i: (0,) * a.ndim)

    return pl.pallas_call(
        functools.partial(_nsa_kernel, seq=s, tq=tq),
        grid=(b, s // tq),
        in_specs=[pl.BlockSpec((None, tq, qw), lambda bi, i: (bi, i, COL_NQ * LANE // qw))]
        + [seq_block(COL_NCMP + c) for c in range(2)]
        + [seq_block(COL16_NKV + c) for c in range(4)]
        + [pl.BlockSpec((None, tq, LANE), lambda bi, i: (bi, i, COL_NGATE)),
           whole(wk), whole(wv), whole(pk), whole(pv)],
        out_specs=pl.BlockSpec((None, tq, qw), lambda bi, i: (bi, i, 0)),
        out_shape=jax.ShapeDtypeStruct((b, s, qw), BF16),
        scratch_shapes=[pltpu.VMEM((nb, LANE), F32),
                        pltpu.VMEM((LANE, nb), BF16),
                        pltpu.VMEM((n_kb, LANE, NSA_TK), BF16),
                        pltpu.VMEM((n_kb, LANE, NSA_TK), BF16),
                        pltpu.VMEM((NSA_KV_HEADS, s, tq), F32)],
        compiler_params=_cparams(2),
        name="native_sparse_attention",
    )(proj3, proj3, proj3, proj16, proj16, proj16, proj16, proj3, wk, wv, pk, pv)


def _pad_cols(w, width):
    return jnp.pad(w, ((0, 0), (0, width - w.shape[1])))


def _proj_weight(w_in):
    n_ab = 2 * GDN_HEADS
    n_gate = 3 * NSA_HEADS
    n_cmp = 2 * NSA_KV_HEADS * NSA_DH
    n_slc_win = 4 * NSA_KV_HEADS * NSA_DH
    n_sb = 3 * SB_HEADS * SB_DH
    o_ab = 4 * GDN_HEADS * GDN_D
    o_nq = o_ab + n_ab
    o_cmp = o_nq + NSA_HEADS * NSA_DH
    o_slc = o_cmp + n_cmp
    o_gate = o_slc + n_slc_win
    o_sb = o_gate + n_gate
    o_merge = o_sb + n_sb
    w = jnp.concatenate([w_in[:, :o_ab], w_in[:, o_nq:o_slc],
                         _pad_cols(w_in[:, o_ab:o_nq], LANE),
                         _pad_cols(w_in[:, o_gate:o_sb], LANE),
                         w_in[:, o_slc:o_gate], w_in[:, o_sb:o_merge], w_in[:, o_merge:]], axis=1)
    assert w.shape[1] == N_PROJ32 + N_PROJ16, w.shape
    return w.astype(BF16)


def _layer(x, p, *, tiles):
    b, s, d = x.shape
    x2 = x.reshape(b * s, d)
    proj, proj16 = _norm_matmul(x2, p["g_mix_pre"], _proj_weight(p["w_in"]),
                                tm=tiles["tm"], tn=tiles["tn"])
    proj3 = proj.reshape(b, s, N_PROJ32)
    proj16 = proj16.reshape(b, s, N_PROJ16)
    ya = _gated_deltanet(proj3, p["gdn_conv_w"], p["gdn_a_log"], p["gdn_dt_bias"], p["gdn_norm_g"])
    yb = _native_sparse_attention(proj3, proj16, p["nsa_cmp_wk"], p["nsa_cmp_pk"], p["nsa_cmp_wv"],
                                  p["nsa_cmp_pv"], tq=tiles["tq"])
    yc = _sb_attention(proj16, tq=tiles["tq_sb"])
    x2 = _merge_out(ya.reshape(b * s, -1), yb.reshape(b * s, -1), yc.reshape(b * s, -1),
                    proj16.reshape(b * s, N_PROJ16), x2,
                    p["w_br_gdn"].astype(BF16), p["w_br_nsa"].astype(BF16),
                    p["w_br_sb"].astype(BF16), p["w_out"].astype(BF16), p["g_mix_post"],
                    tm=tiles["tm_out"])
    x2 = _mlp(x2, p["g_ff_pre"], p["w_ff1"].astype(BF16), p["w_ff2"].astype(BF16), p["g_ff_post"],
              tm=tiles["tm"], tn=tiles["tn"])
    return x2.reshape(b, s, d)


def _tiles(b, s):
    t = b * s
    return {"tm": min(1024, t), "tn": 1024, "tm_out": min(512, t), "tq": min(128, s),
            "tq_sb": min(256, s)}


def kernel(x, g_mix_pre, g_mix_post, g_ff_pre, g_ff_post, w_in, gdn_conv_w, gdn_a_log, gdn_dt_bias,
           gdn_norm_g, nsa_cmp_wk, nsa_cmp_pk, nsa_cmp_wv, nsa_cmp_pv, w_br_gdn, w_br_nsa, w_br_sb,
           w_out, w_ff1, w_ff2):
    params = dict(g_mix_pre=g_mix_pre, g_mix_post=g_mix_post, g_ff_pre=g_ff_pre, g_ff_post=g_ff_post,
                  w_in=w_in, gdn_conv_w=gdn_conv_w, gdn_a_log=gdn_a_log, gdn_dt_bias=gdn_dt_bias,
                  gdn_norm_g=gdn_norm_g, nsa_cmp_wk=nsa_cmp_wk, nsa_cmp_pk=nsa_cmp_pk,
                  nsa_cmp_wv=nsa_cmp_wv, nsa_cmp_pv=nsa_cmp_pv, w_br_gdn=w_br_gdn,
                  w_br_nsa=w_br_nsa, w_br_sb=w_br_sb, w_out=w_out, w_ff1=w_ff1, w_ff2=w_ff2)
    tiles = _tiles(x.shape[0], x.shape[1])
    for layer in range(w_in.shape[0]):
        x = _layer(x, {k: v[layer] for k, v in params.items()}, tiles=tiles)
    return x
```

```python
import functools

import jax
import jax.numpy as jnp
from jax import lax
from jax.experimental import pallas as pl
from jax.experimental.pallas import tpu as pltpu

F32 = jnp.float32
BF16 = jnp.bfloat16
HIGHEST = lax.Precision.HIGHEST

EPS = 1e-6
MASK_NEG = -1e30
LOG2E = 1.4426950408889634

GDN_HEADS = 4
GDN_D = 128
GDN_CONV = 4
GDN_CHUNK = 64
GDN_PREP_CHUNKS = 4
GDN_SCAN_TOKENS = 512

NSA_HEADS = 8
NSA_KV_HEADS = 2
NSA_GROUP = NSA_HEADS // NSA_KV_HEADS
NSA_DH = 64
CMP_LEN = 32
CMP_STRIDE = 16
SEL_LEN = 64
SEL_TOPK = 8
WINDOW = 512
FORCE_BONUS = 1e3
NSA_TK = 128
NSA_SLC_BLOCKS = 4

SB_HEADS = 8
SB_DH = 64
SB_TK = 128
SB_DEAD_LOG2 = -160.0

LANE = 128

COL_GQ, COL_GK, COL_GV, COL_GZ = 0, 4, 8, 12
COL_NQ, COL_NKV, COL_GAB, COL_NGATE = 16, 20, 26, 27
COL_SQ, COL_SK, COL_SV = 28, 32, 36
COL_MERGE = 40
N_PROJ = 64 * LANE

VMEM_LIMIT = 48 * 1024 * 1024


def _cparams(n_axes):
    return pltpu.CompilerParams(dimension_semantics=("arbitrary",) * n_axes,
                                vmem_limit_bytes=VMEM_LIMIT)


def _nt_dot(a, b, precision=None):
    return lax.dot_general(a, b, (((1,), (1,)), ((), ())), precision=precision,
                           preferred_element_type=F32)


def _tn_dot(a, b):
    return lax.dot_general(a, b, (((0,), (0,)), ((), ())), preferred_element_type=F32)


def _dot(a, b, precision=None):
    return jnp.dot(a, b, precision=precision, preferred_element_type=F32)


def _rms_scale(y):
    return y * lax.rsqrt(jnp.mean(y * y, axis=-1, keepdims=True) + EPS)


def _softplus(z):
    return jnp.maximum(z, 0.0) + jnp.log1p(jnp.exp(-jnp.abs(z)))


def _sigmoid(z):
    return jax.nn.sigmoid(z)


def _split_bf16(x):
    hi = x.astype(BF16)
    return hi, (x - hi.astype(F32)).astype(BF16)


def _norm_matmul_kernel(x_ref, g_ref, w_ref, o_ref, h_scr, *, relu2):
    @pl.when(pl.program_id(1) == 0)
    def _():
        h_scr[...] = (_rms_scale(x_ref[...]) * g_ref[...]).astype(BF16)

    y = _dot(h_scr[...], w_ref[...])
    if relu2:
        y = jnp.square(jnp.maximum(y, 0.0))
    o_ref[...] = y.astype(o_ref.dtype)


def _norm_matmul(x, g, w, *, relu2, out_dtype, tm, tn):
    t, d = x.shape
    n = w.shape[1]
    return pl.pallas_call(
        functools.partial(_norm_matmul_kernel, relu2=relu2),
        grid=(t // tm, n // tn),
        in_specs=[pl.BlockSpec((tm, d), lambda i, j: (i, 0)),
                  pl.BlockSpec((1, d), lambda i, j: (0, 0)),
                  pl.BlockSpec((d, tn), lambda i, j: (0, j))],
        out_specs=pl.BlockSpec((tm, tn), lambda i, j: (i, j)),
        out_shape=jax.ShapeDtypeStruct((t, n), out_dtype),
        scratch_shapes=[pltpu.VMEM((tm, d), BF16)],
        compiler_params=_cparams(2),
        name="norm_matmul_relu2" if relu2 else "norm_matmul",
    )(x, g.reshape(1, d), w)


def _matmul_norm_res_kernel(a_ref, w_ref, x_ref, g_ref, o_ref):
    y = _dot(a_ref[...], w_ref[...])
    o_ref[...] = x_ref[...] + _rms_scale(y) * g_ref[...]


def _matmul_norm_res(a, w, x, g, *, tm):
    t, k = a.shape
    d = w.shape[1]
    return pl.pallas_call(
        _matmul_norm_res_kernel,
        grid=(t // tm,),
        in_specs=[pl.BlockSpec((tm, k), lambda i: (i, 0)),
                  pl.BlockSpec((k, d), lambda i: (0, 0)),
                  pl.BlockSpec((tm, d), lambda i: (i, 0)),
                  pl.BlockSpec((1, d), lambda i: (0, 0))],
        out_specs=pl.BlockSpec((tm, d), lambda i: (i, 0)),
        out_shape=jax.ShapeDtypeStruct((t, d), F32),
        compiler_params=_cparams(1),
        name="matmul_norm_res",
    )(a, w, x, g.reshape(1, d))


def _merge_out_kernel(ya_ref, yb_ref, yc_ref, m0_ref, m1_ref, m2_ref, x_ref,
                      wa_ref, wb_ref, wc_ref, wo_ref, g_ref, o_ref):
    merged = (_sigmoid(m0_ref[...]) * _dot(ya_ref[...], wa_ref[...])
              + _sigmoid(m1_ref[...]) * _dot(yb_ref[...], wb_ref[...])
              + _sigmoid(m2_ref[...]) * _dot(yc_ref[...], wc_ref[...]))
    y = _dot(merged.astype(BF16), wo_ref[...])
    o_ref[...] = x_ref[...] + _rms_scale(y) * g_ref[...]


def _merge_out(ya, yb, yc, proj, x, wa, wb, wc, wo, g, *, tm):
    t, d = x.shape
    w_in = ya.shape[1]
    mcol = COL_MERGE * LANE // d

    def row(i):
        return (i, 0)

    def const(i):
        return (0, 0)

    return pl.pallas_call(
        _merge_out_kernel,
        grid=(t // tm,),
        in_specs=[pl.BlockSpec((tm, w_in), row),
                  pl.BlockSpec((tm, w_in), row),
                  pl.BlockSpec((tm, w_in), row),
                  pl.BlockSpec((tm, d), lambda i: (i, mcol)),
                  pl.BlockSpec((tm, d), lambda i: (i, mcol + 1)),
                  pl.BlockSpec((tm, d), lambda i: (i, mcol + 2)),
                  pl.BlockSpec((tm, d), row),
                  pl.BlockSpec((w_in, d), const),
                  pl.BlockSpec((w_in, d), const),
                  pl.BlockSpec((w_in, d), const),
                  pl.BlockSpec((d, d), const),
                  pl.BlockSpec((1, d), const)],
        out_specs=pl.BlockSpec((tm, d), row),
        out_shape=jax.ShapeDtypeStruct((t, d), F32),
        compiler_params=_cparams(1),
        name="merge_out",
    )(ya, yb, yc, proj, proj, proj, x, wa, wb, wc, wo, g.reshape(1, d))


def _sb_kernel(q_ref, k_ref, v_ref, o_ref, run_scr, acc_scr, *, tq):
    qi = pl.program_id(1)
    tk = SB_TK
    n_diag = tq // tk
    dh = SB_DH
    n_pair = SB_HEADS * dh // LANE
    row = lax.broadcasted_iota(jnp.int32, (tq, tk), 0)
    col = lax.broadcasted_iota(jnp.int32, (tq, tk), 1)
    first_head = lax.broadcasted_iota(jnp.int32, (tk, LANE), 1) < dh
    r2 = lax.broadcasted_iota(jnp.int32, (2 * tk, 2 * tk), 0) & (tk - 1)
    c2 = lax.broadcasted_iota(jnp.int32, (2 * tk, 2 * tk), 1)
    later_ones = jnp.where((r2 > c2) | (c2 >= tk), 1.0, 0.0).astype(BF16)

    q_pairs = [(q_ref[:, p * LANE:(p + 1) * LANE] * (dh ** -0.5 * LOG2E)).astype(BF16)
               for p in range(n_pair)]

    run_scr[...] = jnp.zeros_like(run_scr)
    acc_scr[...] = jnp.zeros_like(acc_scr)

    def split_heads(x):
        zero = jnp.zeros_like(x)
        return jnp.concatenate([jnp.where(first_head, x, zero), jnp.where(first_head, zero, x)],
                               axis=0)

    def block(j, diag_offset):
        k0 = pl.multiple_of(j * tk, tk)
        pairs = range(n_pair)
        diagonal = diag_offset is not None
        if diagonal:
            causal = col + diag_offset < row
            causal2 = jnp.concatenate([causal, causal], axis=1)
        z = [_nt_dot(q_pairs[p],
                     split_heads(k_ref[pl.ds(k0, tk), p * LANE:(p + 1) * LANE].astype(BF16)))
             for p in pairs]
        log_sig, hi, lo = [], [], []
        for p in pairs:
            log_sig.append(jnp.minimum(z[p], 0.0) - jnp.log2(1.0 + jnp.exp2(-jnp.abs(z[p]))))
            log_keep = log_sig[p] - z[p]
            if diagonal:
                log_keep = jnp.where(causal2, log_keep, 0.0)
            hi.append(log_keep.astype(BF16))
            lo.append((log_keep - hi[p].astype(F32)).astype(BF16))
        sums = [[_dot(jnp.concatenate([hi[p][:, hh * tk:(hh + 1) * tk],
                                       lo[p][:, hh * tk:(hh + 1) * tk]], axis=1), later_ones)
                 for hh in range(2)] for p in pairs]
        for p in pairs:
            a = []
            for hh in range(2):
                h = 2 * p + hh
                run = run_scr[h]
                a_h = jnp.exp2(log_sig[p][:, hh * tk:(hh + 1) * tk] + sums[p][hh][:, :tk] + run)
                if diagonal:
                    a_h = jnp.where(causal, a_h, 0.0)
                run_scr[h] = run + sums[p][hh][:, tk:]
                a.append(a_h.astype(BF16))
            vp = v_ref[pl.ds(k0, tk), p * LANE:(p + 1) * LANE].astype(BF16)
            acc_scr[p] += _dot(jnp.concatenate(a, axis=1), split_heads(vp))

    for d in reversed(range(n_diag)):
        block(qi * n_diag + d, d * tk)

    def still_live():
        top = run_scr[0]
        for h in range(1, SB_HEADS):
            top = jnp.maximum(top, run_scr[h])
        return jnp.max(top) > SB_DEAD_LOG2

    def body(carry):
        it, _ = carry
        block(qi * n_diag - 1 - it, None)
        return it + 1, still_live()

    lax.while_loop(lambda c: (c[0] < qi * n_diag) & c[1], body, (jnp.int32(0), still_live()))
    o_ref[...] = jnp.concatenate([acc_scr[p] for p in range(n_pair)], axis=-1).astype(o_ref.dtype)


def _sb_attention(proj3, *, tq):
    b, s, _ = proj3.shape
    width = SB_HEADS * SB_DH
    n_pair = width // LANE
    return pl.pallas_call(
        functools.partial(_sb_kernel, tq=tq),
        grid=(b, s // tq),
        in_specs=[pl.BlockSpec((None, tq, width), lambda bi, i: (bi, i, COL_SQ * LANE // width)),
                  pl.BlockSpec((None, s, width), lambda bi, i: (bi, 0, COL_SK * LANE // width)),
                  pl.BlockSpec((None, s, width), lambda bi, i: (bi, 0, COL_SV * LANE // width))],
        out_specs=pl.BlockSpec((None, tq, width), lambda bi, i: (bi, i, 0)),
        out_shape=jax.ShapeDtypeStruct((b, s, width), BF16),
        scratch_shapes=[pltpu.VMEM((SB_HEADS, tq, LANE), F32), pltpu.VMEM((n_pair, tq, LANE), F32)],
        compiler_params=_cparams(2),
        name="sb_attention",
    )(proj3, proj3, proj3)


def _gdn_prep_kernel(scal_ref, q_ref, qh_ref, k_ref, kh_ref, v_ref, vh_ref, ab_ref, cw_ref,
                     u_ref, w_ref, qe_ref, kd_ref, attn_ref, egl_ref, *, cb):
    i = pl.program_id(1)
    c_len = GDN_CHUNK
    d = GDN_D
    n_h = GDN_HEADS
    ts = cb * c_len
    halo = qh_ref.shape[0]
    width = n_h * d

    def conv_silu(x_ref, halo_ref, w):
        ext = jnp.concatenate([halo_ref[...] * jnp.where(i > 0, 1.0, 0.0), x_ref[...]], axis=0)
        y = jnp.zeros((ts, width), F32)
        for tap in range(GDN_CONV):
            off = halo - (GDN_CONV - 1) + tap
            y = y + w[tap:tap + 1, :] * ext[off:off + ts]
        return y * _sigmoid(y)

    qc = conv_silu(q_ref, qh_ref, cw_ref[:, 0:width])
    kc = conv_silu(k_ref, kh_ref, cw_ref[:, width:2 * width])
    vc = conv_silu(v_ref, vh_ref, cw_ref[:, 2 * width:3 * width])

    ab = ab_ref[...]
    lane = lax.broadcasted_iota(jnp.int32, (1, LANE), 1)
    a_log = jnp.zeros((1, LANE), F32)
    dt_bias = jnp.zeros((1, LANE), F32)
    for h in range(n_h):
        a_log = jnp.where(lane == h, scal_ref[0, h], a_log)
        dt_bias = jnp.where(lane == h, scal_ref[1, h], dt_bias)
    g_cum = -jnp.exp(a_log) * _softplus(ab + dt_bias)
    row_in_chunk = lax.broadcasted_iota(jnp.int32, (ts, LANE), 0) & (c_len - 1)
    shift = 1
    while shift < c_len:
        g_cum = g_cum + jnp.where(row_in_chunk >= shift, pltpu.roll(g_cum, shift, axis=0), 0.0)
        shift *= 2
    beta_all = _sigmoid(ab)

    row = lax.broadcasted_iota(jnp.int32, (c_len, c_len), 0)
    col = lax.broadcasted_iota(jnp.int32, (c_len, c_len), 1)
    incl = row >= col
    strict = row > col
    eye = jnp.where(row == col, 1.0, 0.0)

    problems = [(c, h) for c in range(cb) for h in range(n_h)]
    qn, kn, k_beta, v_beta, g_col, decay = {}, {}, {}, {}, {}, {}
    for c, h in problems:
        rows = slice(c * c_len, (c + 1) * c_len)
        cols = slice(h * d, (h + 1) * d)
        qh, kh = qc[rows, cols], kc[rows, cols]
        qn[c, h] = qh * lax.rsqrt(jnp.sum(qh * qh, axis=-1, keepdims=True) + EPS) * (d ** -0.5)
        kn[c, h] = kh * lax.rsqrt(jnp.sum(kh * kh, axis=-1, keepdims=True) + EPS)
        beta = beta_all[rows, n_h + h:n_h + h + 1]
        k_beta[c, h] = kn[c, h] * beta
        v_beta[c, h] = (vc[rows, cols] * beta).astype(BF16)
        g_col[c, h] = g_cum[rows, h:h + 1]
        g_sq = jnp.broadcast_to(g_col[c, h], (c_len, c_len))
        decay[c, h] = jnp.where(incl, jnp.exp(jnp.where(incl, g_sq - g_sq.T, 0.0)), 0.0)

    kn_b = {p: kn[p].astype(BF16) for p in problems}
    kkt = {p: _nt_dot(k_beta[p].astype(BF16), kn_b[p]) for p in problems}
    qkt = {p: _nt_dot(qn[p].astype(BF16), kn_b[p]) for p in problems}

    a_mat = {p: jnp.where(strict, kkt[p] * decay[p], 0.0) for p in problems}
    left = lax.broadcasted_iota(jnp.int32, (c_len, 2 * c_len), 1) < c_len
    pair = {p: jnp.concatenate([-a_mat[p], eye], axis=1) for p in problems}
    for _ in range(max(c_len - 1, 1).bit_length()):
        y = {p: _dot(pair[p][:, :c_len].astype(BF16), pair[p].astype(BF16)) for p in problems}
        pair = {p: jnp.where(left, y[p], pair[p] + y[p]) for p in problems}
    t_inv = {p: pair[p][:, c_len:] for p in problems}
    resid = {}
    for p in problems:
        m_hi, m_lo = _split_bf16(eye + a_mat[p])
        t_hi, t_lo = _split_bf16(t_inv[p])
        resid[p] = (eye - (_dot(m_hi, t_hi) + _dot(m_lo, t_hi) + _dot(m_hi, t_lo)), t_hi)
    t_b = {p: (t_inv[p] + _dot(resid[p][1], resid[p][0].astype(BF16))).astype(BF16)
           for p in problems}

    for c, h in problems:
        p = (c, h)
        rows = slice(c * c_len, (c + 1) * c_len)
        cols = slice(h * d, (h + 1) * d)
        exp_g = jnp.exp(g_col[p])
        g_last = g_col[p][c_len - 1:c_len, :]
        u_ref[rows, cols] = _dot(t_b[p], v_beta[p])
        w_ref[rows, cols] = _dot(t_b[p], (k_beta[p] * exp_g).astype(BF16)).astype(BF16)
        qe_ref[rows, cols] = (qn[p] * exp_g).astype(BF16)
        kd_ref[rows, cols] = (kn[p] * jnp.exp(g_last - g_col[p])).astype(BF16)
    egl_rows = [jnp.concatenate([jnp.broadcast_to(jnp.exp(g_col[c, h][c_len - 1:c_len, :]), (1, d))
                                 for h in range(n_h)], axis=1) for c in range(cb)]
    if egl_ref.shape[0] > cb:
        egl_rows.append(jnp.zeros((egl_ref.shape[0] - cb, width), F32))
    egl_ref[...] = jnp.concatenate(egl_rows, axis=0)
    for c in range(cb):
        rows = slice(c * c_len, (c + 1) * c_len)
        attn_ref[rows, :] = jnp.concatenate(
            [(qkt[c, h] * decay[c, h]).astype(BF16) for h in range(n_h)], axis=1)


def _gdn_scan_kernel(u_ref, w_ref, qe_ref, kd_ref, attn_ref, egl_ref, z_ref, ng_ref, o_ref,
                     state_scr, *, bb, ts, cb):
    j = pl.program_id(1)
    c_len = GDN_CHUNK
    d = GDN_D
    n_c = ts // c_len
    chains = [(b_, h) for b_ in range(bb) for h in range(GDN_HEADS)]

    @pl.when(j == 0)
    def _():
        state_scr[...] = jnp.zeros_like(state_scr)

    def chunk(c, carry):
        r0 = pl.multiple_of(c * c_len, c_len)
        rows = pl.ds(r0, c_len)
        cg = j * n_c + c
        state = {p: state_scr[p[0], p[1]] for p in chains}
        state_b = {p: state[p].astype(BF16) for p in chains}
        w_s = {(b_, h): _dot(w_ref[b_, rows, h * d:(h + 1) * d], state_b[b_, h]) for b_, h in chains}
        q_s = {(b_, h): _dot(qe_ref[b_, rows, h * d:(h + 1) * d], state_b[b_, h]) for b_, h in chains}
        v_new = {(b_, h): (u_ref[b_, rows, h * d:(h + 1) * d] - w_s[b_, h]).astype(BF16)
                 for b_, h in chains}
        sub = lax.broadcasted_iota(jnp.int32, egl_ref.shape[2:], 0)
        chunk_decay = [jnp.sum(jnp.where(sub == cg % cb, egl_ref[b_, cg // cb], 0.0),
                               axis=0, keepdims=True) for b_ in range(bb)]
        for b_, h in chains:
            cols = slice(h * d, (h + 1) * d)
            o = q_s[b_, h] + _dot(attn_ref[b_, rows, h * c_len:(h + 1) * c_len], v_new[b_, h])
            state_scr[b_, h] = (state[b_, h] * chunk_decay[b_][:, cols]
                                + _tn_dot(kd_ref[b_, rows, cols], v_new[b_, h]))
            zc = z_ref[b_, rows, cols]
            o_ref[b_, rows, cols] = (_rms_scale(o) * ng_ref[...] * (zc * _sigmoid(zc))
                                     ).astype(o_ref.dtype)
        return carry

    lax.fori_loop(0, n_c, chunk, 0)


def _gated_deltanet(proj3, conv_w, a_log, dt_bias, norm_g):
    b, s, _ = proj3.shape
    scal = jnp.stack([a_log, dt_bias]).astype(F32)
    width = GDN_HEADS * GDN_D
    n_chunks = s // GDN_CHUNK
    cb = min(GDN_PREP_CHUNKS, n_chunks)
    ts = cb * GDN_CHUNK
    halo = 8
    egl_rows = -(-cb // 8) * 8

    def main(col0):
        return pl.BlockSpec((None, ts, width), lambda bi, i: (bi, i, col0 * LANE // width))

    def before(col0):
        return pl.BlockSpec((None, halo, width),
                            lambda bi, i: (bi, jnp.maximum(i * (ts // halo) - 1, 0),
                                           col0 * LANE // width))

    def out_block(w_):
        return pl.BlockSpec((None, ts, w_), lambda bi, i: (bi, i, 0))

    u, w, qe, kd, attn, egl = pl.pallas_call(
        functools.partial(_gdn_prep_kernel, cb=cb),
        grid=(b, n_chunks // cb),
        in_specs=[pl.BlockSpec(memory_space=pltpu.SMEM),
                  main(COL_GQ), before(COL_GQ), main(COL_GK), before(COL_GK),
                  main(COL_GV), before(COL_GV),
                  pl.BlockSpec((None, ts, LANE), lambda bi, i: (bi, i, COL_GAB)),
                  pl.BlockSpec(conv_w.shape, lambda bi, i: (0, 0))],
        out_specs=[out_block(width), out_block(width), out_block(width), out_block(width),
                   out_block(GDN_HEADS * GDN_CHUNK),
                   pl.BlockSpec((None, None, egl_rows, width), lambda bi, i: (bi, i, 0, 0))],
        out_shape=[jax.ShapeDtypeStruct((b, s, width), F32),
                   jax.ShapeDtypeStruct((b, s, width), BF16),
                   jax.ShapeDtypeStruct((b, s, width), BF16),
                   jax.ShapeDtypeStruct((b, s, width), BF16),
                   jax.ShapeDtypeStruct((b, s, GDN_HEADS * GDN_CHUNK), BF16),
                   jax.ShapeDtypeStruct((b, n_chunks // cb, egl_rows, width), F32)],
        compiler_params=_cparams(2),
        name="gdn_prep",
    )(scal, proj3, proj3, proj3, proj3, proj3, proj3, proj3, conv_w)

    bb = 2 if b % 2 == 0 else 1
    t_scan = min(GDN_SCAN_TOKENS, s)

    def scan_block(w_):
        return pl.BlockSpec((bb, t_scan, w_), lambda bi, j: (bi, j, 0))

    return pl.pallas_call(
        functools.partial(_gdn_scan_kernel, bb=bb, ts=t_scan, cb=cb),
        grid=(b // bb, s // t_scan),
        in_specs=[scan_block(width), scan_block(width), scan_block(width), scan_block(width),
                  scan_block(GDN_HEADS * GDN_CHUNK),
                  pl.BlockSpec((bb, n_chunks // cb, egl_rows, width), lambda bi, j: (bi, 0, 0, 0)),
                  pl.BlockSpec((bb, t_scan, width), lambda bi, j: (bi, j, COL_GZ * LANE // width)),
                  pl.BlockSpec((1, GDN_D), lambda bi, j: (0, 0))],
        out_specs=scan_block(width),
        out_shape=jax.ShapeDtypeStruct((b, s, width), BF16),
        scratch_shapes=[pltpu.VMEM((bb, GDN_HEADS, GDN_D, GDN_D), F32)],
        compiler_params=_cparams(2),
        name="gdn_scan",
    )(u, w, qe, kd, attn, egl, proj3, norm_g.reshape(1, GDN_D))


def _nsa_kernel(q_ref, kcmp_ref, vcmp_ref, kslc_ref, vslc_ref, kwin_ref, vwin_ref, gate_ref,
                wk_ref, wv_ref, pk_ref, pv_ref, o_ref,
                kc_scr, vct_scr, vst_scr, vwt_scr, selk_scr, *, seq, tq):
    qi = pl.program_id(1)
    tk = NSA_TK
    dh = NSA_DH
    hg = NSA_GROUP
    n_grp = NSA_KV_HEADS
    nq = hg * tq
    nb = seq // CMP_STRIDE
    n_sel = seq // SEL_LEN
    n_kb = seq // tk
    top_k = min(SEL_TOPK, n_sel)
    half = CMP_LEN // 2
    slc_step = min(NSA_SLC_BLOCKS, n_kb)
    n_win = WINDOW // tk + 1

    @pl.when(qi == 0)
    def _():
        cmp_out = []
        for src, pos, w_ref in ((kcmp_ref, pk_ref, wk_ref), (vcmp_ref, pv_ref, wv_ref)):
            lo_parts, hi_parts = [], []
            for l in range(half):
                rows = src[pl.ds(l, nb, stride=CMP_STRIDE), :]
                lo_parts.append(rows + pos[l:l + 1, :])
                hi_parts.append(rows + pos[half + l:half + l + 1, :])
            a_lo = _dot(jnp.concatenate(lo_parts, axis=1), w_ref[0], HIGHEST)
            a_hi = _dot(jnp.concatenate(hi_parts, axis=1), w_ref[1], HIGHEST)
            a_hi = jnp.concatenate([a_hi[1:], jnp.zeros((1, LANE), F32)], axis=0)
            cmp_out.append(a_lo + a_hi)
        kc_scr[...] = cmp_out[0]
        vct_scr[...] = cmp_out[1].T.astype(BF16)

        def transpose_values(kb, carry):
            k0 = pl.multiple_of(kb * tk, tk)
            vst_scr[kb] = vslc_ref[pl.ds(k0, tk), :].T.astype(BF16)
            vwt_scr[kb] = vwin_ref[pl.ds(k0, tk), :].T.astype(BF16)
            return carry

        lax.fori_loop(0, n_kb, transpose_values, 0)

    lane = lax.broadcasted_iota(jnp.int32, (tq, LANE), 1)
    qg = []
    for g in range(n_grp):
        parts = []
        for i in range(hg):
            hd = g * hg + i
            pair = q_ref[:, (hd // 2) * LANE:(hd // 2 + 1) * LANE] * (dh ** -0.5)
            if hd % 2 != g:
                pair = pltpu.roll(pair, dh, axis=1)
            parts.append(jnp.where((lane >= g * dh) & (lane < (g + 1) * dh), pair, 0.0))
        qg.append(jnp.concatenate(parts, axis=0))

    t_row = qi * tq + lax.broadcasted_iota(jnp.int32, (1, tq), 1)
    t_row4 = jnp.concatenate([t_row] * hg, axis=1)
    key_iota = lax.broadcasted_iota(jnp.int32, (tk, 1), 0)

    j_sel = lax.broadcasted_iota(jnp.int32, (n_sel, tq), 0)
    cur = lax.shift_right_logical(t_row, SEL_LEN.bit_length() - 1)
    sel_valid = j_sel <= cur
    forced = (j_sel == 0) | (j_sel == cur) | (j_sel == cur - 1)
    ov_j = lax.broadcasted_iota(jnp.int32, (n_sel, nb), 0) * SEL_LEN
    ov_n = lax.broadcasted_iota(jnp.int32, (n_sel, nb), 1) * CMP_STRIDE
    overlap = jnp.where((ov_n < ov_j + SEL_LEN) & (ov_n + CMP_LEN > ov_j), 1.0, 0.0).astype(BF16)
    n_idx = lax.broadcasted_iota(jnp.int32, (nb, 1), 0)
    cmp_valid = (n_idx * CMP_STRIDE + CMP_LEN - 1 <= t_row4) & (n_idx < nb - 1)

    def mask_heads(ok, s):
        return jnp.where(jnp.concatenate([ok] * hg, axis=1), s, MASK_NEG)

    groups = range(n_grp)
    kc_hi, kc_lo = _split_bf16(kc_scr[...])
    q_split = [_split_bf16(qg[g]) for g in groups]
    q_b = [q_split[g][0] for g in groups]

    win_blocks = []
    for w in range(n_win):
        kb_int = qi - w
        kb = jnp.maximum(kb_int, 0)
        kpos = kb_int * tk + key_iota
        ok = (kpos <= t_row) & (kpos > t_row - WINDOW) & (kpos >= 0)
        k_blk = kwin_ref[pl.ds(pl.multiple_of(kb * tk, tk), tk), :].astype(BF16)
        win_blocks.append((kb, k_blk, ok))

    s_cmp = [_nt_dot(kc_hi, q_split[g][0]) + _nt_dot(kc_lo, q_split[g][0])
             + _nt_dot(kc_hi, q_split[g][1]) for g in groups]
    s_win = [jnp.concatenate([mask_heads(ok, _nt_dot(k_blk, q_b[g]))
                              for (_, k_blk, ok) in win_blocks], axis=0) for g in groups]

    p_cmp, p_win, l_win = [], [], []
    for g in groups:
        s_m = jnp.where(cmp_valid, s_cmp[g], MASK_NEG)
        e = jnp.where(cmp_valid, jnp.exp(s_m - jnp.max(s_m, axis=0, keepdims=True)), 0.0)
        den = jnp.sum(e, axis=0, keepdims=True)
        p_cmp.append(jnp.where(den > 0.0, e / jnp.where(den > 0.0, den, 1.0), 0.0))
    for g in groups:
        p = jnp.exp(s_win[g] - jnp.max(s_win[g], axis=0, keepdims=True))
        l_win.append(jnp.sum(p, axis=0, keepdims=True))
        p_win.append(p.astype(BF16))

    o_cmp = [_dot(vct_scr[g * dh:(g + 1) * dh, :], p_cmp[g].astype(BF16)) for g in groups]
    imp = []
    for g in groups:
        p_sum = p_cmp[g][:, 0:tq]
        for i in range(1, hg):
            p_sum = p_sum + p_cmp[g][:, i * tq:(i + 1) * tq]
        ps_hi, ps_lo = _split_bf16(p_sum)
        imp.append(_dot(overlap, ps_hi) + _dot(overlap, ps_lo))
    o_win = []
    for g in groups:
        v_t = jnp.concatenate([vwt_scr[kb, g * dh:(g + 1) * dh, :] for (kb, _, _) in win_blocks],
                              axis=1)
        o_win.append(_dot(v_t, p_win[g]) / l_win[g])

    for g in groups:
        score = jnp.where(sel_valid, imp[g] + jnp.where(forced, FORCE_BONUS, 0.0), MASK_NEG)
        beaten = jnp.zeros((n_sel, tq), F32)
        for jp in range(n_sel):
            other = score[jp:jp + 1, :]
            ge = jnp.where(other >= score, 1.0, 0.0)
            gt = jnp.where(other > score, 1.0, 0.0)
            beaten = beaten + jnp.where(j_sel > jp, ge, gt)
        picked = jnp.where(beaten < top_k, 1.0, 0.0)
        for j in range(n_sel):
            selk_scr[g, j * SEL_LEN:(j + 1) * SEL_LEN, :] = jnp.broadcast_to(
                picked[j:j + 1, :], (SEL_LEN, tq))

    def slc_blocks(n_tiles):
        def run(kb0, carry):
            k_blocks = [kslc_ref[pl.ds(pl.multiple_of((kb0 + u) * tk, tk), tk), :].astype(BF16)
                        for u in range(n_tiles)]
            tiles = {}
            for g in range(n_grp):
                for u in range(n_tiles):
                    k0 = pl.multiple_of((kb0 + u) * tk, tk)
                    ok = (selk_scr[g, pl.ds(k0, tk), :] > 0.5) & (k0 + key_iota <= t_row)
                    tiles[g, u] = mask_heads(ok, _nt_dot(k_blocks[u], q_b[g]))
            new = []
            for g in range(n_grp):
                m, l, acc = carry[g]
                for u in range(n_tiles):
                    s = tiles[g, u]
                    m_new = jnp.maximum(m, jnp.max(s, axis=0, keepdims=True))
                    alpha = jnp.exp(m - m_new)
                    p = jnp.exp(s - m_new)
                    l = alpha * l + jnp.sum(p, axis=0, keepdims=True)
                    acc = alpha * acc + _dot(vst_scr[kb0 + u, g * dh:(g + 1) * dh, :],
                                             p.astype(BF16))
                    m = m_new
                new.append((m, l, acc))
            return tuple(new)
        return run

    init = tuple((jnp.full((1, nq), MASK_NEG, F32), jnp.zeros((1, nq), F32),
                  jnp.zeros((dh, nq), F32)) for _ in range(n_grp))
    n_need = (qi + 1) * (tq // tk)
    full_body = slc_blocks(slc_step)
    slc = lax.fori_loop(0, n_need // slc_step, lambda it, c: full_body(it * slc_step, c), init)
    done = (n_need // slc_step) * slc_step
    arm = slc_step // 2
    while arm >= 1:
        take = (n_need & arm) != 0
        slc = lax.cond(take, functools.partial(slc_blocks(arm), done), lambda c: c, slc)
        done = done + jnp.where(take, arm, 0)
        arm //= 2
    o_slc = [acc / l for (_, l, acc) in slc]

    gates = _sigmoid(gate_ref[...].T)
    outs = []
    for g in range(n_grp):
        for i in range(hg):
            hd = g * hg + i
            cols = slice(i * tq, (i + 1) * tq)
            outs.append(gates[3 * hd:3 * hd + 1, :] * o_cmp[g][:, cols]
                        + gates[3 * hd + 1:3 * hd + 2, :] * o_slc[g][:, cols]
                        + gates[3 * hd + 2:3 * hd + 3, :] * o_win[g][:, cols])
    o_ref[...] = jnp.concatenate(outs, axis=0).T.astype(o_ref.dtype)


def _cmp_weight(w):
    half = CMP_LEN // 2
    w = w.reshape(2, half, NSA_DH, NSA_DH)
    z = jnp.zeros_like(w)
    big = jnp.stack([jnp.concatenate([w, z], axis=-1), jnp.concatenate([z, w], axis=-1)], axis=2)
    return big.reshape(2, half * NSA_KV_HEADS * NSA_DH, NSA_KV_HEADS * NSA_DH)


def _native_sparse_attention(proj3, cmp_wk, cmp_pk, cmp_wv, cmp_pv, *, tq):
    b, s, _ = proj3.shape
    nb = s // CMP_STRIDE
    n_kb = s // NSA_TK
    assert tq == NSA_TK and n_kb % min(NSA_SLC_BLOCKS, n_kb) == 0
    wk = _cmp_weight(cmp_wk)
    wv = _cmp_weight(cmp_wv)
    pk = jnp.tile(cmp_pk, (1, NSA_KV_HEADS))
    pv = jnp.tile(cmp_pv, (1, NSA_KV_HEADS))
    qw = NSA_HEADS * NSA_DH

    def seq_block(col):
        return pl.BlockSpec((None, s, LANE), lambda bi, i: (bi, 0, col))

    def whole(a):
        return pl.BlockSpec(a.shape, lambda bi, i: (0,) * a.ndim)

    return pl.pallas_call(
        functools.partial(_nsa_kernel, seq=s, tq=tq),
        grid=(b, s // tq),
        in_specs=[pl.BlockSpec((None, tq, qw), lambda bi, i: (bi, i, COL_NQ * LANE // qw))]
        + [seq_block(COL_NKV + c) for c in range(6)]
        + [pl.BlockSpec((None, tq, LANE), lambda bi, i: (bi, i, COL_NGATE)),
           whole(wk), whole(wv), whole(pk), whole(pv)],
        out_specs=pl.BlockSpec((None, tq, qw), lambda bi, i: (bi, i, 0)),
        out_shape=jax.ShapeDtypeStruct((b, s, qw), BF16),
        scratch_shapes=[pltpu.VMEM((nb, LANE), F32),
                        pltpu.VMEM((LANE, nb), BF16),
                        pltpu.VMEM((n_kb, LANE, NSA_TK), BF16),
                        pltpu.VMEM((n_kb, LANE, NSA_TK), BF16),
                        pltpu.VMEM((NSA_KV_HEADS, s, tq), F32)],
        compiler_params=_cparams(2),
        name="native_sparse_attention",
    )(proj3, proj3, proj3, proj3, proj3, proj3, proj3, proj3, wk, wv, pk, pv)


def _pad_cols(w, width):
    return jnp.pad(w, ((0, 0), (0, width - w.shape[1])))


def _proj_weight(w_in):
    gqk_v_z = 4 * GDN_HEADS * GDN_D
    n_ab = 2 * GDN_HEADS
    n_nsa = (NSA_HEADS + 6 * NSA_KV_HEADS) * NSA_DH
    n_gate = 3 * NSA_HEADS
    o_ab = gqk_v_z
    o_nsa = o_ab + n_ab
    o_gate = o_nsa + n_nsa
    o_rest = o_gate + n_gate
    w = jnp.concatenate([w_in[:, :o_ab], w_in[:, o_nsa:o_gate],
                         _pad_cols(w_in[:, o_ab:o_nsa], LANE),
                         _pad_cols(w_in[:, o_gate:o_rest], LANE),
                         w_in[:, o_rest:]], axis=1)
    assert w.shape[1] == N_PROJ, w.shape
    return w.astype(BF16)


def _layer(x, p, *, tiles):
    b, s, d = x.shape
    x2 = x.reshape(b * s, d)
    proj = _norm_matmul(x2, p["g_mix_pre"], _proj_weight(p["w_in"]), relu2=False, out_dtype=F32,
                        tm=tiles["tm"], tn=tiles["tn"])
    proj3 = proj.reshape(b, s, N_PROJ)
    ya = _gated_deltanet(proj3, p["gdn_conv_w"], p["gdn_a_log"], p["gdn_dt_bias"], p["gdn_norm_g"])
    yb = _native_sparse_attention(proj3, p["nsa_cmp_wk"], p["nsa_cmp_pk"], p["nsa_cmp_wv"],
                                  p["nsa_cmp_pv"], tq=tiles["tq"])
    yc = _sb_attention(proj3, tq=tiles["tq_sb"])
    x2 = _merge_out(ya.reshape(b * s, -1), yb.reshape(b * s, -1), yc.reshape(b * s, -1), proj, x2,
                    p["w_br_gdn"].astype(BF16), p["w_br_nsa"].astype(BF16),
                    p["w_br_sb"].astype(BF16), p["w_out"].astype(BF16), p["g_mix_post"],
                    tm=tiles["tm_out"])
    act = _norm_matmul(x2, p["g_ff_pre"], p["w_ff1"].astype(BF16), relu2=True, out_dtype=BF16,
                       tm=tiles["tm"], tn=tiles["tn"])
    x2 = _matmul_norm_res(act, p["w_ff2"].astype(BF16), x2, p["g_ff_post"], tm=tiles["tm_out"])
    return x2.reshape(b, s, d)


def _tiles(b, s):
    t = b * s
    return {"tm": min(1024, t), "tn": 2048, "tm_out": min(512, t), "tq": min(128, s),
            "tq_sb": min(256, s)}


def kernel(x, g_mix_pre, g_mix_post, g_ff_pre, g_ff_post, w_in, gdn_conv_w, gdn_a_log, gdn_dt_bias,
           gdn_norm_g, nsa_cmp_wk, nsa_cmp_pk, nsa_cmp_wv, nsa_cmp_pv, w_br_gdn, w_br_nsa, w_br_sb,
           w_out, w_ff1, w_ff2):
    params = dict(g_mix_pre=g_mix_pre, g_mix_post=g_mix_post, g_ff_pre=g_ff_pre, g_ff_post=g_ff_post,
                  w_in=w_in, gdn_conv_w=gdn_conv_w, gdn_a_log=gdn_a_log, gdn_dt_bias=gdn_dt_bias,
                  gdn_norm_g=gdn_norm_g, nsa_cmp_wk=nsa_cmp_wk, nsa_cmp_pk=nsa_cmp_pk,
                  nsa_cmp_wv=nsa_cmp_wv, nsa_cmp_pv=nsa_cmp_pv, w_br_gdn=w_br_gdn,
                  w_br_nsa=w_br_nsa, w_br_sb=w_br_sb, w_out=w_out, w_ff1=w_ff1, w_ff2=w_ff2)
    tiles = _tiles(x.shape[0], x.shape[1])
    for layer in range(w_in.shape[0]):
        x = _layer(x, {k: v[layer] for k, v in params.items()}, tiles=tiles)
    return x
```

```python
import functools

import jax
import jax.numpy as jnp
from jax import lax
from jax.experimental import pallas as pl
from jax.experimental.pallas import tpu as pltpu

F32 = jnp.float32
BF16 = jnp.bfloat16
HIGHEST = lax.Precision.HIGHEST

EPS = 1e-6
MASK_NEG = -1e30
LOG2E = 1.4426950408889634

GDN_HEADS = 4
GDN_D = 128
GDN_CONV = 4
GDN_CHUNK = 64
GDN_PREP_CHUNKS = 8
GDN_SCAN_TOKENS = 512

NSA_HEADS = 8
NSA_KV_HEADS = 2
NSA_GROUP = NSA_HEADS // NSA_KV_HEADS
NSA_DH = 64
CMP_LEN = 32
CMP_STRIDE = 16
SEL_LEN = 64
SEL_TOPK = 8
WINDOW = 512
FORCE_BONUS = 1e3
NSA_TK = 128
NSA_SLC_BLOCKS = 4
NSA_BLOCKS_PER_STEP = 2

SB_HEADS = 8
SB_DH = 64
SB_TK = 128
SB_DEAD_LOG2 = -160.0
SB_BLOCKS_PER_STEP = 2

LANE = 128

COL_GQ, COL_GK, COL_GV, COL_GZ = 0, 4, 8, 12
COL_NQ, COL_NKV, COL_GAB, COL_NGATE = 16, 20, 26, 27
COL_SQ, COL_SK, COL_SV = 28, 32, 36
COL_MERGE = 40
N_PROJ = 64 * LANE

VMEM_LIMIT = 48 * 1024 * 1024


def _cparams(n_axes):
    return pltpu.CompilerParams(dimension_semantics=("arbitrary",) * n_axes,
                                vmem_limit_bytes=VMEM_LIMIT)


def _nt_dot(a, b, precision=None):
    return lax.dot_general(a, b, (((1,), (1,)), ((), ())), precision=precision,
                           preferred_element_type=F32)


def _tn_dot(a, b):
    return lax.dot_general(a, b, (((0,), (0,)), ((), ())), preferred_element_type=F32)


def _dot(a, b, precision=None):
    return jnp.dot(a, b, precision=precision, preferred_element_type=F32)


def _rms_scale(y):
    return y * lax.rsqrt(jnp.mean(y * y, axis=-1, keepdims=True) + EPS)


def _softplus(z):
    return jnp.maximum(z, 0.0) + jnp.log1p(jnp.exp(-jnp.abs(z)))


def _sigmoid(z):
    return jax.nn.sigmoid(z)


def _split_bf16(x):
    hi = x.astype(BF16)
    return hi, (x - hi.astype(F32)).astype(BF16)


def _norm_matmul_kernel(x_ref, g_ref, w_ref, o_ref, h_scr, *, relu2):
    @pl.when(pl.program_id(1) == 0)
    def _():
        h_scr[...] = (_rms_scale(x_ref[...]) * g_ref[...]).astype(BF16)

    y = _dot(h_scr[...], w_ref[...])
    if relu2:
        y = jnp.square(jnp.maximum(y, 0.0))
    o_ref[...] = y.astype(o_ref.dtype)


def _norm_matmul(x, g, w, *, relu2, out_dtype, tm, tn):
    t, d = x.shape
    n = w.shape[1]
    return pl.pallas_call(
        functools.partial(_norm_matmul_kernel, relu2=relu2),
        grid=(t // tm, n // tn),
        in_specs=[pl.BlockSpec((tm, d), lambda i, j: (i, 0)),
                  pl.BlockSpec((1, d), lambda i, j: (0, 0)),
                  pl.BlockSpec((d, tn), lambda i, j: (0, j))],
        out_specs=pl.BlockSpec((tm, tn), lambda i, j: (i, j)),
        out_shape=jax.ShapeDtypeStruct((t, n), out_dtype),
        scratch_shapes=[pltpu.VMEM((tm, d), BF16)],
        compiler_params=_cparams(2),
        name="norm_matmul_relu2" if relu2 else "norm_matmul",
    )(x, g.reshape(1, d), w)


def _matmul_norm_res_kernel(a_ref, w_ref, x_ref, g_ref, o_ref):
    y = _dot(a_ref[...], w_ref[...])
    o_ref[...] = x_ref[...] + _rms_scale(y) * g_ref[...]


def _matmul_norm_res(a, w, x, g, *, tm):
    t, k = a.shape
    d = w.shape[1]
    return pl.pallas_call(
        _matmul_norm_res_kernel,
        grid=(t // tm,),
        in_specs=[pl.BlockSpec((tm, k), lambda i: (i, 0)),
                  pl.BlockSpec((k, d), lambda i: (0, 0)),
                  pl.BlockSpec((tm, d), lambda i: (i, 0)),
                  pl.BlockSpec((1, d), lambda i: (0, 0))],
        out_specs=pl.BlockSpec((tm, d), lambda i: (i, 0)),
        out_shape=jax.ShapeDtypeStruct((t, d), F32),
        compiler_params=_cparams(1),
        name="matmul_norm_res",
    )(a, w, x, g.reshape(1, d))


def _merge_out_kernel(ya_ref, yb_ref, yc_ref, m0_ref, m1_ref, m2_ref, x_ref,
                      wa_ref, wb_ref, wc_ref, wo_ref, g_ref, o_ref):
    merged = (_sigmoid(m0_ref[...]) * _dot(ya_ref[...], wa_ref[...])
              + _sigmoid(m1_ref[...]) * _dot(yb_ref[...], wb_ref[...])
              + _sigmoid(m2_ref[...]) * _dot(yc_ref[...], wc_ref[...]))
    y = _dot(merged.astype(BF16), wo_ref[...])
    o_ref[...] = x_ref[...] + _rms_scale(y) * g_ref[...]


def _merge_out(ya, yb, yc, proj, x, wa, wb, wc, wo, g, *, tm):
    t, d = x.shape
    w_in = ya.shape[1]
    mcol = COL_MERGE * LANE // d

    def row(i):
        return (i, 0)

    def const(i):
        return (0, 0)

    return pl.pallas_call(
        _merge_out_kernel,
        grid=(t // tm,),
        in_specs=[pl.BlockSpec((tm, w_in), row),
                  pl.BlockSpec((tm, w_in), row),
                  pl.BlockSpec((tm, w_in), row),
                  pl.BlockSpec((tm, d), lambda i: (i, mcol)),
                  pl.BlockSpec((tm, d), lambda i: (i, mcol + 1)),
                  pl.BlockSpec((tm, d), lambda i: (i, mcol + 2)),
                  pl.BlockSpec((tm, d), row),
                  pl.BlockSpec((w_in, d), const),
                  pl.BlockSpec((w_in, d), const),
                  pl.BlockSpec((w_in, d), const),
                  pl.BlockSpec((d, d), const),
                  pl.BlockSpec((1, d), const)],
        out_specs=pl.BlockSpec((tm, d), row),
        out_shape=jax.ShapeDtypeStruct((t, d), F32),
        compiler_params=_cparams(1),
        name="merge_out",
    )(ya, yb, yc, proj, proj, proj, x, wa, wb, wc, wo, g.reshape(1, d))


def _sb_kernel(q_ref, k_ref, v_ref, o_ref, run_scr, acc_scr, *, tq, n_sub):
    step = pl.program_id(1)
    for sub in range(n_sub):
        rows = pl.ds(sub * tq, tq)
        _sb_query_block(step * n_sub + sub, q_ref.at[rows], k_ref, v_ref, o_ref.at[rows],
                        run_scr, acc_scr, tq=tq)


def _sb_query_block(qi, q_ref, k_ref, v_ref, o_ref, run_scr, acc_scr, *, tq):
    tk = SB_TK
    n_diag = tq // tk
    dh = SB_DH
    n_pair = SB_HEADS * dh // LANE
    row = lax.broadcasted_iota(jnp.int32, (tq, tk), 0)
    col = lax.broadcasted_iota(jnp.int32, (tq, tk), 1)
    first_head = lax.broadcasted_iota(jnp.int32, (tk, LANE), 1) < dh
    r2 = lax.broadcasted_iota(jnp.int32, (2 * tk, 2 * tk), 0) & (tk - 1)
    c2 = lax.broadcasted_iota(jnp.int32, (2 * tk, 2 * tk), 1)
    later_ones = jnp.where((r2 > c2) | (c2 >= tk), 1.0, 0.0).astype(BF16)

    q_pairs = [(q_ref[:, p * LANE:(p + 1) * LANE] * (dh ** -0.5 * LOG2E)).astype(BF16)
               for p in range(n_pair)]

    run_scr[...] = jnp.zeros_like(run_scr)
    acc_scr[...] = jnp.zeros_like(acc_scr)

    def split_heads(x):
        zero = jnp.zeros_like(x)
        return jnp.concatenate([jnp.where(first_head, x, zero), jnp.where(first_head, zero, x)],
                               axis=0)

    def block(j, diag_offset):
        k0 = pl.multiple_of(j * tk, tk)
        pairs = range(n_pair)
        diagonal = diag_offset is not None
        if diagonal:
            causal = col + diag_offset < row
            causal2 = jnp.concatenate([causal, causal], axis=1)
        z = [_nt_dot(q_pairs[p],
                     split_heads(k_ref[pl.ds(k0, tk), p * LANE:(p + 1) * LANE].astype(BF16)))
             for p in pairs]
        log_sig, hi, lo = [], [], []
        for p in pairs:
            log_sig.append(jnp.minimum(z[p], 0.0) - jnp.log2(1.0 + jnp.exp2(-jnp.abs(z[p]))))
            log_keep = log_sig[p] - z[p]
            if diagonal:
                log_keep = jnp.where(causal2, log_keep, 0.0)
            hi.append(log_keep.astype(BF16))
            lo.append((log_keep - hi[p].astype(F32)).astype(BF16))
        sums = [[_dot(jnp.concatenate([hi[p][:, hh * tk:(hh + 1) * tk],
                                       lo[p][:, hh * tk:(hh + 1) * tk]], axis=1), later_ones)
                 for hh in range(2)] for p in pairs]
        for p in pairs:
            a = []
            for hh in range(2):
                h = 2 * p + hh
                run = run_scr[h]
                a_h = jnp.exp2(log_sig[p][:, hh * tk:(hh + 1) * tk] + sums[p][hh][:, :tk] + run)
                if diagonal:
                    a_h = jnp.where(causal, a_h, 0.0)
                run_scr[h] = run + sums[p][hh][:, tk:]
                a.append(a_h.astype(BF16))
            vp = v_ref[pl.ds(k0, tk), p * LANE:(p + 1) * LANE].astype(BF16)
            acc_scr[p] += _dot(jnp.concatenate(a, axis=1), split_heads(vp))

    for d in reversed(range(n_diag)):
        block(qi * n_diag + d, d * tk)

    def still_live():
        top = run_scr[0]
        for h in range(1, SB_HEADS):
            top = jnp.maximum(top, run_scr[h])
        return jnp.max(top) > SB_DEAD_LOG2

    def body(carry):
        it, _ = carry
        block(qi * n_diag - 1 - it, None)
        return it + 1, still_live()

    lax.while_loop(lambda c: (c[0] < qi * n_diag) & c[1], body, (jnp.int32(0), still_live()))
    o_ref[...] = jnp.concatenate([acc_scr[p] for p in range(n_pair)], axis=-1).astype(o_ref.dtype)


def _sb_attention(proj3, *, tq):
    b, s, _ = proj3.shape
    width = SB_HEADS * SB_DH
    n_pair = width // LANE
    n_sub = min(SB_BLOCKS_PER_STEP, s // tq)
    t_step = n_sub * tq
    return pl.pallas_call(
        functools.partial(_sb_kernel, tq=tq, n_sub=n_sub),
        grid=(b, s // t_step),
        in_specs=[pl.BlockSpec((None, t_step, width), lambda bi, i: (bi, i, COL_SQ * LANE // width)),
                  pl.BlockSpec((None, s, width), lambda bi, i: (bi, 0, COL_SK * LANE // width)),
                  pl.BlockSpec((None, s, width), lambda bi, i: (bi, 0, COL_SV * LANE // width))],
        out_specs=pl.BlockSpec((None, t_step, width), lambda bi, i: (bi, i, 0)),
        out_shape=jax.ShapeDtypeStruct((b, s, width), BF16),
        scratch_shapes=[pltpu.VMEM((SB_HEADS, tq, LANE), F32), pltpu.VMEM((n_pair, tq, LANE), F32)],
        compiler_params=_cparams(2),
        name="sb_attention",
    )(proj3, proj3, proj3)


def _gdn_prep_kernel(scal_ref, q_ref, qh_ref, k_ref, kh_ref, v_ref, vh_ref, ab_ref, cw_ref,
                     u_ref, w_ref, qe_ref, kd_ref, attn_ref, egl_ref, *, cb):
    i = pl.program_id(1)
    c_len = GDN_CHUNK
    d = GDN_D
    n_h = GDN_HEADS
    ts = cb * c_len
    halo = qh_ref.shape[0]
    width = n_h * d

    def conv_silu(x_ref, halo_ref, w):
        ext = jnp.concatenate([halo_ref[...] * jnp.where(i > 0, 1.0, 0.0), x_ref[...]], axis=0)
        y = jnp.zeros((ts, width), F32)
        for tap in range(GDN_CONV):
            off = halo - (GDN_CONV - 1) + tap
            y = y + w[tap:tap + 1, :] * ext[off:off + ts]
        return y * _sigmoid(y)

    qc = conv_silu(q_ref, qh_ref, cw_ref[:, 0:width])
    kc = conv_silu(k_ref, kh_ref, cw_ref[:, width:2 * width])
    vc = conv_silu(v_ref, vh_ref, cw_ref[:, 2 * width:3 * width])

    ab = ab_ref[...]
    lane = lax.broadcasted_iota(jnp.int32, (1, LANE), 1)
    a_log = jnp.zeros((1, LANE), F32)
    dt_bias = jnp.zeros((1, LANE), F32)
    for h in range(n_h):
        a_log = jnp.where(lane == h, scal_ref[0, h], a_log)
        dt_bias = jnp.where(lane == h, scal_ref[1, h], dt_bias)
    g_cum = -jnp.exp(a_log) * _softplus(ab + dt_bias)
    row_in_chunk = lax.broadcasted_iota(jnp.int32, (ts, LANE), 0) & (c_len - 1)
    shift = 1
    while shift < c_len:
        g_cum = g_cum + jnp.where(row_in_chunk >= shift, pltpu.roll(g_cum, shift, axis=0), 0.0)
        shift *= 2
    beta_all = _sigmoid(ab)

    row = lax.broadcasted_iota(jnp.int32, (c_len, c_len), 0)
    col = lax.broadcasted_iota(jnp.int32, (c_len, c_len), 1)
    incl = row >= col
    strict = row > col
    eye = jnp.where(row == col, 1.0, 0.0)

    problems = [(c, h) for c in range(cb) for h in range(n_h)]
    qn, kn, k_beta, v_beta, g_col, decay = {}, {}, {}, {}, {}, {}
    for c, h in problems:
        rows = slice(c * c_len, (c + 1) * c_len)
        cols = slice(h * d, (h + 1) * d)
        qh, kh = qc[rows, cols], kc[rows, cols]
        qn[c, h] = qh * lax.rsqrt(jnp.sum(qh * qh, axis=-1, keepdims=True) + EPS) * (d ** -0.5)
        kn[c, h] = kh * lax.rsqrt(jnp.sum(kh * kh, axis=-1, keepdims=True) + EPS)
        beta = beta_all[rows, n_h + h:n_h + h + 1]
        k_beta[c, h] = kn[c, h] * beta
        v_beta[c, h] = (vc[rows, cols] * beta).astype(BF16)
        g_col[c, h] = g_cum[rows, h:h + 1]
        g_sq = jnp.broadcast_to(g_col[c, h], (c_len, c_len))
        decay[c, h] = jnp.where(incl, jnp.exp(jnp.where(incl, g_sq - g_sq.T, 0.0)), 0.0)

    kn_b = {p: kn[p].astype(BF16) for p in problems}
    kkt = {p: _nt_dot(k_beta[p].astype(BF16), kn_b[p]) for p in problems}
    qkt = {p: _nt_dot(qn[p].astype(BF16), kn_b[p]) for p in problems}

    a_mat = {p: jnp.where(strict, kkt[p] * decay[p], 0.0) for p in problems}
    left = lax.broadcasted_iota(jnp.int32, (c_len, 2 * c_len), 1) < c_len
    pair = {p: jnp.concatenate([-a_mat[p], eye], axis=1) for p in problems}
    for _ in range(max(c_len - 1, 1).bit_length()):
        y = {p: _dot(pair[p][:, :c_len].astype(BF16), pair[p].astype(BF16)) for p in problems}
        pair = {p: jnp.where(left, y[p], pair[p] + y[p]) for p in problems}
    t_inv = {p: pair[p][:, c_len:] for p in problems}
    resid = {}
    for p in problems:
        m_hi, m_lo = _split_bf16(eye + a_mat[p])
        t_hi, t_lo = _split_bf16(t_inv[p])
        resid[p] = (eye - (_dot(m_hi, t_hi) + _dot(m_lo, t_hi) + _dot(m_hi, t_lo)), t_hi)
    t_b = {p: (t_inv[p] + _dot(resid[p][1], resid[p][0].astype(BF16))).astype(BF16)
           for p in problems}

    for c, h in problems:
        p = (c, h)
        rows = slice(c * c_len, (c + 1) * c_len)
        cols = slice(h * d, (h + 1) * d)
        exp_g = jnp.exp(g_col[p])
        g_last = g_col[p][c_len - 1:c_len, :]
        u_ref[rows, cols] = _dot(t_b[p], v_beta[p])
        w_ref[rows, cols] = _dot(t_b[p], (k_beta[p] * exp_g).astype(BF16)).astype(BF16)
        qe_ref[rows, cols] = (qn[p] * exp_g).astype(BF16)
        kd_ref[rows, cols] = (kn[p] * jnp.exp(g_last - g_col[p])).astype(BF16)
    egl_rows = [jnp.concatenate([jnp.broadcast_to(jnp.exp(g_col[c, h][c_len - 1:c_len, :]), (1, d))
                                 for h in range(n_h)], axis=1) for c in range(cb)]
    if egl_ref.shape[0] > cb:
        egl_rows.append(jnp.zeros((egl_ref.shape[0] - cb, width), F32))
    egl_ref[...] = jnp.concatenate(egl_rows, axis=0)
    for c in range(cb):
        rows = slice(c * c_len, (c + 1) * c_len)
        attn_ref[rows, :] = jnp.concatenate(
            [(qkt[c, h] * decay[c, h]).astype(BF16) for h in range(n_h)], axis=1)


def _gdn_scan_kernel(u_ref, w_ref, qe_ref, kd_ref, attn_ref, egl_ref, z_ref, ng_ref, o_ref,
                     state_scr, *, bb, ts, cb):
    j = pl.program_id(1)
    c_len = GDN_CHUNK
    d = GDN_D
    n_c = ts // c_len
    chains = [(b_, h) for b_ in range(bb) for h in range(GDN_HEADS)]

    @pl.when(j == 0)
    def _():
        state_scr[...] = jnp.zeros_like(state_scr)

    def chunk(c, carry):
        r0 = pl.multiple_of(c * c_len, c_len)
        rows = pl.ds(r0, c_len)
        cg = j * n_c + c
        state = {p: state_scr[p[0], p[1]] for p in chains}
        state_b = {p: state[p].astype(BF16) for p in chains}
        w_s = {(b_, h): _dot(w_ref[b_, rows, h * d:(h + 1) * d], state_b[b_, h]) for b_, h in chains}
        q_s = {(b_, h): _dot(qe_ref[b_, rows, h * d:(h + 1) * d], state_b[b_, h]) for b_, h in chains}
        v_new = {(b_, h): (u_ref[b_, rows, h * d:(h + 1) * d] - w_s[b_, h]).astype(BF16)
                 for b_, h in chains}
        sub = lax.broadcasted_iota(jnp.int32, egl_ref.shape[2:], 0)
        chunk_decay = [jnp.sum(jnp.where(sub == cg % cb, egl_ref[b_, cg // cb], 0.0),
                               axis=0, keepdims=True) for b_ in range(bb)]
        for b_, h in chains:
            cols = slice(h * d, (h + 1) * d)
            o = q_s[b_, h] + _dot(attn_ref[b_, rows, h * c_len:(h + 1) * c_len], v_new[b_, h])
            state_scr[b_, h] = (state[b_, h] * chunk_decay[b_][:, cols]
                                + _tn_dot(kd_ref[b_, rows, cols], v_new[b_, h]))
            zc = z_ref[b_, rows, cols]
            o_ref[b_, rows, cols] = (_rms_scale(o) * ng_ref[...] * (zc * _sigmoid(zc))
                                     ).astype(o_ref.dtype)
        return carry

    lax.fori_loop(0, n_c, chunk, 0)


def _gated_deltanet(proj3, conv_w, a_log, dt_bias, norm_g):
    b, s, _ = proj3.shape
    scal = jnp.stack([a_log, dt_bias]).astype(F32)
    width = GDN_HEADS * GDN_D
    n_chunks = s // GDN_CHUNK
    cb = min(GDN_PREP_CHUNKS, n_chunks)
    ts = cb * GDN_CHUNK
    halo = 8
    egl_rows = -(-cb // 8) * 8

    def main(col0):
        return pl.BlockSpec((None, ts, width), lambda bi, i: (bi, i, col0 * LANE // width))

    def before(col0):
        return pl.BlockSpec((None, halo, width),
                            lambda bi, i: (bi, jnp.maximum(i * (ts // halo) - 1, 0),
                                           col0 * LANE // width))

    def out_block(w_):
        return pl.BlockSpec((None, ts, w_), lambda bi, i: (bi, i, 0))

    u, w, qe, kd, attn, egl = pl.pallas_call(
        functools.partial(_gdn_prep_kernel, cb=cb),
        grid=(b, n_chunks // cb),
        in_specs=[pl.BlockSpec(memory_space=pltpu.SMEM),
                  main(COL_GQ), before(COL_GQ), main(COL_GK), before(COL_GK),
                  main(COL_GV), before(COL_GV),
                  pl.BlockSpec((None, ts, LANE), lambda bi, i: (bi, i, COL_GAB)),
                  pl.BlockSpec(conv_w.shape, lambda bi, i: (0, 0))],
        out_specs=[out_block(width), out_block(width), out_block(width), out_block(width),
                   out_block(GDN_HEADS * GDN_CHUNK),
                   pl.BlockSpec((None, None, egl_rows, width), lambda bi, i: (bi, i, 0, 0))],
        out_shape=[jax.ShapeDtypeStruct((b, s, width), F32),
                   jax.ShapeDtypeStruct((b, s, width), BF16),
                   jax.ShapeDtypeStruct((b, s, width), BF16),
                   jax.ShapeDtypeStruct((b, s, width), BF16),
                   jax.ShapeDtypeStruct((b, s, GDN_HEADS * GDN_CHUNK), BF16),
                   jax.ShapeDtypeStruct((b, n_chunks // cb, egl_rows, width), F32)],
        compiler_params=_cparams(2),
        name="gdn_prep",
    )(scal, proj3, proj3, proj3, proj3, proj3, proj3, proj3, conv_w)

    bb = 2 if b % 2 == 0 else 1
    t_scan = min(GDN_SCAN_TOKENS, s)

    def scan_block(w_):
        return pl.BlockSpec((bb, t_scan, w_), lambda bi, j: (bi, j, 0))

    return pl.pallas_call(
        functools.partial(_gdn_scan_kernel, bb=bb, ts=t_scan, cb=cb),
        grid=(b // bb, s // t_scan),
        in_specs=[scan_block(width), scan_block(width), scan_block(width), scan_block(width),
                  scan_block(GDN_HEADS * GDN_CHUNK),
                  pl.BlockSpec((bb, n_chunks // cb, egl_rows, width), lambda bi, j: (bi, 0, 0, 0)),
                  pl.BlockSpec((bb, t_scan, width), lambda bi, j: (bi, j, COL_GZ * LANE // width)),
                  pl.BlockSpec((1, GDN_D), lambda bi, j: (0, 0))],
        out_specs=scan_block(width),
        out_shape=jax.ShapeDtypeStruct((b, s, width), BF16),
        scratch_shapes=[pltpu.VMEM((bb, GDN_HEADS, GDN_D, GDN_D), F32)],
        compiler_params=_cparams(2),
        name="gdn_scan",
    )(u, w, qe, kd, attn, egl, proj3, norm_g.reshape(1, GDN_D))


def _nsa_stage_kv(kcmp_ref, vcmp_ref, vslc_ref, vwin_ref, wk_ref, wv_ref, pk_ref, pv_ref,
                  kc_scr, vct_scr, vst_scr, vwt_scr, *, seq):
    tk = NSA_TK
    nb = seq // CMP_STRIDE
    half = CMP_LEN // 2
    cmp_out = []
    for src, pos, w_ref in ((kcmp_ref, pk_ref, wk_ref), (vcmp_ref, pv_ref, wv_ref)):
        lo_parts, hi_parts = [], []
        for l in range(half):
            rows = src[pl.ds(l, nb, stride=CMP_STRIDE), :]
            lo_parts.append(rows + pos[l:l + 1, :])
            hi_parts.append(rows + pos[half + l:half + l + 1, :])
        a_lo = _dot(jnp.concatenate(lo_parts, axis=1), w_ref[0], HIGHEST)
        a_hi = _dot(jnp.concatenate(hi_parts, axis=1), w_ref[1], HIGHEST)
        a_hi = jnp.concatenate([a_hi[1:], jnp.zeros((1, LANE), F32)], axis=0)
        cmp_out.append(a_lo + a_hi)
    kc_scr[...] = cmp_out[0]
    vct_scr[...] = cmp_out[1].T.astype(BF16)

    def transpose_values(kb, carry):
        k0 = pl.multiple_of(kb * tk, tk)
        vst_scr[kb] = vslc_ref[pl.ds(k0, tk), :].T.astype(BF16)
        vwt_scr[kb] = vwin_ref[pl.ds(k0, tk), :].T.astype(BF16)
        return carry

    lax.fori_loop(0, seq // tk, transpose_values, 0)


def _nsa_kernel(q_ref, kcmp_ref, vcmp_ref, kslc_ref, vslc_ref, kwin_ref, vwin_ref, gate_ref,
                wk_ref, wv_ref, pk_ref, pv_ref, o_ref,
                kc_scr, vct_scr, vst_scr, vwt_scr, selk_scr, *, seq, tq, n_sub):
    step = pl.program_id(1)

    @pl.when(step == 0)
    def _():
        _nsa_stage_kv(kcmp_ref, vcmp_ref, vslc_ref, vwin_ref, wk_ref, wv_ref, pk_ref, pv_ref,
                      kc_scr, vct_scr, vst_scr, vwt_scr, seq=seq)

    for sub in range(n_sub):
        rows = pl.ds(sub * tq, tq)
        _nsa_query_block(step * n_sub + sub, q_ref.at[rows], kslc_ref, kwin_ref, gate_ref.at[rows],
                         o_ref.at[rows], kc_scr, vct_scr, vst_scr, vwt_scr, selk_scr,
                         seq=seq, tq=tq)


def _nsa_query_block(qi, q_ref, kslc_ref, kwin_ref, gate_ref, o_ref,
                     kc_scr, vct_scr, vst_scr, vwt_scr, selk_scr, *, seq, tq):
    tk = NSA_TK
    dh = NSA_DH
    hg = NSA_GROUP
    n_grp = NSA_KV_HEADS
    nq = hg * tq
    nb = seq // CMP_STRIDE
    n_sel = seq // SEL_LEN
    n_kb = seq // tk
    top_k = min(SEL_TOPK, n_sel)
    slc_step = min(NSA_SLC_BLOCKS, n_kb)
    n_win = WINDOW // tk + 1

    lane = lax.broadcasted_iota(jnp.int32, (tq, LANE), 1)
    qg = []
    for g in range(n_grp):
        parts = []
        for i in range(hg):
            hd = g * hg + i
            pair = q_ref[:, (hd // 2) * LANE:(hd // 2 + 1) * LANE] * (dh ** -0.5)
            if hd % 2 != g:
                pair = pltpu.roll(pair, dh, axis=1)
            parts.append(jnp.where((lane >= g * dh) & (lane < (g + 1) * dh), pair, 0.0))
        qg.append(jnp.concatenate(parts, axis=0))

    t_row = qi * tq + lax.broadcasted_iota(jnp.int32, (1, tq), 1)
    t_row4 = jnp.concatenate([t_row] * hg, axis=1)
    key_iota = lax.broadcasted_iota(jnp.int32, (tk, 1), 0)

    j_sel = lax.broadcasted_iota(jnp.int32, (n_sel, tq), 0)
    cur = lax.shift_right_logical(t_row, SEL_LEN.bit_length() - 1)
    sel_valid = j_sel <= cur
    forced = (j_sel == 0) | (j_sel == cur) | (j_sel == cur - 1)
    ov_j = lax.broadcasted_iota(jnp.int32, (n_sel, nb), 0) * SEL_LEN
    ov_n = lax.broadcasted_iota(jnp.int32, (n_sel, nb), 1) * CMP_STRIDE
    overlap = jnp.where((ov_n < ov_j + SEL_LEN) & (ov_n + CMP_LEN > ov_j), 1.0, 0.0).astype(BF16)
    n_idx = lax.broadcasted_iota(jnp.int32, (nb, 1), 0)
    cmp_valid = (n_idx * CMP_STRIDE + CMP_LEN - 1 <= t_row4) & (n_idx < nb - 1)

    def mask_heads(ok, s):
        return jnp.where(jnp.concatenate([ok] * hg, axis=1), s, MASK_NEG)

    groups = range(n_grp)
    kc_hi, kc_lo = _split_bf16(kc_scr[...])
    q_split = [_split_bf16(qg[g]) for g in groups]
    q_b = [q_split[g][0] for g in groups]

    win_blocks = []
    for w in range(n_win):
        kb_int = qi - w
        kb = jnp.maximum(kb_int, 0)
        kpos = kb_int * tk + key_iota
        ok = (kpos <= t_row) & (kpos > t_row - WINDOW) & (kpos >= 0)
        k_blk = kwin_ref[pl.ds(pl.multiple_of(kb * tk, tk), tk), :].astype(BF16)
        win_blocks.append((kb, k_blk, ok))

    s_cmp = [_nt_dot(kc_hi, q_split[g][0]) + _nt_dot(kc_lo, q_split[g][0])
             + _nt_dot(kc_hi, q_split[g][1]) for g in groups]
    s_win = [jnp.concatenate([mask_heads(ok, _nt_dot(k_blk, q_b[g]))
                              for (_, k_blk, ok) in win_blocks], axis=0) for g in groups]

    p_cmp, p_win, l_win = [], [], []
    for g in groups:
        s_m = jnp.where(cmp_valid, s_cmp[g], MASK_NEG)
        e = jnp.where(cmp_valid, jnp.exp(s_m - jnp.max(s_m, axis=0, keepdims=True)), 0.0)
        den = jnp.sum(e, axis=0, keepdims=True)
        p_cmp.append(jnp.where(den > 0.0, e / jnp.where(den > 0.0, den, 1.0), 0.0))
    for g in groups:
        p = jnp.exp(s_win[g] - jnp.max(s_win[g], axis=0, keepdims=True))
        l_win.append(jnp.sum(p, axis=0, keepdims=True))
        p_win.append(p.astype(BF16))

    o_cmp = [_dot(vct_scr[g * dh:(g + 1) * dh, :], p_cmp[g].astype(BF16)) for g in groups]
    imp = []
    for g in groups:
        p_sum = p_cmp[g][:, 0:tq]
        for i in range(1, hg):
            p_sum = p_sum + p_cmp[g][:, i * tq:(i + 1) * tq]
        ps_hi, ps_lo = _split_bf16(p_sum)
        imp.append(_dot(overlap, ps_hi) + _dot(overlap, ps_lo))
    o_win = []
    for g in groups:
        v_t = jnp.concatenate([vwt_scr[kb, g * dh:(g + 1) * dh, :] for (kb, _, _) in win_blocks],
                              axis=1)
        o_win.append(_dot(v_t, p_win[g]) / l_win[g])

    for g in groups:
        score = jnp.where(sel_valid, imp[g] + jnp.where(forced, FORCE_BONUS, 0.0), MASK_NEG)
        beaten = jnp.zeros((n_sel, tq), F32)
        for jp in range(n_sel):
            other = score[jp:jp + 1, :]
            ge = jnp.where(other >= score, 1.0, 0.0)
            gt = jnp.where(other > score, 1.0, 0.0)
            beaten = beaten + jnp.where(j_sel > jp, ge, gt)
        picked = jnp.where(beaten < top_k, 1.0, 0.0)
        for j in range(n_sel):
            selk_scr[g, j * SEL_LEN:(j + 1) * SEL_LEN, :] = jnp.broadcast_to(
                picked[j:j + 1, :], (SEL_LEN, tq))

    def slc_blocks(n_tiles):
        def run(kb0, carry):
            k_blocks = [kslc_ref[pl.ds(pl.multiple_of((kb0 + u) * tk, tk), tk), :].astype(BF16)
                        for u in range(n_tiles)]
            tiles = {}
            for g in range(n_grp):
                for u in range(n_tiles):
                    k0 = pl.multiple_of((kb0 + u) * tk, tk)
                    ok = (selk_scr[g, pl.ds(k0, tk), :] > 0.5) & (k0 + key_iota <= t_row)
                    tiles[g, u] = mask_heads(ok, _nt_dot(k_blocks[u], q_b[g]))
            new = []
            for g in range(n_grp):
                m, l, acc = carry[g]
                for u in range(n_tiles):
                    s = tiles[g, u]
                    m_new = jnp.maximum(m, jnp.max(s, axis=0, keepdims=True))
                    alpha = jnp.exp(m - m_new)
                    p = jnp.exp(s - m_new)
                    l = alpha * l + jnp.sum(p, axis=0, keepdims=True)
                    acc = alpha * acc + _dot(vst_scr[kb0 + u, g * dh:(g + 1) * dh, :],
                                             p.astype(BF16))
                    m = m_new
                new.append((m, l, acc))
            return tuple(new)
        return run

    init = tuple((jnp.full((1, nq), MASK_NEG, F32), jnp.zeros((1, nq), F32),
                  jnp.zeros((dh, nq), F32)) for _ in range(n_grp))
    n_need = (qi + 1) * (tq // tk)
    full_body = slc_blocks(slc_step)
    slc = lax.fori_loop(0, n_need // slc_step, lambda it, c: full_body(it * slc_step, c), init)
    done = (n_need // slc_step) * slc_step
    arm = slc_step // 2
    while arm >= 1:
        take = (n_need & arm) != 0
        slc = lax.cond(take, functools.partial(slc_blocks(arm), done), lambda c: c, slc)
        done = done + jnp.where(take, arm, 0)
        arm //= 2
    o_slc = [acc / l for (_, l, acc) in slc]

    gates = _sigmoid(gate_ref[...].T)
    outs = []
    for g in range(n_grp):
        for i in range(hg):
            hd = g * hg + i
            cols = slice(i * tq, (i + 1) * tq)
            outs.append(gates[3 * hd:3 * hd + 1, :] * o_cmp[g][:, cols]
                        + gates[3 * hd + 1:3 * hd + 2, :] * o_slc[g][:, cols]
                        + gates[3 * hd + 2:3 * hd + 3, :] * o_win[g][:, cols])
    o_ref[...] = jnp.concatenate(outs, axis=0).T.astype(o_ref.dtype)


def _cmp_weight(w):
    half = CMP_LEN // 2
    w = w.reshape(2, half, NSA_DH, NSA_DH)
    z = jnp.zeros_like(w)
    big = jnp.stack([jnp.concatenate([w, z], axis=-1), jnp.concatenate([z, w], axis=-1)], axis=2)
    return big.reshape(2, half * NSA_KV_HEADS * NSA_DH, NSA_KV_HEADS * NSA_DH)


def _native_sparse_attention(proj3, cmp_wk, cmp_pk, cmp_wv, cmp_pv, *, tq):
    b, s, _ = proj3.shape
    nb = s // CMP_STRIDE
    n_kb = s // NSA_TK
    assert tq == NSA_TK and n_kb % min(NSA_SLC_BLOCKS, n_kb) == 0
    wk = _cmp_weight(cmp_wk)
    wv = _cmp_weight(cmp_wv)
    pk = jnp.tile(cmp_pk, (1, NSA_KV_HEADS))
    pv = jnp.tile(cmp_pv, (1, NSA_KV_HEADS))
    qw = NSA_HEADS * NSA_DH

    def seq_block(col):
        return pl.BlockSpec((None, s, LANE), lambda bi, i: (bi, 0, col))

    def whole(a):
        return pl.BlockSpec(a.shape, lambda bi, i: (0,) * a.ndim)

    n_sub = min(NSA_BLOCKS_PER_STEP, s // tq)
    t_step = n_sub * tq
    return pl.pallas_call(
        functools.partial(_nsa_kernel, seq=s, tq=tq, n_sub=n_sub),
        grid=(b, s // t_step),
        in_specs=[pl.BlockSpec((None, t_step, qw), lambda bi, i: (bi, i, COL_NQ * LANE // qw))]
        + [seq_block(COL_NKV + c) for c in range(6)]
        + [pl.BlockSpec((None, t_step, LANE), lambda bi, i: (bi, i, COL_NGATE)),
           whole(wk), whole(wv), whole(pk), whole(pv)],
        out_specs=pl.BlockSpec((None, t_step, qw), lambda bi, i: (bi, i, 0)),
        out_shape=jax.ShapeDtypeStruct((b, s, qw), BF16),
        scratch_shapes=[pltpu.VMEM((nb, LANE), F32),
                        pltpu.VMEM((LANE, nb), BF16),
                        pltpu.VMEM((n_kb, LANE, NSA_TK), BF16),
                        pltpu.VMEM((n_kb, LANE, NSA_TK), BF16),
                        pltpu.VMEM((NSA_KV_HEADS, s, tq), F32)],
        compiler_params=_cparams(2),
        name="native_sparse_attention",
    )(proj3, proj3, proj3, proj3, proj3, proj3, proj3, proj3, wk, wv, pk, pv)


def _pad_cols(w, width):
    return jnp.pad(w, ((0, 0), (0, width - w.shape[1])))


def _proj_weight(w_in):
    gqk_v_z = 4 * GDN_HEADS * GDN_D
    n_ab = 2 * GDN_HEADS
    n_nsa = (NSA_HEADS + 6 * NSA_KV_HEADS) * NSA_DH
    n_gate = 3 * NSA_HEADS
    o_ab = gqk_v_z
    o_nsa = o_ab + n_ab
    o_gate = o_nsa + n_nsa
    o_rest = o_gate + n_gate
    w = jnp.concatenate([w_in[:, :o_ab], w_in[:, o_nsa:o_gate],
                         _pad_cols(w_in[:, o_ab:o_nsa], LANE),
                         _pad_cols(w_in[:, o_gate:o_rest], LANE),
                         w_in[:, o_rest:]], axis=1)
    assert w.shape[1] == N_PROJ, w.shape
    return w.astype(BF16)


def _layer(x, p, *, tiles):
    b, s, d = x.shape
    x2 = x.reshape(b * s, d)
    proj = _norm_matmul(x2, p["g_mix_pre"], _proj_weight(p["w_in"]), relu2=False, out_dtype=F32,
                        tm=tiles["tm"], tn=tiles["tn"])
    proj3 = proj.reshape(b, s, N_PROJ)
    ya = _gated_deltanet(proj3, p["gdn_conv_w"], p["gdn_a_log"], p["gdn_dt_bias"], p["gdn_norm_g"])
    yb = _native_sparse_attention(proj3, p["nsa_cmp_wk"], p["nsa_cmp_pk"], p["nsa_cmp_wv"],
                                  p["nsa_cmp_pv"], tq=tiles["tq"])
    yc = _sb_attention(proj3, tq=tiles["tq_sb"])
    x2 = _merge_out(ya.reshape(b * s, -1), yb.reshape(b * s, -1), yc.reshape(b * s, -1), proj, x2,
                    p["w_br_gdn"].astype(BF16), p["w_br_nsa"].astype(BF16),
                    p["w_br_sb"].astype(BF16), p["w_out"].astype(BF16), p["g_mix_post"],
                    tm=tiles["tm_out"])
    act = _norm_matmul(x2, p["g_ff_pre"], p["w_ff1"].astype(BF16), relu2=True, out_dtype=BF16,
                       tm=tiles["tm"], tn=tiles["tn"])
    x2 = _matmul_norm_res(act, p["w_ff2"].astype(BF16), x2, p["g_ff_post"], tm=tiles["tm_out"])
    return x2.reshape(b, s, d)


def _tiles(b, s):
    t = b * s
    return {"tm": min(1024, t), "tn": 2048, "tm_out": min(512, t), "tq": min(128, s),
            "tq_sb": min(256, s)}


def kernel(x, g_mix_pre, g_mix_post, g_ff_pre, g_ff_post, w_in, gdn_conv_w, gdn_a_log, gdn_dt_bias,
           gdn_norm_g, nsa_cmp_wk, nsa_cmp_pk, nsa_cmp_wv, nsa_cmp_pv, w_br_gdn, w_br_nsa, w_br_sb,
           w_out, w_ff1, w_ff2):
    params = dict(g_mix_pre=g_mix_pre, g_mix_post=g_mix_post, g_ff_pre=g_ff_pre, g_ff_post=g_ff_post,
                  w_in=w_in, gdn_conv_w=gdn_conv_w, gdn_a_log=gdn_a_log, gdn_dt_bias=gdn_dt_bias,
                  gdn_norm_g=gdn_norm_g, nsa_cmp_wk=nsa_cmp_wk, nsa_cmp_pk=nsa_cmp_pk,
                  nsa_cmp_wv=nsa_cmp_wv, nsa_cmp_pv=nsa_cmp_pv, w_br_gdn=w_br_gdn,
                  w_br_nsa=w_br_nsa, w_br_sb=w_br_sb, w_out=w_out, w_ff1=w_ff1, w_ff2=w_ff2)
    tiles = _tiles(x.shape[0], x.shape[1])
    for layer in range(w_in.shape[0]):
        x = _layer(x, {k: v[layer] for k, v in params.items()}, tiles=tiles)
    return x
```

```python
import functools

import jax
import jax.numpy as jnp
from jax import lax
from jax.experimental import pallas as pl
from jax.experimental.pallas import tpu as pltpu

F32 = jnp.float32
BF16 = jnp.bfloat16
HIGHEST = lax.Precision.HIGHEST

EPS = 1e-6
MASK_NEG = -1e30
LOG2E = 1.4426950408889634

GDN_HEADS = 4
GDN_D = 128
GDN_CONV = 4
GDN_CHUNK = 64
GDN_PREP_CHUNKS = 8
GDN_SCAN_TOKENS = 512

NSA_HEADS = 8
NSA_KV_HEADS = 2
NSA_GROUP = NSA_HEADS // NSA_KV_HEADS
NSA_DH = 64
CMP_LEN = 32
CMP_STRIDE = 16
SEL_LEN = 64
SEL_TOPK = 8
WINDOW = 512
FORCE_BONUS = 1e3
NSA_TK = 128
NSA_SLC_BLOCKS = 4
NSA_BLOCKS_PER_STEP = 2

SB_HEADS = 8
SB_DH = 64
SB_TK = 128
SB_DEAD_LOG2 = -160.0
SB_BLOCKS_PER_STEP = 2

LANE = 128

COL_GQ, COL_GK, COL_GV, COL_GZ = 0, 4, 8, 12
COL_NQ, COL_NKV, COL_GAB, COL_NGATE = 16, 20, 26, 27
COL_SQ, COL_SK, COL_SV = 28, 32, 36
COL_MERGE = 40
N_PROJ = 64 * LANE

VMEM_LIMIT = 48 * 1024 * 1024


def _cparams(n_axes):
    return pltpu.CompilerParams(dimension_semantics=("arbitrary",) * n_axes,
                                vmem_limit_bytes=VMEM_LIMIT)


def _nt_dot(a, b, precision=None):
    return lax.dot_general(a, b, (((1,), (1,)), ((), ())), precision=precision,
                           preferred_element_type=F32)


def _tn_dot(a, b):
    return lax.dot_general(a, b, (((0,), (0,)), ((), ())), preferred_element_type=F32)


def _dot(a, b, precision=None):
    return jnp.dot(a, b, precision=precision, preferred_element_type=F32)


def _rms_scale(y):
    return y * lax.rsqrt(jnp.mean(y * y, axis=-1, keepdims=True) + EPS)


def _softplus(z):
    return jnp.maximum(z, 0.0) + jnp.log1p(jnp.exp(-jnp.abs(z)))


def _sigmoid(z):
    return jax.nn.sigmoid(z)


def _split_bf16(x):
    hi = x.astype(BF16)
    return hi, (x - hi.astype(F32)).astype(BF16)


def _norm_matmul_kernel(x_ref, g_ref, w_ref, o_ref, h_scr, *, relu2):
    @pl.when(pl.program_id(1) == 0)
    def _():
        h_scr[...] = (_rms_scale(x_ref[...]) * g_ref[...]).astype(BF16)

    y = _dot(h_scr[...], w_ref[...])
    if relu2:
        y = jnp.square(jnp.maximum(y, 0.0))
    o_ref[...] = y.astype(o_ref.dtype)


def _norm_matmul(x, g, w, layer, *, relu2, out_dtype, tm, tn):
    t, d = x.shape
    n = w.shape[2]
    return pl.pallas_call(
        functools.partial(_norm_matmul_kernel, relu2=relu2),
        grid=(t // tm, n // tn),
        in_specs=[pl.BlockSpec((tm, d), lambda i, j: (i, 0)),
                  pl.BlockSpec((1, d), lambda i, j: (0, 0)),
                  pl.BlockSpec((None, d, tn), lambda i, j: (layer, 0, j))],
        out_specs=pl.BlockSpec((tm, tn), lambda i, j: (i, j)),
        out_shape=jax.ShapeDtypeStruct((t, n), out_dtype),
        scratch_shapes=[pltpu.VMEM((tm, d), BF16)],
        compiler_params=_cparams(2),
        name="norm_matmul_relu2" if relu2 else "norm_matmul",
    )(x, g.reshape(1, d), w)


def _matmul_norm_res_kernel(a_ref, w_ref, x_ref, g_ref, o_ref):
    y = _dot(a_ref[...], w_ref[...])
    o_ref[...] = x_ref[...] + _rms_scale(y) * g_ref[...]


def _matmul_norm_res(a, w, layer, x, g, *, tm):
    t, k = a.shape
    d = w.shape[2]
    return pl.pallas_call(
        _matmul_norm_res_kernel,
        grid=(t // tm,),
        in_specs=[pl.BlockSpec((tm, k), lambda i: (i, 0)),
                  pl.BlockSpec((None, k, d), lambda i: (layer, 0, 0)),
                  pl.BlockSpec((tm, d), lambda i: (i, 0)),
                  pl.BlockSpec((1, d), lambda i: (0, 0))],
        out_specs=pl.BlockSpec((tm, d), lambda i: (i, 0)),
        out_shape=jax.ShapeDtypeStruct((t, d), F32),
        compiler_params=_cparams(1),
        name="matmul_norm_res",
    )(a, w, x, g.reshape(1, d))


def _merge_out_kernel(ya_ref, yb_ref, yc_ref, m0_ref, m1_ref, m2_ref, x_ref,
                      wa_ref, wb_ref, wc_ref, wo_ref, g_ref, o_ref):
    merged = (_sigmoid(m0_ref[...]) * _dot(ya_ref[...], wa_ref[...])
              + _sigmoid(m1_ref[...]) * _dot(yb_ref[...], wb_ref[...])
              + _sigmoid(m2_ref[...]) * _dot(yc_ref[...], wc_ref[...]))
    y = _dot(merged.astype(BF16), wo_ref[...])
    o_ref[...] = x_ref[...] + _rms_scale(y) * g_ref[...]


def _merge_out(ya, yb, yc, proj, x, wa, wb, wc, wo, layer, g, *, tm):
    t, d = x.shape
    w_in = ya.shape[1]
    mcol = COL_MERGE * LANE // d

    def row(i):
        return (i, 0)

    def const(i):
        return (0, 0)

    def weight(i):
        return (layer, 0, 0)

    return pl.pallas_call(
        _merge_out_kernel,
        grid=(t // tm,),
        in_specs=[pl.BlockSpec((tm, w_in), row),
                  pl.BlockSpec((tm, w_in), row),
                  pl.BlockSpec((tm, w_in), row),
                  pl.BlockSpec((tm, d), lambda i: (i, mcol)),
                  pl.BlockSpec((tm, d), lambda i: (i, mcol + 1)),
                  pl.BlockSpec((tm, d), lambda i: (i, mcol + 2)),
                  pl.BlockSpec((tm, d), row),
                  pl.BlockSpec((None, w_in, d), weight),
                  pl.BlockSpec((None, w_in, d), weight),
                  pl.BlockSpec((None, w_in, d), weight),
                  pl.BlockSpec((None, d, d), weight),
                  pl.BlockSpec((1, d), const)],
        out_specs=pl.BlockSpec((tm, d), row),
        out_shape=jax.ShapeDtypeStruct((t, d), F32),
        compiler_params=_cparams(1),
        name="merge_out",
    )(ya, yb, yc, proj, proj, proj, x, wa, wb, wc, wo, g.reshape(1, d))


def _sb_kernel(q_ref, k_ref, v_ref, o_ref, run_scr, acc_scr, *, tq, n_sub):
    step = pl.program_id(1)
    for sub in range(n_sub):
        rows = pl.ds(sub * tq, tq)
        _sb_query_block(step * n_sub + sub, q_ref.at[rows], k_ref, v_ref, o_ref.at[rows],
                        run_scr, acc_scr, tq=tq)


def _sb_query_block(qi, q_ref, k_ref, v_ref, o_ref, run_scr, acc_scr, *, tq):
    tk = SB_TK
    n_diag = tq // tk
    dh = SB_DH
    n_pair = SB_HEADS * dh // LANE
    row = lax.broadcasted_iota(jnp.int32, (tq, tk), 0)
    col = lax.broadcasted_iota(jnp.int32, (tq, tk), 1)
    first_head = lax.broadcasted_iota(jnp.int32, (tk, LANE), 1) < dh
    r2 = lax.broadcasted_iota(jnp.int32, (2 * tk, 2 * tk), 0) & (tk - 1)
    c2 = lax.broadcasted_iota(jnp.int32, (2 * tk, 2 * tk), 1)
    later_ones = jnp.where((r2 > c2) | (c2 >= tk), 1.0, 0.0).astype(BF16)

    q_pairs = [(q_ref[:, p * LANE:(p + 1) * LANE] * (dh ** -0.5 * LOG2E)).astype(BF16)
               for p in range(n_pair)]

    run_scr[...] = jnp.zeros_like(run_scr)
    acc_scr[...] = jnp.zeros_like(acc_scr)

    def split_heads(x):
        zero = jnp.zeros_like(x)
        return jnp.concatenate([jnp.where(first_head, x, zero), jnp.where(first_head, zero, x)],
                               axis=0)

    def block(j, diag_offset):
        k0 = pl.multiple_of(j * tk, tk)
        pairs = range(n_pair)
        diagonal = diag_offset is not None
        if diagonal:
            causal = col + diag_offset < row
            causal2 = jnp.concatenate([causal, causal], axis=1)
        z = [_nt_dot(q_pairs[p],
                     split_heads(k_ref[pl.ds(k0, tk), p * LANE:(p + 1) * LANE].astype(BF16)))
             for p in pairs]
        log_sig, hi, lo = [], [], []
        for p in pairs:
            log_sig.append(jnp.minimum(z[p], 0.0) - jnp.log2(1.0 + jnp.exp2(-jnp.abs(z[p]))))
            log_keep = log_sig[p] - z[p]
            if diagonal:
                log_keep = jnp.where(causal2, log_keep, 0.0)
            hi.append(log_keep.astype(BF16))
            lo.append((log_keep - hi[p].astype(F32)).astype(BF16))
        sums = [[_dot(jnp.concatenate([hi[p][:, hh * tk:(hh + 1) * tk],
                                       lo[p][:, hh * tk:(hh + 1) * tk]], axis=1), later_ones)
                 for hh in range(2)] for p in pairs]
        for p in pairs:
            a = []
            for hh in range(2):
                h = 2 * p + hh
                run = run_scr[h]
                a_h = jnp.exp2(log_sig[p][:, hh * tk:(hh + 1) * tk] + sums[p][hh][:, :tk] + run)
                if diagonal:
                    a_h = jnp.where(causal, a_h, 0.0)
                run_scr[h] = run + sums[p][hh][:, tk:]
                a.append(a_h.astype(BF16))
            vp = v_ref[pl.ds(k0, tk), p * LANE:(p + 1) * LANE].astype(BF16)
            acc_scr[p] += _dot(jnp.concatenate(a, axis=1), split_heads(vp))

    for d in reversed(range(n_diag)):
        block(qi * n_diag + d, d * tk)

    def still_live():
        top = run_scr[0]
        for h in range(1, SB_HEADS):
            top = jnp.maximum(top, run_scr[h])
        return jnp.max(top) > SB_DEAD_LOG2

    def body(carry):
        it, _ = carry
        block(qi * n_diag - 1 - it, None)
        return it + 1, still_live()

    lax.while_loop(lambda c: (c[0] < qi * n_diag) & c[1], body, (jnp.int32(0), still_live()))
    o_ref[...] = jnp.concatenate([acc_scr[p] for p in range(n_pair)], axis=-1).astype(o_ref.dtype)


def _sb_attention(proj3, *, tq):
    b, s, _ = proj3.shape
    width = SB_HEADS * SB_DH
    n_pair = width // LANE
    n_sub = min(SB_BLOCKS_PER_STEP, s // tq)
    t_step = n_sub * tq
    return pl.pallas_call(
        functools.partial(_sb_kernel, tq=tq, n_sub=n_sub),
        grid=(b, s // t_step),
        in_specs=[pl.BlockSpec((None, t_step, width), lambda bi, i: (bi, i, COL_SQ * LANE // width)),
                  pl.BlockSpec((None, s, width), lambda bi, i: (bi, 0, COL_SK * LANE // width)),
                  pl.BlockSpec((None, s, width), lambda bi, i: (bi, 0, COL_SV * LANE // width))],
        out_specs=pl.BlockSpec((None, t_step, width), lambda bi, i: (bi, i, 0)),
        out_shape=jax.ShapeDtypeStruct((b, s, width), BF16),
        scratch_shapes=[pltpu.VMEM((SB_HEADS, tq, LANE), F32), pltpu.VMEM((n_pair, tq, LANE), F32)],
        compiler_params=_cparams(2),
        name="sb_attention",
    )(proj3, proj3, proj3)


def _gdn_prep_kernel(scal_ref, q_ref, qh_ref, k_ref, kh_ref, v_ref, vh_ref, ab_ref, cw_ref,
                     u_ref, w_ref, qe_ref, kd_ref, attn_ref, egl_ref, *, cb):
    i = pl.program_id(1)
    c_len = GDN_CHUNK
    d = GDN_D
    n_h = GDN_HEADS
    ts = cb * c_len
    halo = qh_ref.shape[0]
    width = n_h * d

    def conv_silu(x_ref, halo_ref, w):
        ext = jnp.concatenate([halo_ref[...] * jnp.where(i > 0, 1.0, 0.0), x_ref[...]], axis=0)
        y = jnp.zeros((ts, width), F32)
        for tap in range(GDN_CONV):
            off = halo - (GDN_CONV - 1) + tap
            y = y + w[tap:tap + 1, :] * ext[off:off + ts]
        return y * _sigmoid(y)

    qc = conv_silu(q_ref, qh_ref, cw_ref[:, 0:width])
    kc = conv_silu(k_ref, kh_ref, cw_ref[:, width:2 * width])
    vc = conv_silu(v_ref, vh_ref, cw_ref[:, 2 * width:3 * width])

    ab = ab_ref[...]
    lane = lax.broadcasted_iota(jnp.int32, (1, LANE), 1)
    a_log = jnp.zeros((1, LANE), F32)
    dt_bias = jnp.zeros((1, LANE), F32)
    for h in range(n_h):
        a_log = jnp.where(lane == h, scal_ref[0, h], a_log)
        dt_bias = jnp.where(lane == h, scal_ref[1, h], dt_bias)
    g_cum = -jnp.exp(a_log) * _softplus(ab + dt_bias)
    row_in_chunk = lax.broadcasted_iota(jnp.int32, (ts, LANE), 0) & (c_len - 1)
    shift = 1
    while shift < c_len:
        g_cum = g_cum + jnp.where(row_in_chunk >= shift, pltpu.roll(g_cum, shift, axis=0), 0.0)
        shift *= 2
    beta_all = _sigmoid(ab)

    row = lax.broadcasted_iota(jnp.int32, (c_len, c_len), 0)
    col = lax.broadcasted_iota(jnp.int32, (c_len, c_len), 1)
    incl = row >= col
    strict = row > col
    eye = jnp.where(row == col, 1.0, 0.0)

    problems = [(c, h) for c in range(cb) for h in range(n_h)]
    qn, kn, k_beta, v_beta, g_col, decay = {}, {}, {}, {}, {}, {}
    for c, h in problems:
        rows = slice(c * c_len, (c + 1) * c_len)
        cols = slice(h * d, (h + 1) * d)
        qh, kh = qc[rows, cols], kc[rows, cols]
        qn[c, h] = qh * lax.rsqrt(jnp.sum(qh * qh, axis=-1, keepdims=True) + EPS) * (d ** -0.5)
        kn[c, h] = kh * lax.rsqrt(jnp.sum(kh * kh, axis=-1, keepdims=True) + EPS)
        beta = beta_all[rows, n_h + h:n_h + h + 1]
        k_beta[c, h] = kn[c, h] * beta
        v_beta[c, h] = (vc[rows, cols] * beta).astype(BF16)
        g_col[c, h] = g_cum[rows, h:h + 1]
        g_sq = jnp.broadcast_to(g_col[c, h], (c_len, c_len))
        decay[c, h] = jnp.where(incl, jnp.exp(jnp.where(incl, g_sq - g_sq.T, 0.0)), 0.0)

    kn_b = {p: kn[p].astype(BF16) for p in problems}
    kkt = {p: _nt_dot(k_beta[p].astype(BF16), kn_b[p]) for p in problems}
    qkt = {p: _nt_dot(qn[p].astype(BF16), kn_b[p]) for p in problems}

    a_mat = {p: jnp.where(strict, kkt[p] * decay[p], 0.0) for p in problems}
    left = lax.broadcasted_iota(jnp.int32, (c_len, 2 * c_len), 1) < c_len
    pair = {p: jnp.concatenate([-a_mat[p], eye], axis=1) for p in problems}
    for _ in range(max(c_len - 1, 1).bit_length()):
        y = {p: _dot(pair[p][:, :c_len].astype(BF16), pair[p].astype(BF16)) for p in problems}
        pair = {p: jnp.where(left, y[p], pair[p] + y[p]) for p in problems}
    t_inv = {p: pair[p][:, c_len:] for p in problems}
    resid = {}
    for p in problems:
        m_hi, m_lo = _split_bf16(eye + a_mat[p])
        t_hi, t_lo = _split_bf16(t_inv[p])
        resid[p] = (eye - (_dot(m_hi, t_hi) + _dot(m_lo, t_hi) + _dot(m_hi, t_lo)), t_hi)
    t_b = {p: (t_inv[p] + _dot(resid[p][1], resid[p][0].astype(BF16))).astype(BF16)
           for p in problems}

    for c, h in problems:
        p = (c, h)
        rows = slice(c * c_len, (c + 1) * c_len)
        cols = slice(h * d, (h + 1) * d)
        exp_g = jnp.exp(g_col[p])
        g_last = g_col[p][c_len - 1:c_len, :]
        u_ref[rows, cols] = _dot(t_b[p], v_beta[p])
        w_ref[rows, cols] = _dot(t_b[p], (k_beta[p] * exp_g).astype(BF16)).astype(BF16)
        qe_ref[rows, cols] = (qn[p] * exp_g).astype(BF16)
        kd_ref[rows, cols] = (kn[p] * jnp.exp(g_last - g_col[p])).astype(BF16)
    egl_rows = [jnp.concatenate([jnp.broadcast_to(jnp.exp(g_col[c, h][c_len - 1:c_len, :]), (1, d))
                                 for h in range(n_h)], axis=1) for c in range(cb)]
    if egl_ref.shape[0] > cb:
        egl_rows.append(jnp.zeros((egl_ref.shape[0] - cb, width), F32))
    egl_ref[...] = jnp.concatenate(egl_rows, axis=0)
    for c in range(cb):
        rows = slice(c * c_len, (c + 1) * c_len)
        attn_ref[rows, :] = jnp.concatenate(
            [(qkt[c, h] * decay[c, h]).astype(BF16) for h in range(n_h)], axis=1)


def _gdn_scan_kernel(u_ref, w_ref, qe_ref, kd_ref, attn_ref, egl_ref, z_ref, ng_ref, o_ref,
                     state_scr, *, bb, ts, cb):
    j = pl.program_id(1)
    c_len = GDN_CHUNK
    d = GDN_D
    n_c = ts // c_len
    chains = [(b_, h) for b_ in range(bb) for h in range(GDN_HEADS)]

    @pl.when(j == 0)
    def _():
        state_scr[...] = jnp.zeros_like(state_scr)

    def chunk(c, carry):
        r0 = pl.multiple_of(c * c_len, c_len)
        rows = pl.ds(r0, c_len)
        cg = j * n_c + c
        state = {p: state_scr[p[0], p[1]] for p in chains}
        state_b = {p: state[p].astype(BF16) for p in chains}
        w_s = {(b_, h): _dot(w_ref[b_, rows, h * d:(h + 1) * d], state_b[b_, h]) for b_, h in chains}
        q_s = {(b_, h): _dot(qe_ref[b_, rows, h * d:(h + 1) * d], state_b[b_, h]) for b_, h in chains}
        v_new = {(b_, h): (u_ref[b_, rows, h * d:(h + 1) * d] - w_s[b_, h]).astype(BF16)
                 for b_, h in chains}
        sub = lax.broadcasted_iota(jnp.int32, egl_ref.shape[2:], 0)
        chunk_decay = [jnp.sum(jnp.where(sub == cg % cb, egl_ref[b_, cg // cb], 0.0),
                               axis=0, keepdims=True) for b_ in range(bb)]
        for b_, h in chains:
            cols = slice(h * d, (h + 1) * d)
            o = q_s[b_, h] + _dot(attn_ref[b_, rows, h * c_len:(h + 1) * c_len], v_new[b_, h])
            state_scr[b_, h] = (state[b_, h] * chunk_decay[b_][:, cols]
                                + _tn_dot(kd_ref[b_, rows, cols], v_new[b_, h]))
            zc = z_ref[b_, rows, cols]
            o_ref[b_, rows, cols] = (_rms_scale(o) * ng_ref[...] * (zc * _sigmoid(zc))
                                     ).astype(o_ref.dtype)
        return carry

    lax.fori_loop(0, n_c, chunk, 0)


def _gated_deltanet(proj3, conv_w, a_log, dt_bias, norm_g):
    b, s, _ = proj3.shape
    scal = jnp.stack([a_log, dt_bias]).astype(F32)
    width = GDN_HEADS * GDN_D
    n_chunks = s // GDN_CHUNK
    cb = min(GDN_PREP_CHUNKS, n_chunks)
    ts = cb * GDN_CHUNK
    halo = 8
    egl_rows = -(-cb // 8) * 8

    def main(col0):
        return pl.BlockSpec((None, ts, width), lambda bi, i: (bi, i, col0 * LANE // width))

    def before(col0):
        return pl.BlockSpec((None, halo, width),
                            lambda bi, i: (bi, jnp.maximum(i * (ts // halo) - 1, 0),
                                           col0 * LANE // width))

    def out_block(w_):
        return pl.BlockSpec((None, ts, w_), lambda bi, i: (bi, i, 0))

    u, w, qe, kd, attn, egl = pl.pallas_call(
        functools.partial(_gdn_prep_kernel, cb=cb),
        grid=(b, n_chunks // cb),
        in_specs=[pl.BlockSpec(memory_space=pltpu.SMEM),
                  main(COL_GQ), before(COL_GQ), main(COL_GK), before(COL_GK),
                  main(COL_GV), before(COL_GV),
                  pl.BlockSpec((None, ts, LANE), lambda bi, i: (bi, i, COL_GAB)),
                  pl.BlockSpec(conv_w.shape, lambda bi, i: (0, 0))],
        out_specs=[out_block(width), out_block(width), out_block(width), out_block(width),
                   out_block(GDN_HEADS * GDN_CHUNK),
                   pl.BlockSpec((None, None, egl_rows, width), lambda bi, i: (bi, i, 0, 0))],
        out_shape=[jax.ShapeDtypeStruct((b, s, width), F32),
                   jax.ShapeDtypeStruct((b, s, width), BF16),
                   jax.ShapeDtypeStruct((b, s, width), BF16),
                   jax.ShapeDtypeStruct((b, s, width), BF16),
                   jax.ShapeDtypeStruct((b, s, GDN_HEADS * GDN_CHUNK), BF16),
                   jax.ShapeDtypeStruct((b, n_chunks // cb, egl_rows, width), F32)],
        compiler_params=_cparams(2),
        name="gdn_prep",
    )(scal, proj3, proj3, proj3, proj3, proj3, proj3, proj3, conv_w)

    bb = 2 if b % 2 == 0 else 1
    t_scan = min(GDN_SCAN_TOKENS, s)

    def scan_block(w_):
        return pl.BlockSpec((bb, t_scan, w_), lambda bi, j: (bi, j, 0))

    return pl.pallas_call(
        functools.partial(_gdn_scan_kernel, bb=bb, ts=t_scan, cb=cb),
        grid=(b // bb, s // t_scan),
        in_specs=[scan_block(width), scan_block(width), scan_block(width), scan_block(width),
                  scan_block(GDN_HEADS * GDN_CHUNK),
                  pl.BlockSpec((bb, n_chunks // cb, egl_rows, width), lambda bi, j: (bi, 0, 0, 0)),
                  pl.BlockSpec((bb, t_scan, width), lambda bi, j: (bi, j, COL_GZ * LANE // width)),
                  pl.BlockSpec((1, GDN_D), lambda bi, j: (0, 0))],
        out_specs=scan_block(width),
        out_shape=jax.ShapeDtypeStruct((b, s, width), BF16),
        scratch_shapes=[pltpu.VMEM((bb, GDN_HEADS, GDN_D, GDN_D), F32)],
        compiler_params=_cparams(2),
        name="gdn_scan",
    )(u, w, qe, kd, attn, egl, proj3, norm_g.reshape(1, GDN_D))


def _nsa_stage_kv(kcmp_ref, vcmp_ref, vslc_ref, vwin_ref, wk_ref, wv_ref, pk_ref, pv_ref,
                  kc_scr, vct_scr, vst_scr, vwt_scr, *, seq):
    tk = NSA_TK
    nb = seq // CMP_STRIDE
    half = CMP_LEN // 2
    cmp_out = []
    for src, pos, w_ref in ((kcmp_ref, pk_ref, wk_ref), (vcmp_ref, pv_ref, wv_ref)):
        lo_parts, hi_parts = [], []
        for l in range(half):
            rows = src[pl.ds(l, nb, stride=CMP_STRIDE), :]
            lo_parts.append(rows + pos[l:l + 1, :])
            hi_parts.append(rows + pos[half + l:half + l + 1, :])
        a_lo = _dot(jnp.concatenate(lo_parts, axis=1), w_ref[0], HIGHEST)
        a_hi = _dot(jnp.concatenate(hi_parts, axis=1), w_ref[1], HIGHEST)
        a_hi = jnp.concatenate([a_hi[1:], jnp.zeros((1, LANE), F32)], axis=0)
        cmp_out.append(a_lo + a_hi)
    kc_scr[...] = cmp_out[0]
    vct_scr[...] = cmp_out[1].T.astype(BF16)

    def transpose_values(kb, carry):
        k0 = pl.multiple_of(kb * tk, tk)
        vst_scr[kb] = vslc_ref[pl.ds(k0, tk), :].T.astype(BF16)
        vwt_scr[kb] = vwin_ref[pl.ds(k0, tk), :].T.astype(BF16)
        return carry

    lax.fori_loop(0, seq // tk, transpose_values, 0)


def _nsa_kernel(q_ref, kcmp_ref, vcmp_ref, kslc_ref, vslc_ref, kwin_ref, vwin_ref, gate_ref,
                wk_ref, wv_ref, pk_ref, pv_ref, o_ref,
                kc_scr, vct_scr, vst_scr, vwt_scr, selk_scr, *, seq, tq, n_sub):
    step = pl.program_id(1)

    @pl.when(step == 0)
    def _():
        _nsa_stage_kv(kcmp_ref, vcmp_ref, vslc_ref, vwin_ref, wk_ref, wv_ref, pk_ref, pv_ref,
                      kc_scr, vct_scr, vst_scr, vwt_scr, seq=seq)

    for sub in range(n_sub):
        rows = pl.ds(sub * tq, tq)
        _nsa_query_block(step * n_sub + sub, q_ref.at[rows], kslc_ref, kwin_ref, gate_ref.at[rows],
                         o_ref.at[rows], kc_scr, vct_scr, vst_scr, vwt_scr, selk_scr,
                         seq=seq, tq=tq)


def _nsa_query_block(qi, q_ref, kslc_ref, kwin_ref, gate_ref, o_ref,
                     kc_scr, vct_scr, vst_scr, vwt_scr, selk_scr, *, seq, tq):
    tk = NSA_TK
    dh = NSA_DH
    hg = NSA_GROUP
    n_grp = NSA_KV_HEADS
    nq = hg * tq
    nb = seq // CMP_STRIDE
    n_sel = seq // SEL_LEN
    n_kb = seq // tk
    top_k = min(SEL_TOPK, n_sel)
    slc_step = min(NSA_SLC_BLOCKS, n_kb)
    n_win = WINDOW // tk + 1

    lane = lax.broadcasted_iota(jnp.int32, (tq, LANE), 1)
    qg = []
    for g in range(n_grp):
        parts = []
        for i in range(hg):
            hd = g * hg + i
            pair = q_ref[:, (hd // 2) * LANE:(hd // 2 + 1) * LANE] * (dh ** -0.5)
            if hd % 2 != g:
                pair = pltpu.roll(pair, dh, axis=1)
            parts.append(jnp.where((lane >= g * dh) & (lane < (g + 1) * dh), pair, 0.0))
        qg.append(jnp.concatenate(parts, axis=0))

    t_row = qi * tq + lax.broadcasted_iota(jnp.int32, (1, tq), 1)
    t_row4 = jnp.concatenate([t_row] * hg, axis=1)
    key_iota = lax.broadcasted_iota(jnp.int32, (tk, 1), 0)

    j_sel = lax.broadcasted_iota(jnp.int32, (n_sel, tq), 0)
    cur = lax.shift_right_logical(t_row, SEL_LEN.bit_length() - 1)
    sel_valid = j_sel <= cur
    forced = (j_sel == 0) | (j_sel == cur) | (j_sel == cur - 1)
    ov_j = lax.broadcasted_iota(jnp.int32, (n_sel, nb), 0) * SEL_LEN
    ov_n = lax.broadcasted_iota(jnp.int32, (n_sel, nb), 1) * CMP_STRIDE
    overlap = jnp.where((ov_n < ov_j + SEL_LEN) & (ov_n + CMP_LEN > ov_j), 1.0, 0.0).astype(BF16)
    n_idx = lax.broadcasted_iota(jnp.int32, (nb, 1), 0)
    cmp_valid = (n_idx * CMP_STRIDE + CMP_LEN - 1 <= t_row4) & (n_idx < nb - 1)

    def mask_heads(ok, s):
        return jnp.where(jnp.concatenate([ok] * hg, axis=1), s, MASK_NEG)

    groups = range(n_grp)
    kc_hi, kc_lo = _split_bf16(kc_scr[...])
    q_split = [_split_bf16(qg[g]) for g in groups]
    q_b = [q_split[g][0] for g in groups]

    win_blocks = []
    for w in range(n_win):
        kb_int = qi - w
        kb = jnp.maximum(kb_int, 0)
        kpos = kb_int * tk + key_iota
        ok = (kpos <= t_row) & (kpos > t_row - WINDOW) & (kpos >= 0)
        k_blk = kwin_ref[pl.ds(pl.multiple_of(kb * tk, tk), tk), :].astype(BF16)
        win_blocks.append((kb, k_blk, ok))

    s_cmp = [_nt_dot(kc_hi, q_split[g][0]) + _nt_dot(kc_lo, q_split[g][0])
             + _nt_dot(kc_hi, q_split[g][1]) for g in groups]
    s_win = [jnp.concatenate([mask_heads(ok, _nt_dot(k_blk, q_b[g]))
                              for (_, k_blk, ok) in win_blocks], axis=0) for g in groups]

    p_cmp, p_win, l_win = [], [], []
    for g in groups:
        s_m = jnp.where(cmp_valid, s_cmp[g], MASK_NEG)
        e = jnp.where(cmp_valid, jnp.exp(s_m - jnp.max(s_m, axis=0, keepdims=True)), 0.0)
        den = jnp.sum(e, axis=0, keepdims=True)
        p_cmp.append(jnp.where(den > 0.0, e / jnp.where(den > 0.0, den, 1.0), 0.0))
    for g in groups:
        p = jnp.exp(s_win[g] - jnp.max(s_win[g], axis=0, keepdims=True))
        l_win.append(jnp.sum(p, axis=0, keepdims=True))
        p_win.append(p.astype(BF16))

    o_cmp = [_dot(vct_scr[g * dh:(g + 1) * dh, :], p_cmp[g].astype(BF16)) for g in groups]
    imp = []
    for g in groups:
        p_sum = p_cmp[g][:, 0:tq]
        for i in range(1, hg):
            p_sum = p_sum + p_cmp[g][:, i * tq:(i + 1) * tq]
        ps_hi, ps_lo = _split_bf16(p_sum)
        imp.append(_dot(overlap, ps_hi) + _dot(overlap, ps_lo))
    o_win = []
    for g in groups:
        v_t = jnp.concatenate([vwt_scr[kb, g * dh:(g + 1) * dh, :] for (kb, _, _) in win_blocks],
                              axis=1)
        o_win.append(_dot(v_t, p_win[g]) / l_win[g])

    for g in groups:
        score = jnp.where(sel_valid, imp[g] + jnp.where(forced, FORCE_BONUS, 0.0), MASK_NEG)
        beaten = jnp.zeros((n_sel, tq), F32)
        for jp in range(n_sel):
            other = score[jp:jp + 1, :]
            ge = jnp.where(other >= score, 1.0, 0.0)
            gt = jnp.where(other > score, 1.0, 0.0)
            beaten = beaten + jnp.where(j_sel > jp, ge, gt)
        picked = jnp.where(beaten < top_k, 1.0, 0.0)
        for j in range(n_sel):
            selk_scr[g, j * SEL_LEN:(j + 1) * SEL_LEN, :] = jnp.broadcast_to(
                picked[j:j + 1, :], (SEL_LEN, tq))

    def slc_blocks(n_tiles):
        def run(kb0, carry):
            k_blocks = [kslc_ref[pl.ds(pl.multiple_of((kb0 + u) * tk, tk), tk), :].astype(BF16)
                        for u in range(n_tiles)]
            tiles = {}
            for g in range(n_grp):
                for u in range(n_tiles):
                    k0 = pl.multiple_of((kb0 + u) * tk, tk)
                    ok = (selk_scr[g, pl.ds(k0, tk), :] > 0.5) & (k0 + key_iota <= t_row)
                    tiles[g, u] = mask_heads(ok, _nt_dot(k_blocks[u], q_b[g]))
            new = []
            for g in range(n_grp):
                m, l, acc = carry[g]
                for u in range(n_tiles):
                    s = tiles[g, u]
                    m_new = jnp.maximum(m, jnp.max(s, axis=0, keepdims=True))
                    alpha = jnp.exp(m - m_new)
                    p = jnp.exp(s - m_new)
                    l = alpha * l + jnp.sum(p, axis=0, keepdims=True)
                    acc = alpha * acc + _dot(vst_scr[kb0 + u, g * dh:(g + 1) * dh, :],
                                             p.astype(BF16))
                    m = m_new
                new.append((m, l, acc))
            return tuple(new)
        return run

    init = tuple((jnp.full((1, nq), MASK_NEG, F32), jnp.zeros((1, nq), F32),
                  jnp.zeros((dh, nq), F32)) for _ in range(n_grp))
    n_need = (qi + 1) * (tq // tk)
    full_body = slc_blocks(slc_step)
    slc = lax.fori_loop(0, n_need // slc_step, lambda it, c: full_body(it * slc_step, c), init)
    done = (n_need // slc_step) * slc_step
    arm = slc_step // 2
    while arm >= 1:
        take = (n_need & arm) != 0
        slc = lax.cond(take, functools.partial(slc_blocks(arm), done), lambda c: c, slc)
        done = done + jnp.where(take, arm, 0)
        arm //= 2
    o_slc = [acc / l for (_, l, acc) in slc]

    gates = _sigmoid(gate_ref[...].T)
    outs = []
    for g in range(n_grp):
        for i in range(hg):
            hd = g * hg + i
            cols = slice(i * tq, (i + 1) * tq)
            outs.append(gates[3 * hd:3 * hd + 1, :] * o_cmp[g][:, cols]
                        + gates[3 * hd + 1:3 * hd + 2, :] * o_slc[g][:, cols]
                        + gates[3 * hd + 2:3 * hd + 3, :] * o_win[g][:, cols])
    o_ref[...] = jnp.concatenate(outs, axis=0).T.astype(o_ref.dtype)


def _cmp_weight(w):
    half = CMP_LEN // 2
    w = w.reshape(2, half, NSA_DH, NSA_DH)
    z = jnp.zeros_like(w)
    big = jnp.stack([jnp.concatenate([w, z], axis=-1), jnp.concatenate([z, w], axis=-1)], axis=2)
    return big.reshape(2, half * NSA_KV_HEADS * NSA_DH, NSA_KV_HEADS * NSA_DH)


def _native_sparse_attention(proj3, cmp_wk, cmp_pk, cmp_wv, cmp_pv, *, tq):
    b, s, _ = proj3.shape
    nb = s // CMP_STRIDE
    n_kb = s // NSA_TK
    assert tq == NSA_TK and n_kb % min(NSA_SLC_BLOCKS, n_kb) == 0
    wk = _cmp_weight(cmp_wk)
    wv = _cmp_weight(cmp_wv)
    pk = jnp.tile(cmp_pk, (1, NSA_KV_HEADS))
    pv = jnp.tile(cmp_pv, (1, NSA_KV_HEADS))
    qw = NSA_HEADS * NSA_DH

    def seq_block(col):
        return pl.BlockSpec((None, s, LANE), lambda bi, i: (bi, 0, col))

    def whole(a):
        return pl.BlockSpec(a.shape, lambda bi, i: (0,) * a.ndim)

    n_sub = min(NSA_BLOCKS_PER_STEP, s // tq)
    t_step = n_sub * tq
    return pl.pallas_call(
        functools.partial(_nsa_kernel, seq=s, tq=tq, n_sub=n_sub),
        grid=(b, s // t_step),
        in_specs=[pl.BlockSpec((None, t_step, qw), lambda bi, i: (bi, i, COL_NQ * LANE // qw))]
        + [seq_block(COL_NKV + c) for c in range(6)]
        + [pl.BlockSpec((None, t_step, LANE), lambda bi, i: (bi, i, COL_NGATE)),
           whole(wk), whole(wv), whole(pk), whole(pv)],
        out_specs=pl.BlockSpec((None, t_step, qw), lambda bi, i: (bi, i, 0)),
        out_shape=jax.ShapeDtypeStruct((b, s, qw), BF16),
        scratch_shapes=[pltpu.VMEM((nb, LANE), F32),
                        pltpu.VMEM((LANE, nb), BF16),
                        pltpu.VMEM((n_kb, LANE, NSA_TK), BF16),
                        pltpu.VMEM((n_kb, LANE, NSA_TK), BF16),
                        pltpu.VMEM((NSA_KV_HEADS, s, tq), F32)],
        compiler_params=_cparams(2),
        name="native_sparse_attention",
    )(proj3, proj3, proj3, proj3, proj3, proj3, proj3, proj3, wk, wv, pk, pv)


def _pad_cols(w, width):
    return jnp.pad(w, ((0, 0),) * (w.ndim - 1) + ((0, width - w.shape[-1]),))


def _proj_weight(w_in):
    gqk_v_z = 4 * GDN_HEADS * GDN_D
    n_ab = 2 * GDN_HEADS
    n_nsa = (NSA_HEADS + 6 * NSA_KV_HEADS) * NSA_DH
    n_gate = 3 * NSA_HEADS
    o_ab = gqk_v_z
    o_nsa = o_ab + n_ab
    o_gate = o_nsa + n_nsa
    o_rest = o_gate + n_gate
    w = jnp.concatenate([w_in[..., :o_ab], w_in[..., o_nsa:o_gate],
                         _pad_cols(w_in[..., o_ab:o_nsa], LANE),
                         _pad_cols(w_in[..., o_gate:o_rest], LANE),
                         w_in[..., o_rest:]], axis=-1)
    assert w.shape[-1] == N_PROJ, w.shape
    return w.astype(BF16)


def _layer(x, p, big, layer, *, tiles):
    b, s, d = x.shape
    x2 = x.reshape(b * s, d)
    proj = _norm_matmul(x2, p["g_mix_pre"], big["w_proj"], layer, relu2=False, out_dtype=F32,
                        tm=tiles["tm"], tn=tiles["tn"])
    proj3 = proj.reshape(b, s, N_PROJ)
    ya = _gated_deltanet(proj3, p["gdn_conv_w"], p["gdn_a_log"], p["gdn_dt_bias"], p["gdn_norm_g"])
    yb = _native_sparse_attention(proj3, p["nsa_cmp_wk"], p["nsa_cmp_pk"], p["nsa_cmp_wv"],
                                  p["nsa_cmp_pv"], tq=tiles["tq"])
    yc = _sb_attention(proj3, tq=tiles["tq_sb"])
    x2 = _merge_out(ya.reshape(b * s, -1), yb.reshape(b * s, -1), yc.reshape(b * s, -1), proj, x2,
                    big["w_br_gdn"], big["w_br_nsa"], big["w_br_sb"], big["w_out"], layer,
                    p["g_mix_post"], tm=tiles["tm_out"])
    act = _norm_matmul(x2, p["g_ff_pre"], big["w_ff1"], layer, relu2=True, out_dtype=BF16,
                       tm=tiles["tm"], tn=tiles["tn"])
    x2 = _matmul_norm_res(act, big["w_ff2"], layer, x2, p["g_ff_post"], tm=tiles["tm_out"])
    return x2.reshape(b, s, d)


def _tiles(b, s):
    t = b * s
    return {"tm": min(1024, t), "tn": 2048, "tm_out": min(512, t), "tq": min(128, s),
            "tq_sb": min(256, s)}


def kernel(x, g_mix_pre, g_mix_post, g_ff_pre, g_ff_post, w_in, gdn_conv_w, gdn_a_log, gdn_dt_bias,
           gdn_norm_g, nsa_cmp_wk, nsa_cmp_pk, nsa_cmp_wv, nsa_cmp_pv, w_br_gdn, w_br_nsa, w_br_sb,
           w_out, w_ff1, w_ff2):
    small = dict(g_mix_pre=g_mix_pre, g_mix_post=g_mix_post, g_ff_pre=g_ff_pre, g_ff_post=g_ff_post,
                 gdn_conv_w=gdn_conv_w, gdn_a_log=gdn_a_log, gdn_dt_bias=gdn_dt_bias,
                 gdn_norm_g=gdn_norm_g, nsa_cmp_wk=nsa_cmp_wk, nsa_cmp_pk=nsa_cmp_pk,
                 nsa_cmp_wv=nsa_cmp_wv, nsa_cmp_pv=nsa_cmp_pv)
    big = dict(w_proj=_proj_weight(w_in), w_br_gdn=w_br_gdn.astype(BF16),
               w_br_nsa=w_br_nsa.astype(BF16), w_br_sb=w_br_sb.astype(BF16),
               w_out=w_out.astype(BF16), w_ff1=w_ff1.astype(BF16), w_ff2=w_ff2.astype(BF16))
    tiles = _tiles(x.shape[0], x.shape[1])
    for layer in range(w_in.shape[0]):
        x = _layer(x, {k: v[layer] for k, v in small.items()}, big, layer, tiles=tiles)
    return x
```

```python
import functools

import jax
import jax.numpy as jnp
from jax import lax
from jax.experimental import pallas as pl
from jax.experimental.pallas import tpu as pltpu

F32 = jnp.float32
BF16 = jnp.bfloat16

EPS = 1e-6
MASK_NEG = -1e30
LOG2E = 1.4426950408889634

GDN_HEADS = 4
GDN_D = 128
GDN_CONV = 4
GDN_CHUNK = 64
GDN_PREP_CHUNKS = 8
GDN_SCAN_TOKENS = 512

NSA_HEADS = 8
NSA_KV_HEADS = 2
NSA_GROUP = NSA_HEADS // NSA_KV_HEADS
NSA_DH = 64
CMP_LEN = 32
CMP_STRIDE = 16
SEL_LEN = 64
SEL_TOPK = 8
WINDOW = 512
FORCE_BONUS = 1e3
NSA_TK = 128
NSA_SLC_BLOCKS = 4
NSA_BLOCKS_PER_STEP = 2

SB_HEADS = 8
SB_DH = 64
SB_TK = 128
SB_DEAD_LOG2 = -160.0
SB_BLOCKS_PER_STEP = 2

LANE = 128

COL_GQ, COL_GK, COL_GV, COL_GZ = 0, 4, 8, 12
COL_NQ, COL_NKV, COL_GAB, COL_NGATE = 16, 20, 26, 27
COL_SQ, COL_SK, COL_SV = 28, 32, 36
COL_MERGE = 40
N_PROJ = 64 * LANE

VMEM_LIMIT = 48 * 1024 * 1024


def _cparams(n_axes):
    return pltpu.CompilerParams(dimension_semantics=("arbitrary",) * n_axes,
                                vmem_limit_bytes=VMEM_LIMIT)


def _nt_dot(a, b):
    return lax.dot_general(a, b, (((1,), (1,)), ((), ())), preferred_element_type=F32)


def _tn_dot(a, b):
    return lax.dot_general(a, b, (((0,), (0,)), ((), ())), preferred_element_type=F32)


def _dot(a, b):
    return jnp.dot(a, b, preferred_element_type=F32)


def _rms_scale(y):
    return y * lax.rsqrt(jnp.mean(y * y, axis=-1, keepdims=True) + EPS)


def _softplus(z):
    return jnp.maximum(z, 0.0) + jnp.log1p(jnp.exp(-jnp.abs(z)))


def _sigmoid(z):
    return jax.nn.sigmoid(z)


def _split_bf16(x):
    hi = x.astype(BF16)
    return hi, (x - hi.astype(F32)).astype(BF16)


def _dot3(a, b):
    a_hi, a_lo = _split_bf16(a)
    b_hi, b_lo = _split_bf16(b)
    return _dot(a_hi, b_hi) + _dot(a_lo, b_hi) + _dot(a_hi, b_lo)


def _norm_matmul_kernel(x_ref, g_ref, w_ref, o_ref, h_scr, *, relu2):
    @pl.when(pl.program_id(1) == 0)
    def _():
        h_scr[...] = (_rms_scale(x_ref[...]) * g_ref[...]).astype(BF16)

    y = _dot(h_scr[...], w_ref[...])
    if relu2:
        y = jnp.square(jnp.maximum(y, 0.0))
    o_ref[...] = y.astype(o_ref.dtype)


def _norm_matmul(x, g, w, layer, *, relu2, out_dtype, tm, tn):
    t, d = x.shape
    n = w.shape[2]
    return pl.pallas_call(
        functools.partial(_norm_matmul_kernel, relu2=relu2),
        grid=(t // tm, n // tn),
        in_specs=[pl.BlockSpec((tm, d), lambda i, j: (i, 0)),
                  pl.BlockSpec((1, d), lambda i, j: (0, 0)),
                  pl.BlockSpec((None, d, tn), lambda i, j: (layer, 0, j))],
        out_specs=pl.BlockSpec((tm, tn), lambda i, j: (i, j)),
        out_shape=jax.ShapeDtypeStruct((t, n), out_dtype),
        scratch_shapes=[pltpu.VMEM((tm, d), BF16)],
        compiler_params=_cparams(2),
        name="norm_matmul_relu2" if relu2 else "norm_matmul",
    )(x, g.reshape(1, d), w)


def _matmul_norm_res_kernel(a_ref, w_ref, x_ref, g_ref, o_ref):
    y = _dot(a_ref[...], w_ref[...])
    o_ref[...] = x_ref[...] + _rms_scale(y) * g_ref[...]


def _matmul_norm_res(a, w, layer, x, g, *, tm):
    t, k = a.shape
    d = w.shape[2]
    return pl.pallas_call(
        _matmul_norm_res_kernel,
        grid=(t // tm,),
        in_specs=[pl.BlockSpec((tm, k), lambda i: (i, 0)),
                  pl.BlockSpec((None, k, d), lambda i: (layer, 0, 0)),
                  pl.BlockSpec((tm, d), lambda i: (i, 0)),
                  pl.BlockSpec((1, d), lambda i: (0, 0))],
        out_specs=pl.BlockSpec((tm, d), lambda i: (i, 0)),
        out_shape=jax.ShapeDtypeStruct((t, d), F32),
        compiler_params=_cparams(1),
        name="matmul_norm_res",
    )(a, w, x, g.reshape(1, d))


def _merge_out_kernel(ya_ref, yb_ref, yc_ref, m0_ref, m1_ref, m2_ref, x_ref,
                      wa_ref, wb_ref, wc_ref, wo_ref, g_ref, o_ref):
    merged = (_sigmoid(m0_ref[...]) * _dot(ya_ref[...], wa_ref[...])
              + _sigmoid(m1_ref[...]) * _dot(yb_ref[...], wb_ref[...])
              + _sigmoid(m2_ref[...]) * _dot(yc_ref[...], wc_ref[...]))
    y = _dot(merged.astype(BF16), wo_ref[...])
    o_ref[...] = x_ref[...] + _rms_scale(y) * g_ref[...]


def _merge_out(ya, yb, yc, proj, x, wa, wb, wc, wo, layer, g, *, tm):
    t, d = x.shape
    w_in = ya.shape[1]
    mcol = COL_MERGE * LANE // d

    def row(i):
        return (i, 0)

    def const(i):
        return (0, 0)

    def weight(i):
        return (layer, 0, 0)

    return pl.pallas_call(
        _merge_out_kernel,
        grid=(t // tm,),
        in_specs=[pl.BlockSpec((tm, w_in), row),
                  pl.BlockSpec((tm, w_in), row),
                  pl.BlockSpec((tm, w_in), row),
                  pl.BlockSpec((tm, d), lambda i: (i, mcol)),
                  pl.BlockSpec((tm, d), lambda i: (i, mcol + 1)),
                  pl.BlockSpec((tm, d), lambda i: (i, mcol + 2)),
                  pl.BlockSpec((tm, d), row),
                  pl.BlockSpec((None, w_in, d), weight),
                  pl.BlockSpec((None, w_in, d), weight),
                  pl.BlockSpec((None, w_in, d), weight),
                  pl.BlockSpec((None, d, d), weight),
                  pl.BlockSpec((1, d), const)],
        out_specs=pl.BlockSpec((tm, d), row),
        out_shape=jax.ShapeDtypeStruct((t, d), F32),
        compiler_params=_cparams(1),
        name="merge_out",
    )(ya, yb, yc, proj, proj, proj, x, wa, wb, wc, wo, g.reshape(1, d))


def _sb_kernel(q_ref, k_ref, v_ref, o_ref, run_scr, acc_scr, *, tq, n_sub):
    step = pl.program_id(1)
    for sub in range(n_sub):
        rows = pl.ds(sub * tq, tq)
        _sb_query_block(step * n_sub + sub, q_ref.at[rows], k_ref, v_ref, o_ref.at[rows],
                        run_scr, acc_scr, tq=tq)


def _sb_query_block(qi, q_ref, k_ref, v_ref, o_ref, run_scr, acc_scr, *, tq):
    tk = SB_TK
    n_diag = tq // tk
    dh = SB_DH
    n_pair = SB_HEADS * dh // LANE
    first_head = lax.broadcasted_iota(jnp.int32, (tk, LANE), 1) < dh
    r2 = lax.broadcasted_iota(jnp.int32, (2 * tk, 2 * tk), 0) & (tk - 1)
    c2 = lax.broadcasted_iota(jnp.int32, (2 * tk, 2 * tk), 1)
    later_ones = jnp.where((r2 > c2) | (c2 >= tk), 1.0, 0.0).astype(BF16)

    q_pairs = [(q_ref[:, p * LANE:(p + 1) * LANE] * (dh ** -0.5 * LOG2E)).astype(BF16)
               for p in range(n_pair)]

    run_scr[...] = jnp.zeros_like(run_scr)
    acc_scr[...] = jnp.zeros_like(acc_scr)

    def split_heads(x):
        zero = jnp.zeros_like(x)
        return jnp.concatenate([jnp.where(first_head, x, zero), jnp.where(first_head, zero, x)],
                               axis=0)

    def block(j, diag_offset):
        k0 = pl.multiple_of(j * tk, tk)
        pairs = range(n_pair)
        diagonal = diag_offset is not None
        rs = slice(diag_offset, tq) if diagonal else slice(0, tq)
        if diagonal:
            shape = (tq - diag_offset, tk)
            causal = (lax.broadcasted_iota(jnp.int32, shape, 1)
                      < lax.broadcasted_iota(jnp.int32, shape, 0))
            causal2 = jnp.concatenate([causal, causal], axis=1)
        z = [_nt_dot(q_pairs[p][rs],
                     split_heads(k_ref[pl.ds(k0, tk), p * LANE:(p + 1) * LANE].astype(BF16)))
             for p in pairs]
        log_sig, hi, lo = [], [], []
        for p in pairs:
            log_sig.append(jnp.minimum(z[p], 0.0) - jnp.log2(1.0 + jnp.exp2(-jnp.abs(z[p]))))
            log_keep = log_sig[p] - z[p]
            if diagonal:
                log_keep = jnp.where(causal2, log_keep, 0.0)
            hi.append(log_keep.astype(BF16))
            lo.append((log_keep - hi[p].astype(F32)).astype(BF16))
        sums = [[_dot(jnp.concatenate([hi[p][:, hh * tk:(hh + 1) * tk],
                                       lo[p][:, hh * tk:(hh + 1) * tk]], axis=1), later_ones)
                 for hh in range(2)] for p in pairs]
        for p in pairs:
            a = []
            for hh in range(2):
                h = 2 * p + hh
                run = run_scr[h, rs, :]
                a_h = jnp.exp2(log_sig[p][:, hh * tk:(hh + 1) * tk] + sums[p][hh][:, :tk] + run)
                if diagonal:
                    a_h = jnp.where(causal, a_h, 0.0)
                run_scr[h, rs, :] = run + sums[p][hh][:, tk:]
                a.append(a_h.astype(BF16))
            vp = v_ref[pl.ds(k0, tk), p * LANE:(p + 1) * LANE].astype(BF16)
            acc_scr[p, rs, :] += _dot(jnp.concatenate(a, axis=1), split_heads(vp))

    for d in reversed(range(n_diag)):
        block(qi * n_diag + d, d * tk)

    def still_live():
        top = run_scr[0]
        for h in range(1, SB_HEADS):
            top = jnp.maximum(top, run_scr[h])
        return jnp.max(top) > SB_DEAD_LOG2

    def body(carry):
        it, _ = carry
        block(qi * n_diag - 1 - it, None)
        return it + 1, still_live()

    lax.while_loop(lambda c: (c[0] < qi * n_diag) & c[1], body, (jnp.int32(0), still_live()))
    o_ref[...] = jnp.concatenate([acc_scr[p] for p in range(n_pair)], axis=-1).astype(o_ref.dtype)


def _sb_attention(proj3, *, tq):
    b, s, _ = proj3.shape
    width = SB_HEADS * SB_DH
    n_pair = width // LANE
    n_sub = min(SB_BLOCKS_PER_STEP, s // tq)
    t_step = n_sub * tq
    return pl.pallas_call(
        functools.partial(_sb_kernel, tq=tq, n_sub=n_sub),
        grid=(b, s // t_step),
        in_specs=[pl.BlockSpec((None, t_step, width), lambda bi, i: (bi, i, COL_SQ * LANE // width)),
                  pl.BlockSpec((None, s, width), lambda bi, i: (bi, 0, COL_SK * LANE // width)),
                  pl.BlockSpec((None, s, width), lambda bi, i: (bi, 0, COL_SV * LANE // width))],
        out_specs=pl.BlockSpec((None, t_step, width), lambda bi, i: (bi, i, 0)),
        out_shape=jax.ShapeDtypeStruct((b, s, width), BF16),
        scratch_shapes=[pltpu.VMEM((SB_HEADS, tq, LANE), F32), pltpu.VMEM((n_pair, tq, LANE), F32)],
        compiler_params=_cparams(2),
        name="sb_attention",
    )(proj3, proj3, proj3)


def _gdn_prep_kernel(scal_ref, q_ref, qh_ref, k_ref, kh_ref, v_ref, vh_ref, ab_ref, cw_ref,
                     u_ref, w_ref, qe_ref, kd_ref, attn_ref, egl_ref, *, cb):
    i = pl.program_id(1)
    c_len = GDN_CHUNK
    d = GDN_D
    n_h = GDN_HEADS
    ts = cb * c_len
    halo = qh_ref.shape[0]
    width = n_h * d

    def conv_silu(x_ref, halo_ref, w):
        ext = jnp.concatenate([halo_ref[...] * jnp.where(i > 0, 1.0, 0.0), x_ref[...]], axis=0)
        y = jnp.zeros((ts, width), F32)
        for tap in range(GDN_CONV):
            off = halo - (GDN_CONV - 1) + tap
            y = y + w[tap:tap + 1, :] * ext[off:off + ts]
        return y * _sigmoid(y)

    qc = conv_silu(q_ref, qh_ref, cw_ref[:, 0:width])
    kc = conv_silu(k_ref, kh_ref, cw_ref[:, width:2 * width])
    vc = conv_silu(v_ref, vh_ref, cw_ref[:, 2 * width:3 * width])

    ab = ab_ref[...]
    lane = lax.broadcasted_iota(jnp.int32, (1, LANE), 1)
    a_log = jnp.zeros((1, LANE), F32)
    dt_bias = jnp.zeros((1, LANE), F32)
    for h in range(n_h):
        a_log = jnp.where(lane == h, scal_ref[0, h], a_log)
        dt_bias = jnp.where(lane == h, scal_ref[1, h], dt_bias)
    g_cum = -jnp.exp(a_log) * _softplus(ab + dt_bias)
    row_in_chunk = lax.broadcasted_iota(jnp.int32, (ts, LANE), 0) & (c_len - 1)
    shift = 1
    while shift < c_len:
        g_cum = g_cum + jnp.where(row_in_chunk >= shift, pltpu.roll(g_cum, shift, axis=0), 0.0)
        shift *= 2
    beta_all = _sigmoid(ab)

    row = lax.broadcasted_iota(jnp.int32, (c_len, c_len), 0)
    col = lax.broadcasted_iota(jnp.int32, (c_len, c_len), 1)
    incl = row >= col
    strict = row > col
    eye = jnp.where(row == col, 1.0, 0.0)

    problems = [(c, h) for c in range(cb) for h in range(n_h)]
    qn, kn, k_beta, v_beta, g_col, decay = {}, {}, {}, {}, {}, {}
    for c, h in problems:
        rows = slice(c * c_len, (c + 1) * c_len)
        cols = slice(h * d, (h + 1) * d)
        qh, kh = qc[rows, cols], kc[rows, cols]
        qn[c, h] = qh * lax.rsqrt(jnp.sum(qh * qh, axis=-1, keepdims=True) + EPS) * (d ** -0.5)
        kn[c, h] = kh * lax.rsqrt(jnp.sum(kh * kh, axis=-1, keepdims=True) + EPS)
        beta = beta_all[rows, n_h + h:n_h + h + 1]
        k_beta[c, h] = kn[c, h] * beta
        v_beta[c, h] = (vc[rows, cols] * beta).astype(BF16)
        g_col[c, h] = g_cum[rows, h:h + 1]
        g_sq = jnp.broadcast_to(g_col[c, h], (c_len, c_len))
        decay[c, h] = jnp.where(incl, jnp.exp(jnp.where(incl, g_sq - g_sq.T, 0.0)), 0.0)

    kn_b = {p: kn[p].astype(BF16) for p in problems}
    kkt = {p: _nt_dot(k_beta[p].astype(BF16), kn_b[p]) for p in problems}
    qkt = {p: _nt_dot(qn[p].astype(BF16), kn_b[p]) for p in problems}

    a_mat = {p: jnp.where(strict, kkt[p] * decay[p], 0.0) for p in problems}
    left = lax.broadcasted_iota(jnp.int32, (c_len, 2 * c_len), 1) < c_len
    pair = {p: jnp.concatenate([-a_mat[p], eye], axis=1) for p in problems}
    for _ in range(max(c_len - 1, 1).bit_length()):
        y = {p: _dot(pair[p][:, :c_len].astype(BF16), pair[p].astype(BF16)) for p in problems}
        pair = {p: jnp.where(left, y[p], pair[p] + y[p]) for p in problems}
    t_inv = {p: pair[p][:, c_len:] for p in problems}
    resid = {}
    for p in problems:
        m_hi, m_lo = _split_bf16(eye + a_mat[p])
        t_hi, t_lo = _split_bf16(t_inv[p])
        resid[p] = (eye - (_dot(m_hi, t_hi) + _dot(m_lo, t_hi) + _dot(m_hi, t_lo)), t_hi)
    t_b = {p: (t_inv[p] + _dot(resid[p][1], resid[p][0].astype(BF16))).astype(BF16)
           for p in problems}

    for c, h in problems:
        p = (c, h)
        rows = slice(c * c_len, (c + 1) * c_len)
        cols = slice(h * d, (h + 1) * d)
        exp_g = jnp.exp(g_col[p])
        g_last = g_col[p][c_len - 1:c_len, :]
        u_ref[rows, cols] = _dot(t_b[p], v_beta[p])
        w_ref[rows, cols] = _dot(t_b[p], (k_beta[p] * exp_g).astype(BF16)).astype(BF16)
        qe_ref[rows, cols] = (qn[p] * exp_g).astype(BF16)
        kd_ref[rows, cols] = (kn[p] * jnp.exp(g_last - g_col[p])).astype(BF16)
    egl_rows = [jnp.concatenate([jnp.broadcast_to(jnp.exp(g_col[c, h][c_len - 1:c_len, :]), (1, d))
                                 for h in range(n_h)], axis=1) for c in range(cb)]
    if egl_ref.shape[0] > cb:
        egl_rows.append(jnp.zeros((egl_ref.shape[0] - cb, width), F32))
    egl_ref[...] = jnp.concatenate(egl_rows, axis=0)
    for c in range(cb):
        rows = slice(c * c_len, (c + 1) * c_len)
        attn_ref[rows, :] = jnp.concatenate(
            [(qkt[c, h] * decay[c, h]).astype(BF16) for h in range(n_h)], axis=1)


def _gdn_scan_kernel(u_ref, w_ref, qe_ref, kd_ref, attn_ref, egl_ref, z_ref, ng_ref, o_ref,
                     state_scr, *, bb, ts, cb):
    j = pl.program_id(1)
    c_len = GDN_CHUNK
    d = GDN_D
    n_c = ts // c_len
    chains = [(b_, h) for b_ in range(bb) for h in range(GDN_HEADS)]

    @pl.when(j == 0)
    def _():
        state_scr[...] = jnp.zeros_like(state_scr)

    def chunk(c, carry):
        r0 = pl.multiple_of(c * c_len, c_len)
        rows = pl.ds(r0, c_len)
        cg = j * n_c + c
        state = {p: state_scr[p[0], p[1]] for p in chains}
        state_b = {p: state[p].astype(BF16) for p in chains}
        w_s = {(b_, h): _dot(w_ref[b_, rows, h * d:(h + 1) * d], state_b[b_, h]) for b_, h in chains}
        q_s = {(b_, h): _dot(qe_ref[b_, rows, h * d:(h + 1) * d], state_b[b_, h]) for b_, h in chains}
        v_new = {(b_, h): (u_ref[b_, rows, h * d:(h + 1) * d] - w_s[b_, h]).astype(BF16)
                 for b_, h in chains}
        sub = lax.broadcasted_iota(jnp.int32, egl_ref.shape[2:], 0)
        chunk_decay = [jnp.sum(jnp.where(sub == cg % cb, egl_ref[b_, cg // cb], 0.0),
                               axis=0, keepdims=True) for b_ in range(bb)]
        for b_, h in chains:
            cols = slice(h * d, (h + 1) * d)
            o = q_s[b_, h] + _dot(attn_ref[b_, rows, h * c_len:(h + 1) * c_len], v_new[b_, h])
            state_scr[b_, h] = (state[b_, h] * chunk_decay[b_][:, cols]
                                + _tn_dot(kd_ref[b_, rows, cols], v_new[b_, h]))
            zc = z_ref[b_, rows, cols]
            o_ref[b_, rows, cols] = (_rms_scale(o) * ng_ref[...] * (zc * _sigmoid(zc))
                                     ).astype(o_ref.dtype)
        return carry

    lax.fori_loop(0, n_c, chunk, 0)


def _gated_deltanet(proj3, conv_w, a_log, dt_bias, norm_g):
    b, s, _ = proj3.shape
    scal = jnp.stack([a_log, dt_bias]).astype(F32)
    width = GDN_HEADS * GDN_D
    n_chunks = s // GDN_CHUNK
    cb = min(GDN_PREP_CHUNKS, n_chunks)
    ts = cb * GDN_CHUNK
    halo = 8
    egl_rows = -(-cb // 8) * 8

    def main(col0):
        return pl.BlockSpec((None, ts, width), lambda bi, i: (bi, i, col0 * LANE // width))

    def before(col0):
        return pl.BlockSpec((None, halo, width),
                            lambda bi, i: (bi, jnp.maximum(i * (ts // halo) - 1, 0),
                                           col0 * LANE // width))

    def out_block(w_):
        return pl.BlockSpec((None, ts, w_), lambda bi, i: (bi, i, 0))

    u, w, qe, kd, attn, egl = pl.pallas_call(
        functools.partial(_gdn_prep_kernel, cb=cb),
        grid=(b, n_chunks // cb),
        in_specs=[pl.BlockSpec(memory_space=pltpu.SMEM),
                  main(COL_GQ), before(COL_GQ), main(COL_GK), before(COL_GK),
                  main(COL_GV), before(COL_GV),
                  pl.BlockSpec((None, ts, LANE), lambda bi, i: (bi, i, COL_GAB)),
                  pl.BlockSpec(conv_w.shape, lambda bi, i: (0, 0))],
        out_specs=[out_block(width), out_block(width), out_block(width), out_block(width),
                   out_block(GDN_HEADS * GDN_CHUNK),
                   pl.BlockSpec((None, None, egl_rows, width), lambda bi, i: (bi, i, 0, 0))],
        out_shape=[jax.ShapeDtypeStruct((b, s, width), F32),
                   jax.ShapeDtypeStruct((b, s, width), BF16),
                   jax.ShapeDtypeStruct((b, s, width), BF16),
                   jax.ShapeDtypeStruct((b, s, width), BF16),
                   jax.ShapeDtypeStruct((b, s, GDN_HEADS * GDN_CHUNK), BF16),
                   jax.ShapeDtypeStruct((b, n_chunks // cb, egl_rows, width), F32)],
        compiler_params=_cparams(2),
        name="gdn_prep",
    )(scal, proj3, proj3, proj3, proj3, proj3, proj3, proj3, conv_w)

    bb = 2 if b % 2 == 0 else 1
    t_scan = min(GDN_SCAN_TOKENS, s)

    def scan_block(w_):
        return pl.BlockSpec((bb, t_scan, w_), lambda bi, j: (bi, j, 0))

    return pl.pallas_call(
        functools.partial(_gdn_scan_kernel, bb=bb, ts=t_scan, cb=cb),
        grid=(b // bb, s // t_scan),
        in_specs=[scan_block(width), scan_block(width), scan_block(width), scan_block(width),
                  scan_block(GDN_HEADS * GDN_CHUNK),
                  pl.BlockSpec((bb, n_chunks // cb, egl_rows, width), lambda bi, j: (bi, 0, 0, 0)),
                  pl.BlockSpec((bb, t_scan, width), lambda bi, j: (bi, j, COL_GZ * LANE // width)),
                  pl.BlockSpec((1, GDN_D), lambda bi, j: (0, 0))],
        out_specs=scan_block(width),
        out_shape=jax.ShapeDtypeStruct((b, s, width), BF16),
        scratch_shapes=[pltpu.VMEM((bb, GDN_HEADS, GDN_D, GDN_D), F32)],
        compiler_params=_cparams(2),
        name="gdn_scan",
    )(u, w, qe, kd, attn, egl, proj3, norm_g.reshape(1, GDN_D))


def _nsa_stage_kv(kcmp_ref, vcmp_ref, vslc_ref, vwin_ref, wk_ref, wv_ref, pk_ref, pv_ref,
                  kc_scr, vct_scr, vst_scr, vwt_scr, *, seq):
    tk = NSA_TK
    nb = seq // CMP_STRIDE
    half = CMP_LEN // 2
    cmp_out = []
    for src, pos, w_ref in ((kcmp_ref, pk_ref, wk_ref), (vcmp_ref, pv_ref, wv_ref)):
        lo_parts, hi_parts = [], []
        for l in range(half):
            rows = src[pl.ds(l, nb, stride=CMP_STRIDE), :]
            lo_parts.append(rows + pos[l:l + 1, :])
            hi_parts.append(rows + pos[half + l:half + l + 1, :])
        a_lo = _dot3(jnp.concatenate(lo_parts, axis=1), w_ref[0])
        a_hi = _dot3(jnp.concatenate(hi_parts, axis=1), w_ref[1])
        a_hi = jnp.concatenate([a_hi[1:], jnp.zeros((1, LANE), F32)], axis=0)
        cmp_out.append(a_lo + a_hi)
    kc_scr[...] = cmp_out[0]
    vct_scr[...] = cmp_out[1].T.astype(BF16)

    def transpose_values(kb, carry):
        k0 = pl.multiple_of(kb * tk, tk)
        vst_scr[kb] = vslc_ref[pl.ds(k0, tk), :].T.astype(BF16)
        vwt_scr[kb] = vwin_ref[pl.ds(k0, tk), :].T.astype(BF16)
        return carry

    lax.fori_loop(0, seq // tk, transpose_values, 0)


def _nsa_kernel(q_ref, kcmp_ref, vcmp_ref, kslc_ref, vslc_ref, kwin_ref, vwin_ref, gate_ref,
                wk_ref, wv_ref, pk_ref, pv_ref, o_ref,
                kc_scr, vct_scr, vst_scr, vwt_scr, selk_scr, *, seq, tq, n_sub):
    step = pl.program_id(1)

    @pl.when(step == 0)
    def _():
        _nsa_stage_kv(kcmp_ref, vcmp_ref, vslc_ref, vwin_ref, wk_ref, wv_ref, pk_ref, pv_ref,
                      kc_scr, vct_scr, vst_scr, vwt_scr, seq=seq)

    for sub in range(n_sub):
        rows = pl.ds(sub * tq, tq)
        _nsa_query_block(step * n_sub + sub, q_ref.at[rows], kslc_ref, kwin_ref, gate_ref.at[rows],
                         o_ref.at[rows], kc_scr, vct_scr, vst_scr, vwt_scr, selk_scr,
                         seq=seq, tq=tq)


def _nsa_query_block(qi, q_ref, kslc_ref, kwin_ref, gate_ref, o_ref,
                     kc_scr, vct_scr, vst_scr, vwt_scr, selk_scr, *, seq, tq):
    tk = NSA_TK
    dh = NSA_DH
    hg = NSA_GROUP
    n_grp = NSA_KV_HEADS
    nq = hg * tq
    nb = seq // CMP_STRIDE
    n_sel = seq // SEL_LEN
    n_kb = seq // tk
    top_k = min(SEL_TOPK, n_sel)
    slc_step = min(NSA_SLC_BLOCKS, n_kb)
    n_win = WINDOW // tk + 1

    lane = lax.broadcasted_iota(jnp.int32, (tq, LANE), 1)
    qg = []
    for g in range(n_grp):
        parts = []
        for i in range(hg):
            hd = g * hg + i
            pair = q_ref[:, (hd // 2) * LANE:(hd // 2 + 1) * LANE] * (dh ** -0.5)
            if hd % 2 != g:
                pair = pltpu.roll(pair, dh, axis=1)
            parts.append(jnp.where((lane >= g * dh) & (lane < (g + 1) * dh), pair, 0.0))
        qg.append(jnp.concatenate(parts, axis=0))

    t_row = qi * tq + lax.broadcasted_iota(jnp.int32, (1, tq), 1)
    t_row4 = jnp.concatenate([t_row] * hg, axis=1)
    key_iota = lax.broadcasted_iota(jnp.int32, (tk, 1), 0)

    j_sel = lax.broadcasted_iota(jnp.int32, (n_sel, tq), 0)
    cur = lax.shift_right_logical(t_row, SEL_LEN.bit_length() - 1)
    sel_valid = j_sel <= cur
    forced = (j_sel == 0) | (j_sel == cur) | (j_sel == cur - 1)
    ov_j = lax.broadcasted_iota(jnp.int32, (n_sel, nb), 0) * SEL_LEN
    ov_n = lax.broadcasted_iota(jnp.int32, (n_sel, nb), 1) * CMP_STRIDE
    overlap = jnp.where((ov_n < ov_j + SEL_LEN) & (ov_n + CMP_LEN > ov_j), 1.0, 0.0).astype(BF16)
    n_idx = lax.broadcasted_iota(jnp.int32, (nb, 1), 0)
    cmp_valid = (n_idx * CMP_STRIDE + CMP_LEN - 1 <= t_row4) & (n_idx < nb - 1)

    def mask_heads(ok, s):
        return jnp.where(jnp.concatenate([ok] * hg, axis=1), s, MASK_NEG)

    groups = range(n_grp)
    kc_hi, kc_lo = _split_bf16(kc_scr[...])
    q_split = [_split_bf16(qg[g]) for g in groups]
    q_b = [q_split[g][0] for g in groups]

    win_blocks = []
    for w in range(n_win):
        kb_int = qi - w
        kb = jnp.maximum(kb_int, 0)
        kpos = kb_int * tk + key_iota
        ok = (kpos <= t_row) & (kpos > t_row - WINDOW) & (kpos >= 0)
        k_blk = kwin_ref[pl.ds(pl.multiple_of(kb * tk, tk), tk), :].astype(BF16)
        win_blocks.append((kb, k_blk, ok))

    s_cmp = [_nt_dot(kc_hi, q_split[g][0]) + _nt_dot(kc_lo, q_split[g][0])
             + _nt_dot(kc_hi, q_split[g][1]) for g in groups]
    s_win = [jnp.concatenate([mask_heads(ok, _nt_dot(k_blk, q_b[g]))
                              for (_, k_blk, ok) in win_blocks], axis=0) for g in groups]

    p_cmp, p_win, l_win = [], [], []
    for g in groups:
        s_m = jnp.where(cmp_valid, s_cmp[g], MASK_NEG)
        e = jnp.where(cmp_valid, jnp.exp(s_m - jnp.max(s_m, axis=0, keepdims=True)), 0.0)
        den = jnp.sum(e, axis=0, keepdims=True)
        p_cmp.append(jnp.where(den > 0.0, e / jnp.where(den > 0.0, den, 1.0), 0.0))
    for g in groups:
        p = jnp.exp(s_win[g] - jnp.max(s_win[g], axis=0, keepdims=True))
        l_win.append(jnp.sum(p, axis=0, keepdims=True))
        p_win.append(p.astype(BF16))

    o_cmp = [_dot(vct_scr[g * dh:(g + 1) * dh, :], p_cmp[g].astype(BF16)) for g in groups]
    imp = []
    for g in groups:
        p_sum = p_cmp[g][:, 0:tq]
        for i in range(1, hg):
            p_sum = p_sum + p_cmp[g][:, i * tq:(i + 1) * tq]
        ps_hi, ps_lo = _split_bf16(p_sum)
        imp.append(_dot(overlap, ps_hi) + _dot(overlap, ps_lo))
    o_win = []
    for g in groups:
        v_t = jnp.concatenate([vwt_scr[kb, g * dh:(g + 1) * dh, :] for (kb, _, _) in win_blocks],
                              axis=1)
        o_win.append(_dot(v_t, p_win[g]) / l_win[g])

    for g in groups:
        score = jnp.where(sel_valid, imp[g] + jnp.where(forced, FORCE_BONUS, 0.0), MASK_NEG)
        beaten = jnp.zeros((n_sel, tq), F32)
        for jp in range(n_sel):
            other = score[jp:jp + 1, :]
            ge = jnp.where(other >= score, 1.0, 0.0)
            gt = jnp.where(other > score, 1.0, 0.0)
            beaten = beaten + jnp.where(j_sel > jp, ge, gt)
        picked = jnp.where(beaten < top_k, 1.0, 0.0)
        for j in range(n_sel):
            selk_scr[g, j * SEL_LEN:(j + 1) * SEL_LEN, :] = jnp.broadcast_to(
                picked[j:j + 1, :], (SEL_LEN, tq))

    def slc_blocks(n_tiles):
        def run(kb0, carry):
            k_blocks = [kslc_ref[pl.ds(pl.multiple_of((kb0 + u) * tk, tk), tk), :].astype(BF16)
                        for u in range(n_tiles)]
            tiles = {}
            for g in range(n_grp):
                for u in range(n_tiles):
                    k0 = pl.multiple_of((kb0 + u) * tk, tk)
                    ok = (selk_scr[g, pl.ds(k0, tk), :] > 0.5) & (k0 + key_iota <= t_row)
                    tiles[g, u] = mask_heads(ok, _nt_dot(k_blocks[u], q_b[g]))
            new = []
            for g in range(n_grp):
                m, l, acc = carry[g]
                for u in range(n_tiles):
                    s = tiles[g, u]
                    m_new = jnp.maximum(m, jnp.max(s, axis=0, keepdims=True))
                    alpha = jnp.exp(m - m_new)
                    p = jnp.exp(s - m_new)
                    l = alpha * l + jnp.sum(p, axis=0, keepdims=True)
                    acc = alpha * acc + _dot(vst_scr[kb0 + u, g * dh:(g + 1) * dh, :],
                                             p.astype(BF16))
                    m = m_new
                new.append((m, l, acc))
            return tuple(new)
        return run

    init = tuple((jnp.full((1, nq), MASK_NEG, F32), jnp.zeros((1, nq), F32),
                  jnp.zeros((dh, nq), F32)) for _ in range(n_grp))
    n_need = (qi + 1) * (tq // tk)
    full_body = slc_blocks(slc_step)
    slc = lax.fori_loop(0, n_need // slc_step, lambda it, c: full_body(it * slc_step, c), init)
    done = (n_need // slc_step) * slc_step
    arm = slc_step // 2
    while arm >= 1:
        take = (n_need & arm) != 0
        slc = lax.cond(take, functools.partial(slc_blocks(arm), done), lambda c: c, slc)
        done = done + jnp.where(take, arm, 0)
        arm //= 2
    o_slc = [acc / l for (_, l, acc) in slc]

    gates = _sigmoid(gate_ref[...].T)
    outs = []
    for g in range(n_grp):
        for i in range(hg):
            hd = g * hg + i
            cols = slice(i * tq, (i + 1) * tq)
            outs.append(gates[3 * hd:3 * hd + 1, :] * o_cmp[g][:, cols]
                        + gates[3 * hd + 1:3 * hd + 2, :] * o_slc[g][:, cols]
                        + gates[3 * hd + 2:3 * hd + 3, :] * o_win[g][:, cols])
    o_ref[...] = jnp.concatenate(outs, axis=0).T.astype(o_ref.dtype)


def _cmp_weight(w):
    half = CMP_LEN // 2
    w = w.reshape(2, half, NSA_DH, NSA_DH)
    z = jnp.zeros_like(w)
    big = jnp.stack([jnp.concatenate([w, z], axis=-1), jnp.concatenate([z, w], axis=-1)], axis=2)
    return big.reshape(2, half * NSA_KV_HEADS * NSA_DH, NSA_KV_HEADS * NSA_DH)


def _native_sparse_attention(proj3, cmp_wk, cmp_pk, cmp_wv, cmp_pv, *, tq):
    b, s, _ = proj3.shape
    nb = s // CMP_STRIDE
    n_kb = s // NSA_TK
    assert tq == NSA_TK and n_kb % min(NSA_SLC_BLOCKS, n_kb) == 0
    wk = _cmp_weight(cmp_wk)
    wv = _cmp_weight(cmp_wv)
    pk = jnp.tile(cmp_pk, (1, NSA_KV_HEADS))
    pv = jnp.tile(cmp_pv, (1, NSA_KV_HEADS))
    qw = NSA_HEADS * NSA_DH

    def seq_block(col):
        return pl.BlockSpec((None, s, LANE), lambda bi, i: (bi, 0, col))

    def whole(a):
        return pl.BlockSpec(a.shape, lambda bi, i: (0,) * a.ndim)

    n_sub = min(NSA_BLOCKS_PER_STEP, s // tq)
    t_step = n_sub * tq
    return pl.pallas_call(
        functools.partial(_nsa_kernel, seq=s, tq=tq, n_sub=n_sub),
        grid=(b, s // t_step),
        in_specs=[pl.BlockSpec((None, t_step, qw), lambda bi, i: (bi, i, COL_NQ * LANE // qw))]
        + [seq_block(COL_NKV + c) for c in range(6)]
        + [pl.BlockSpec((None, t_step, LANE), lambda bi, i: (bi, i, COL_NGATE)),
           whole(wk), whole(wv), whole(pk), whole(pv)],
        out_specs=pl.BlockSpec((None, t_step, qw), lambda bi, i: (bi, i, 0)),
        out_shape=jax.ShapeDtypeStruct((b, s, qw), BF16),
        scratch_shapes=[pltpu.VMEM((nb, LANE), F32),
                        pltpu.VMEM((LANE, nb), BF16),
                        pltpu.VMEM((n_kb, LANE, NSA_TK), BF16),
                        pltpu.VMEM((n_kb, LANE, NSA_TK), BF16),
                        pltpu.VMEM((NSA_KV_HEADS, s, tq), F32)],
        compiler_params=_cparams(2),
        name="native_sparse_attention",
    )(proj3, proj3, proj3, proj3, proj3, proj3, proj3, proj3, wk, wv, pk, pv)


def _pad_cols(w, width):
    return jnp.pad(w, ((0, 0),) * (w.ndim - 1) + ((0, width - w.shape[-1]),))


def _proj_weight(w_in):
    gqk_v_z = 4 * GDN_HEADS * GDN_D
    n_ab = 2 * GDN_HEADS
    n_nsa = (NSA_HEADS + 6 * NSA_KV_HEADS) * NSA_DH
    n_gate = 3 * NSA_HEADS
    o_ab = gqk_v_z
    o_nsa = o_ab + n_ab
    o_gate = o_nsa + n_nsa
    o_rest = o_gate + n_gate
    w = jnp.concatenate([w_in[..., :o_ab], w_in[..., o_nsa:o_gate],
                         _pad_cols(w_in[..., o_ab:o_nsa], LANE),
                         _pad_cols(w_in[..., o_gate:o_rest], LANE),
                         w_in[..., o_rest:]], axis=-1)
    assert w.shape[-1] == N_PROJ, w.shape
    return w.astype(BF16)


def _layer(x, p, big, layer, *, tiles):
    b, s, d = x.shape
    x2 = x.reshape(b * s, d)
    proj = _norm_matmul(x2, p["g_mix_pre"], big["w_proj"], layer, relu2=False, out_dtype=F32,
                        tm=tiles["tm"], tn=tiles["tn"])
    proj3 = proj.reshape(b, s, N_PROJ)
    ya = _gated_deltanet(proj3, p["gdn_conv_w"], p["gdn_a_log"], p["gdn_dt_bias"], p["gdn_norm_g"])
    yb = _native_sparse_attention(proj3, p["nsa_cmp_wk"], p["nsa_cmp_pk"], p["nsa_cmp_wv"],
                                  p["nsa_cmp_pv"], tq=tiles["tq"])
    yc = _sb_attention(proj3, tq=tiles["tq_sb"])
    x2 = _merge_out(ya.reshape(b * s, -1), yb.reshape(b * s, -1), yc.reshape(b * s, -1), proj, x2,
                    big["w_br_gdn"], big["w_br_nsa"], big["w_br_sb"], big["w_out"], layer,
                    p["g_mix_post"], tm=tiles["tm_out"])
    act = _norm_matmul(x2, p["g_ff_pre"], big["w_ff1"], layer, relu2=True, out_dtype=BF16,
                       tm=tiles["tm"], tn=tiles["tn"])
    x2 = _matmul_norm_res(act, big["w_ff2"], layer, x2, p["g_ff_post"], tm=tiles["tm_out"])
    return x2.reshape(b, s, d)


def _tiles(b, s):
    t = b * s
    return {"tm": min(1024, t), "tn": 2048, "tm_out": min(512, t), "tq": min(128, s),
            "tq_sb": min(256, s)}


def kernel(x, g_mix_pre, g_mix_post, g_ff_pre, g_ff_post, w_in, gdn_conv_w, gdn_a_log, gdn_dt_bias,
           gdn_norm_g, nsa_cmp_wk, nsa_cmp_pk, nsa_cmp_wv, nsa_cmp_pv, w_br_gdn, w_br_nsa, w_br_sb,
           w_out, w_ff1, w_ff2):
    small = dict(g_mix_pre=g_mix_pre, g_mix_post=g_mix_post, g_ff_pre=g_ff_pre, g_ff_post=g_ff_post,
                 gdn_conv_w=gdn_conv_w, gdn_a_log=gdn_a_log, gdn_dt_bias=gdn_dt_bias,
                 gdn_norm_g=gdn_norm_g, nsa_cmp_wk=nsa_cmp_wk, nsa_cmp_pk=nsa_cmp_pk,
                 nsa_cmp_wv=nsa_cmp_wv, nsa_cmp_pv=nsa_cmp_pv)
    big = dict(w_proj=_proj_weight(w_in), w_br_gdn=w_br_gdn.astype(BF16),
               w_br_nsa=w_br_nsa.astype(BF16), w_br_sb=w_br_sb.astype(BF16),
               w_out=w_out.astype(BF16), w_ff1=w_ff1.astype(BF16), w_ff2=w_ff2.astype(BF16))
    tiles = _tiles(x.shape[0], x.shape[1])
    for layer in range(w_in.shape[0]):
        x = _layer(x, {k: v[layer] for k, v in small.items()}, big, layer, tiles=tiles)
    return x
```

```python
import functools

import jax
import jax.numpy as jnp
from jax import lax
from jax.experimental import pallas as pl
from jax.experimental.pallas import tpu as pltpu

F32 = jnp.float32
BF16 = jnp.bfloat16

EPS = 1e-6
MASK_NEG = -1e30
LOG2E = 1.4426950408889634

GDN_HEADS = 4
GDN_D = 128
GDN_CONV = 4
GDN_CHUNK = 64
GDN_PREP_CHUNKS = 8
GDN_SCAN_TOKENS = 512
GDN_SCAN_BATCHES = 4

NSA_HEADS = 8
NSA_KV_HEADS = 2
NSA_GROUP = NSA_HEADS // NSA_KV_HEADS
NSA_DH = 64
CMP_LEN = 32
CMP_STRIDE = 16
SEL_LEN = 64
SEL_TOPK = 8
WINDOW = 512
FORCE_BONUS = 1e3
NSA_TK = 128
NSA_SLC_BLOCKS = 4
NSA_BLOCKS_PER_STEP = 2

SB_HEADS = 8
SB_DH = 64
SB_TK = 128
SB_DEAD_LOG2 = -160.0
SB_BLOCKS_PER_STEP = 2

LANE = 128

COL_GQ, COL_GK, COL_GV, COL_GZ = 0, 4, 8, 12
COL_NQ, COL_NKV, COL_GAB, COL_NGATE = 16, 20, 26, 27
COL_SQ, COL_SK, COL_SV = 28, 32, 36
COL_MERGE = 40
N_PROJ = 64 * LANE

VMEM_LIMIT = 48 * 1024 * 1024


def _cparams(n_axes):
    return pltpu.CompilerParams(dimension_semantics=("arbitrary",) * n_axes,
                                vmem_limit_bytes=VMEM_LIMIT)


def _nt_dot(a, b):
    return lax.dot_general(a, b, (((1,), (1,)), ((), ())), preferred_element_type=F32)


def _tn_dot(a, b):
    return lax.dot_general(a, b, (((0,), (0,)), ((), ())), preferred_element_type=F32)


def _dot(a, b):
    return jnp.dot(a, b, preferred_element_type=F32)


def _rms_scale(y):
    return y * lax.rsqrt(jnp.mean(y * y, axis=-1, keepdims=True) + EPS)


def _softplus(z):
    return jnp.maximum(z, 0.0) + jnp.log1p(jnp.exp(-jnp.abs(z)))


def _sigmoid(z):
    return jax.nn.sigmoid(z)


def _split_bf16(x):
    hi = x.astype(BF16)
    return hi, (x - hi.astype(F32)).astype(BF16)


def _dot3(a, b):
    a_hi, a_lo = _split_bf16(a)
    b_hi, b_lo = _split_bf16(b)
    return _dot(a_hi, b_hi) + _dot(a_lo, b_hi) + _dot(a_hi, b_lo)


def _norm_matmul_kernel(x_ref, g_ref, w_ref, o_ref, h_scr, *, relu2):
    @pl.when(pl.program_id(1) == 0)
    def _():
        h_scr[...] = (_rms_scale(x_ref[...]) * g_ref[...]).astype(BF16)

    y = _dot(h_scr[...], w_ref[...])
    if relu2:
        y = jnp.square(jnp.maximum(y, 0.0))
    o_ref[...] = y.astype(o_ref.dtype)


def _norm_matmul(x, g, w, layer, *, relu2, out_dtype, tm, tn):
    t, d = x.shape
    n = w.shape[2]
    return pl.pallas_call(
        functools.partial(_norm_matmul_kernel, relu2=relu2),
        grid=(t // tm, n // tn),
        in_specs=[pl.BlockSpec((tm, d), lambda i, j: (i, 0)),
                  pl.BlockSpec((1, d), lambda i, j: (0, 0)),
                  pl.BlockSpec((None, d, tn), lambda i, j: (layer, 0, j))],
        out_specs=pl.BlockSpec((tm, tn), lambda i, j: (i, j)),
        out_shape=jax.ShapeDtypeStruct((t, n), out_dtype),
        scratch_shapes=[pltpu.VMEM((tm, d), BF16)],
        compiler_params=_cparams(2),
        name="norm_matmul_relu2" if relu2 else "norm_matmul",
    )(x, g.reshape(1, d), w)


def _matmul_norm_res_kernel(a_ref, w_ref, x_ref, g_ref, o_ref):
    y = _dot(a_ref[...], w_ref[...])
    o_ref[...] = x_ref[...] + _rms_scale(y) * g_ref[...]


def _matmul_norm_res(a, w, layer, x, g, *, tm):
    t, k = a.shape
    d = w.shape[2]
    return pl.pallas_call(
        _matmul_norm_res_kernel,
        grid=(t // tm,),
        in_specs=[pl.BlockSpec((tm, k), lambda i: (i, 0)),
                  pl.BlockSpec((None, k, d), lambda i: (layer, 0, 0)),
                  pl.BlockSpec((tm, d), lambda i: (i, 0)),
                  pl.BlockSpec((1, d), lambda i: (0, 0))],
        out_specs=pl.BlockSpec((tm, d), lambda i: (i, 0)),
        out_shape=jax.ShapeDtypeStruct((t, d), F32),
        compiler_params=_cparams(1),
        name="matmul_norm_res",
    )(a, w, x, g.reshape(1, d))


def _merge_out_kernel(ya_ref, yb_ref, yc_ref, m0_ref, m1_ref, m2_ref, x_ref,
                      wa_ref, wb_ref, wc_ref, wo_ref, g_ref, o_ref):
    merged = (_sigmoid(m0_ref[...]) * _dot(ya_ref[...], wa_ref[...])
              + _sigmoid(m1_ref[...]) * _dot(yb_ref[...], wb_ref[...])
              + _sigmoid(m2_ref[...]) * _dot(yc_ref[...], wc_ref[...]))
    y = _dot(merged.astype(BF16), wo_ref[...])
    o_ref[...] = x_ref[...] + _rms_scale(y) * g_ref[...]


def _merge_out(ya, yb, yc, proj, x, wa, wb, wc, wo, layer, g, *, tm):
    t, d = x.shape
    w_in = ya.shape[1]
    mcol = COL_MERGE * LANE // d

    def row(i):
        return (i, 0)

    def const(i):
        return (0, 0)

    def weight(i):
        return (layer, 0, 0)

    return pl.pallas_call(
        _merge_out_kernel,
        grid=(t // tm,),
        in_specs=[pl.BlockSpec((tm, w_in), row),
                  pl.BlockSpec((tm, w_in), row),
                  pl.BlockSpec((tm, w_in), row),
                  pl.BlockSpec((tm, d), lambda i: (i, mcol)),
                  pl.BlockSpec((tm, d), lambda i: (i, mcol + 1)),
                  pl.BlockSpec((tm, d), lambda i: (i, mcol + 2)),
                  pl.BlockSpec((tm, d), row),
                  pl.BlockSpec((None, w_in, d), weight),
                  pl.BlockSpec((None, w_in, d), weight),
                  pl.BlockSpec((None, w_in, d), weight),
                  pl.BlockSpec((None, d, d), weight),
                  pl.BlockSpec((1, d), const)],
        out_specs=pl.BlockSpec((tm, d), row),
        out_shape=jax.ShapeDtypeStruct((t, d), F32),
        compiler_params=_cparams(1),
        name="merge_out",
    )(ya, yb, yc, proj, proj, proj, x, wa, wb, wc, wo, g.reshape(1, d))


def _sb_kernel(q_ref, k_ref, v_ref, o_ref, run_scr, acc_scr, *, tq, n_sub):
    step = pl.program_id(1)
    for sub in range(n_sub):
        rows = pl.ds(sub * tq, tq)
        _sb_query_block(step * n_sub + sub, q_ref.at[rows], k_ref, v_ref, o_ref.at[rows],
                        run_scr, acc_scr, tq=tq)


def _sb_query_block(qi, q_ref, k_ref, v_ref, o_ref, run_scr, acc_scr, *, tq):
    tk = SB_TK
    n_diag = tq // tk
    dh = SB_DH
    n_pair = SB_HEADS * dh // LANE
    first_head = lax.broadcasted_iota(jnp.int32, (tk, LANE), 1) < dh
    r2 = lax.broadcasted_iota(jnp.int32, (2 * tk, 2 * tk), 0) & (tk - 1)
    c2 = lax.broadcasted_iota(jnp.int32, (2 * tk, 2 * tk), 1)
    later_ones = jnp.where((r2 > c2) | (c2 >= tk), 1.0, 0.0).astype(BF16)

    q_pairs = [(q_ref[:, p * LANE:(p + 1) * LANE] * (dh ** -0.5 * LOG2E)).astype(BF16)
               for p in range(n_pair)]

    run_scr[...] = jnp.zeros_like(run_scr)
    acc_scr[...] = jnp.zeros_like(acc_scr)

    def split_heads(x):
        zero = jnp.zeros_like(x)
        return jnp.concatenate([jnp.where(first_head, x, zero), jnp.where(first_head, zero, x)],
                               axis=0)

    def block(j, diag_offset):
        k0 = pl.multiple_of(j * tk, tk)
        pairs = range(n_pair)
        diagonal = diag_offset is not None
        rs = slice(diag_offset, tq) if diagonal else slice(0, tq)
        if diagonal:
            shape = (tq - diag_offset, tk)
            causal = (lax.broadcasted_iota(jnp.int32, shape, 1)
                      < lax.broadcasted_iota(jnp.int32, shape, 0))
            causal2 = jnp.concatenate([causal, causal], axis=1)
        z = [_nt_dot(q_pairs[p][rs],
                     split_heads(k_ref[pl.ds(k0, tk), p * LANE:(p + 1) * LANE].astype(BF16)))
             for p in pairs]
        log_sig, hi, lo = [], [], []
        for p in pairs:
            log_sig.append(jnp.minimum(z[p], 0.0) - jnp.log2(1.0 + jnp.exp2(-jnp.abs(z[p]))))
            log_keep = log_sig[p] - z[p]
            if diagonal:
                log_keep = jnp.where(causal2, log_keep, 0.0)
            hi.append(log_keep.astype(BF16))
            lo.append((log_keep - hi[p].astype(F32)).astype(BF16))
        sums = [[_dot(jnp.concatenate([hi[p][:, hh * tk:(hh + 1) * tk],
                                       lo[p][:, hh * tk:(hh + 1) * tk]], axis=1), later_ones)
                 for hh in range(2)] for p in pairs]
        for p in pairs:
            a = []
            for hh in range(2):
                h = 2 * p + hh
                run = run_scr[h, rs, :]
                a_h = jnp.exp2(log_sig[p][:, hh * tk:(hh + 1) * tk] + sums[p][hh][:, :tk] + run)
                if diagonal:
                    a_h = jnp.where(causal, a_h, 0.0)
                run_scr[h, rs, :] = run + sums[p][hh][:, tk:]
                a.append(a_h.astype(BF16))
            vp = v_ref[pl.ds(k0, tk), p * LANE:(p + 1) * LANE].astype(BF16)
            acc_scr[p, rs, :] += _dot(jnp.concatenate(a, axis=1), split_heads(vp))

    for d in reversed(range(n_diag)):
        block(qi * n_diag + d, d * tk)

    def still_live():
        top = run_scr[0]
        for h in range(1, SB_HEADS):
            top = jnp.maximum(top, run_scr[h])
        return jnp.max(top) > SB_DEAD_LOG2

    def body(carry):
        it, _ = carry
        block(qi * n_diag - 1 - it, None)
        return it + 1, still_live()

    lax.while_loop(lambda c: (c[0] < qi * n_diag) & c[1], body, (jnp.int32(0), still_live()))
    o_ref[...] = jnp.concatenate([acc_scr[p] for p in range(n_pair)], axis=-1).astype(o_ref.dtype)


def _sb_attention(proj3, *, tq):
    b, s, _ = proj3.shape
    width = SB_HEADS * SB_DH
    n_pair = width // LANE
    n_sub = min(SB_BLOCKS_PER_STEP, s // tq)
    t_step = n_sub * tq
    return pl.pallas_call(
        functools.partial(_sb_kernel, tq=tq, n_sub=n_sub),
        grid=(b, s // t_step),
        in_specs=[pl.BlockSpec((None, t_step, width), lambda bi, i: (bi, i, COL_SQ * LANE // width)),
                  pl.BlockSpec((None, s, width), lambda bi, i: (bi, 0, COL_SK * LANE // width)),
                  pl.BlockSpec((None, s, width), lambda bi, i: (bi, 0, COL_SV * LANE // width))],
        out_specs=pl.BlockSpec((None, t_step, width), lambda bi, i: (bi, i, 0)),
        out_shape=jax.ShapeDtypeStruct((b, s, width), BF16),
        scratch_shapes=[pltpu.VMEM((SB_HEADS, tq, LANE), F32), pltpu.VMEM((n_pair, tq, LANE), F32)],
        compiler_params=_cparams(2),
        name="sb_attention",
    )(proj3, proj3, proj3)


def _gdn_prep_kernel(scal_ref, q_ref, qh_ref, k_ref, kh_ref, v_ref, vh_ref, ab_ref, cw_ref,
                     u_ref, w_ref, qe_ref, kd_ref, attn_ref, egl_ref, *, cb):
    i = pl.program_id(1)
    c_len = GDN_CHUNK
    d = GDN_D
    n_h = GDN_HEADS
    ts = cb * c_len
    halo = qh_ref.shape[0]
    width = n_h * d

    def conv_silu(x_ref, halo_ref, w):
        ext = jnp.concatenate([halo_ref[...] * jnp.where(i > 0, 1.0, 0.0), x_ref[...]], axis=0)
        y = jnp.zeros((ts, width), F32)
        for tap in range(GDN_CONV):
            off = halo - (GDN_CONV - 1) + tap
            y = y + w[tap:tap + 1, :] * ext[off:off + ts]
        return y * _sigmoid(y)

    qc = conv_silu(q_ref, qh_ref, cw_ref[:, 0:width])
    kc = conv_silu(k_ref, kh_ref, cw_ref[:, width:2 * width])
    vc = conv_silu(v_ref, vh_ref, cw_ref[:, 2 * width:3 * width])

    ab = ab_ref[...]
    lane = lax.broadcasted_iota(jnp.int32, (1, LANE), 1)
    a_log = jnp.zeros((1, LANE), F32)
    dt_bias = jnp.zeros((1, LANE), F32)
    for h in range(n_h):
        a_log = jnp.where(lane == h, scal_ref[0, h], a_log)
        dt_bias = jnp.where(lane == h, scal_ref[1, h], dt_bias)
    g_cum = -jnp.exp(a_log) * _softplus(ab + dt_bias)
    row_in_chunk = lax.broadcasted_iota(jnp.int32, (ts, LANE), 0) & (c_len - 1)
    shift = 1
    while shift < c_len:
        g_cum = g_cum + jnp.where(row_in_chunk >= shift, pltpu.roll(g_cum, shift, axis=0), 0.0)
        shift *= 2
    beta_all = _sigmoid(ab)

    row = lax.broadcasted_iota(jnp.int32, (c_len, c_len), 0)
    col = lax.broadcasted_iota(jnp.int32, (c_len, c_len), 1)
    incl = row >= col
    strict = row > col
    eye = jnp.where(row == col, 1.0, 0.0)

    problems = [(c, h) for c in range(cb) for h in range(n_h)]
    qn, kn, k_beta, v_beta, g_col, decay = {}, {}, {}, {}, {}, {}
    for c, h in problems:
        rows = slice(c * c_len, (c + 1) * c_len)
        cols = slice(h * d, (h + 1) * d)
        qh, kh = qc[rows, cols], kc[rows, cols]
        qn[c, h] = qh * lax.rsqrt(jnp.sum(qh * qh, axis=-1, keepdims=True) + EPS) * (d ** -0.5)
        kn[c, h] = kh * lax.rsqrt(jnp.sum(kh * kh, axis=-1, keepdims=True) + EPS)
        beta = beta_all[rows, n_h + h:n_h + h + 1]
        k_beta[c, h] = kn[c, h] * beta
        v_beta[c, h] = (vc[rows, cols] * beta).astype(BF16)
        g_col[c, h] = g_cum[rows, h:h + 1]
        g_sq = jnp.broadcast_to(g_col[c, h], (c_len, c_len))
        decay[c, h] = jnp.where(incl, jnp.exp(jnp.where(incl, g_sq - g_sq.T, 0.0)), 0.0)

    kn_b = {p: kn[p].astype(BF16) for p in problems}
    kkt = {p: _nt_dot(k_beta[p].astype(BF16), kn_b[p]) for p in problems}
    qkt = {p: _nt_dot(qn[p].astype(BF16), kn_b[p]) for p in problems}

    a_mat = {p: jnp.where(strict, kkt[p] * decay[p], 0.0) for p in problems}
    left = lax.broadcasted_iota(jnp.int32, (c_len, 2 * c_len), 1) < c_len
    pair = {p: jnp.concatenate([-a_mat[p], eye], axis=1) for p in problems}
    for _ in range(max(c_len - 1, 1).bit_length()):
        y = {p: _dot(pair[p][:, :c_len].astype(BF16), pair[p].astype(BF16)) for p in problems}
        pair = {p: jnp.where(left, y[p], pair[p] + y[p]) for p in problems}
    t_inv = {p: pair[p][:, c_len:] for p in problems}
    resid = {}
    for p in problems:
        m_hi, m_lo = _split_bf16(eye + a_mat[p])
        t_hi, t_lo = _split_bf16(t_inv[p])
        resid[p] = (eye - (_dot(m_hi, t_hi) + _dot(m_lo, t_hi) + _dot(m_hi, t_lo)), t_hi)
    t_b = {p: (t_inv[p] + _dot(resid[p][1], resid[p][0].astype(BF16))).astype(BF16)
           for p in problems}

    for c, h in problems:
        p = (c, h)
        rows = slice(c * c_len, (c + 1) * c_len)
        cols = slice(h * d, (h + 1) * d)
        exp_g = jnp.exp(g_col[p])
        g_last = g_col[p][c_len - 1:c_len, :]
        u_ref[rows, cols] = _dot(t_b[p], v_beta[p])
        w_ref[rows, cols] = _dot(t_b[p], (k_beta[p] * exp_g).astype(BF16)).astype(BF16)
        qe_ref[rows, cols] = (qn[p] * exp_g).astype(BF16)
        kd_ref[rows, cols] = (kn[p] * jnp.exp(g_last - g_col[p])).astype(BF16)
    egl_rows = [jnp.concatenate([jnp.broadcast_to(jnp.exp(g_col[c, h][c_len - 1:c_len, :]), (1, d))
                                 for h in range(n_h)], axis=1) for c in range(cb)]
    if egl_ref.shape[0] > cb:
        egl_rows.append(jnp.zeros((egl_ref.shape[0] - cb, width), F32))
    egl_ref[...] = jnp.concatenate(egl_rows, axis=0)
    for c in range(cb):
        rows = slice(c * c_len, (c + 1) * c_len)
        attn_ref[rows, :] = jnp.concatenate(
            [(qkt[c, h] * decay[c, h]).astype(BF16) for h in range(n_h)], axis=1)


def _gdn_scan_kernel(u_ref, w_ref, qe_ref, kd_ref, attn_ref, egl_ref, z_ref, ng_ref, o_ref,
                     state_scr, *, bb, ts, cb):
    j = pl.program_id(1)
    c_len = GDN_CHUNK
    d = GDN_D
    n_c = ts // c_len
    chains = [(b_, h) for b_ in range(bb) for h in range(GDN_HEADS)]

    @pl.when(j == 0)
    def _():
        state_scr[...] = jnp.zeros_like(state_scr)

    def chunk(c, carry):
        r0 = pl.multiple_of(c * c_len, c_len)
        rows = pl.ds(r0, c_len)
        cg = j * n_c + c
        state = {p: state_scr[p[0], p[1]] for p in chains}
        state_b = {p: state[p].astype(BF16) for p in chains}
        w_s = {(b_, h): _dot(w_ref[b_, rows, h * d:(h + 1) * d], state_b[b_, h]) for b_, h in chains}
        q_s = {(b_, h): _dot(qe_ref[b_, rows, h * d:(h + 1) * d], state_b[b_, h]) for b_, h in chains}
        v_new = {(b_, h): (u_ref[b_, rows, h * d:(h + 1) * d] - w_s[b_, h]).astype(BF16)
                 for b_, h in chains}
        sub = lax.broadcasted_iota(jnp.int32, egl_ref.shape[2:], 0)
        chunk_decay = [jnp.sum(jnp.where(sub == cg % cb, egl_ref[b_, cg // cb], 0.0),
                               axis=0, keepdims=True) for b_ in range(bb)]
        for b_, h in chains:
            cols = slice(h * d, (h + 1) * d)
            o = q_s[b_, h] + _dot(attn_ref[b_, rows, h * c_len:(h + 1) * c_len], v_new[b_, h])
            state_scr[b_, h] = (state[b_, h] * chunk_decay[b_][:, cols]
                                + _tn_dot(kd_ref[b_, rows, cols], v_new[b_, h]))
            zc = z_ref[b_, rows, cols]
            o_ref[b_, rows, cols] = (_rms_scale(o) * ng_ref[...] * (zc * _sigmoid(zc))
                                     ).astype(o_ref.dtype)
        return carry

    lax.fori_loop(0, n_c, chunk, 0)


def _gated_deltanet(proj3, conv_w, a_log, dt_bias, norm_g):
    b, s, _ = proj3.shape
    scal = jnp.stack([a_log, dt_bias]).astype(F32)
    width = GDN_HEADS * GDN_D
    n_chunks = s // GDN_CHUNK
    cb = min(GDN_PREP_CHUNKS, n_chunks)
    ts = cb * GDN_CHUNK
    halo = 8
    egl_rows = -(-cb // 8) * 8

    def main(col0):
        return pl.BlockSpec((None, ts, width), lambda bi, i: (bi, i, col0 * LANE // width))

    def before(col0):
        return pl.BlockSpec((None, halo, width),
                            lambda bi, i: (bi, jnp.maximum(i * (ts // halo) - 1, 0),
                                           col0 * LANE // width))

    def out_block(w_):
        return pl.BlockSpec((None, ts, w_), lambda bi, i: (bi, i, 0))

    u, w, qe, kd, attn, egl = pl.pallas_call(
        functools.partial(_gdn_prep_kernel, cb=cb),
        grid=(b, n_chunks // cb),
        in_specs=[pl.BlockSpec(memory_space=pltpu.SMEM),
                  main(COL_GQ), before(COL_GQ), main(COL_GK), before(COL_GK),
                  main(COL_GV), before(COL_GV),
                  pl.BlockSpec((None, ts, LANE), lambda bi, i: (bi, i, COL_GAB)),
                  pl.BlockSpec(conv_w.shape, lambda bi, i: (0, 0))],
        out_specs=[out_block(width), out_block(width), out_block(width), out_block(width),
                   out_block(GDN_HEADS * GDN_CHUNK),
                   pl.BlockSpec((None, None, egl_rows, width), lambda bi, i: (bi, i, 0, 0))],
        out_shape=[jax.ShapeDtypeStruct((b, s, width), F32),
                   jax.ShapeDtypeStruct((b, s, width), BF16),
                   jax.ShapeDtypeStruct((b, s, width), BF16),
                   jax.ShapeDtypeStruct((b, s, width), BF16),
                   jax.ShapeDtypeStruct((b, s, GDN_HEADS * GDN_CHUNK), BF16),
                   jax.ShapeDtypeStruct((b, n_chunks // cb, egl_rows, width), F32)],
        compiler_params=_cparams(2),
        name="gdn_prep",
    )(scal, proj3, proj3, proj3, proj3, proj3, proj3, proj3, conv_w)

    bb = max(n for n in (1, 2, GDN_SCAN_BATCHES) if b % n == 0)
    t_scan = min(GDN_SCAN_TOKENS, s)

    def scan_block(w_):
        return pl.BlockSpec((bb, t_scan, w_), lambda bi, j: (bi, j, 0))

    return pl.pallas_call(
        functools.partial(_gdn_scan_kernel, bb=bb, ts=t_scan, cb=cb),
        grid=(b // bb, s // t_scan),
        in_specs=[scan_block(width), scan_block(width), scan_block(width), scan_block(width),
                  scan_block(GDN_HEADS * GDN_CHUNK),
                  pl.BlockSpec((bb, n_chunks // cb, egl_rows, width), lambda bi, j: (bi, 0, 0, 0)),
                  pl.BlockSpec((bb, t_scan, width), lambda bi, j: (bi, j, COL_GZ * LANE // width)),
                  pl.BlockSpec((1, GDN_D), lambda bi, j: (0, 0))],
        out_specs=scan_block(width),
        out_shape=jax.ShapeDtypeStruct((b, s, width), BF16),
        scratch_shapes=[pltpu.VMEM((bb, GDN_HEADS, GDN_D, GDN_D), F32)],
        compiler_params=_cparams(2),
        name="gdn_scan",
    )(u, w, qe, kd, attn, egl, proj3, norm_g.reshape(1, GDN_D))


def _nsa_stage_kv(kcmp_ref, vcmp_ref, vslc_ref, vwin_ref, wk_ref, wv_ref, pk_ref, pv_ref,
                  kc_scr, vct_scr, vst_scr, vwt_scr, *, seq):
    tk = NSA_TK
    nb = seq // CMP_STRIDE
    half = CMP_LEN // 2
    cmp_out = []
    for src, pos, w_ref in ((kcmp_ref, pk_ref, wk_ref), (vcmp_ref, pv_ref, wv_ref)):
        lo_parts, hi_parts = [], []
        for l in range(half):
            rows = src[pl.ds(l, nb, stride=CMP_STRIDE), :]
            lo_parts.append(rows + pos[l:l + 1, :])
            hi_parts.append(rows + pos[half + l:half + l + 1, :])
        a_lo = _dot3(jnp.concatenate(lo_parts, axis=1), w_ref[0])
        a_hi = _dot3(jnp.concatenate(hi_parts, axis=1), w_ref[1])
        a_hi = jnp.concatenate([a_hi[1:], jnp.zeros((1, LANE), F32)], axis=0)
        cmp_out.append(a_lo + a_hi)
    kc_scr[...] = cmp_out[0]
    vct_scr[...] = cmp_out[1].T.astype(BF16)

    def transpose_values(kb, carry):
        k0 = pl.multiple_of(kb * tk, tk)
        vst_scr[kb] = vslc_ref[pl.ds(k0, tk), :].T.astype(BF16)
        vwt_scr[kb] = vwin_ref[pl.ds(k0, tk), :].T.astype(BF16)
        return carry

    lax.fori_loop(0, seq // tk, transpose_values, 0)


def _nsa_kernel(q_ref, kcmp_ref, vcmp_ref, kslc_ref, vslc_ref, kwin_ref, vwin_ref, gate_ref,
                wk_ref, wv_ref, pk_ref, pv_ref, o_ref,
                kc_scr, vct_scr, vst_scr, vwt_scr, selk_scr, *, seq, tq, n_sub):
    step = pl.program_id(1)

    @pl.when(step == 0)
    def _():
        _nsa_stage_kv(kcmp_ref, vcmp_ref, vslc_ref, vwin_ref, wk_ref, wv_ref, pk_ref, pv_ref,
                      kc_scr, vct_scr, vst_scr, vwt_scr, seq=seq)

    for sub in range(n_sub):
        rows = pl.ds(sub * tq, tq)
        _nsa_query_block(step * n_sub + sub, q_ref.at[rows], kslc_ref, kwin_ref, gate_ref.at[rows],
                         o_ref.at[rows], kc_scr, vct_scr, vst_scr, vwt_scr, selk_scr,
                         seq=seq, tq=tq)


def _nsa_query_block(qi, q_ref, kslc_ref, kwin_ref, gate_ref, o_ref,
                     kc_scr, vct_scr, vst_scr, vwt_scr, selk_scr, *, seq, tq):
    tk = NSA_TK
    dh = NSA_DH
    hg = NSA_GROUP
    n_grp = NSA_KV_HEADS
    nq = hg * tq
    nb = seq // CMP_STRIDE
    n_sel = seq // SEL_LEN
    n_kb = seq // tk
    top_k = min(SEL_TOPK, n_sel)
    slc_step = min(NSA_SLC_BLOCKS, n_kb)
    n_win = WINDOW // tk + 1

    lane = lax.broadcasted_iota(jnp.int32, (tq, LANE), 1)
    qg = []
    for g in range(n_grp):
        parts = []
        for i in range(hg):
            hd = g * hg + i
            pair = q_ref[:, (hd // 2) * LANE:(hd // 2 + 1) * LANE] * (dh ** -0.5)
            if hd % 2 != g:
                pair = pltpu.roll(pair, dh, axis=1)
            parts.append(jnp.where((lane >= g * dh) & (lane < (g + 1) * dh), pair, 0.0))
        qg.append(jnp.concatenate(parts, axis=0))

    t_row = qi * tq + lax.broadcasted_iota(jnp.int32, (1, tq), 1)
    t_row4 = jnp.concatenate([t_row] * hg, axis=1)
    key_iota = lax.broadcasted_iota(jnp.int32, (tk, 1), 0)

    j_sel = lax.broadcasted_iota(jnp.int32, (n_sel, tq), 0)
    cur = lax.shift_right_logical(t_row, SEL_LEN.bit_length() - 1)
    sel_valid = j_sel <= cur
    forced = (j_sel == 0) | (j_sel == cur) | (j_sel == cur - 1)
    ov_j = lax.broadcasted_iota(jnp.int32, (n_sel, nb), 0) * SEL_LEN
    ov_n = lax.broadcasted_iota(jnp.int32, (n_sel, nb), 1) * CMP_STRIDE
    overlap = jnp.where((ov_n < ov_j + SEL_LEN) & (ov_n + CMP_LEN > ov_j), 1.0, 0.0).astype(BF16)
    n_idx = lax.broadcasted_iota(jnp.int32, (nb, 1), 0)
    cmp_valid = (n_idx * CMP_STRIDE + CMP_LEN - 1 <= t_row4) & (n_idx < nb - 1)

    def mask_heads(ok, s):
        return jnp.where(jnp.concatenate([ok] * hg, axis=1), s, MASK_NEG)

    groups = range(n_grp)
    kc_hi, kc_lo = _split_bf16(kc_scr[...])
    q_split = [_split_bf16(qg[g]) for g in groups]
    q_b = [q_split[g][0] for g in groups]

    win_blocks = []
    for w in range(n_win):
        kb_int = qi - w
        kb = jnp.maximum(kb_int, 0)
        kpos = kb_int * tk + key_iota
        ok = (kpos <= t_row) & (kpos > t_row - WINDOW) & (kpos >= 0)
        k_blk = kwin_ref[pl.ds(pl.multiple_of(kb * tk, tk), tk), :].astype(BF16)
        win_blocks.append((kb, k_blk, ok))

    s_cmp = [_nt_dot(kc_hi, q_split[g][0]) + _nt_dot(kc_lo, q_split[g][0])
             + _nt_dot(kc_hi, q_split[g][1]) for g in groups]
    s_win = [jnp.concatenate([mask_heads(ok, _nt_dot(k_blk, q_b[g]))
                              for (_, k_blk, ok) in win_blocks], axis=0) for g in groups]

    p_cmp, p_win, l_win = [], [], []
    for g in groups:
        s_m = jnp.where(cmp_valid, s_cmp[g], MASK_NEG)
        e = jnp.where(cmp_valid, jnp.exp(s_m - jnp.max(s_m, axis=0, keepdims=True)), 0.0)
        den = jnp.sum(e, axis=0, keepdims=True)
        p_cmp.append(jnp.where(den > 0.0, e / jnp.where(den > 0.0, den, 1.0), 0.0))
    for g in groups:
        p = jnp.exp(s_win[g] - jnp.max(s_win[g], axis=0, keepdims=True))
        l_win.append(jnp.sum(p, axis=0, keepdims=True))
        p_win.append(p.astype(BF16))

    o_cmp = [_dot(vct_scr[g * dh:(g + 1) * dh, :], p_cmp[g].astype(BF16)) for g in groups]
    imp = []
    for g in groups:
        p_sum = p_cmp[g][:, 0:tq]
        for i in range(1, hg):
            p_sum = p_sum + p_cmp[g][:, i * tq:(i + 1) * tq]
        ps_hi, ps_lo = _split_bf16(p_sum)
        imp.append(_dot(overlap, ps_hi) + _dot(overlap, ps_lo))
    o_win = []
    for g in groups:
        v_t = jnp.concatenate([vwt_scr[kb, g * dh:(g + 1) * dh, :] for (kb, _, _) in win_blocks],
                              axis=1)
        o_win.append(_dot(v_t, p_win[g]) / l_win[g])

    for g in groups:
        score = jnp.where(sel_valid, imp[g] + jnp.where(forced, FORCE_BONUS, 0.0), MASK_NEG)
        beaten = jnp.zeros((n_sel, tq), F32)
        for jp in range(n_sel):
            other = score[jp:jp + 1, :]
            ge = jnp.where(other >= score, 1.0, 0.0)
            gt = jnp.where(other > score, 1.0, 0.0)
            beaten = beaten + jnp.where(j_sel > jp, ge, gt)
        picked = jnp.where(beaten < top_k, 1.0, 0.0)
        for j in range(n_sel):
            selk_scr[g, j * SEL_LEN:(j + 1) * SEL_LEN, :] = jnp.broadcast_to(
                picked[j:j + 1, :], (SEL_LEN, tq))

    def slc_blocks(n_tiles):
        def run(kb0, carry):
            k_blocks = [kslc_ref[pl.ds(pl.multiple_of((kb0 + u) * tk, tk), tk), :].astype(BF16)
                        for u in range(n_tiles)]
            tiles = {}
            for g in range(n_grp):
                for u in range(n_tiles):
                    k0 = pl.multiple_of((kb0 + u) * tk, tk)
                    ok = (selk_scr[g, pl.ds(k0, tk), :] > 0.5) & (k0 + key_iota <= t_row)
                    tiles[g, u] = mask_heads(ok, _nt_dot(k_blocks[u], q_b[g]))
            new = []
            for g in range(n_grp):
                m, l, acc = carry[g]
                for u in range(n_tiles):
                    s = tiles[g, u]
                    m_new = jnp.maximum(m, jnp.max(s, axis=0, keepdims=True))
                    alpha = jnp.exp(m - m_new)
                    p = jnp.exp(s - m_new)
                    l = alpha * l + jnp.sum(p, axis=0, keepdims=True)
                    acc = alpha * acc + _dot(vst_scr[kb0 + u, g * dh:(g + 1) * dh, :],
                                             p.astype(BF16))
                    m = m_new
                new.append((m, l, acc))
            return tuple(new)
        return run

    init = tuple((jnp.full((1, nq), MASK_NEG, F32), jnp.zeros((1, nq), F32),
                  jnp.zeros((dh, nq), F32)) for _ in range(n_grp))
    n_need = (qi + 1) * (tq // tk)
    full_body = slc_blocks(slc_step)
    slc = lax.fori_loop(0, n_need // slc_step, lambda it, c: full_body(it * slc_step, c), init)
    done = (n_need // slc_step) * slc_step
    arm = slc_step // 2
    while arm >= 1:
        take = (n_need & arm) != 0
        slc = lax.cond(take, functools.partial(slc_blocks(arm), done), lambda c: c, slc)
        done = done + jnp.where(take, arm, 0)
        arm //= 2
    o_slc = [acc / l for (_, l, acc) in slc]

    gates = _sigmoid(gate_ref[...].T)
    outs = []
    for g in range(n_grp):
        for i in range(hg):
            hd = g * hg + i
            cols = slice(i * tq, (i + 1) * tq)
            outs.append(gates[3 * hd:3 * hd + 1, :] * o_cmp[g][:, cols]
                        + gates[3 * hd + 1:3 * hd + 2, :] * o_slc[g][:, cols]
                        + gates[3 * hd + 2:3 * hd + 3, :] * o_win[g][:, cols])
    o_ref[...] = jnp.concatenate(outs, axis=0).T.astype(o_ref.dtype)


def _cmp_weight(w):
    half = CMP_LEN // 2
    w = w.reshape(2, half, NSA_DH, NSA_DH)
    z = jnp.zeros_like(w)
    big = jnp.stack([jnp.concatenate([w, z], axis=-1), jnp.concatenate([z, w], axis=-1)], axis=2)
    return big.reshape(2, half * NSA_KV_HEADS * NSA_DH, NSA_KV_HEADS * NSA_DH)


def _native_sparse_attention(proj3, cmp_wk, cmp_pk, cmp_wv, cmp_pv, *, tq):
    b, s, _ = proj3.shape
    nb = s // CMP_STRIDE
    n_kb = s // NSA_TK
    assert tq == NSA_TK and n_kb % min(NSA_SLC_BLOCKS, n_kb) == 0
    wk = _cmp_weight(cmp_wk)
    wv = _cmp_weight(cmp_wv)
    pk = jnp.tile(cmp_pk, (1, NSA_KV_HEADS))
    pv = jnp.tile(cmp_pv, (1, NSA_KV_HEADS))
    qw = NSA_HEADS * NSA_DH

    def seq_block(col):
        return pl.BlockSpec((None, s, LANE), lambda bi, i: (bi, 0, col))

    def whole(a):
        return pl.BlockSpec(a.shape, lambda bi, i: (0,) * a.ndim)

    n_sub = min(NSA_BLOCKS_PER_STEP, s // tq)
    t_step = n_sub * tq
    return pl.pallas_call(
        functools.partial(_nsa_kernel, seq=s, tq=tq, n_sub=n_sub),
        grid=(b, s // t_step),
        in_specs=[pl.BlockSpec((None, t_step, qw), lambda bi, i: (bi, i, COL_NQ * LANE // qw))]
        + [seq_block(COL_NKV + c) for c in range(6)]
        + [pl.BlockSpec((None, t_step, LANE), lambda bi, i: (bi, i, COL_NGATE)),
           whole(wk), whole(wv), whole(pk), whole(pv)],
        out_specs=pl.BlockSpec((None, t_step, qw), lambda bi, i: (bi, i, 0)),
        out_shape=jax.ShapeDtypeStruct((b, s, qw), BF16),
        scratch_shapes=[pltpu.VMEM((nb, LANE), F32),
                        pltpu.VMEM((LANE, nb), BF16),
                        pltpu.VMEM((n_kb, LANE, NSA_TK), BF16),
                        pltpu.VMEM((n_kb, LANE, NSA_TK), BF16),
                        pltpu.VMEM((NSA_KV_HEADS, s, tq), F32)],
        compiler_params=_cparams(2),
        name="native_sparse_attention",
    )(proj3, proj3, proj3, proj3, proj3, proj3, proj3, proj3, wk, wv, pk, pv)


def _pad_cols(w, width):
    return jnp.pad(w, ((0, 0),) * (w.ndim - 1) + ((0, width - w.shape[-1]),))


def _proj_weight(w_in):
    gqk_v_z = 4 * GDN_HEADS * GDN_D
    n_ab = 2 * GDN_HEADS
    n_nsa = (NSA_HEADS + 6 * NSA_KV_HEADS) * NSA_DH
    n_gate = 3 * NSA_HEADS
    o_ab = gqk_v_z
    o_nsa = o_ab + n_ab
    o_gate = o_nsa + n_nsa
    o_rest = o_gate + n_gate
    w = jnp.concatenate([w_in[..., :o_ab], w_in[..., o_nsa:o_gate],
                         _pad_cols(w_in[..., o_ab:o_nsa], LANE),
                         _pad_cols(w_in[..., o_gate:o_rest], LANE),
                         w_in[..., o_rest:]], axis=-1)
    assert w.shape[-1] == N_PROJ, w.shape
    return w.astype(BF16)


def _layer(x, p, big, layer, *, tiles):
    b, s, d = x.shape
    x2 = x.reshape(b * s, d)
    proj = _norm_matmul(x2, p["g_mix_pre"], big["w_proj"], layer, relu2=False, out_dtype=F32,
                        tm=tiles["tm"], tn=tiles["tn"])
    proj3 = proj.reshape(b, s, N_PROJ)
    ya = _gated_deltanet(proj3, p["gdn_conv_w"], p["gdn_a_log"], p["gdn_dt_bias"], p["gdn_norm_g"])
    yb = _native_sparse_attention(proj3, p["nsa_cmp_wk"], p["nsa_cmp_pk"], p["nsa_cmp_wv"],
                                  p["nsa_cmp_pv"], tq=tiles["tq"])
    yc = _sb_attention(proj3, tq=tiles["tq_sb"])
    x2 = _merge_out(ya.reshape(b * s, -1), yb.reshape(b * s, -1), yc.reshape(b * s, -1), proj, x2,
                    big["w_br_gdn"], big["w_br_nsa"], big["w_br_sb"], big["w_out"], layer,
                    p["g_mix_post"], tm=tiles["tm_out"])
    act = _norm_matmul(x2, p["g_ff_pre"], big["w_ff1"], layer, relu2=True, out_dtype=BF16,
                       tm=tiles["tm"], tn=tiles["tn"])
    x2 = _matmul_norm_res(act, big["w_ff2"], layer, x2, p["g_ff_post"], tm=tiles["tm_out"])
    return x2.reshape(b, s, d)


def _tiles(b, s):
    t = b * s
    return {"tm": min(1024, t), "tn": 2048, "tm_out": min(512, t), "tq": min(128, s),
            "tq_sb": min(256, s)}


def kernel(x, g_mix_pre, g_mix_post, g_ff_pre, g_ff_post, w_in, gdn_conv_w, gdn_a_log, gdn_dt_bias,
           gdn_norm_g, nsa_cmp_wk, nsa_cmp_pk, nsa_cmp_wv, nsa_cmp_pv, w_br_gdn, w_br_nsa, w_br_sb,
           w_out, w_ff1, w_ff2):
    small = dict(g_mix_pre=g_mix_pre, g_mix_post=g_mix_post, g_ff_pre=g_ff_pre, g_ff_post=g_ff_post,
                 gdn_conv_w=gdn_conv_w, gdn_a_log=gdn_a_log, gdn_dt_bias=gdn_dt_bias,
                 gdn_norm_g=gdn_norm_g, nsa_cmp_wk=nsa_cmp_wk, nsa_cmp_pk=nsa_cmp_pk,
                 nsa_cmp_wv=nsa_cmp_wv, nsa_cmp_pv=nsa_cmp_pv)
    big = dict(w_proj=_proj_weight(w_in), w_br_gdn=w_br_gdn.astype(BF16),
               w_br_nsa=w_br_nsa.astype(BF16), w_br_sb=w_br_sb.astype(BF16),
               w_out=w_out.astype(BF16), w_ff1=w_ff1.astype(BF16), w_ff2=w_ff2.astype(BF16))
    tiles = _tiles(x.shape[0], x.shape[1])
    for layer in range(w_in.shape[0]):
        x = _layer(x, {k: v[layer] for k, v in small.items()}, big, layer, tiles=tiles)
    return x
```

```python
import functools

import jax
import jax.numpy as jnp
from jax import lax
from jax.experimental import pallas as pl
from jax.experimental.pallas import tpu as pltpu

F32 = jnp.float32
BF16 = jnp.bfloat16

EPS = 1e-6
MASK_NEG = -1e30
LOG2E = 1.4426950408889634

GDN_HEADS = 4
GDN_D = 128
GDN_CONV = 4
GDN_CHUNK = 64
GDN_PREP_CHUNKS = 8
GDN_SCAN_TOKENS = 512
GDN_SCAN_BATCHES = 4

NSA_HEADS = 8
NSA_KV_HEADS = 2
NSA_GROUP = NSA_HEADS // NSA_KV_HEADS
NSA_DH = 64
CMP_LEN = 32
CMP_STRIDE = 16
SEL_LEN = 64
SEL_TOPK = 8
WINDOW = 512
FORCE_BONUS = 1e3
NSA_TK = 128
NSA_SLC_BLOCKS = 8
NSA_BLOCKS_PER_STEP = 2

SB_HEADS = 8
SB_DH = 64
SB_TK = 128
SB_DEAD_LOG2 = -160.0
SB_BLOCKS_PER_STEP = 2

LANE = 128

COL_GQ, COL_GK, COL_GV, COL_GZ = 0, 4, 8, 12
COL_NQ, COL_NKV, COL_GAB, COL_NGATE = 16, 20, 26, 27
COL_SQ, COL_SK, COL_SV = 28, 32, 36
COL_MERGE = 40
N_PROJ = 64 * LANE

VMEM_LIMIT = 48 * 1024 * 1024


def _cparams(n_axes):
    return pltpu.CompilerParams(dimension_semantics=("arbitrary",) * n_axes,
                                vmem_limit_bytes=VMEM_LIMIT)


def _nt_dot(a, b):
    return lax.dot_general(a, b, (((1,), (1,)), ((), ())), preferred_element_type=F32)


def _tn_dot(a, b):
    return lax.dot_general(a, b, (((0,), (0,)), ((), ())), preferred_element_type=F32)


def _dot(a, b):
    return jnp.dot(a, b, preferred_element_type=F32)


def _rms_scale(y):
    return y * lax.rsqrt(jnp.mean(y * y, axis=-1, keepdims=True) + EPS)


def _softplus(z):
    return jnp.maximum(z, 0.0) + jnp.log1p(jnp.exp(-jnp.abs(z)))


def _sigmoid(z):
    return jax.nn.sigmoid(z)


def _split_bf16(x):
    hi = x.astype(BF16)
    return hi, (x - hi.astype(F32)).astype(BF16)


def _dot3(a, b):
    a_hi, a_lo = _split_bf16(a)
    b_hi, b_lo = _split_bf16(b)
    return _dot(a_hi, b_hi) + _dot(a_lo, b_hi) + _dot(a_hi, b_lo)


def _norm_matmul_kernel(x_ref, g_ref, w_ref, o_ref, h_scr, *, relu2):
    @pl.when(pl.program_id(1) == 0)
    def _():
        h_scr[...] = (_rms_scale(x_ref[...]) * g_ref[...]).astype(BF16)

    y = _dot(h_scr[...], w_ref[...])
    if relu2:
        y = jnp.square(jnp.maximum(y, 0.0))
    o_ref[...] = y.astype(o_ref.dtype)


def _norm_matmul(x, g, w, layer, *, relu2, out_dtype, tm, tn):
    t, d = x.shape
    n = w.shape[2]
    return pl.pallas_call(
        functools.partial(_norm_matmul_kernel, relu2=relu2),
        grid=(t // tm, n // tn),
        in_specs=[pl.BlockSpec((tm, d), lambda i, j: (i, 0)),
                  pl.BlockSpec((1, d), lambda i, j: (0, 0)),
                  pl.BlockSpec((None, d, tn), lambda i, j: (layer, 0, j))],
        out_specs=pl.BlockSpec((tm, tn), lambda i, j: (i, j)),
        out_shape=jax.ShapeDtypeStruct((t, n), out_dtype),
        scratch_shapes=[pltpu.VMEM((tm, d), BF16)],
        compiler_params=_cparams(2),
        name="norm_matmul_relu2" if relu2 else "norm_matmul",
    )(x, g.reshape(1, d), w)


def _matmul_norm_res_kernel(a_ref, w_ref, x_ref, g_ref, o_ref):
    y = _dot(a_ref[...], w_ref[...])
    o_ref[...] = x_ref[...] + _rms_scale(y) * g_ref[...]


def _matmul_norm_res(a, w, layer, x, g, *, tm):
    t, k = a.shape
    d = w.shape[2]
    return pl.pallas_call(
        _matmul_norm_res_kernel,
        grid=(t // tm,),
        in_specs=[pl.BlockSpec((tm, k), lambda i: (i, 0)),
                  pl.BlockSpec((None, k, d), lambda i: (layer, 0, 0)),
                  pl.BlockSpec((tm, d), lambda i: (i, 0)),
                  pl.BlockSpec((1, d), lambda i: (0, 0))],
        out_specs=pl.BlockSpec((tm, d), lambda i: (i, 0)),
        out_shape=jax.ShapeDtypeStruct((t, d), F32),
        compiler_params=_cparams(1),
        name="matmul_norm_res",
    )(a, w, x, g.reshape(1, d))


def _merge_out_kernel(ya_ref, yb_ref, yc_ref, m0_ref, m1_ref, m2_ref, x_ref,
                      wa_ref, wb_ref, wc_ref, wo_ref, g_ref, o_ref):
    merged = (_sigmoid(m0_ref[...]) * _dot(ya_ref[...], wa_ref[...])
              + _sigmoid(m1_ref[...]) * _dot(yb_ref[...], wb_ref[...])
              + _sigmoid(m2_ref[...]) * _dot(yc_ref[...], wc_ref[...]))
    y = _dot(merged.astype(BF16), wo_ref[...])
    o_ref[...] = x_ref[...] + _rms_scale(y) * g_ref[...]


def _merge_out(ya, yb, yc, proj, x, wa, wb, wc, wo, layer, g, *, tm):
    t, d = x.shape
    w_in = ya.shape[1]
    mcol = COL_MERGE * LANE // d

    def row(i):
        return (i, 0)

    def const(i):
        return (0, 0)

    def weight(i):
        return (layer, 0, 0)

    return pl.pallas_call(
        _merge_out_kernel,
        grid=(t // tm,),
        in_specs=[pl.BlockSpec((tm, w_in), row),
                  pl.BlockSpec((tm, w_in), row),
                  pl.BlockSpec((tm, w_in), row),
                  pl.BlockSpec((tm, d), lambda i: (i, mcol)),
                  pl.BlockSpec((tm, d), lambda i: (i, mcol + 1)),
                  pl.BlockSpec((tm, d), lambda i: (i, mcol + 2)),
                  pl.BlockSpec((tm, d), row),
                  pl.BlockSpec((None, w_in, d), weight),
                  pl.BlockSpec((None, w_in, d), weight),
                  pl.BlockSpec((None, w_in, d), weight),
                  pl.BlockSpec((None, d, d), weight),
                  pl.BlockSpec((1, d), const)],
        out_specs=pl.BlockSpec((tm, d), row),
        out_shape=jax.ShapeDtypeStruct((t, d), F32),
        compiler_params=_cparams(1),
        name="merge_out",
    )(ya, yb, yc, proj, proj, proj, x, wa, wb, wc, wo, g.reshape(1, d))


def _sb_kernel(q_ref, k_ref, v_ref, o_ref, run_scr, acc_scr, *, tq, n_sub):
    step = pl.program_id(1)
    for sub in range(n_sub):
        rows = pl.ds(sub * tq, tq)
        _sb_query_block(step * n_sub + sub, q_ref.at[rows], k_ref, v_ref, o_ref.at[rows],
                        run_scr, acc_scr, tq=tq)


def _sb_query_block(qi, q_ref, k_ref, v_ref, o_ref, run_scr, acc_scr, *, tq):
    tk = SB_TK
    n_diag = tq // tk
    dh = SB_DH
    n_pair = SB_HEADS * dh // LANE
    first_head = lax.broadcasted_iota(jnp.int32, (tk, LANE), 1) < dh
    r2 = lax.broadcasted_iota(jnp.int32, (2 * tk, 2 * tk), 0) & (tk - 1)
    c2 = lax.broadcasted_iota(jnp.int32, (2 * tk, 2 * tk), 1)
    later_ones = jnp.where((r2 > c2) | (c2 >= tk), 1.0, 0.0).astype(BF16)

    q_pairs = [(q_ref[:, p * LANE:(p + 1) * LANE] * (dh ** -0.5 * LOG2E)).astype(BF16)
               for p in range(n_pair)]

    run_scr[...] = jnp.zeros_like(run_scr)
    acc_scr[...] = jnp.zeros_like(acc_scr)

    def split_heads(x):
        zero = jnp.zeros_like(x)
        return jnp.concatenate([jnp.where(first_head, x, zero), jnp.where(first_head, zero, x)],
                               axis=0)

    def block(j, diag_offset):
        k0 = pl.multiple_of(j * tk, tk)
        pairs = range(n_pair)
        diagonal = diag_offset is not None
        rs = slice(diag_offset, tq) if diagonal else slice(0, tq)
        if diagonal:
            shape = (tq - diag_offset, tk)
            causal = (lax.broadcasted_iota(jnp.int32, shape, 1)
                      < lax.broadcasted_iota(jnp.int32, shape, 0))
            causal2 = jnp.concatenate([causal, causal], axis=1)
        z = [_nt_dot(q_pairs[p][rs],
                     split_heads(k_ref[pl.ds(k0, tk), p * LANE:(p + 1) * LANE].astype(BF16)))
             for p in pairs]
        log_sig, hi, lo = [], [], []
        for p in pairs:
            log_sig.append(jnp.minimum(z[p], 0.0) - jnp.log2(1.0 + jnp.exp2(-jnp.abs(z[p]))))
            log_keep = log_sig[p] - z[p]
            if diagonal:
                log_keep = jnp.where(causal2, log_keep, 0.0)
            hi.append(log_keep.astype(BF16))
            lo.append((log_keep - hi[p].astype(F32)).astype(BF16))
        sums = [[_dot(jnp.concatenate([hi[p][:, hh * tk:(hh + 1) * tk],
                                       lo[p][:, hh * tk:(hh + 1) * tk]], axis=1), later_ones)
                 for hh in range(2)] for p in pairs]
        for p in pairs:
            a = []
            for hh in range(2):
                h = 2 * p + hh
                run = run_scr[h, rs, :]
                a_h = jnp.exp2(log_sig[p][:, hh * tk:(hh + 1) * tk] + sums[p][hh][:, :tk] + run)
                if diagonal:
                    a_h = jnp.where(causal, a_h, 0.0)
                run_scr[h, rs, :] = run + sums[p][hh][:, tk:]
                a.append(a_h.astype(BF16))
            vp = v_ref[pl.ds(k0, tk), p * LANE:(p + 1) * LANE].astype(BF16)
            acc_scr[p, rs, :] += _dot(jnp.concatenate(a, axis=1), split_heads(vp))

    for d in reversed(range(n_diag)):
        block(qi * n_diag + d, d * tk)

    def still_live():
        top = run_scr[0]
        for h in range(1, SB_HEADS):
            top = jnp.maximum(top, run_scr[h])
        return jnp.max(top) > SB_DEAD_LOG2

    def body(carry):
        it, _ = carry
        block(qi * n_diag - 1 - it, None)
        return it + 1, still_live()

    lax.while_loop(lambda c: (c[0] < qi * n_diag) & c[1], body, (jnp.int32(0), still_live()))
    o_ref[...] = jnp.concatenate([acc_scr[p] for p in range(n_pair)], axis=-1).astype(o_ref.dtype)


def _sb_attention(proj3, *, tq):
    b, s, _ = proj3.shape
    width = SB_HEADS * SB_DH
    n_pair = width // LANE
    n_sub = min(SB_BLOCKS_PER_STEP, s // tq)
    t_step = n_sub * tq
    return pl.pallas_call(
        functools.partial(_sb_kernel, tq=tq, n_sub=n_sub),
        grid=(b, s // t_step),
        in_specs=[pl.BlockSpec((None, t_step, width), lambda bi, i: (bi, i, COL_SQ * LANE // width)),
                  pl.BlockSpec((None, s, width), lambda bi, i: (bi, 0, COL_SK * LANE // width)),
                  pl.BlockSpec((None, s, width), lambda bi, i: (bi, 0, COL_SV * LANE // width))],
        out_specs=pl.BlockSpec((None, t_step, width), lambda bi, i: (bi, i, 0)),
        out_shape=jax.ShapeDtypeStruct((b, s, width), BF16),
        scratch_shapes=[pltpu.VMEM((SB_HEADS, tq, LANE), F32), pltpu.VMEM((n_pair, tq, LANE), F32)],
        compiler_params=_cparams(2),
        name="sb_attention",
    )(proj3, proj3, proj3)


def _gdn_prep_kernel(scal_ref, q_ref, qh_ref, k_ref, kh_ref, v_ref, vh_ref, ab_ref, cw_ref,
                     u_ref, w_ref, qe_ref, kd_ref, attn_ref, egl_ref, *, cb):
    i = pl.program_id(1)
    c_len = GDN_CHUNK
    d = GDN_D
    n_h = GDN_HEADS
    ts = cb * c_len
    halo = qh_ref.shape[0]
    width = n_h * d

    def conv_silu(x_ref, halo_ref, w):
        ext = jnp.concatenate([halo_ref[...] * jnp.where(i > 0, 1.0, 0.0), x_ref[...]], axis=0)
        y = jnp.zeros((ts, width), F32)
        for tap in range(GDN_CONV):
            off = halo - (GDN_CONV - 1) + tap
            y = y + w[tap:tap + 1, :] * ext[off:off + ts]
        return y * _sigmoid(y)

    qc = conv_silu(q_ref, qh_ref, cw_ref[:, 0:width])
    kc = conv_silu(k_ref, kh_ref, cw_ref[:, width:2 * width])
    vc = conv_silu(v_ref, vh_ref, cw_ref[:, 2 * width:3 * width])

    ab = ab_ref[...]
    lane = lax.broadcasted_iota(jnp.int32, (1, LANE), 1)
    a_log = jnp.zeros((1, LANE), F32)
    dt_bias = jnp.zeros((1, LANE), F32)
    for h in range(n_h):
        a_log = jnp.where(lane == h, scal_ref[0, h], a_log)
        dt_bias = jnp.where(lane == h, scal_ref[1, h], dt_bias)
    g_cum = -jnp.exp(a_log) * _softplus(ab + dt_bias)
    row_in_chunk = lax.broadcasted_iota(jnp.int32, (ts, LANE), 0) & (c_len - 1)
    shift = 1
    while shift < c_len:
        g_cum = g_cum + jnp.where(row_in_chunk >= shift, pltpu.roll(g_cum, shift, axis=0), 0.0)
        shift *= 2
    beta_all = _sigmoid(ab)

    row = lax.broadcasted_iota(jnp.int32, (c_len, c_len), 0)
    col = lax.broadcasted_iota(jnp.int32, (c_len, c_len), 1)
    incl = row >= col
    strict = row > col
    eye = jnp.where(row == col, 1.0, 0.0)

    problems = [(c, h) for c in range(cb) for h in range(n_h)]
    qn, kn, k_beta, v_beta, g_col, decay = {}, {}, {}, {}, {}, {}
    for c, h in problems:
        rows = slice(c * c_len, (c + 1) * c_len)
        cols = slice(h * d, (h + 1) * d)
        qh, kh = qc[rows, cols], kc[rows, cols]
        qn[c, h] = qh * lax.rsqrt(jnp.sum(qh * qh, axis=-1, keepdims=True) + EPS) * (d ** -0.5)
        kn[c, h] = kh * lax.rsqrt(jnp.sum(kh * kh, axis=-1, keepdims=True) + EPS)
        beta = beta_all[rows, n_h + h:n_h + h + 1]
        k_beta[c, h] = kn[c, h] * beta
        v_beta[c, h] = (vc[rows, cols] * beta).astype(BF16)
        g_col[c, h] = g_cum[rows, h:h + 1]
        g_sq = jnp.broadcast_to(g_col[c, h], (c_len, c_len))
        decay[c, h] = jnp.where(incl, jnp.exp(jnp.where(incl, g_sq - g_sq.T, 0.0)), 0.0)

    kn_b = {p: kn[p].astype(BF16) for p in problems}
    kkt = {p: _nt_dot(k_beta[p].astype(BF16), kn_b[p]) for p in problems}
    qkt = {p: _nt_dot(qn[p].astype(BF16), kn_b[p]) for p in problems}

    a_mat = {p: jnp.where(strict, kkt[p] * decay[p], 0.0) for p in problems}
    left = lax.broadcasted_iota(jnp.int32, (c_len, 2 * c_len), 1) < c_len
    pair = {p: jnp.concatenate([-a_mat[p], eye], axis=1) for p in problems}
    for _ in range(max(c_len - 1, 1).bit_length()):
        y = {p: _dot(pair[p][:, :c_len].astype(BF16), pair[p].astype(BF16)) for p in problems}
        pair = {p: jnp.where(left, y[p], pair[p] + y[p]) for p in problems}
    t_inv = {p: pair[p][:, c_len:] for p in problems}
    resid = {}
    for p in problems:
        m_hi, m_lo = _split_bf16(eye + a_mat[p])
        t_hi, t_lo = _split_bf16(t_inv[p])
        resid[p] = (eye - (_dot(m_hi, t_hi) + _dot(m_lo, t_hi) + _dot(m_hi, t_lo)), t_hi)
    t_b = {p: (t_inv[p] + _dot(resid[p][1], resid[p][0].astype(BF16))).astype(BF16)
           for p in problems}

    for c, h in problems:
        p = (c, h)
        rows = slice(c * c_len, (c + 1) * c_len)
        cols = slice(h * d, (h + 1) * d)
        exp_g = jnp.exp(g_col[p])
        g_last = g_col[p][c_len - 1:c_len, :]
        u_ref[rows, cols] = _dot(t_b[p], v_beta[p])
        w_ref[rows, cols] = _dot(t_b[p], (k_beta[p] * exp_g).astype(BF16)).astype(BF16)
        qe_ref[rows, cols] = (qn[p] * exp_g).astype(BF16)
        kd_ref[rows, cols] = (kn[p] * jnp.exp(g_last - g_col[p])).astype(BF16)
    egl_rows = [jnp.concatenate([jnp.broadcast_to(jnp.exp(g_col[c, h][c_len - 1:c_len, :]), (1, d))
                                 for h in range(n_h)], axis=1) for c in range(cb)]
    if egl_ref.shape[0] > cb:
        egl_rows.append(jnp.zeros((egl_ref.shape[0] - cb, width), F32))
    egl_ref[...] = jnp.concatenate(egl_rows, axis=0)
    for c in range(cb):
        rows = slice(c * c_len, (c + 1) * c_len)
        attn_ref[rows, :] = jnp.concatenate(
            [(qkt[c, h] * decay[c, h]).astype(BF16) for h in range(n_h)], axis=1)


def _gdn_scan_kernel(u_ref, w_ref, qe_ref, kd_ref, attn_ref, egl_ref, z_ref, ng_ref, o_ref,
                     state_scr, *, bb, ts, cb):
    j = pl.program_id(1)
    c_len = GDN_CHUNK
    d = GDN_D
    n_c = ts // c_len
    chains = [(b_, h) for b_ in range(bb) for h in range(GDN_HEADS)]

    @pl.when(j == 0)
    def _():
        state_scr[...] = jnp.zeros_like(state_scr)

    def chunk(c, carry):
        r0 = pl.multiple_of(c * c_len, c_len)
        rows = pl.ds(r0, c_len)
        cg = j * n_c + c
        state = {p: state_scr[p[0], p[1]] for p in chains}
        state_b = {p: state[p].astype(BF16) for p in chains}
        w_s = {(b_, h): _dot(w_ref[b_, rows, h * d:(h + 1) * d], state_b[b_, h]) for b_, h in chains}
        q_s = {(b_, h): _dot(qe_ref[b_, rows, h * d:(h + 1) * d], state_b[b_, h]) for b_, h in chains}
        v_new = {(b_, h): (u_ref[b_, rows, h * d:(h + 1) * d] - w_s[b_, h]).astype(BF16)
                 for b_, h in chains}
        sub = lax.broadcasted_iota(jnp.int32, egl_ref.shape[2:], 0)
        chunk_decay = [jnp.sum(jnp.where(sub == cg % cb, egl_ref[b_, cg // cb], 0.0),
                               axis=0, keepdims=True) for b_ in range(bb)]
        for b_, h in chains:
            cols = slice(h * d, (h + 1) * d)
            o = q_s[b_, h] + _dot(attn_ref[b_, rows, h * c_len:(h + 1) * c_len], v_new[b_, h])
            state_scr[b_, h] = (state[b_, h] * chunk_decay[b_][:, cols]
                                + _tn_dot(kd_ref[b_, rows, cols], v_new[b_, h]))
            zc = z_ref[b_, rows, cols]
            o_ref[b_, rows, cols] = (_rms_scale(o) * ng_ref[...] * (zc * _sigmoid(zc))
                                     ).astype(o_ref.dtype)
        return carry

    lax.fori_loop(0, n_c, chunk, 0)


def _gated_deltanet(proj3, conv_w, a_log, dt_bias, norm_g):
    b, s, _ = proj3.shape
    scal = jnp.stack([a_log, dt_bias]).astype(F32)
    width = GDN_HEADS * GDN_D
    n_chunks = s // GDN_CHUNK
    cb = min(GDN_PREP_CHUNKS, n_chunks)
    ts = cb * GDN_CHUNK
    halo = 8
    egl_rows = -(-cb // 8) * 8

    def main(col0):
        return pl.BlockSpec((None, ts, width), lambda bi, i: (bi, i, col0 * LANE // width))

    def before(col0):
        return pl.BlockSpec((None, halo, width),
                            lambda bi, i: (bi, jnp.maximum(i * (ts // halo) - 1, 0),
                                           col0 * LANE // width))

    def out_block(w_):
        return pl.BlockSpec((None, ts, w_), lambda bi, i: (bi, i, 0))

    u, w, qe, kd, attn, egl = pl.pallas_call(
        functools.partial(_gdn_prep_kernel, cb=cb),
        grid=(b, n_chunks // cb),
        in_specs=[pl.BlockSpec(memory_space=pltpu.SMEM),
                  main(COL_GQ), before(COL_GQ), main(COL_GK), before(COL_GK),
                  main(COL_GV), before(COL_GV),
                  pl.BlockSpec((None, ts, LANE), lambda bi, i: (bi, i, COL_GAB)),
                  pl.BlockSpec(conv_w.shape, lambda bi, i: (0, 0))],
        out_specs=[out_block(width), out_block(width), out_block(width), out_block(width),
                   out_block(GDN_HEADS * GDN_CHUNK),
                   pl.BlockSpec((None, None, egl_rows, width), lambda bi, i: (bi, i, 0, 0))],
        out_shape=[jax.ShapeDtypeStruct((b, s, width), F32),
                   jax.ShapeDtypeStruct((b, s, width), BF16),
                   jax.ShapeDtypeStruct((b, s, width), BF16),
                   jax.ShapeDtypeStruct((b, s, width), BF16),
                   jax.ShapeDtypeStruct((b, s, GDN_HEADS * GDN_CHUNK), BF16),
                   jax.ShapeDtypeStruct((b, n_chunks // cb, egl_rows, width), F32)],
        compiler_params=_cparams(2),
        name="gdn_prep",
    )(scal, proj3, proj3, proj3, proj3, proj3, proj3, proj3, conv_w)

    bb = max(n for n in (1, 2, GDN_SCAN_BATCHES) if b % n == 0)
    t_scan = min(GDN_SCAN_TOKENS, s)

    def scan_block(w_):
        return pl.BlockSpec((bb, t_scan, w_), lambda bi, j: (bi, j, 0))

    return pl.pallas_call(
        functools.partial(_gdn_scan_kernel, bb=bb, ts=t_scan, cb=cb),
        grid=(b // bb, s // t_scan),
        in_specs=[scan_block(width), scan_block(width), scan_block(width), scan_block(width),
                  scan_block(GDN_HEADS * GDN_CHUNK),
                  pl.BlockSpec((bb, n_chunks // cb, egl_rows, width), lambda bi, j: (bi, 0, 0, 0)),
                  pl.BlockSpec((bb, t_scan, width), lambda bi, j: (bi, j, COL_GZ * LANE // width)),
                  pl.BlockSpec((1, GDN_D), lambda bi, j: (0, 0))],
        out_specs=scan_block(width),
        out_shape=jax.ShapeDtypeStruct((b, s, width), BF16),
        scratch_shapes=[pltpu.VMEM((bb, GDN_HEADS, GDN_D, GDN_D), F32)],
        compiler_params=_cparams(2),
        name="gdn_scan",
    )(u, w, qe, kd, attn, egl, proj3, norm_g.reshape(1, GDN_D))


def _nsa_stage_kv(kcmp_ref, vcmp_ref, vslc_ref, vwin_ref, wk_ref, wv_ref, pk_ref, pv_ref,
                  kc_scr, vct_scr, vst_scr, vwt_scr, *, seq):
    tk = NSA_TK
    nb = seq // CMP_STRIDE
    half = CMP_LEN // 2
    cmp_out = []
    for src, pos, w_ref in ((kcmp_ref, pk_ref, wk_ref), (vcmp_ref, pv_ref, wv_ref)):
        lo_parts, hi_parts = [], []
        for l in range(half):
            rows = src[pl.ds(l, nb, stride=CMP_STRIDE), :]
            lo_parts.append(rows + pos[l:l + 1, :])
            hi_parts.append(rows + pos[half + l:half + l + 1, :])
        a_lo = _dot3(jnp.concatenate(lo_parts, axis=1), w_ref[0])
        a_hi = _dot3(jnp.concatenate(hi_parts, axis=1), w_ref[1])
        a_hi = jnp.concatenate([a_hi[1:], jnp.zeros((1, LANE), F32)], axis=0)
        cmp_out.append(a_lo + a_hi)
    kc_scr[...] = cmp_out[0]
    vct_scr[...] = cmp_out[1].T.astype(BF16)

    def transpose_values(kb, carry):
        k0 = pl.multiple_of(kb * tk, tk)
        vst_scr[kb] = vslc_ref[pl.ds(k0, tk), :].T.astype(BF16)
        vwt_scr[kb] = vwin_ref[pl.ds(k0, tk), :].T.astype(BF16)
        return carry

    lax.fori_loop(0, seq // tk, transpose_values, 0)


def _nsa_kernel(q_ref, kcmp_ref, vcmp_ref, kslc_ref, vslc_ref, kwin_ref, vwin_ref, gate_ref,
                wk_ref, wv_ref, pk_ref, pv_ref, o_ref,
                kc_scr, vct_scr, vst_scr, vwt_scr, selk_scr, *, seq, tq, n_sub):
    step = pl.program_id(1)

    @pl.when(step == 0)
    def _():
        _nsa_stage_kv(kcmp_ref, vcmp_ref, vslc_ref, vwin_ref, wk_ref, wv_ref, pk_ref, pv_ref,
                      kc_scr, vct_scr, vst_scr, vwt_scr, seq=seq)

    for sub in range(n_sub):
        rows = pl.ds(sub * tq, tq)
        _nsa_query_block(step * n_sub + sub, q_ref.at[rows], kslc_ref, kwin_ref, gate_ref.at[rows],
                         o_ref.at[rows], kc_scr, vct_scr, vst_scr, vwt_scr, selk_scr,
                         seq=seq, tq=tq)


def _nsa_query_block(qi, q_ref, kslc_ref, kwin_ref, gate_ref, o_ref,
                     kc_scr, vct_scr, vst_scr, vwt_scr, selk_scr, *, seq, tq):
    tk = NSA_TK
    dh = NSA_DH
    hg = NSA_GROUP
    n_grp = NSA_KV_HEADS
    nq = hg * tq
    nb = seq // CMP_STRIDE
    n_sel = seq // SEL_LEN
    n_kb = seq // tk
    top_k = min(SEL_TOPK, n_sel)
    slc_step = min(NSA_SLC_BLOCKS, n_kb)
    n_win = WINDOW // tk + 1

    lane = lax.broadcasted_iota(jnp.int32, (tq, LANE), 1)
    qg = []
    for g in range(n_grp):
        parts = []
        for i in range(hg):
            hd = g * hg + i
            pair = q_ref[:, (hd // 2) * LANE:(hd // 2 + 1) * LANE] * (dh ** -0.5)
            if hd % 2 != g:
                pair = pltpu.roll(pair, dh, axis=1)
            parts.append(jnp.where((lane >= g * dh) & (lane < (g + 1) * dh), pair, 0.0))
        qg.append(jnp.concatenate(parts, axis=0))

    t_row = qi * tq + lax.broadcasted_iota(jnp.int32, (1, tq), 1)
    t_row4 = jnp.concatenate([t_row] * hg, axis=1)
    key_iota = lax.broadcasted_iota(jnp.int32, (tk, 1), 0)

    j_sel = lax.broadcasted_iota(jnp.int32, (n_sel, tq), 0)
    cur = lax.shift_right_logical(t_row, SEL_LEN.bit_length() - 1)
    sel_valid = j_sel <= cur
    forced = (j_sel == 0) | (j_sel == cur) | (j_sel == cur - 1)
    ov_j = lax.broadcasted_iota(jnp.int32, (n_sel, nb), 0) * SEL_LEN
    ov_n = lax.broadcasted_iota(jnp.int32, (n_sel, nb), 1) * CMP_STRIDE
    overlap = jnp.where((ov_n < ov_j + SEL_LEN) & (ov_n + CMP_LEN > ov_j), 1.0, 0.0).astype(BF16)
    n_idx = lax.broadcasted_iota(jnp.int32, (nb, 1), 0)
    cmp_valid = (n_idx * CMP_STRIDE + CMP_LEN - 1 <= t_row4) & (n_idx < nb - 1)

    def mask_heads(ok, s):
        return jnp.where(jnp.concatenate([ok] * hg, axis=1), s, MASK_NEG)

    groups = range(n_grp)
    kc_hi, kc_lo = _split_bf16(kc_scr[...])
    q_split = [_split_bf16(qg[g]) for g in groups]
    q_b = [q_split[g][0] for g in groups]

    win_blocks = []
    for w in range(n_win):
        kb_int = qi - w
        kb = jnp.maximum(kb_int, 0)
        kpos = kb_int * tk + key_iota
        ok = (kpos <= t_row) & (kpos > t_row - WINDOW) & (kpos >= 0)
        k_blk = kwin_ref[pl.ds(pl.multiple_of(kb * tk, tk), tk), :].astype(BF16)
        win_blocks.append((kb, k_blk, ok))

    s_cmp = [_nt_dot(kc_hi, q_split[g][0]) + _nt_dot(kc_lo, q_split[g][0])
             + _nt_dot(kc_hi, q_split[g][1]) for g in groups]
    s_win = [jnp.concatenate([mask_heads(ok, _nt_dot(k_blk, q_b[g]))
                              for (_, k_blk, ok) in win_blocks], axis=0) for g in groups]

    p_cmp, p_win, l_win = [], [], []
    for g in groups:
        s_m = jnp.where(cmp_valid, s_cmp[g], MASK_NEG)
        e = jnp.where(cmp_valid, jnp.exp(s_m - jnp.max(s_m, axis=0, keepdims=True)), 0.0)
        den = jnp.sum(e, axis=0, keepdims=True)
        p_cmp.append(jnp.where(den > 0.0, e / jnp.where(den > 0.0, den, 1.0), 0.0))
    for g in groups:
        p = jnp.exp(s_win[g] - jnp.max(s_win[g], axis=0, keepdims=True))
        l_win.append(jnp.sum(p, axis=0, keepdims=True))
        p_win.append(p.astype(BF16))

    o_cmp = [_dot(vct_scr[g * dh:(g + 1) * dh, :], p_cmp[g].astype(BF16)) for g in groups]
    imp = []
    for g in groups:
        p_sum = p_cmp[g][:, 0:tq]
        for i in range(1, hg):
            p_sum = p_sum + p_cmp[g][:, i * tq:(i + 1) * tq]
        ps_hi, ps_lo = _split_bf16(p_sum)
        imp.append(_dot(overlap, ps_hi) + _dot(overlap, ps_lo))
    o_win = []
    for g in groups:
        v_t = jnp.concatenate([vwt_scr[kb, g * dh:(g + 1) * dh, :] for (kb, _, _) in win_blocks],
                              axis=1)
        o_win.append(_dot(v_t, p_win[g]) / l_win[g])

    for g in groups:
        score = jnp.where(sel_valid, imp[g] + jnp.where(forced, FORCE_BONUS, 0.0), MASK_NEG)
        beaten = jnp.zeros((n_sel, tq), F32)
        for jp in range(n_sel):
            other = score[jp:jp + 1, :]
            ge = jnp.where(other >= score, 1.0, 0.0)
            gt = jnp.where(other > score, 1.0, 0.0)
            beaten = beaten + jnp.where(j_sel > jp, ge, gt)
        picked = jnp.where(beaten < top_k, 1.0, 0.0)
        for j in range(n_sel):
            selk_scr[g, j * SEL_LEN:(j + 1) * SEL_LEN, :] = jnp.broadcast_to(
                picked[j:j + 1, :], (SEL_LEN, tq))

    def slc_blocks(n_tiles):
        def run(kb0, carry):
            k_blocks = [kslc_ref[pl.ds(pl.multiple_of((kb0 + u) * tk, tk), tk), :].astype(BF16)
                        for u in range(n_tiles)]
            tiles = {}
            for g in range(n_grp):
                for u in range(n_tiles):
                    k0 = pl.multiple_of((kb0 + u) * tk, tk)
                    ok = (selk_scr[g, pl.ds(k0, tk), :] > 0.5) & (k0 + key_iota <= t_row)
                    tiles[g, u] = mask_heads(ok, _nt_dot(k_blocks[u], q_b[g]))
            new = []
            for g in range(n_grp):
                m, l, acc = carry[g]
                for u in range(n_tiles):
                    s = tiles[g, u]
                    m_new = jnp.maximum(m, jnp.max(s, axis=0, keepdims=True))
                    alpha = jnp.exp(m - m_new)
                    p = jnp.exp(s - m_new)
                    l = alpha * l + jnp.sum(p, axis=0, keepdims=True)
                    acc = alpha * acc + _dot(vst_scr[kb0 + u, g * dh:(g + 1) * dh, :],
                                             p.astype(BF16))
                    m = m_new
                new.append((m, l, acc))
            return tuple(new)
        return run

    init = tuple((jnp.full((1, nq), MASK_NEG, F32), jnp.zeros((1, nq), F32),
                  jnp.zeros((dh, nq), F32)) for _ in range(n_grp))
    n_need = (qi + 1) * (tq // tk)
    full_body = slc_blocks(slc_step)
    slc = lax.fori_loop(0, n_need // slc_step, lambda it, c: full_body(it * slc_step, c), init)
    done = (n_need // slc_step) * slc_step
    arm = slc_step // 2
    while arm >= 1:
        take = (n_need & arm) != 0
        slc = lax.cond(take, functools.partial(slc_blocks(arm), done), lambda c: c, slc)
        done = done + jnp.where(take, arm, 0)
        arm //= 2
    o_slc = [acc / l for (_, l, acc) in slc]

    gates = _sigmoid(gate_ref[...].T)
    outs = []
    for g in range(n_grp):
        for i in range(hg):
            hd = g * hg + i
            cols = slice(i * tq, (i + 1) * tq)
            outs.append(gates[3 * hd:3 * hd + 1, :] * o_cmp[g][:, cols]
                        + gates[3 * hd + 1:3 * hd + 2, :] * o_slc[g][:, cols]
                        + gates[3 * hd + 2:3 * hd + 3, :] * o_win[g][:, cols])
    o_ref[...] = jnp.concatenate(outs, axis=0).T.astype(o_ref.dtype)


def _cmp_weight(w):
    half = CMP_LEN // 2
    w = w.reshape(2, half, NSA_DH, NSA_DH)
    z = jnp.zeros_like(w)
    big = jnp.stack([jnp.concatenate([w, z], axis=-1), jnp.concatenate([z, w], axis=-1)], axis=2)
    return big.reshape(2, half * NSA_KV_HEADS * NSA_DH, NSA_KV_HEADS * NSA_DH)


def _native_sparse_attention(proj3, cmp_wk, cmp_pk, cmp_wv, cmp_pv, *, tq):
    b, s, _ = proj3.shape
    nb = s // CMP_STRIDE
    n_kb = s // NSA_TK
    assert tq == NSA_TK and n_kb % min(NSA_SLC_BLOCKS, n_kb) == 0
    wk = _cmp_weight(cmp_wk)
    wv = _cmp_weight(cmp_wv)
    pk = jnp.tile(cmp_pk, (1, NSA_KV_HEADS))
    pv = jnp.tile(cmp_pv, (1, NSA_KV_HEADS))
    qw = NSA_HEADS * NSA_DH

    def seq_block(col):
        return pl.BlockSpec((None, s, LANE), lambda bi, i: (bi, 0, col))

    def whole(a):
        return pl.BlockSpec(a.shape, lambda bi, i: (0,) * a.ndim)

    n_sub = min(NSA_BLOCKS_PER_STEP, s // tq)
    t_step = n_sub * tq
    return pl.pallas_call(
        functools.partial(_nsa_kernel, seq=s, tq=tq, n_sub=n_sub),
        grid=(b, s // t_step),
        in_specs=[pl.BlockSpec((None, t_step, qw), lambda bi, i: (bi, i, COL_NQ * LANE // qw))]
        + [seq_block(COL_NKV + c) for c in range(6)]
        + [pl.BlockSpec((None, t_step, LANE), lambda bi, i: (bi, i, COL_NGATE)),
           whole(wk), whole(wv), whole(pk), whole(pv)],
        out_specs=pl.BlockSpec((None, t_step, qw), lambda bi, i: (bi, i, 0)),
        out_shape=jax.ShapeDtypeStruct((b, s, qw), BF16),
        scratch_shapes=[pltpu.VMEM((nb, LANE), F32),
                        pltpu.VMEM((LANE, nb), BF16),
                        pltpu.VMEM((n_kb, LANE, NSA_TK), BF16),
                        pltpu.VMEM((n_kb, LANE, NSA_TK), BF16),
                        pltpu.VMEM((NSA_KV_HEADS, s, tq), F32)],
        compiler_params=_cparams(2),
        name="native_sparse_attention",
    )(proj3, proj3, proj3, proj3, proj3, proj3, proj3, proj3, wk, wv, pk, pv)


def _pad_cols(w, width):
    return jnp.pad(w, ((0, 0),) * (w.ndim - 1) + ((0, width - w.shape[-1]),))


def _proj_weight(w_in):
    gqk_v_z = 4 * GDN_HEADS * GDN_D
    n_ab = 2 * GDN_HEADS
    n_nsa = (NSA_HEADS + 6 * NSA_KV_HEADS) * NSA_DH
    n_gate = 3 * NSA_HEADS
    o_ab = gqk_v_z
    o_nsa = o_ab + n_ab
    o_gate = o_nsa + n_nsa
    o_rest = o_gate + n_gate
    w = jnp.concatenate([w_in[..., :o_ab], w_in[..., o_nsa:o_gate],
                         _pad_cols(w_in[..., o_ab:o_nsa], LANE),
                         _pad_cols(w_in[..., o_gate:o_rest], LANE),
                         w_in[..., o_rest:]], axis=-1)
    assert w.shape[-1] == N_PROJ, w.shape
    return w.astype(BF16)


def _layer(x, p, big, layer, *, tiles):
    b, s, d = x.shape
    x2 = x.reshape(b * s, d)
    proj = _norm_matmul(x2, p["g_mix_pre"], big["w_proj"], layer, relu2=False, out_dtype=F32,
                        tm=tiles["tm"], tn=tiles["tn"])
    proj3 = proj.reshape(b, s, N_PROJ)
    ya = _gated_deltanet(proj3, p["gdn_conv_w"], p["gdn_a_log"], p["gdn_dt_bias"], p["gdn_norm_g"])
    yb = _native_sparse_attention(proj3, p["nsa_cmp_wk"], p["nsa_cmp_pk"], p["nsa_cmp_wv"],
                                  p["nsa_cmp_pv"], tq=tiles["tq"])
    yc = _sb_attention(proj3, tq=tiles["tq_sb"])
    x2 = _merge_out(ya.reshape(b * s, -1), yb.reshape(b * s, -1), yc.reshape(b * s, -1), proj, x2,
                    big["w_br_gdn"], big["w_br_nsa"], big["w_br_sb"], big["w_out"], layer,
                    p["g_mix_post"], tm=tiles["tm_out"])
    act = _norm_matmul(x2, p["g_ff_pre"], big["w_ff1"], layer, relu2=True, out_dtype=BF16,
                       tm=tiles["tm"], tn=tiles["tn"])
    x2 = _matmul_norm_res(act, big["w_ff2"], layer, x2, p["g_ff_post"], tm=tiles["tm_out"])
    return x2.reshape(b, s, d)


def _tiles(b, s):
    t = b * s
    return {"tm": min(1024, t), "tn": 2048, "tm_out": min(512, t), "tq": min(128, s),
            "tq_sb": min(256, s)}


def kernel(x, g_mix_pre, g_mix_post, g_ff_pre, g_ff_post, w_in, gdn_conv_w, gdn_a_log, gdn_dt_bias,
           gdn_norm_g, nsa_cmp_wk, nsa_cmp_pk, nsa_cmp_wv, nsa_cmp_pv, w_br_gdn, w_br_nsa, w_br_sb,
           w_out, w_ff1, w_ff2):
    small = dict(g_mix_pre=g_mix_pre, g_mix_post=g_mix_post, g_ff_pre=g_ff_pre, g_ff_post=g_ff_post,
                 gdn_conv_w=gdn_conv_w, gdn_a_log=gdn_a_log, gdn_dt_bias=gdn_dt_bias,
                 gdn_norm_g=gdn_norm_g, nsa_cmp_wk=nsa_cmp_wk, nsa_cmp_pk=nsa_cmp_pk,
                 nsa_cmp_wv=nsa_cmp_wv, nsa_cmp_pv=nsa_cmp_pv)
    big = dict(w_proj=_proj_weight(w_in), w_br_gdn=w_br_gdn.astype(BF16),
               w_br_nsa=w_br_nsa.astype(BF16), w_br_sb=w_br_sb.astype(BF16),
               w_out=w_out.astype(BF16), w_ff1=w_ff1.astype(BF16), w_ff2=w_ff2.astype(BF16))
    tiles = _tiles(x.shape[0], x.shape[1])
    for layer in range(w_in.shape[0]):
        x = _layer(x, {k: v[layer] for k, v in small.items()}, big, layer, tiles=tiles)
    return x
```

```python
import functools

import jax
import jax.numpy as jnp
from jax import lax
from jax.experimental import pallas as pl
from jax.experimental.pallas import tpu as pltpu

F32 = jnp.float32
BF16 = jnp.bfloat16

EPS = 1e-6
MASK_NEG = -1e30
LOG2E = 1.4426950408889634

GDN_HEADS = 4
GDN_D = 128
GDN_CONV = 4
GDN_CHUNK = 64
GDN_PREP_CHUNKS = 8
GDN_SCAN_TOKENS = 512
GDN_SCAN_BATCHES = 4

NSA_HEADS = 8
NSA_KV_HEADS = 2
NSA_GROUP = NSA_HEADS // NSA_KV_HEADS
NSA_DH = 64
CMP_LEN = 32
CMP_STRIDE = 16
SEL_LEN = 64
SEL_TOPK = 8
WINDOW = 512
FORCE_BONUS = 1e3
NSA_TK = 128
NSA_SLC_BLOCKS = 8
NSA_BLOCKS_PER_STEP = 2

SB_HEADS = 8
SB_DH = 64
SB_TK = 128
SB_DEAD_LOG2 = -160.0
SB_BLOCKS_PER_STEP = 2

LANE = 128

COL_GQ, COL_GK, COL_GV, COL_GZ = 0, 4, 8, 12
COL_NQ, COL_NKV, COL_GAB, COL_NGATE = 16, 20, 26, 27
COL_SQ, COL_SK, COL_SV = 28, 32, 36
COL_MERGE = 40
N_PROJ = 64 * LANE

VMEM_LIMIT = 48 * 1024 * 1024


def _cparams(n_axes):
    return pltpu.CompilerParams(dimension_semantics=("arbitrary",) * n_axes,
                                vmem_limit_bytes=VMEM_LIMIT)


def _nt_dot(a, b):
    return lax.dot_general(a, b, (((1,), (1,)), ((), ())), preferred_element_type=F32)


def _tn_dot(a, b):
    return lax.dot_general(a, b, (((0,), (0,)), ((), ())), preferred_element_type=F32)


def _dot(a, b):
    return jnp.dot(a, b, preferred_element_type=F32)


def _rms_scale(y):
    return y * lax.rsqrt(jnp.mean(y * y, axis=-1, keepdims=True) + EPS)


def _softplus(z):
    return jnp.maximum(z, 0.0) + jnp.log1p(jnp.exp(-jnp.abs(z)))


def _sigmoid(z):
    return jax.nn.sigmoid(z)


def _split_bf16(x):
    hi = x.astype(BF16)
    return hi, (x - hi.astype(F32)).astype(BF16)


def _dot3(a, b):
    a_hi, a_lo = _split_bf16(a)
    b_hi, b_lo = _split_bf16(b)
    return _dot(a_hi, b_hi) + _dot(a_lo, b_hi) + _dot(a_hi, b_lo)


def _norm_matmul_kernel(x_ref, g_ref, w_ref, o_ref, h_scr, *, relu2):
    @pl.when(pl.program_id(1) == 0)
    def _():
        h_scr[...] = (_rms_scale(x_ref[...]) * g_ref[...]).astype(BF16)

    y = _dot(h_scr[...], w_ref[...])
    if relu2:
        y = jnp.square(jnp.maximum(y, 0.0))
    o_ref[...] = y.astype(o_ref.dtype)


def _norm_matmul(x, g, w, layer, *, relu2, out_dtype, tm, tn):
    t, d = x.shape
    n = w.shape[2]
    return pl.pallas_call(
        functools.partial(_norm_matmul_kernel, relu2=relu2),
        grid=(t // tm, n // tn),
        in_specs=[pl.BlockSpec((tm, d), lambda i, j: (i, 0)),
                  pl.BlockSpec((1, d), lambda i, j: (0, 0)),
                  pl.BlockSpec((None, d, tn), lambda i, j: (layer, 0, j))],
        out_specs=pl.BlockSpec((tm, tn), lambda i, j: (i, j)),
        out_shape=jax.ShapeDtypeStruct((t, n), out_dtype),
        scratch_shapes=[pltpu.VMEM((tm, d), BF16)],
        compiler_params=_cparams(2),
        name="norm_matmul_relu2" if relu2 else "norm_matmul",
    )(x, g.reshape(1, d), w)


def _matmul_norm_res_kernel(a_ref, w_ref, x_ref, g_ref, o_ref):
    y = _dot(a_ref[...], w_ref[...])
    o_ref[...] = x_ref[...] + _rms_scale(y) * g_ref[...]


def _matmul_norm_res(a, w, layer, x, g, *, tm):
    t, k = a.shape
    d = w.shape[2]
    return pl.pallas_call(
        _matmul_norm_res_kernel,
        grid=(t // tm,),
        in_specs=[pl.BlockSpec((tm, k), lambda i: (i, 0)),
                  pl.BlockSpec((None, k, d), lambda i: (layer, 0, 0)),
                  pl.BlockSpec((tm, d), lambda i: (i, 0)),
                  pl.BlockSpec((1, d), lambda i: (0, 0))],
        out_specs=pl.BlockSpec((tm, d), lambda i: (i, 0)),
        out_shape=jax.ShapeDtypeStruct((t, d), F32),
        compiler_params=_cparams(1),
        name="matmul_norm_res",
    )(a, w, x, g.reshape(1, d))


def _merge_out_kernel(ya_ref, yb_ref, yc_ref, m0_ref, m1_ref, m2_ref, x_ref,
                      wa_ref, wb_ref, wc_ref, wo_ref, g_ref, o_ref):
    merged = (_sigmoid(m0_ref[...]) * _dot(ya_ref[...], wa_ref[...])
              + _sigmoid(m1_ref[...]) * _dot(yb_ref[...], wb_ref[...])
              + _sigmoid(m2_ref[...]) * _dot(yc_ref[...], wc_ref[...]))
    y = _dot(merged.astype(BF16), wo_ref[...])
    o_ref[...] = x_ref[...] + _rms_scale(y) * g_ref[...]


def _merge_out(ya, yb, yc, proj, x, wa, wb, wc, wo, layer, g, *, tm):
    t, d = x.shape
    w_in = ya.shape[1]
    mcol = COL_MERGE * LANE // d

    def row(i):
        return (i, 0)

    def const(i):
        return (0, 0)

    def weight(i):
        return (layer, 0, 0)

    return pl.pallas_call(
        _merge_out_kernel,
        grid=(t // tm,),
        in_specs=[pl.BlockSpec((tm, w_in), row),
                  pl.BlockSpec((tm, w_in), row),
                  pl.BlockSpec((tm, w_in), row),
                  pl.BlockSpec((tm, d), lambda i: (i, mcol)),
                  pl.BlockSpec((tm, d), lambda i: (i, mcol + 1)),
                  pl.BlockSpec((tm, d), lambda i: (i, mcol + 2)),
                  pl.BlockSpec((tm, d), row),
                  pl.BlockSpec((None, w_in, d), weight),
                  pl.BlockSpec((None, w_in, d), weight),
                  pl.BlockSpec((None, w_in, d), weight),
                  pl.BlockSpec((None, d, d), weight),
                  pl.BlockSpec((1, d), const)],
        out_specs=pl.BlockSpec((tm, d), row),
        out_shape=jax.ShapeDtypeStruct((t, d), F32),
        compiler_params=_cparams(1),
        name="merge_out",
    )(ya, yb, yc, proj, proj, proj, x, wa, wb, wc, wo, g.reshape(1, d))


def _sb_kernel(q_ref, k_ref, v_ref, o_ref, run_scr, acc_scr, *, tq, n_sub):
    step = pl.program_id(1)
    for sub in range(n_sub):
        rows = pl.ds(sub * tq, tq)
        _sb_query_block(step * n_sub + sub, q_ref.at[rows], k_ref, v_ref, o_ref.at[rows],
                        run_scr, acc_scr, tq=tq)


def _sb_query_block(qi, q_ref, k_ref, v_ref, o_ref, run_scr, acc_scr, *, tq):
    tk = SB_TK
    n_diag = tq // tk
    dh = SB_DH
    n_pair = SB_HEADS * dh // LANE
    first_head = lax.broadcasted_iota(jnp.int32, (tk, LANE), 1) < dh
    r2 = lax.broadcasted_iota(jnp.int32, (2 * tk, 2 * tk), 0) & (tk - 1)
    c2 = lax.broadcasted_iota(jnp.int32, (2 * tk, 2 * tk), 1)
    later_ones = jnp.where((r2 > c2) | (c2 >= tk), 1.0, 0.0).astype(BF16)

    q_pairs = [(q_ref[:, p * LANE:(p + 1) * LANE] * (dh ** -0.5 * LOG2E)).astype(BF16)
               for p in range(n_pair)]

    run_scr[...] = jnp.zeros_like(run_scr)
    acc_scr[...] = jnp.zeros_like(acc_scr)

    def split_heads(x):
        zero = jnp.zeros_like(x)
        return jnp.concatenate([jnp.where(first_head, x, zero), jnp.where(first_head, zero, x)],
                               axis=0)

    def block(j, diag_offset):
        k0 = pl.multiple_of(j * tk, tk)
        pairs = range(n_pair)
        diagonal = diag_offset is not None
        rs = slice(diag_offset, tq) if diagonal else slice(0, tq)
        if diagonal:
            shape = (tq - diag_offset, tk)
            causal = (lax.broadcasted_iota(jnp.int32, shape, 1)
                      < lax.broadcasted_iota(jnp.int32, shape, 0))
            causal2 = jnp.concatenate([causal, causal], axis=1)
        z = [_nt_dot(q_pairs[p][rs],
                     split_heads(k_ref[pl.ds(k0, tk), p * LANE:(p + 1) * LANE].astype(BF16)))
             for p in pairs]
        log_sig, hi, lo = [], [], []
        for p in pairs:
            log_sig.append(jnp.minimum(z[p], 0.0) - jnp.log2(1.0 + jnp.exp2(-jnp.abs(z[p]))))
            log_keep = log_sig[p] - z[p]
            if diagonal:
                log_keep = jnp.where(causal2, log_keep, 0.0)
            hi.append(log_keep.astype(BF16))
            lo.append((log_keep - hi[p].astype(F32)).astype(BF16))
        sums = [[_dot(jnp.concatenate([hi[p][:, hh * tk:(hh + 1) * tk],
                                       lo[p][:, hh * tk:(hh + 1) * tk]], axis=1), later_ones)
                 for hh in range(2)] for p in pairs]
        for p in pairs:
            a = []
            for hh in range(2):
                h = 2 * p + hh
                run = run_scr[h, rs, :]
                a_h = jnp.exp2(log_sig[p][:, hh * tk:(hh + 1) * tk] + sums[p][hh][:, :tk] + run)
                if diagonal:
                    a_h = jnp.where(causal, a_h, 0.0)
                run_scr[h, rs, :] = run + sums[p][hh][:, tk:]
                a.append(a_h.astype(BF16))
            vp = v_ref[pl.ds(k0, tk), p * LANE:(p + 1) * LANE].astype(BF16)
            acc_scr[p, rs, :] += _dot(jnp.concatenate(a, axis=1), split_heads(vp))

    for d in reversed(range(n_diag)):
        block(qi * n_diag + d, d * tk)

    def still_live():
        top = run_scr[0]
        for h in range(1, SB_HEADS):
            top = jnp.maximum(top, run_scr[h])
        return jnp.max(top) > SB_DEAD_LOG2

    def body(carry):
        it, _ = carry
        block(qi * n_diag - 1 - it, None)
        return it + 1, still_live()

    lax.while_loop(lambda c: (c[0] < qi * n_diag) & c[1], body, (jnp.int32(0), still_live()))
    o_ref[...] = jnp.concatenate([acc_scr[p] for p in range(n_pair)], axis=-1).astype(o_ref.dtype)


def _sb_attention(proj3, *, tq):
    b, s, _ = proj3.shape
    width = SB_HEADS * SB_DH
    n_pair = width // LANE
    n_sub = min(SB_BLOCKS_PER_STEP, s // tq)
    t_step = n_sub * tq
    return pl.pallas_call(
        functools.partial(_sb_kernel, tq=tq, n_sub=n_sub),
        grid=(b, s // t_step),
        in_specs=[pl.BlockSpec((None, t_step, width), lambda bi, i: (bi, i, COL_SQ * LANE // width)),
                  pl.BlockSpec((None, s, width), lambda bi, i: (bi, 0, COL_SK * LANE // width)),
                  pl.BlockSpec((None, s, width), lambda bi, i: (bi, 0, COL_SV * LANE // width))],
        out_specs=pl.BlockSpec((None, t_step, width), lambda bi, i: (bi, i, 0)),
        out_shape=jax.ShapeDtypeStruct((b, s, width), BF16),
        scratch_shapes=[pltpu.VMEM((SB_HEADS, tq, LANE), F32), pltpu.VMEM((n_pair, tq, LANE), F32)],
        compiler_params=_cparams(2),
        name="sb_attention",
    )(proj3, proj3, proj3)


def _gdn_prep_kernel(scal_ref, q_ref, qh_ref, k_ref, kh_ref, v_ref, vh_ref, ab_ref, cw_ref,
                     u_ref, w_ref, qe_ref, kd_ref, attn_ref, egl_ref, *, cb):
    i = pl.program_id(1)
    c_len = GDN_CHUNK
    d = GDN_D
    n_h = GDN_HEADS
    ts = cb * c_len
    halo = qh_ref.shape[0]
    width = n_h * d

    def conv_silu(x_ref, halo_ref, w):
        ext = jnp.concatenate([halo_ref[...] * jnp.where(i > 0, 1.0, 0.0), x_ref[...]], axis=0)
        y = jnp.zeros((ts, width), F32)
        for tap in range(GDN_CONV):
            off = halo - (GDN_CONV - 1) + tap
            y = y + w[tap:tap + 1, :] * ext[off:off + ts]
        return y * _sigmoid(y)

    qc = conv_silu(q_ref, qh_ref, cw_ref[:, 0:width])
    kc = conv_silu(k_ref, kh_ref, cw_ref[:, width:2 * width])
    vc = conv_silu(v_ref, vh_ref, cw_ref[:, 2 * width:3 * width])

    ab = ab_ref[...]
    lane = lax.broadcasted_iota(jnp.int32, (1, LANE), 1)
    a_log = jnp.zeros((1, LANE), F32)
    dt_bias = jnp.zeros((1, LANE), F32)
    for h in range(n_h):
        a_log = jnp.where(lane == h, scal_ref[0, h], a_log)
        dt_bias = jnp.where(lane == h, scal_ref[1, h], dt_bias)
    g_cum = -jnp.exp(a_log) * _softplus(ab + dt_bias)
    row_in_chunk = lax.broadcasted_iota(jnp.int32, (ts, LANE), 0) & (c_len - 1)
    shift = 1
    while shift < c_len:
        g_cum = g_cum + jnp.where(row_in_chunk >= shift, pltpu.roll(g_cum, shift, axis=0), 0.0)
        shift *= 2
    beta_all = _sigmoid(ab)

    row = lax.broadcasted_iota(jnp.int32, (c_len, c_len), 0)
    col = lax.broadcasted_iota(jnp.int32, (c_len, c_len), 1)
    incl = row >= col
    strict = row > col
    eye = jnp.where(row == col, 1.0, 0.0)

    problems = [(c, h) for c in range(cb) for h in range(n_h)]
    qn, kn, k_beta, v_beta, g_col, decay = {}, {}, {}, {}, {}, {}
    for c, h in problems:
        rows = slice(c * c_len, (c + 1) * c_len)
        cols = slice(h * d, (h + 1) * d)
        qh, kh = qc[rows, cols], kc[rows, cols]
        qn[c, h] = qh * lax.rsqrt(jnp.sum(qh * qh, axis=-1, keepdims=True) + EPS) * (d ** -0.5)
        kn[c, h] = kh * lax.rsqrt(jnp.sum(kh * kh, axis=-1, keepdims=True) + EPS)
        beta = beta_all[rows, n_h + h:n_h + h + 1]
        k_beta[c, h] = kn[c, h] * beta
        v_beta[c, h] = (vc[rows, cols] * beta).astype(BF16)
        g_col[c, h] = g_cum[rows, h:h + 1]
        g_sq = jnp.broadcast_to(g_col[c, h], (c_len, c_len))
        decay[c, h] = jnp.where(incl, jnp.exp(jnp.where(incl, g_sq - g_sq.T, 0.0)), 0.0)

    kn_b = {p: kn[p].astype(BF16) for p in problems}
    kkt = {p: _nt_dot(k_beta[p].astype(BF16), kn_b[p]) for p in problems}
    qkt = {p: _nt_dot(qn[p].astype(BF16), kn_b[p]) for p in problems}

    a_mat = {p: jnp.where(strict, kkt[p] * decay[p], 0.0) for p in problems}
    left = lax.broadcasted_iota(jnp.int32, (c_len, 2 * c_len), 1) < c_len
    pair = {p: jnp.concatenate([-a_mat[p], eye], axis=1) for p in problems}
    for _ in range(max(c_len - 1, 1).bit_length()):
        y = {p: _dot(pair[p][:, :c_len].astype(BF16), pair[p].astype(BF16)) for p in problems}
        pair = {p: jnp.where(left, y[p], pair[p] + y[p]) for p in problems}
    t_inv = {p: pair[p][:, c_len:] for p in problems}
    resid = {}
    for p in problems:
        m_hi, m_lo = _split_bf16(eye + a_mat[p])
        t_hi, t_lo = _split_bf16(t_inv[p])
        resid[p] = (eye - (_dot(m_hi, t_hi) + _dot(m_lo, t_hi) + _dot(m_hi, t_lo)), t_hi)
    t_b = {p: (t_inv[p] + _dot(resid[p][1], resid[p][0].astype(BF16))).astype(BF16)
           for p in problems}

    for c, h in problems:
        p = (c, h)
        rows = slice(c * c_len, (c + 1) * c_len)
        cols = slice(h * d, (h + 1) * d)
        exp_g = jnp.exp(g_col[p])
        g_last = g_col[p][c_len - 1:c_len, :]
        u_ref[rows, cols] = _dot(t_b[p], v_beta[p])
        w_ref[rows, cols] = _dot(t_b[p], (k_beta[p] * exp_g).astype(BF16)).astype(BF16)
        qe_ref[rows, cols] = (qn[p] * exp_g).astype(BF16)
        kd_ref[rows, cols] = (kn[p] * jnp.exp(g_last - g_col[p])).astype(BF16)
    egl_rows = [jnp.concatenate([jnp.broadcast_to(jnp.exp(g_col[c, h][c_len - 1:c_len, :]), (1, d))
                                 for h in range(n_h)], axis=1) for c in range(cb)]
    if egl_ref.shape[0] > cb:
        egl_rows.append(jnp.zeros((egl_ref.shape[0] - cb, width), F32))
    egl_ref[...] = jnp.concatenate(egl_rows, axis=0)
    for c in range(cb):
        rows = slice(c * c_len, (c + 1) * c_len)
        attn_ref[rows, :] = jnp.concatenate(
            [(qkt[c, h] * decay[c, h]).astype(BF16) for h in range(n_h)], axis=1)


def _gdn_scan_kernel(u_ref, w_ref, qe_ref, kd_ref, attn_ref, egl_ref, z_ref, ng_ref, o_ref,
                     state_scr, *, bb, ts, cb):
    j = pl.program_id(1)
    c_len = GDN_CHUNK
    d = GDN_D
    n_c = ts // c_len
    chains = [(b_, h) for b_ in range(bb) for h in range(GDN_HEADS)]

    @pl.when(j == 0)
    def _():
        state_scr[...] = jnp.zeros_like(state_scr)

    def chunk(c, carry):
        r0 = pl.multiple_of(c * c_len, c_len)
        rows = pl.ds(r0, c_len)
        cg = j * n_c + c
        state = {p: state_scr[p[0], p[1]] for p in chains}
        state_b = {p: state[p].astype(BF16) for p in chains}
        w_s = {(b_, h): _dot(w_ref[b_, rows, h * d:(h + 1) * d], state_b[b_, h]) for b_, h in chains}
        q_s = {(b_, h): _dot(qe_ref[b_, rows, h * d:(h + 1) * d], state_b[b_, h]) for b_, h in chains}
        v_new = {(b_, h): (u_ref[b_, rows, h * d:(h + 1) * d] - w_s[b_, h]).astype(BF16)
                 for b_, h in chains}
        sub = lax.broadcasted_iota(jnp.int32, egl_ref.shape[2:], 0)
        chunk_decay = [jnp.sum(jnp.where(sub == cg % cb, egl_ref[b_, cg // cb], 0.0),
                               axis=0, keepdims=True) for b_ in range(bb)]
        for b_, h in chains:
            cols = slice(h * d, (h + 1) * d)
            o = q_s[b_, h] + _dot(attn_ref[b_, rows, h * c_len:(h + 1) * c_len], v_new[b_, h])
            state_scr[b_, h] = (state[b_, h] * chunk_decay[b_][:, cols]
                                + _tn_dot(kd_ref[b_, rows, cols], v_new[b_, h]))
            zc = z_ref[b_, rows, cols]
            o_ref[b_, rows, cols] = (_rms_scale(o) * ng_ref[...] * (zc * _sigmoid(zc))
                                     ).astype(o_ref.dtype)
        return carry

    lax.fori_loop(0, n_c, chunk, 0)


def _gated_deltanet(proj3, conv_w, a_log, dt_bias, norm_g):
    b, s, _ = proj3.shape
    scal = jnp.stack([a_log, dt_bias]).astype(F32)
    width = GDN_HEADS * GDN_D
    n_chunks = s // GDN_CHUNK
    cb = min(GDN_PREP_CHUNKS, n_chunks)
    ts = cb * GDN_CHUNK
    halo = 8
    egl_rows = -(-cb // 8) * 8

    def main(col0):
        return pl.BlockSpec((None, ts, width), lambda bi, i: (bi, i, col0 * LANE // width))

    def before(col0):
        return pl.BlockSpec((None, halo, width),
                            lambda bi, i: (bi, jnp.maximum(i * (ts // halo) - 1, 0),
                                           col0 * LANE // width))

    def out_block(w_):
        return pl.BlockSpec((None, ts, w_), lambda bi, i: (bi, i, 0))

    u, w, qe, kd, attn, egl = pl.pallas_call(
        functools.partial(_gdn_prep_kernel, cb=cb),
        grid=(b, n_chunks // cb),
        in_specs=[pl.BlockSpec(memory_space=pltpu.SMEM),
                  main(COL_GQ), before(COL_GQ), main(COL_GK), before(COL_GK),
                  main(COL_GV), before(COL_GV),
                  pl.BlockSpec((None, ts, LANE), lambda bi, i: (bi, i, COL_GAB)),
                  pl.BlockSpec(conv_w.shape, lambda bi, i: (0, 0))],
        out_specs=[out_block(width), out_block(width), out_block(width), out_block(width),
                   out_block(GDN_HEADS * GDN_CHUNK),
                   pl.BlockSpec((None, None, egl_rows, width), lambda bi, i: (bi, i, 0, 0))],
        out_shape=[jax.ShapeDtypeStruct((b, s, width), F32),
                   jax.ShapeDtypeStruct((b, s, width), BF16),
                   jax.ShapeDtypeStruct((b, s, width), BF16),
                   jax.ShapeDtypeStruct((b, s, width), BF16),
                   jax.ShapeDtypeStruct((b, s, GDN_HEADS * GDN_CHUNK), BF16),
                   jax.ShapeDtypeStruct((b, n_chunks // cb, egl_rows, width), F32)],
        compiler_params=_cparams(2),
        name="gdn_prep",
    )(scal, proj3, proj3, proj3, proj3, proj3, proj3, proj3, conv_w)

    bb = max(n for n in (1, 2, GDN_SCAN_BATCHES) if b % n == 0)
    t_scan = min(GDN_SCAN_TOKENS, s)

    def scan_block(w_):
        return pl.BlockSpec((bb, t_scan, w_), lambda bi, j: (bi, j, 0))

    return pl.pallas_call(
        functools.partial(_gdn_scan_kernel, bb=bb, ts=t_scan, cb=cb),
        grid=(b // bb, s // t_scan),
        in_specs=[scan_block(width), scan_block(width), scan_block(width), scan_block(width),
                  scan_block(GDN_HEADS * GDN_CHUNK),
                  pl.BlockSpec((bb, n_chunks // cb, egl_rows, width), lambda bi, j: (bi, 0, 0, 0)),
                  pl.BlockSpec((bb, t_scan, width), lambda bi, j: (bi, j, COL_GZ * LANE // width)),
                  pl.BlockSpec((1, GDN_D), lambda bi, j: (0, 0))],
        out_specs=scan_block(width),
        out_shape=jax.ShapeDtypeStruct((b, s, width), BF16),
        scratch_shapes=[pltpu.VMEM((bb, GDN_HEADS, GDN_D, GDN_D), F32)],
        compiler_params=_cparams(2),
        name="gdn_scan",
    )(u, w, qe, kd, attn, egl, proj3, norm_g.reshape(1, GDN_D))


def _nsa_stage_kv(kcmp_ref, vcmp_ref, vslc_ref, vwin_ref, wk_ref, wv_ref, pk_ref, pv_ref,
                  kc_scr, vct_scr, vst_scr, vwt_scr, *, seq):
    tk = NSA_TK
    nb = seq // CMP_STRIDE
    half = CMP_LEN // 2
    cmp_out = []
    for src, pos, w_ref in ((kcmp_ref, pk_ref, wk_ref), (vcmp_ref, pv_ref, wv_ref)):
        lo_parts, hi_parts = [], []
        for l in range(half):
            rows = src[pl.ds(l, nb, stride=CMP_STRIDE), :]
            lo_parts.append(rows + pos[l:l + 1, :])
            hi_parts.append(rows + pos[half + l:half + l + 1, :])
        a_lo = _dot3(jnp.concatenate(lo_parts, axis=1), w_ref[0])
        a_hi = _dot3(jnp.concatenate(hi_parts, axis=1), w_ref[1])
        a_hi = jnp.concatenate([a_hi[1:], jnp.zeros((1, LANE), F32)], axis=0)
        cmp_out.append(a_lo + a_hi)
    kc_scr[...] = cmp_out[0]
    vct_scr[...] = cmp_out[1].T.astype(BF16)

    def transpose_values(kb, carry):
        k0 = pl.multiple_of(kb * tk, tk)
        vst_scr[kb] = vslc_ref[pl.ds(k0, tk), :].T.astype(BF16)
        vwt_scr[kb] = vwin_ref[pl.ds(k0, tk), :].T.astype(BF16)
        return carry

    lax.fori_loop(0, seq // tk, transpose_values, 0)


def _nsa_kernel(q_ref, kcmp_ref, vcmp_ref, kslc_ref, vslc_ref, kwin_ref, vwin_ref, gate_ref,
                wk_ref, wv_ref, pk_ref, pv_ref, o_ref,
                kc_scr, vct_scr, vst_scr, vwt_scr, selk_scr, *, seq, tq, n_sub):
    step = pl.program_id(1)

    @pl.when(step == 0)
    def _():
        _nsa_stage_kv(kcmp_ref, vcmp_ref, vslc_ref, vwin_ref, wk_ref, wv_ref, pk_ref, pv_ref,
                      kc_scr, vct_scr, vst_scr, vwt_scr, seq=seq)

    for sub in range(n_sub):
        rows = pl.ds(sub * tq, tq)
        _nsa_query_block(step * n_sub + sub, q_ref.at[rows], kslc_ref, kwin_ref, gate_ref.at[rows],
                         o_ref.at[rows], kc_scr, vct_scr, vst_scr, vwt_scr, selk_scr,
                         seq=seq, tq=tq)


def _nsa_query_block(qi, q_ref, kslc_ref, kwin_ref, gate_ref, o_ref,
                     kc_scr, vct_scr, vst_scr, vwt_scr, selk_scr, *, seq, tq):
    tk = NSA_TK
    dh = NSA_DH
    hg = NSA_GROUP
    n_grp = NSA_KV_HEADS
    nq = hg * tq
    nb = seq // CMP_STRIDE
    n_sel = seq // SEL_LEN
    n_kb = seq // tk
    top_k = min(SEL_TOPK, n_sel)
    slc_step = min(NSA_SLC_BLOCKS, n_kb)
    n_win = WINDOW // tk + 1

    lane = lax.broadcasted_iota(jnp.int32, (tq, LANE), 1)
    qg = []
    for g in range(n_grp):
        parts = []
        for i in range(hg):
            hd = g * hg + i
            pair = q_ref[:, (hd // 2) * LANE:(hd // 2 + 1) * LANE] * (dh ** -0.5)
            if hd % 2 != g:
                pair = pltpu.roll(pair, dh, axis=1)
            parts.append(jnp.where((lane >= g * dh) & (lane < (g + 1) * dh), pair, 0.0))
        qg.append(jnp.concatenate(parts, axis=0))

    t_row = qi * tq + lax.broadcasted_iota(jnp.int32, (1, tq), 1)
    t_row4 = jnp.concatenate([t_row] * hg, axis=1)
    key_iota = lax.broadcasted_iota(jnp.int32, (tk, 1), 0)

    j_sel = lax.broadcasted_iota(jnp.int32, (n_sel, tq), 0)
    cur = lax.shift_right_logical(t_row, SEL_LEN.bit_length() - 1)
    sel_valid = j_sel <= cur
    forced = (j_sel == 0) | (j_sel == cur) | (j_sel == cur - 1)
    ov_j = lax.broadcasted_iota(jnp.int32, (n_sel, nb), 0) * SEL_LEN
    ov_n = lax.broadcasted_iota(jnp.int32, (n_sel, nb), 1) * CMP_STRIDE
    overlap = jnp.where((ov_n < ov_j + SEL_LEN) & (ov_n + CMP_LEN > ov_j), 1.0, 0.0).astype(BF16)
    n_idx = lax.broadcasted_iota(jnp.int32, (nb, 1), 0)
    cmp_valid = (n_idx * CMP_STRIDE + CMP_LEN - 1 <= t_row4) & (n_idx < nb - 1)

    def mask_heads(ok, s):
        return jnp.where(jnp.concatenate([ok] * hg, axis=1), s, MASK_NEG)

    groups = range(n_grp)
    kc_hi, kc_lo = _split_bf16(kc_scr[...])
    q_split = [_split_bf16(qg[g]) for g in groups]
    q_b = [q_split[g][0] for g in groups]

    win_blocks = []
    for w in range(n_win):
        kb_int = qi - w
        kb = jnp.maximum(kb_int, 0)
        kpos = kb_int * tk + key_iota
        ok = (kpos <= t_row) & (kpos > t_row - WINDOW) & (kpos >= 0)
        k_blk = kwin_ref[pl.ds(pl.multiple_of(kb * tk, tk), tk), :].astype(BF16)
        win_blocks.append((kb, k_blk, ok))

    s_cmp = [_nt_dot(kc_hi, q_split[g][0]) + _nt_dot(kc_lo, q_split[g][0])
             + _nt_dot(kc_hi, q_split[g][1]) for g in groups]
    s_win = [jnp.concatenate([mask_heads(ok, _nt_dot(k_blk, q_b[g]))
                              for (_, k_blk, ok) in win_blocks], axis=0) for g in groups]

    p_cmp, p_win, l_win = [], [], []
    for g in groups:
        s_m = jnp.where(cmp_valid, s_cmp[g], MASK_NEG)
        e = jnp.where(cmp_valid, jnp.exp(s_m - jnp.max(s_m, axis=0, keepdims=True)), 0.0)
        den = jnp.sum(e, axis=0, keepdims=True)
        p_cmp.append(jnp.where(den > 0.0, e / jnp.where(den > 0.0, den, 1.0), 0.0))
    for g in groups:
        p = jnp.exp(s_win[g] - jnp.max(s_win[g], axis=0, keepdims=True))
        l_win.append(jnp.sum(p, axis=0, keepdims=True))
        p_win.append(p.astype(BF16))

    o_cmp = [_dot(vct_scr[g * dh:(g + 1) * dh, :], p_cmp[g].astype(BF16)) for g in groups]
    imp = []
    for g in groups:
        p_sum = p_cmp[g][:, 0:tq]
        for i in range(1, hg):
            p_sum = p_sum + p_cmp[g][:, i * tq:(i + 1) * tq]
        ps_hi, ps_lo = _split_bf16(p_sum)
        imp.append(_dot(overlap, ps_hi) + _dot(overlap, ps_lo))
    o_win = []
    for g in groups:
        v_t = jnp.concatenate([vwt_scr[kb, g * dh:(g + 1) * dh, :] for (kb, _, _) in win_blocks],
                              axis=1)
        o_win.append(_dot(v_t, p_win[g]) / l_win[g])

    for g in groups:
        score = jnp.where(sel_valid, imp[g] + jnp.where(forced, FORCE_BONUS, 0.0), MASK_NEG)
        beaten = jnp.zeros((n_sel, tq), F32)
        for jp in range(n_sel):
            other = score[jp:jp + 1, :]
            ge = jnp.where(other >= score, 1.0, 0.0)
            gt = jnp.where(other > score, 1.0, 0.0)
            beaten = beaten + jnp.where(j_sel > jp, ge, gt)
        picked = jnp.where(beaten < top_k, 1.0, 0.0)
        for j in range(n_sel):
            selk_scr[g, j * SEL_LEN:(j + 1) * SEL_LEN, :] = jnp.broadcast_to(
                picked[j:j + 1, :], (SEL_LEN, tq))

    def slc_blocks(n_tiles):
        def run(kb0, carry):
            k_blocks = [kslc_ref[pl.ds(pl.multiple_of((kb0 + u) * tk, tk), tk), :].astype(BF16)
                        for u in range(n_tiles)]
            tiles = {}
            for g in range(n_grp):
                for u in range(n_tiles):
                    k0 = pl.multiple_of((kb0 + u) * tk, tk)
                    ok = (selk_scr[g, pl.ds(k0, tk), :] > 0.5) & (k0 + key_iota <= t_row)
                    tiles[g, u] = mask_heads(ok, _nt_dot(k_blocks[u], q_b[g]))
            new = []
            for g in range(n_grp):
                m, l, acc = carry[g]
                for u in range(n_tiles):
                    s = tiles[g, u]
                    m_new = jnp.maximum(m, jnp.max(s, axis=0, keepdims=True))
                    alpha = jnp.exp(m - m_new)
                    p = jnp.exp(s - m_new)
                    l = alpha * l + jnp.sum(p, axis=0, keepdims=True)
                    acc = alpha * acc + _dot(vst_scr[kb0 + u, g * dh:(g + 1) * dh, :],
                                             p.astype(BF16))
                    m = m_new
                new.append((m, l, acc))
            return tuple(new)
        return run

    init = tuple((jnp.full((1, nq), MASK_NEG, F32), jnp.zeros((1, nq), F32),
                  jnp.zeros((dh, nq), F32)) for _ in range(n_grp))
    n_need = (qi + 1) * (tq // tk)
    full_body = slc_blocks(slc_step)
    slc = lax.fori_loop(0, n_need // slc_step, lambda it, c: full_body(it * slc_step, c), init)
    done = (n_need // slc_step) * slc_step
    arm = slc_step // 2
    while arm >= 1:
        take = (n_need & arm) != 0
        slc = lax.cond(take, functools.partial(slc_blocks(arm), done), lambda c: c, slc)
        done = done + jnp.where(take, arm, 0)
        arm //= 2
    o_slc = [acc / l for (_, l, acc) in slc]

    gates = _sigmoid(gate_ref[...].T)
    outs = []
    for g in range(n_grp):
        for i in range(hg):
            hd = g * hg + i
            cols = slice(i * tq, (i + 1) * tq)
            outs.append(gates[3 * hd:3 * hd + 1, :] * o_cmp[g][:, cols]
                        + gates[3 * hd + 1:3 * hd + 2, :] * o_slc[g][:, cols]
                        + gates[3 * hd + 2:3 * hd + 3, :] * o_win[g][:, cols])
    o_ref[...] = jnp.concatenate(outs, axis=0).T.astype(o_ref.dtype)


def _cmp_weight(w):
    half = CMP_LEN // 2
    w = w.reshape(2, half, NSA_DH, NSA_DH)
    z = jnp.zeros_like(w)
    big = jnp.stack([jnp.concatenate([w, z], axis=-1), jnp.concatenate([z, w], axis=-1)], axis=2)
    return big.reshape(2, half * NSA_KV_HEADS * NSA_DH, NSA_KV_HEADS * NSA_DH)


def _native_sparse_attention(proj3, cmp_wk, cmp_pk, cmp_wv, cmp_pv, *, tq):
    b, s, _ = proj3.shape
    nb = s // CMP_STRIDE
    n_kb = s // NSA_TK
    assert tq == NSA_TK and n_kb % min(NSA_SLC_BLOCKS, n_kb) == 0
    wk = _cmp_weight(cmp_wk)
    wv = _cmp_weight(cmp_wv)
    pk = jnp.tile(cmp_pk, (1, NSA_KV_HEADS))
    pv = jnp.tile(cmp_pv, (1, NSA_KV_HEADS))
    qw = NSA_HEADS * NSA_DH

    def seq_block(col):
        return pl.BlockSpec((None, s, LANE), lambda bi, i: (bi, 0, col))

    def whole(a):
        return pl.BlockSpec(a.shape, lambda bi, i: (0,) * a.ndim)

    n_sub = min(NSA_BLOCKS_PER_STEP, s // tq)
    t_step = n_sub * tq
    return pl.pallas_call(
        functools.partial(_nsa_kernel, seq=s, tq=tq, n_sub=n_sub),
        grid=(b, s // t_step),
        in_specs=[pl.BlockSpec((None, t_step, qw), lambda bi, i: (bi, i, COL_NQ * LANE // qw))]
        + [seq_block(COL_NKV + c) for c in range(6)]
        + [pl.BlockSpec((None, t_step, LANE), lambda bi, i: (bi, i, COL_NGATE)),
           whole(wk), whole(wv), whole(pk), whole(pv)],
        out_specs=pl.BlockSpec((None, t_step, qw), lambda bi, i: (bi, i, 0)),
        out_shape=jax.ShapeDtypeStruct((b, s, qw), BF16),
        scratch_shapes=[pltpu.VMEM((nb, LANE), F32),
                        pltpu.VMEM((LANE, nb), BF16),
                        pltpu.VMEM((n_kb, LANE, NSA_TK), BF16),
                        pltpu.VMEM((n_kb, LANE, NSA_TK), BF16),
                        pltpu.VMEM((NSA_KV_HEADS, s, tq), F32)],
        compiler_params=_cparams(2),
        name="native_sparse_attention",
    )(proj3, proj3, proj3, proj3, proj3, proj3, proj3, proj3, wk, wv, pk, pv)


def _pad_cols(w, width):
    return jnp.pad(w, ((0, 0),) * (w.ndim - 1) + ((0, width - w.shape[-1]),))


def _proj_weight(w_in):
    gqk_v_z = 4 * GDN_HEADS * GDN_D
    n_ab = 2 * GDN_HEADS
    n_nsa = (NSA_HEADS + 6 * NSA_KV_HEADS) * NSA_DH
    n_gate = 3 * NSA_HEADS
    o_ab = gqk_v_z
    o_nsa = o_ab + n_ab
    o_gate = o_nsa + n_nsa
    o_rest = o_gate + n_gate
    w = jnp.concatenate([w_in[..., :o_ab], w_in[..., o_nsa:o_gate],
                         _pad_cols(w_in[..., o_ab:o_nsa], LANE),
                         _pad_cols(w_in[..., o_gate:o_rest], LANE),
                         w_in[..., o_rest:]], axis=-1)
    assert w.shape[-1] == N_PROJ, w.shape
    return w.astype(BF16)


def _layer(x, p, big, layer, *, tiles):
    b, s, d = x.shape
    x2 = x.reshape(b * s, d)
    proj = _norm_matmul(x2, p["g_mix_pre"], big["w_proj"], layer, relu2=False, out_dtype=F32,
                        tm=tiles["tm"], tn=tiles["tn"])
    proj3 = proj.reshape(b, s, N_PROJ)
    ya = _gated_deltanet(proj3, p["gdn_conv_w"], p["gdn_a_log"], p["gdn_dt_bias"], p["gdn_norm_g"])
    yb = _native_sparse_attention(proj3, p["nsa_cmp_wk"], p["nsa_cmp_pk"], p["nsa_cmp_wv"],
                                  p["nsa_cmp_pv"], tq=tiles["tq"])
    yc = _sb_attention(proj3, tq=tiles["tq_sb"])
    x2 = _merge_out(ya.reshape(b * s, -1), yb.reshape(b * s, -1), yc.reshape(b * s, -1), proj, x2,
                    big["w_br_gdn"], big["w_br_nsa"], big["w_br_sb"], big["w_out"], layer,
                    p["g_mix_post"], tm=tiles["tm_out"])
    act = _norm_matmul(x2, p["g_ff_pre"], big["w_ff1"], layer, relu2=True, out_dtype=BF16,
                       tm=tiles["tm"], tn=tiles["tn_ff"])
    x2 = _matmul_norm_res(act, big["w_ff2"], layer, x2, p["g_ff_post"], tm=tiles["tm_out"])
    return x2.reshape(b, s, d)


def _tiles(b, s):
    t = b * s
    return {"tm": min(1024, t), "tn": 2048, "tn_ff": 4096, "tm_out": min(512, t), "tq": min(128, s),
            "tq_sb": min(256, s)}


def kernel(x, g_mix_pre, g_mix_post, g_ff_pre, g_ff_post, w_in, gdn_conv_w, gdn_a_log, gdn_dt_bias,
           gdn_norm_g, nsa_cmp_wk, nsa_cmp_pk, nsa_cmp_wv, nsa_cmp_pv, w_br_gdn, w_br_nsa, w_br_sb,
           w_out, w_ff1, w_ff2):
    small = dict(g_mix_pre=g_mix_pre, g_mix_post=g_mix_post, g_ff_pre=g_ff_pre, g_ff_post=g_ff_post,
                 gdn_conv_w=gdn_conv_w, gdn_a_log=gdn_a_log, gdn_dt_bias=gdn_dt_bias,
                 gdn_norm_g=gdn_norm_g, nsa_cmp_wk=nsa_cmp_wk, nsa_cmp_pk=nsa_cmp_pk,
                 nsa_cmp_wv=nsa_cmp_wv, nsa_cmp_pv=nsa_cmp_pv)
    big = dict(w_proj=_proj_weight(w_in), w_br_gdn=w_br_gdn.astype(BF16),
               w_br_nsa=w_br_nsa.astype(BF16), w_br_sb=w_br_sb.astype(BF16),
               w_out=w_out.astype(BF16), w_ff1=w_ff1.astype(BF16), w_ff2=w_ff2.astype(BF16))
    tiles = _tiles(x.shape[0], x.shape[1])
    for layer in range(w_in.shape[0]):
        x = _layer(x, {k: v[layer] for k, v in small.items()}, big, layer, tiles=tiles)
    return x
```

```python
import functools

import jax
import jax.numpy as jnp
from jax import lax
from jax.experimental import pallas as pl
from jax.experimental.pallas import tpu as pltpu

F32 = jnp.float32
BF16 = jnp.bfloat16

EPS = 1e-6
MASK_NEG = -1e30
LOG2E = 1.4426950408889634

GDN_HEADS = 4
GDN_D = 128
GDN_CONV = 4
GDN_CHUNK = 64
GDN_PREP_CHUNKS = 8
GDN_SCAN_TOKENS = 512
GDN_SCAN_BATCHES = 4

NSA_HEADS = 8
NSA_KV_HEADS = 2
NSA_GROUP = NSA_HEADS // NSA_KV_HEADS
NSA_DH = 64
CMP_LEN = 32
CMP_STRIDE = 16
SEL_LEN = 64
SEL_TOPK = 8
WINDOW = 512
FORCE_BONUS = 1e3
NSA_TK = 128
NSA_SLC_BLOCKS = 8
NSA_BLOCKS_PER_STEP = 2

SB_HEADS = 8
SB_DH = 64
SB_TK = 128
SB_DEAD_LOG2 = -160.0
SB_BLOCKS_PER_STEP = 2

LANE = 128

COL_GQ, COL_GK, COL_GV, COL_GZ = 0, 4, 8, 12
COL_NQ, COL_NKV, COL_GAB, COL_NGATE = 16, 20, 26, 27
COL_SQ, COL_SK, COL_SV = 28, 32, 36
COL_MERGE = 40
N_PROJ = 64 * LANE

VMEM_LIMIT = 48 * 1024 * 1024
VMEM_LIMIT_LARGE = 56 * 1024 * 1024


def _cparams(n_axes, vmem_limit=VMEM_LIMIT):
    return pltpu.CompilerParams(dimension_semantics=("arbitrary",) * n_axes,
                                vmem_limit_bytes=vmem_limit)


def _nt_dot(a, b):
    return lax.dot_general(a, b, (((1,), (1,)), ((), ())), preferred_element_type=F32)


def _tn_dot(a, b):
    return lax.dot_general(a, b, (((0,), (0,)), ((), ())), preferred_element_type=F32)


def _dot(a, b):
    return jnp.dot(a, b, preferred_element_type=F32)


def _rms_scale(y):
    return y * lax.rsqrt(jnp.mean(y * y, axis=-1, keepdims=True) + EPS)


def _softplus(z):
    return jnp.maximum(z, 0.0) + jnp.log1p(jnp.exp(-jnp.abs(z)))


def _sigmoid(z):
    return jax.nn.sigmoid(z)


def _split_bf16(x):
    hi = x.astype(BF16)
    return hi, (x - hi.astype(F32)).astype(BF16)


def _dot3(a, b):
    a_hi, a_lo = _split_bf16(a)
    b_hi, b_lo = _split_bf16(b)
    return _dot(a_hi, b_hi) + _dot(a_lo, b_hi) + _dot(a_hi, b_lo)


def _norm_matmul_kernel(x_ref, g_ref, w_ref, o_ref, h_scr, *, relu2):
    @pl.when(pl.program_id(1) == 0)
    def _():
        h_scr[...] = (_rms_scale(x_ref[...]) * g_ref[...]).astype(BF16)

    y = _dot(h_scr[...], w_ref[...])
    if relu2:
        y = jnp.square(jnp.maximum(y, 0.0))
    o_ref[...] = y.astype(o_ref.dtype)


def _norm_matmul(x, g, w, layer, *, relu2, out_dtype, tm, tn):
    t, d = x.shape
    n = w.shape[2]
    return pl.pallas_call(
        functools.partial(_norm_matmul_kernel, relu2=relu2),
        grid=(t // tm, n // tn),
        in_specs=[pl.BlockSpec((tm, d), lambda i, j: (i, 0)),
                  pl.BlockSpec((1, d), lambda i, j: (0, 0)),
                  pl.BlockSpec((None, d, tn), lambda i, j: (layer, 0, j))],
        out_specs=pl.BlockSpec((tm, tn), lambda i, j: (i, j)),
        out_shape=jax.ShapeDtypeStruct((t, n), out_dtype),
        scratch_shapes=[pltpu.VMEM((tm, d), BF16)],
        compiler_params=_cparams(2),
        name="norm_matmul_relu2" if relu2 else "norm_matmul",
    )(x, g.reshape(1, d), w)


def _matmul_norm_res_kernel(a_ref, w_ref, x_ref, g_ref, o_ref):
    y = _dot(a_ref[...], w_ref[...])
    o_ref[...] = x_ref[...] + _rms_scale(y) * g_ref[...]


def _matmul_norm_res(a, w, layer, x, g, *, tm):
    t, k = a.shape
    d = w.shape[2]
    return pl.pallas_call(
        _matmul_norm_res_kernel,
        grid=(t // tm,),
        in_specs=[pl.BlockSpec((tm, k), lambda i: (i, 0)),
                  pl.BlockSpec((None, k, d), lambda i: (layer, 0, 0), pipeline_mode=pl.Buffered(1)),
                  pl.BlockSpec((tm, d), lambda i: (i, 0)),
                  pl.BlockSpec((1, d), lambda i: (0, 0))],
        out_specs=pl.BlockSpec((tm, d), lambda i: (i, 0)),
        out_shape=jax.ShapeDtypeStruct((t, d), F32),
        compiler_params=_cparams(1, VMEM_LIMIT_LARGE),
        name="matmul_norm_res",
    )(a, w, x, g.reshape(1, d))


def _merge_out_kernel(ya_ref, yb_ref, yc_ref, m0_ref, m1_ref, m2_ref, x_ref,
                      wa_ref, wb_ref, wc_ref, wo_ref, g_ref, o_ref):
    merged = (_sigmoid(m0_ref[...]) * _dot(ya_ref[...], wa_ref[...])
              + _sigmoid(m1_ref[...]) * _dot(yb_ref[...], wb_ref[...])
              + _sigmoid(m2_ref[...]) * _dot(yc_ref[...], wc_ref[...]))
    y = _dot(merged.astype(BF16), wo_ref[...])
    o_ref[...] = x_ref[...] + _rms_scale(y) * g_ref[...]


def _merge_out(ya, yb, yc, proj, x, wa, wb, wc, wo, layer, g, *, tm):
    t, d = x.shape
    w_in = ya.shape[1]
    mcol = COL_MERGE * LANE // d

    def row(i):
        return (i, 0)

    def const(i):
        return (0, 0)

    def weight(i):
        return (layer, 0, 0)

    return pl.pallas_call(
        _merge_out_kernel,
        grid=(t // tm,),
        in_specs=[pl.BlockSpec((tm, w_in), row),
                  pl.BlockSpec((tm, w_in), row),
                  pl.BlockSpec((tm, w_in), row),
                  pl.BlockSpec((tm, d), lambda i: (i, mcol)),
                  pl.BlockSpec((tm, d), lambda i: (i, mcol + 1)),
                  pl.BlockSpec((tm, d), lambda i: (i, mcol + 2)),
                  pl.BlockSpec((tm, d), row),
                  pl.BlockSpec((None, w_in, d), weight),
                  pl.BlockSpec((None, w_in, d), weight),
                  pl.BlockSpec((None, w_in, d), weight),
                  pl.BlockSpec((None, d, d), weight),
                  pl.BlockSpec((1, d), const)],
        out_specs=pl.BlockSpec((tm, d), row),
        out_shape=jax.ShapeDtypeStruct((t, d), F32),
        compiler_params=_cparams(1),
        name="merge_out",
    )(ya, yb, yc, proj, proj, proj, x, wa, wb, wc, wo, g.reshape(1, d))


def _sb_kernel(q_ref, k_ref, v_ref, o_ref, run_scr, acc_scr, *, tq, n_sub):
    step = pl.program_id(1)
    for sub in range(n_sub):
        rows = pl.ds(sub * tq, tq)
        _sb_query_block(step * n_sub + sub, q_ref.at[rows], k_ref, v_ref, o_ref.at[rows],
                        run_scr, acc_scr, tq=tq)


def _sb_query_block(qi, q_ref, k_ref, v_ref, o_ref, run_scr, acc_scr, *, tq):
    tk = SB_TK
    n_diag = tq // tk
    dh = SB_DH
    n_pair = SB_HEADS * dh // LANE
    first_head = lax.broadcasted_iota(jnp.int32, (tk, LANE), 1) < dh
    r2 = lax.broadcasted_iota(jnp.int32, (2 * tk, 2 * tk), 0) & (tk - 1)
    c2 = lax.broadcasted_iota(jnp.int32, (2 * tk, 2 * tk), 1)
    later_ones = jnp.where((r2 > c2) | (c2 >= tk), 1.0, 0.0).astype(BF16)

    q_pairs = [(q_ref[:, p * LANE:(p + 1) * LANE] * (dh ** -0.5 * LOG2E)).astype(BF16)
               for p in range(n_pair)]

    run_scr[...] = jnp.zeros_like(run_scr)
    acc_scr[...] = jnp.zeros_like(acc_scr)

    def split_heads(x):
        zero = jnp.zeros_like(x)
        return jnp.concatenate([jnp.where(first_head, x, zero), jnp.where(first_head, zero, x)],
                               axis=0)

    def block(j, diag_offset):
        k0 = pl.multiple_of(j * tk, tk)
        pairs = range(n_pair)
        diagonal = diag_offset is not None
        rs = slice(diag_offset, tq) if diagonal else slice(0, tq)
        if diagonal:
            shape = (tq - diag_offset, tk)
            causal = (lax.broadcasted_iota(jnp.int32, shape, 1)
                      < lax.broadcasted_iota(jnp.int32, shape, 0))
            causal2 = jnp.concatenate([causal, causal], axis=1)
        z = [_nt_dot(q_pairs[p][rs],
                     split_heads(k_ref[pl.ds(k0, tk), p * LANE:(p + 1) * LANE].astype(BF16)))
             for p in pairs]
        log_sig, hi, lo = [], [], []
        for p in pairs:
            log_sig.append(jnp.minimum(z[p], 0.0) - jnp.log2(1.0 + jnp.exp2(-jnp.abs(z[p]))))
            log_keep = log_sig[p] - z[p]
            if diagonal:
                log_keep = jnp.where(causal2, log_keep, 0.0)
            hi.append(log_keep.astype(BF16))
            lo.append((log_keep - hi[p].astype(F32)).astype(BF16))
        sums = [[_dot(jnp.concatenate([hi[p][:, hh * tk:(hh + 1) * tk],
                                       lo[p][:, hh * tk:(hh + 1) * tk]], axis=1), later_ones)
                 for hh in range(2)] for p in pairs]
        for p in pairs:
            a = []
            for hh in range(2):
                h = 2 * p + hh
                run = run_scr[h, rs, :]
                a_h = jnp.exp2(log_sig[p][:, hh * tk:(hh + 1) * tk] + sums[p][hh][:, :tk] + run)
                if diagonal:
                    a_h = jnp.where(causal, a_h, 0.0)
                run_scr[h, rs, :] = run + sums[p][hh][:, tk:]
                a.append(a_h.astype(BF16))
            vp = v_ref[pl.ds(k0, tk), p * LANE:(p + 1) * LANE].astype(BF16)
            acc_scr[p, rs, :] += _dot(jnp.concatenate(a, axis=1), split_heads(vp))

    for d in reversed(range(n_diag)):
        block(qi * n_diag + d, d * tk)

    def still_live():
        top = run_scr[0]
        for h in range(1, SB_HEADS):
            top = jnp.maximum(top, run_scr[h])
        return jnp.max(top) > SB_DEAD_LOG2

    def body(carry):
        it, _ = carry
        block(qi * n_diag - 1 - it, None)
        return it + 1, still_live()

    lax.while_loop(lambda c: (c[0] < qi * n_diag) & c[1], body, (jnp.int32(0), still_live()))
    o_ref[...] = jnp.concatenate([acc_scr[p] for p in range(n_pair)], axis=-1).astype(o_ref.dtype)


def _sb_attention(proj3, *, tq):
    b, s, _ = proj3.shape
    width = SB_HEADS * SB_DH
    n_pair = width // LANE
    n_sub = min(SB_BLOCKS_PER_STEP, s // tq)
    t_step = n_sub * tq
    return pl.pallas_call(
        functools.partial(_sb_kernel, tq=tq, n_sub=n_sub),
        grid=(b, s // t_step),
        in_specs=[pl.BlockSpec((None, t_step, width), lambda bi, i: (bi, i, COL_SQ * LANE // width)),
                  pl.BlockSpec((None, s, width), lambda bi, i: (bi, 0, COL_SK * LANE // width)),
                  pl.BlockSpec((None, s, width), lambda bi, i: (bi, 0, COL_SV * LANE // width))],
        out_specs=pl.BlockSpec((None, t_step, width), lambda bi, i: (bi, i, 0)),
        out_shape=jax.ShapeDtypeStruct((b, s, width), BF16),
        scratch_shapes=[pltpu.VMEM((SB_HEADS, tq, LANE), F32), pltpu.VMEM((n_pair, tq, LANE), F32)],
        compiler_params=_cparams(2),
        name="sb_attention",
    )(proj3, proj3, proj3)


def _gdn_prep_kernel(scal_ref, q_ref, qh_ref, k_ref, kh_ref, v_ref, vh_ref, ab_ref, cw_ref,
                     u_ref, w_ref, qe_ref, kd_ref, attn_ref, egl_ref, *, cb):
    i = pl.program_id(1)
    c_len = GDN_CHUNK
    d = GDN_D
    n_h = GDN_HEADS
    ts = cb * c_len
    halo = qh_ref.shape[0]
    width = n_h * d

    def conv_silu(x_ref, halo_ref, w):
        ext = jnp.concatenate([halo_ref[...] * jnp.where(i > 0, 1.0, 0.0), x_ref[...]], axis=0)
        y = jnp.zeros((ts, width), F32)
        for tap in range(GDN_CONV):
            off = halo - (GDN_CONV - 1) + tap
            y = y + w[tap:tap + 1, :] * ext[off:off + ts]
        return y * _sigmoid(y)

    qc = conv_silu(q_ref, qh_ref, cw_ref[:, 0:width])
    kc = conv_silu(k_ref, kh_ref, cw_ref[:, width:2 * width])
    vc = conv_silu(v_ref, vh_ref, cw_ref[:, 2 * width:3 * width])

    ab = ab_ref[...]
    lane = lax.broadcasted_iota(jnp.int32, (1, LANE), 1)
    a_log = jnp.zeros((1, LANE), F32)
    dt_bias = jnp.zeros((1, LANE), F32)
    for h in range(n_h):
        a_log = jnp.where(lane == h, scal_ref[0, h], a_log)
        dt_bias = jnp.where(lane == h, scal_ref[1, h], dt_bias)
    g_cum = -jnp.exp(a_log) * _softplus(ab + dt_bias)
    row_in_chunk = lax.broadcasted_iota(jnp.int32, (ts, LANE), 0) & (c_len - 1)
    shift = 1
    while shift < c_len:
        g_cum = g_cum + jnp.where(row_in_chunk >= shift, pltpu.roll(g_cum, shift, axis=0), 0.0)
        shift *= 2
    beta_all = _sigmoid(ab)

    row = lax.broadcasted_iota(jnp.int32, (c_len, c_len), 0)
    col = lax.broadcasted_iota(jnp.int32, (c_len, c_len), 1)
    incl = row >= col
    strict = row > col
    eye = jnp.where(row == col, 1.0, 0.0)

    problems = [(c, h) for c in range(cb) for h in range(n_h)]
    qn, kn, k_beta, v_beta, g_col, decay = {}, {}, {}, {}, {}, {}
    for c, h in problems:
        rows = slice(c * c_len, (c + 1) * c_len)
        cols = slice(h * d, (h + 1) * d)
        qh, kh = qc[rows, cols], kc[rows, cols]
        qn[c, h] = qh * lax.rsqrt(jnp.sum(qh * qh, axis=-1, keepdims=True) + EPS) * (d ** -0.5)
        kn[c, h] = kh * lax.rsqrt(jnp.sum(kh * kh, axis=-1, keepdims=True) + EPS)
        beta = beta_all[rows, n_h + h:n_h + h + 1]
        k_beta[c, h] = kn[c, h] * beta
        v_beta[c, h] = (vc[rows, cols] * beta).astype(BF16)
        g_col[c, h] = g_cum[rows, h:h + 1]
        g_sq = jnp.broadcast_to(g_col[c, h], (c_len, c_len))
        decay[c, h] = jnp.where(incl, jnp.exp(jnp.where(incl, g_sq - g_sq.T, 0.0)), 0.0)

    kn_b = {p: kn[p].astype(BF16) for p in problems}
    kkt = {p: _nt_dot(k_beta[p].astype(BF16), kn_b[p]) for p in problems}
    qkt = {p: _nt_dot(qn[p].astype(BF16), kn_b[p]) for p in problems}

    a_mat = {p: jnp.where(strict, kkt[p] * decay[p], 0.0) for p in problems}
    left = lax.broadcasted_iota(jnp.int32, (c_len, 2 * c_len), 1) < c_len
    pair = {p: jnp.concatenate([-a_mat[p], eye], axis=1) for p in problems}
    for _ in range(max(c_len - 1, 1).bit_length()):
        y = {p: _dot(pair[p][:, :c_len].astype(BF16), pair[p].astype(BF16)) for p in problems}
        pair = {p: jnp.where(left, y[p], pair[p] + y[p]) for p in problems}
    t_inv = {p: pair[p][:, c_len:] for p in problems}
    resid = {}
    for p in problems:
        m_hi, m_lo = _split_bf16(eye + a_mat[p])
        t_hi, t_lo = _split_bf16(t_inv[p])
        resid[p] = (eye - (_dot(m_hi, t_hi) + _dot(m_lo, t_hi) + _dot(m_hi, t_lo)), t_hi)
    t_b = {p: (t_inv[p] + _dot(resid[p][1], resid[p][0].astype(BF16))).astype(BF16)
           for p in problems}

    for c, h in problems:
        p = (c, h)
        rows = slice(c * c_len, (c + 1) * c_len)
        cols = slice(h * d, (h + 1) * d)
        exp_g = jnp.exp(g_col[p])
        g_last = g_col[p][c_len - 1:c_len, :]
        u_ref[rows, cols] = _dot(t_b[p], v_beta[p])
        w_ref[rows, cols] = _dot(t_b[p], (k_beta[p] * exp_g).astype(BF16)).astype(BF16)
        qe_ref[rows, cols] = (qn[p] * exp_g).astype(BF16)
        kd_ref[rows, cols] = (kn[p] * jnp.exp(g_last - g_col[p])).astype(BF16)
    egl_rows = [jnp.concatenate([jnp.broadcast_to(jnp.exp(g_col[c, h][c_len - 1:c_len, :]), (1, d))
                                 for h in range(n_h)], axis=1) for c in range(cb)]
    if egl_ref.shape[0] > cb:
        egl_rows.append(jnp.zeros((egl_ref.shape[0] - cb, width), F32))
    egl_ref[...] = jnp.concatenate(egl_rows, axis=0)
    for c in range(cb):
        rows = slice(c * c_len, (c + 1) * c_len)
        attn_ref[rows, :] = jnp.concatenate(
            [(qkt[c, h] * decay[c, h]).astype(BF16) for h in range(n_h)], axis=1)


def _gdn_scan_kernel(u_ref, w_ref, qe_ref, kd_ref, attn_ref, egl_ref, z_ref, ng_ref, o_ref,
                     state_scr, *, bb, ts, cb):
    j = pl.program_id(1)
    c_len = GDN_CHUNK
    d = GDN_D
    n_c = ts // c_len
    chains = [(b_, h) for b_ in range(bb) for h in range(GDN_HEADS)]

    @pl.when(j == 0)
    def _():
        state_scr[...] = jnp.zeros_like(state_scr)

    def chunk(c, carry):
        r0 = pl.multiple_of(c * c_len, c_len)
        rows = pl.ds(r0, c_len)
        cg = j * n_c + c
        state = {p: state_scr[p[0], p[1]] for p in chains}
        state_b = {p: state[p].astype(BF16) for p in chains}
        w_s = {(b_, h): _dot(w_ref[b_, rows, h * d:(h + 1) * d], state_b[b_, h]) for b_, h in chains}
        q_s = {(b_, h): _dot(qe_ref[b_, rows, h * d:(h + 1) * d], state_b[b_, h]) for b_, h in chains}
        v_new = {(b_, h): (u_ref[b_, rows, h * d:(h + 1) * d] - w_s[b_, h]).astype(BF16)
                 for b_, h in chains}
        sub = lax.broadcasted_iota(jnp.int32, egl_ref.shape[2:], 0)
        chunk_decay = [jnp.sum(jnp.where(sub == cg % cb, egl_ref[b_, cg // cb], 0.0),
                               axis=0, keepdims=True) for b_ in range(bb)]
        for b_, h in chains:
            cols = slice(h * d, (h + 1) * d)
            o = q_s[b_, h] + _dot(attn_ref[b_, rows, h * c_len:(h + 1) * c_len], v_new[b_, h])
            state_scr[b_, h] = (state[b_, h] * chunk_decay[b_][:, cols]
                                + _tn_dot(kd_ref[b_, rows, cols], v_new[b_, h]))
            zc = z_ref[b_, rows, cols]
            o_ref[b_, rows, cols] = (_rms_scale(o) * ng_ref[...] * (zc * _sigmoid(zc))
                                     ).astype(o_ref.dtype)
        return carry

    lax.fori_loop(0, n_c, chunk, 0)


def _gated_deltanet(proj3, conv_w, a_log, dt_bias, norm_g):
    b, s, _ = proj3.shape
    scal = jnp.stack([a_log, dt_bias]).astype(F32)
    width = GDN_HEADS * GDN_D
    n_chunks = s // GDN_CHUNK
    cb = min(GDN_PREP_CHUNKS, n_chunks)
    ts = cb * GDN_CHUNK
    halo = 8
    egl_rows = -(-cb // 8) * 8

    def main(col0):
        return pl.BlockSpec((None, ts, width), lambda bi, i: (bi, i, col0 * LANE // width))

    def before(col0):
        return pl.BlockSpec((None, halo, width),
                            lambda bi, i: (bi, jnp.maximum(i * (ts // halo) - 1, 0),
                                           col0 * LANE // width))

    def out_block(w_):
        return pl.BlockSpec((None, ts, w_), lambda bi, i: (bi, i, 0))

    u, w, qe, kd, attn, egl = pl.pallas_call(
        functools.partial(_gdn_prep_kernel, cb=cb),
        grid=(b, n_chunks // cb),
        in_specs=[pl.BlockSpec(memory_space=pltpu.SMEM),
                  main(COL_GQ), before(COL_GQ), main(COL_GK), before(COL_GK),
                  main(COL_GV), before(COL_GV),
                  pl.BlockSpec((None, ts, LANE), lambda bi, i: (bi, i, COL_GAB)),
                  pl.BlockSpec(conv_w.shape, lambda bi, i: (0, 0))],
        out_specs=[out_block(width), out_block(width), out_block(width), out_block(width),
                   out_block(GDN_HEADS * GDN_CHUNK),
                   pl.BlockSpec((None, None, egl_rows, width), lambda bi, i: (bi, i, 0, 0))],
        out_shape=[jax.ShapeDtypeStruct((b, s, width), F32),
                   jax.ShapeDtypeStruct((b, s, width), BF16),
                   jax.ShapeDtypeStruct((b, s, width), BF16),
                   jax.ShapeDtypeStruct((b, s, width), BF16),
                   jax.ShapeDtypeStruct((b, s, GDN_HEADS * GDN_CHUNK), BF16),
                   jax.ShapeDtypeStruct((b, n_chunks // cb, egl_rows, width), F32)],
        compiler_params=_cparams(2),
        name="gdn_prep",
    )(scal, proj3, proj3, proj3, proj3, proj3, proj3, proj3, conv_w)

    bb = max(n for n in (1, 2, GDN_SCAN_BATCHES) if b % n == 0)
    t_scan = min(GDN_SCAN_TOKENS, s)

    def scan_block(w_):
        return pl.BlockSpec((bb, t_scan, w_), lambda bi, j: (bi, j, 0))

    return pl.pallas_call(
        functools.partial(_gdn_scan_kernel, bb=bb, ts=t_scan, cb=cb),
        grid=(b // bb, s // t_scan),
        in_specs=[scan_block(width), scan_block(width), scan_block(width), scan_block(width),
                  scan_block(GDN_HEADS * GDN_CHUNK),
                  pl.BlockSpec((bb, n_chunks // cb, egl_rows, width), lambda bi, j: (bi, 0, 0, 0)),
                  pl.BlockSpec((bb, t_scan, width), lambda bi, j: (bi, j, COL_GZ * LANE // width)),
                  pl.BlockSpec((1, GDN_D), lambda bi, j: (0, 0))],
        out_specs=scan_block(width),
        out_shape=jax.ShapeDtypeStruct((b, s, width), BF16),
        scratch_shapes=[pltpu.VMEM((bb, GDN_HEADS, GDN_D, GDN_D), F32)],
        compiler_params=_cparams(2),
        name="gdn_scan",
    )(u, w, qe, kd, attn, egl, proj3, norm_g.reshape(1, GDN_D))


def _nsa_stage_kv(kcmp_ref, vcmp_ref, vslc_ref, vwin_ref, wk_ref, wv_ref, pk_ref, pv_ref,
                  kc_scr, vct_scr, vst_scr, vwt_scr, *, seq):
    tk = NSA_TK
    nb = seq // CMP_STRIDE
    half = CMP_LEN // 2
    cmp_out = []
    for src, pos, w_ref in ((kcmp_ref, pk_ref, wk_ref), (vcmp_ref, pv_ref, wv_ref)):
        lo_parts, hi_parts = [], []
        for l in range(half):
            rows = src[pl.ds(l, nb, stride=CMP_STRIDE), :]
            lo_parts.append(rows + pos[l:l + 1, :])
            hi_parts.append(rows + pos[half + l:half + l + 1, :])
        a_lo = _dot3(jnp.concatenate(lo_parts, axis=1), w_ref[0])
        a_hi = _dot3(jnp.concatenate(hi_parts, axis=1), w_ref[1])
        a_hi = jnp.concatenate([a_hi[1:], jnp.zeros((1, LANE), F32)], axis=0)
        cmp_out.append(a_lo + a_hi)
    kc_scr[...] = cmp_out[0]
    vct_scr[...] = cmp_out[1].T.astype(BF16)

    def transpose_values(kb, carry):
        k0 = pl.multiple_of(kb * tk, tk)
        vst_scr[kb] = vslc_ref[pl.ds(k0, tk), :].T.astype(BF16)
        vwt_scr[kb] = vwin_ref[pl.ds(k0, tk), :].T.astype(BF16)
        return carry

    lax.fori_loop(0, seq // tk, transpose_values, 0)


def _nsa_kernel(q_ref, kcmp_ref, vcmp_ref, kslc_ref, vslc_ref, kwin_ref, vwin_ref, gate_ref,
                wk_ref, wv_ref, pk_ref, pv_ref, o_ref,
                kc_scr, vct_scr, vst_scr, vwt_scr, selk_scr, *, seq, tq, n_sub):
    step = pl.program_id(1)

    @pl.when(step == 0)
    def _():
        _nsa_stage_kv(kcmp_ref, vcmp_ref, vslc_ref, vwin_ref, wk_ref, wv_ref, pk_ref, pv_ref,
                      kc_scr, vct_scr, vst_scr, vwt_scr, seq=seq)

    for sub in range(n_sub):
        rows = pl.ds(sub * tq, tq)
        _nsa_query_block(step * n_sub + sub, q_ref.at[rows], kslc_ref, kwin_ref, gate_ref.at[rows],
                         o_ref.at[rows], kc_scr, vct_scr, vst_scr, vwt_scr, selk_scr,
                         seq=seq, tq=tq)


def _nsa_query_block(qi, q_ref, kslc_ref, kwin_ref, gate_ref, o_ref,
                     kc_scr, vct_scr, vst_scr, vwt_scr, selk_scr, *, seq, tq):
    tk = NSA_TK
    dh = NSA_DH
    hg = NSA_GROUP
    n_grp = NSA_KV_HEADS
    nq = hg * tq
    nb = seq // CMP_STRIDE
    n_sel = seq // SEL_LEN
    n_kb = seq // tk
    top_k = min(SEL_TOPK, n_sel)
    slc_step = min(NSA_SLC_BLOCKS, n_kb)
    n_win = WINDOW // tk + 1

    lane = lax.broadcasted_iota(jnp.int32, (tq, LANE), 1)
    qg = []
    for g in range(n_grp):
        parts = []
        for i in range(hg):
            hd = g * hg + i
            pair = q_ref[:, (hd // 2) * LANE:(hd // 2 + 1) * LANE] * (dh ** -0.5)
            if hd % 2 != g:
                pair = pltpu.roll(pair, dh, axis=1)
            parts.append(jnp.where((lane >= g * dh) & (lane < (g + 1) * dh), pair, 0.0))
        qg.append(jnp.concatenate(parts, axis=0))

    t_row = qi * tq + lax.broadcasted_iota(jnp.int32, (1, tq), 1)
    t_row4 = jnp.concatenate([t_row] * hg, axis=1)
    key_iota = lax.broadcasted_iota(jnp.int32, (tk, 1), 0)

    j_sel = lax.broadcasted_iota(jnp.int32, (n_sel, tq), 0)
    cur = lax.shift_right_logical(t_row, SEL_LEN.bit_length() - 1)
    sel_valid = j_sel <= cur
    forced = (j_sel == 0) | (j_sel == cur) | (j_sel == cur - 1)
    ov_j = lax.broadcasted_iota(jnp.int32, (n_sel, nb), 0) * SEL_LEN
    ov_n = lax.broadcasted_iota(jnp.int32, (n_sel, nb), 1) * CMP_STRIDE
    overlap = jnp.where((ov_n < ov_j + SEL_LEN) & (ov_n + CMP_LEN > ov_j), 1.0, 0.0).astype(BF16)
    n_idx = lax.broadcasted_iota(jnp.int32, (nb, 1), 0)
    cmp_valid = (n_idx * CMP_STRIDE + CMP_LEN - 1 <= t_row4) & (n_idx < nb - 1)

    def mask_heads(ok, s):
        return jnp.where(jnp.concatenate([ok] * hg, axis=1), s, MASK_NEG)

    groups = range(n_grp)
    kc_hi, kc_lo = _split_bf16(kc_scr[...])
    q_split = [_split_bf16(qg[g]) for g in groups]
    q_b = [q_split[g][0] for g in groups]

    win_blocks = []
    for w in range(n_win):
        kb_int = qi - w
        kb = jnp.maximum(kb_int, 0)
        kpos = kb_int * tk + key_iota
        ok = (kpos <= t_row) & (kpos > t_row - WINDOW) & (kpos >= 0)
        k_blk = kwin_ref[pl.ds(pl.multiple_of(kb * tk, tk), tk), :].astype(BF16)
        win_blocks.append((kb, k_blk, ok))

    s_cmp = [_nt_dot(kc_hi, q_split[g][0]) + _nt_dot(kc_lo, q_split[g][0])
             + _nt_dot(kc_hi, q_split[g][1]) for g in groups]
    s_win = [jnp.concatenate([mask_heads(ok, _nt_dot(k_blk, q_b[g]))
                              for (_, k_blk, ok) in win_blocks], axis=0) for g in groups]

    p_cmp, p_win, l_win = [], [], []
    for g in groups:
        s_m = jnp.where(cmp_valid, s_cmp[g], MASK_NEG)
        e = jnp.where(cmp_valid, jnp.exp(s_m - jnp.max(s_m, axis=0, keepdims=True)), 0.0)
        den = jnp.sum(e, axis=0, keepdims=True)
        p_cmp.append(jnp.where(den > 0.0, e / jnp.where(den > 0.0, den, 1.0), 0.0))
    for g in groups:
        p = jnp.exp(s_win[g] - jnp.max(s_win[g], axis=0, keepdims=True))
        l_win.append(jnp.sum(p, axis=0, keepdims=True))
        p_win.append(p.astype(BF16))

    o_cmp = [_dot(vct_scr[g * dh:(g + 1) * dh, :], p_cmp[g].astype(BF16)) for g in groups]
    imp = []
    for g in groups:
        p_sum = p_cmp[g][:, 0:tq]
        for i in range(1, hg):
            p_sum = p_sum + p_cmp[g][:, i * tq:(i + 1) * tq]
        ps_hi, ps_lo = _split_bf16(p_sum)
        imp.append(_dot(overlap, ps_hi) + _dot(overlap, ps_lo))
    o_win = []
    for g in groups:
        v_t = jnp.concatenate([vwt_scr[kb, g * dh:(g + 1) * dh, :] for (kb, _, _) in win_blocks],
                              axis=1)
        o_win.append(_dot(v_t, p_win[g]) / l_win[g])

    for g in groups:
        score = jnp.where(sel_valid, imp[g] + jnp.where(forced, FORCE_BONUS, 0.0), MASK_NEG)
        beaten = jnp.zeros((n_sel, tq), F32)
        for jp in range(n_sel):
            other = score[jp:jp + 1, :]
            ge = jnp.where(other >= score, 1.0, 0.0)
            gt = jnp.where(other > score, 1.0, 0.0)
            beaten = beaten + jnp.where(j_sel > jp, ge, gt)
        picked = jnp.where(beaten < top_k, 1.0, 0.0)
        for j in range(n_sel):
            selk_scr[g, j * SEL_LEN:(j + 1) * SEL_LEN, :] = jnp.broadcast_to(
                picked[j:j + 1, :], (SEL_LEN, tq))

    def slc_blocks(n_tiles):
        def run(kb0, carry):
            k_blocks = [kslc_ref[pl.ds(pl.multiple_of((kb0 + u) * tk, tk), tk), :].astype(BF16)
                        for u in range(n_tiles)]
            tiles = {}
            for g in range(n_grp):
                for u in range(n_tiles):
                    k0 = pl.multiple_of((kb0 + u) * tk, tk)
                    ok = (selk_scr[g, pl.ds(k0, tk), :] > 0.5) & (k0 + key_iota <= t_row)
                    tiles[g, u] = mask_heads(ok, _nt_dot(k_blocks[u], q_b[g]))
            new = []
            for g in range(n_grp):
                m, l, acc = carry[g]
                for u in range(n_tiles):
                    s = tiles[g, u]
                    m_new = jnp.maximum(m, jnp.max(s, axis=0, keepdims=True))
                    alpha = jnp.exp(m - m_new)
                    p = jnp.exp(s - m_new)
                    l = alpha * l + jnp.sum(p, axis=0, keepdims=True)
                    acc = alpha * acc + _dot(vst_scr[kb0 + u, g * dh:(g + 1) * dh, :],
                                             p.astype(BF16))
                    m = m_new
                new.append((m, l, acc))
            return tuple(new)
        return run

    init = tuple((jnp.full((1, nq), MASK_NEG, F32), jnp.zeros((1, nq), F32),
                  jnp.zeros((dh, nq), F32)) for _ in range(n_grp))
    n_need = (qi + 1) * (tq // tk)
    full_body = slc_blocks(slc_step)
    slc = lax.fori_loop(0, n_need // slc_step, lambda it, c: full_body(it * slc_step, c), init)
    done = (n_need // slc_step) * slc_step
    arm = slc_step // 2
    while arm >= 1:
        take = (n_need & arm) != 0
        slc = lax.cond(take, functools.partial(slc_blocks(arm), done), lambda c: c, slc)
        done = done + jnp.where(take, arm, 0)
        arm //= 2
    o_slc = [acc / l for (_, l, acc) in slc]

    gates = _sigmoid(gate_ref[...].T)
    outs = []
    for g in range(n_grp):
        for i in range(hg):
            hd = g * hg + i
            cols = slice(i * tq, (i + 1) * tq)
            outs.append(gates[3 * hd:3 * hd + 1, :] * o_cmp[g][:, cols]
                        + gates[3 * hd + 1:3 * hd + 2, :] * o_slc[g][:, cols]
                        + gates[3 * hd + 2:3 * hd + 3, :] * o_win[g][:, cols])
    o_ref[...] = jnp.concatenate(outs, axis=0).T.astype(o_ref.dtype)


def _cmp_weight(w):
    half = CMP_LEN // 2
    w = w.reshape(2, half, NSA_DH, NSA_DH)
    z = jnp.zeros_like(w)
    big = jnp.stack([jnp.concatenate([w, z], axis=-1), jnp.concatenate([z, w], axis=-1)], axis=2)
    return big.reshape(2, half * NSA_KV_HEADS * NSA_DH, NSA_KV_HEADS * NSA_DH)


def _native_sparse_attention(proj3, cmp_wk, cmp_pk, cmp_wv, cmp_pv, *, tq):
    b, s, _ = proj3.shape
    nb = s // CMP_STRIDE
    n_kb = s // NSA_TK
    assert tq == NSA_TK and n_kb % min(NSA_SLC_BLOCKS, n_kb) == 0
    wk = _cmp_weight(cmp_wk)
    wv = _cmp_weight(cmp_wv)
    pk = jnp.tile(cmp_pk, (1, NSA_KV_HEADS))
    pv = jnp.tile(cmp_pv, (1, NSA_KV_HEADS))
    qw = NSA_HEADS * NSA_DH

    def seq_block(col):
        return pl.BlockSpec((None, s, LANE), lambda bi, i: (bi, 0, col))

    def whole(a):
        return pl.BlockSpec(a.shape, lambda bi, i: (0,) * a.ndim)

    n_sub = min(NSA_BLOCKS_PER_STEP, s // tq)
    t_step = n_sub * tq
    return pl.pallas_call(
        functools.partial(_nsa_kernel, seq=s, tq=tq, n_sub=n_sub),
        grid=(b, s // t_step),
        in_specs=[pl.BlockSpec((None, t_step, qw), lambda bi, i: (bi, i, COL_NQ * LANE // qw))]
        + [seq_block(COL_NKV + c) for c in range(6)]
        + [pl.BlockSpec((None, t_step, LANE), lambda bi, i: (bi, i, COL_NGATE)),
           whole(wk), whole(wv), whole(pk), whole(pv)],
        out_specs=pl.BlockSpec((None, t_step, qw), lambda bi, i: (bi, i, 0)),
        out_shape=jax.ShapeDtypeStruct((b, s, qw), BF16),
        scratch_shapes=[pltpu.VMEM((nb, LANE), F32),
                        pltpu.VMEM((LANE, nb), BF16),
                        pltpu.VMEM((n_kb, LANE, NSA_TK), BF16),
                        pltpu.VMEM((n_kb, LANE, NSA_TK), BF16),
                        pltpu.VMEM((NSA_KV_HEADS, s, tq), F32)],
        compiler_params=_cparams(2),
        name="native_sparse_attention",
    )(proj3, proj3, proj3, proj3, proj3, proj3, proj3, proj3, wk, wv, pk, pv)


def _pad_cols(w, width):
    return jnp.pad(w, ((0, 0),) * (w.ndim - 1) + ((0, width - w.shape[-1]),))


def _proj_weight(w_in):
    gqk_v_z = 4 * GDN_HEADS * GDN_D
    n_ab = 2 * GDN_HEADS
    n_nsa = (NSA_HEADS + 6 * NSA_KV_HEADS) * NSA_DH
    n_gate = 3 * NSA_HEADS
    o_ab = gqk_v_z
    o_nsa = o_ab + n_ab
    o_gate = o_nsa + n_nsa
    o_rest = o_gate + n_gate
    w = jnp.concatenate([w_in[..., :o_ab], w_in[..., o_nsa:o_gate],
                         _pad_cols(w_in[..., o_ab:o_nsa], LANE),
                         _pad_cols(w_in[..., o_gate:o_rest], LANE),
                         w_in[..., o_rest:]], axis=-1)
    assert w.shape[-1] == N_PROJ, w.shape
    return w.astype(BF16)


def _layer(x, p, big, layer, *, tiles):
    b, s, d = x.shape
    x2 = x.reshape(b * s, d)
    proj = _norm_matmul(x2, p["g_mix_pre"], big["w_proj"], layer, relu2=False, out_dtype=F32,
                        tm=tiles["tm"], tn=tiles["tn"])
    proj3 = proj.reshape(b, s, N_PROJ)
    ya = _gated_deltanet(proj3, p["gdn_conv_w"], p["gdn_a_log"], p["gdn_dt_bias"], p["gdn_norm_g"])
    yb = _native_sparse_attention(proj3, p["nsa_cmp_wk"], p["nsa_cmp_pk"], p["nsa_cmp_wv"],
                                  p["nsa_cmp_pv"], tq=tiles["tq"])
    yc = _sb_attention(proj3, tq=tiles["tq_sb"])
    x2 = _merge_out(ya.reshape(b * s, -1), yb.reshape(b * s, -1), yc.reshape(b * s, -1), proj, x2,
                    big["w_br_gdn"], big["w_br_nsa"], big["w_br_sb"], big["w_out"], layer,
                    p["g_mix_post"], tm=tiles["tm_out"])
    act = _norm_matmul(x2, p["g_ff_pre"], big["w_ff1"], layer, relu2=True, out_dtype=BF16,
                       tm=tiles["tm"], tn=tiles["tn_ff"])
    x2 = _matmul_norm_res(act, big["w_ff2"], layer, x2, p["g_ff_post"], tm=tiles["tm"])
    return x2.reshape(b, s, d)


def _tiles(b, s):
    t = b * s
    return {"tm": min(1024, t), "tn": 2048, "tn_ff": 4096, "tm_out": min(512, t), "tq": min(128, s),
            "tq_sb": min(256, s)}


def kernel(x, g_mix_pre, g_mix_post, g_ff_pre, g_ff_post, w_in, gdn_conv_w, gdn_a_log, gdn_dt_bias,
           gdn_norm_g, nsa_cmp_wk, nsa_cmp_pk, nsa_cmp_wv, nsa_cmp_pv, w_br_gdn, w_br_nsa, w_br_sb,
           w_out, w_ff1, w_ff2):
    small = dict(g_mix_pre=g_mix_pre, g_mix_post=g_mix_post, g_ff_pre=g_ff_pre, g_ff_post=g_ff_post,
                 gdn_conv_w=gdn_conv_w, gdn_a_log=gdn_a_log, gdn_dt_bias=gdn_dt_bias,
                 gdn_norm_g=gdn_norm_g, nsa_cmp_wk=nsa_cmp_wk, nsa_cmp_pk=nsa_cmp_pk,
                 nsa_cmp_wv=nsa_cmp_wv, nsa_cmp_pv=nsa_cmp_pv)
    big = dict(w_proj=_proj_weight(w_in), w_br_gdn=w_br_gdn.astype(BF16),
               w_br_nsa=w_br_nsa.astype(BF16), w_br_sb=w_br_sb.astype(BF16),
               w_out=w_out.astype(BF16), w_ff1=w_ff1.astype(BF16), w_ff2=w_ff2.astype(BF16))
    tiles = _tiles(x.shape[0], x.shape[1])
    for layer in range(w_in.shape[0]):
        x = _layer(x, {k: v[layer] for k, v in small.items()}, big, layer, tiles=tiles)
    return x
```
